```python
import jax, jax.numpy as jnp
from jax import lax
import numpy as np

D_MODEL = 1024
BATCH = 8
SEQ = 2048
DEPTH = 1

D_MIX = D_MODEL
D_ATTN = D_MIX // 2
D_POOL = D_MIX - D_ATTN
HEAD_DIM = 64
N_HEADS = D_ATTN // HEAD_DIM
POOL_WINDOWS = (2, 4, 8, 16)
N_POOL_GROUPS = len(POOL_WINDOWS)
POOL_GROUP_DIM = D_POOL // N_POOL_GROUPS
Q_BLOCK = 128
EPS = 1e-6
D_IN = 4 * D_ATTN + 2 * D_POOL

kernel_name = "stickbreak_pool_hybrid_block"


def rms_norm(x, g):
    xf = x.astype(jnp.float32)
    y = xf * lax.rsqrt(jnp.mean(xf * xf, axis=-1, keepdims=True) + EPS)
    return (y * g.astype(jnp.float32)).astype(x.dtype)


def stick_breaking_attention(q, k, v):
    B, H, S, Dh = q.shape
    nb = S // Q_BLOCK
    qf = q.astype(jnp.float32) * (Dh ** -0.5)
    kb = k.astype(jnp.float32).reshape(B, H, nb, Q_BLOCK, Dh).transpose(2, 0, 1, 3, 4)
    vb = v.astype(jnp.float32).reshape(B, H, nb, Q_BLOCK, Dh).transpose(2, 0, 1, 3, 4)
    offs = jnp.arange(Q_BLOCK, dtype=jnp.int32)
    outs = []
    for i in range(nb):
        q_i = qf[:, :, i * Q_BLOCK:(i + 1) * Q_BLOCK]
        q_pos = i * Q_BLOCK + offs

        def step(carry, xs, q_i=q_i, q_pos=q_pos):
            acc, log_rem = carry
            k_j, v_j, j = xs
            k_pos = j * Q_BLOCK + offs
            mask = k_pos[None, :] < q_pos[:, None]
            z = jnp.einsum('bhqd,bhkd->bhqk', q_i, k_j)
            log_one_minus = jnp.where(mask, jax.nn.log_sigmoid(-z), 0.0)
            later = lax.cumsum(log_one_minus, axis=3, reverse=True) - log_one_minus
            log_w = jax.nn.log_sigmoid(z) + later + log_rem
            w = jnp.where(mask, jnp.exp(log_w), 0.0)
            acc = acc + jnp.einsum('bhqk,bhkd->bhqd', w, v_j)
            log_rem = log_rem + jnp.sum(log_one_minus, axis=3, keepdims=True)
            return (acc, log_rem), None

        idx = jnp.arange(i, -1, -1, dtype=jnp.int32)
        init = (jnp.zeros((B, H, Q_BLOCK, Dh), jnp.float32),
                jnp.zeros((B, H, Q_BLOCK, 1), jnp.float32))
        (acc, _), _ = lax.scan(step, init, (kb[idx], vb[idx], idx))
        outs.append(acc)
    return jnp.concatenate(outs, axis=2).astype(v.dtype)


def multiscale_causal_pool(u):
    B, S, _ = u.shape
    uf = u.astype(jnp.float32).reshape(B, S, N_POOL_GROUPS, POOL_GROUP_DIM)
    cs = jnp.concatenate([jnp.zeros((B, 1, N_POOL_GROUPS, POOL_GROUP_DIM), jnp.float32),
                          jnp.cumsum(uf, axis=1)], axis=1)
    t = jnp.arange(S, dtype=jnp.int32)[:, None]
    win = jnp.array(POOL_WINDOWS, dtype=jnp.int32)[None, :]
    lo = jnp.maximum(t + 1 - win, 0)
    grp = jnp.arange(N_POOL_GROUPS, dtype=jnp.int32)[None, :]
    window_sum = cs[:, 1:] - cs[:, lo, grp]
    count = (t + 1 - lo).astype(jnp.float32)
    pooled = window_sum / count[None, :, :, None] - uf
    return pooled.astype(u.dtype)


def setup_inputs(seed: int = 0) -> dict:
    key = jax.random.key(seed)
    ks = jax.random.split(key, 14)
    L = DEPTH
    x = jax.random.normal(ks[0], (BATCH, SEQ, D_MODEL), jnp.float32)
    c = jax.random.normal(ks[1], (BATCH, D_MODEL), jnp.float32)
    w_ada = jax.random.normal(ks[2], (L, D_MODEL, 3 * D_MODEL), jnp.float32) * (0.5 * D_MODEL ** -0.5)
    b_ada = jax.random.normal(ks[3], (L, 3 * D_MODEL), jnp.float32) * 0.02
    norm_g = 1.0 + 0.02 * jax.random.normal(ks[4], (L, D_MODEL), jnp.float32)
    w_in = jax.random.normal(ks[5], (L, D_MODEL, D_IN), jnp.float32) * (D_MODEL ** -0.5)
    q_norm_g = 1.0 + 0.02 * jax.random.normal(ks[6], (L, HEAD_DIM), jnp.float32)
    k_norm_g = 1.0 + 0.02 * jax.random.normal(ks[7], (L, HEAD_DIM), jnp.float32)
    w_pool = jax.random.normal(ks[8], (L, N_POOL_GROUPS, POOL_GROUP_DIM, POOL_GROUP_DIM), jnp.float32) * (POOL_GROUP_DIM ** -0.5)
    b_pool = jax.random.normal(ks[9], (L, N_POOL_GROUPS, POOL_GROUP_DIM), jnp.float32) * 0.02
    pool_scale = 1.0 + 0.1 * jax.random.normal(ks[10], (L, D_POOL), jnp.float32)
    w_out = jax.random.normal(ks[11], (L, D_MIX, D_MODEL), jnp.float32) * (D_MIX ** -0.5)
    return {"x": x, "c": c, "w_ada": w_ada, "b_ada": b_ada, "norm_g": norm_g,
            "w_in": w_in, "q_norm_g": q_norm_g, "k_norm_g": k_norm_g,
            "w_pool": w_pool, "b_pool": b_pool, "pool_scale": pool_scale, "w_out": w_out}


def reference(x, c, w_ada, b_ada, norm_g, w_in, q_norm_g, k_norm_g,
              w_pool, b_pool, pool_scale, w_out):
    B, S, D = x.shape
    h = x
    c_act = jax.nn.silu(c)
    for l in range(DEPTH):
        mod = c_act @ w_ada[l] + b_ada[l]
        shift, scale, gate = jnp.split(mod, 3, axis=-1)
        hn = rms_norm(h, norm_g[l]) * (1.0 + scale[:, None, :]) + shift[:, None, :]

        proj = hn @ w_in[l]
        q, k, v, g_attn, u, g_pool = jnp.split(
            proj, [D_ATTN, 2 * D_ATTN, 3 * D_ATTN, 4 * D_ATTN, 4 * D_ATTN + D_POOL], axis=-1)

        q = rms_norm(q.reshape(B, S, N_HEADS, HEAD_DIM), q_norm_g[l]).transpose(0, 2, 1, 3)
        k = rms_norm(k.reshape(B, S, N_HEADS, HEAD_DIM), k_norm_g[l]).transpose(0, 2, 1, 3)
        v = v.reshape(B, S, N_HEADS, HEAD_DIM).transpose(0, 2, 1, 3)
        attn = stick_breaking_attention(q, k, v).transpose(0, 2, 1, 3).reshape(B, S, D_ATTN)
        attn = attn * jax.nn.silu(g_attn)

        pooled = multiscale_causal_pool(u)
        pooled = jnp.einsum('bsgc,gcd->bsgd', pooled, w_pool[l]) + b_pool[l]
        pool_out = pooled.reshape(B, S, D_POOL) * pool_scale[l]
        pool_out = pool_out * jax.nn.silu(g_pool)

        y = jnp.concatenate([attn, pool_out], axis=-1) @ w_out[l]
        h = h + gate[:, None, :] * y
    return h
```

```python
import functools
import math

import jax
import jax.numpy as jnp
from jax import lax
from jax.experimental import pallas as pl
from jax.experimental.pallas import tpu as pltpu

F32 = jnp.float32
BF16 = jnp.bfloat16

HEAD_DIM = 64
POOL_WINDOWS = (2, 4, 8, 16)
EPS = 1e-6
LOG2E = 1.4426950408889634

LANES = 128
ATTN_BLOCK = 128
HEADS_PER_STEP = LANES // HEAD_DIM
HALO = 16
VMEM_LIMIT_BYTES = 48 * 1024 * 1024


def _silu(v):
    return v * jax.nn.sigmoid(v)


def _ada_kernel(c_ref, w_ref, b_ref, o_ref):
    c = c_ref[...]
    ca = _silu(c).astype(BF16)
    o_ref[...] = jnp.dot(ca, w_ref[...].astype(BF16), preferred_element_type=F32) + b_ref[...]


def _ada_call(c, w_ada, b_ada, *, tn=512):
    bsz, d = c.shape
    n = w_ada.shape[1]
    return pl.pallas_call(
        _ada_kernel,
        grid=(n // tn,),
        in_specs=[
            pl.BlockSpec((bsz, d), lambda j: (0, 0)),
            pl.BlockSpec((d, tn), lambda j: (0, j)),
            pl.BlockSpec((1, tn), lambda j: (0, j)),
        ],
        out_specs=pl.BlockSpec((bsz, tn), lambda j: (0, j)),
        out_shape=jax.ShapeDtypeStruct((bsz, n), F32),
        compiler_params=pltpu.CompilerParams(
            dimension_semantics=("arbitrary",), vmem_limit_bytes=VMEM_LIMIT_BYTES),
        name="adaln_mod",
    )(c, w_ada, b_ada.reshape(1, n))


def _proj_kernel(mod_ref, x_ref, ng_ref, win_ref, wvt_ref, qg_ref, kg_ref, wp_ref, bp_ref, ps_ref,
                 q_ref, k_ref, vt_ref, sg_ref, po_ref, ubuf, *, ts, d_model, d_attn, d_pool):
    t = pl.program_id(1)
    x = x_ref[0]
    ms = jnp.mean(x * x, axis=-1, keepdims=True)
    shift = mod_ref[0, :, 0:d_model]
    scale = mod_ref[0, :, d_model:2 * d_model]
    a = ng_ref[...] * (1.0 + scale)
    hn = ((x * lax.rsqrt(ms + EPS)) * a + shift).astype(BF16)

    def proj(col, width):
        return jnp.dot(hn, win_ref[:, col:col + width], preferred_element_type=F32)

    r = lax.broadcasted_iota(jnp.int32, (d_attn, d_attn), 0) // HEAD_DIM
    c = lax.broadcasted_iota(jnp.int32, (d_attn, d_attn), 1) // HEAD_DIM
    head_avg = jnp.where(r == c, 1.0 / HEAD_DIM, 0.0).astype(BF16)

    def head_norm(v, g):
        msq = jnp.dot((v * v).astype(BF16), head_avg, preferred_element_type=F32)
        return (v * lax.rsqrt(msq + EPS)) * g

    q = proj(0, d_attn)
    q_ref[0] = head_norm(q, qg_ref[...]).astype(BF16)
    k = proj(d_attn, d_attn)
    k_ref[0] = head_norm(k, kg_ref[...]).astype(BF16)

    vt = lax.dot_general(wvt_ref[...], hn, (((1,), (1,)), ((), ())),
                         preferred_element_type=F32).astype(BF16)
    for jj in range(ts // ATTN_BLOCK):
        vt_ref[0, jj] = vt[:, jj * ATTN_BLOCK:(jj + 1) * ATTN_BLOCK]

    g_attn = proj(3 * d_attn, d_attn)
    sg_ref[0] = _silu(g_attn).astype(BF16)

    u = proj(4 * d_attn, d_pool)
    g_pool = proj(4 * d_attn + d_pool, d_pool)

    @pl.when(t == 0)
    def _():
        ubuf[0:HALO, :] = jnp.zeros((HALO, d_pool), F32)

    @pl.when(t > 0)
    def _():
        ubuf[0:HALO, :] = ubuf[ts:ts + HALO, :]

    ubuf[HALO:HALO + ts, :] = u
    gdim = d_pool // len(POOL_WINDOWS)
    pos = t * ts + lax.broadcasted_iota(jnp.int32, (ts, gdim), 0)
    for g, win in enumerate(POOL_WINDOWS):
        lo, hi = g * gdim, (g + 1) * gdim
        wsum = u[:, lo:hi]
        for dlt in range(1, win):
            wsum = wsum + ubuf[HALO - dlt:HALO - dlt + ts, lo:hi]
        cnt = jnp.minimum(pos + 1, win).astype(F32)
        pooled = wsum / cnt - u[:, lo:hi]
        mixed = jnp.dot(pooled.astype(BF16), wp_ref[g], preferred_element_type=F32) + bp_ref[:, lo:hi]
        po_ref[0, :, lo:hi] = (mixed * ps_ref[:, lo:hi] * _silu(g_pool[:, lo:hi])).astype(BF16)


def _proj_call(mod3, x, norm_g, w_in_bf, wv_t, qg, kg, wp_bf, b_pool, pool_scale, *, ts=512):
    bsz, s, d = x.shape
    d_attn = qg.shape[1]
    d_pool = pool_scale.shape[1]
    nt = s // ts
    nb = s // ATTN_BLOCK
    kern = functools.partial(_proj_kernel, ts=ts, d_model=d, d_attn=d_attn, d_pool=d_pool)
    const = lambda *shape: pl.BlockSpec(shape, lambda b, t: (0,) * len(shape))
    seq_out = lambda width: pl.BlockSpec((1, ts, width), lambda b, t: (b, t, 0))
    return pl.pallas_call(
        kern,
        grid=(bsz, nt),
        in_specs=[
            pl.BlockSpec((1, 1, mod3.shape[2]), lambda b, t: (b, 0, 0)),
            pl.BlockSpec((1, ts, d), lambda b, t: (b, t, 0)),
            const(1, d),
            const(*w_in_bf.shape),
            const(*wv_t.shape),
            const(1, d_attn),
            const(1, d_attn),
            const(*wp_bf.shape),
            const(1, d_pool),
            const(1, d_pool),
        ],
        out_specs=[
            seq_out(d_attn),
            seq_out(d_attn),
            pl.BlockSpec((1, ts // ATTN_BLOCK, d_attn, ATTN_BLOCK), lambda b, t: (b, t, 0, 0)),
            seq_out(d_attn),
            seq_out(d_pool),
        ],
        out_shape=[
            jax.ShapeDtypeStruct((bsz, s, d_attn), BF16),
            jax.ShapeDtypeStruct((bsz, s, d_attn), BF16),
            jax.ShapeDtypeStruct((bsz, nb, d_attn, ATTN_BLOCK), BF16),
            jax.ShapeDtypeStruct((bsz, s, d_attn), BF16),
            jax.ShapeDtypeStruct((bsz, s, d_pool), BF16),
        ],
        scratch_shapes=[pltpu.VMEM((HALO + ts, d_pool), F32)],
        compiler_params=pltpu.CompilerParams(
            dimension_semantics=("arbitrary", "arbitrary"), vmem_limit_bytes=VMEM_LIMIT_BYTES),
        name="norm_inproj_pool",
    )(mod3, x, norm_g, w_in_bf, wv_t, qg, kg, wp_bf, b_pool, pool_scale)


def _attn_kernel(q_ref, k_ref, vt_ref, sg_ref, o_ref, *, nblk):
    tb = ATTN_BLOCK
    row = lax.broadcasted_iota(jnp.int32, (tb, tb), 0)
    col = lax.broadcasted_iota(jnp.int32, (tb, tb), 1)
    later_keys = (col > row).astype(BF16)
    head0_d = (col < HEAD_DIM).astype(BF16)
    head1_d = (col >= HEAD_DIM).astype(BF16)
    head0_rows = row < HEAD_DIM
    causal = row < col
    causal2 = jnp.concatenate([causal, causal], axis=1)

    def pair(j, qcat, log_rem, acc, diag):
        kb = k_ref[0, pl.ds(pl.multiple_of(j * tb, tb), tb), :]
        z = lax.dot_general(kb, qcat, (((1,), (1,)), ((), ())), preferred_element_type=F32)
        e = jnp.exp2(-jnp.abs(z))
        sp = jnp.maximum(z, 0.0) + jnp.log2(1.0 + e)
        lsp = z - sp
        if diag:
            sp = jnp.where(causal2, sp, 0.0)
        later = jnp.dot(later_keys, sp.astype(BF16), preferred_element_type=F32)
        w = jnp.exp2(lsp - later + log_rem)
        if diag:
            w = jnp.where(causal2, w, 0.0)
        res = jnp.dot(vt_ref[0, j], w.astype(BF16), preferred_element_type=F32)
        acc = acc + jnp.where(head0_rows, res[:, :tb], res[:, tb:])
        log_rem = log_rem - (later[0:1, :] + sp[0:1, :])
        return log_rem, acc

    def q_block(i, carry):
        rows = pl.ds(pl.multiple_of(i * tb, tb), tb)
        q = q_ref[0, rows, :]
        qcat = jnp.concatenate([q * head0_d, q * head1_d], axis=0)
        state = (jnp.zeros((1, 2 * tb), F32), jnp.zeros((tb, tb), F32))
        state = pair(i, qcat, *state, True)
        state = lax.fori_loop(0, i, lambda jj, st: pair(i - 1 - jj, qcat, *st, False), state)
        out = state[1].T * sg_ref[0, rows, :].astype(F32)
        o_ref[0, rows, :] = out.astype(BF16)
        return carry

    lax.fori_loop(0, nblk, q_block, 0)


def _attn_call(qn, kn, vt, sg):
    bsz, s, d_attn = qn.shape
    nblk = s // ATTN_BLOCK
    ngroups = d_attn // LANES
    seq = pl.BlockSpec((1, s, LANES), lambda b, h: (b, 0, h))
    return pl.pallas_call(
        functools.partial(_attn_kernel, nblk=nblk),
        grid=(bsz, ngroups),
        in_specs=[seq, seq, pl.BlockSpec((1, nblk, LANES, ATTN_BLOCK), lambda b, h: (b, 0, h, 0)), seq],
        out_specs=seq,
        out_shape=jax.ShapeDtypeStruct((bsz, s, d_attn), BF16),
        compiler_params=pltpu.CompilerParams(
            dimension_semantics=("arbitrary", "arbitrary"), vmem_limit_bytes=VMEM_LIMIT_BYTES),
        name="stickbreak_attn",
    )(qn, kn, vt, sg)


def _out_kernel(a_ref, p_ref, x_ref, mod_ref, w_ref, o_ref, *, d_model, d_attn):
    y = jnp.dot(a_ref[0], w_ref[0:d_attn, :], preferred_element_type=F32)
    y = y + jnp.dot(p_ref[0], w_ref[d_attn:, :], preferred_element_type=F32)
    gate = mod_ref[0, :, 2 * d_model:3 * d_model]
    o_ref[0] = x_ref[0] + gate * y


def _out_call(attn, pool_out, x, mod3, w_out_bf, *, ts=512):
    bsz, s, d = x.shape
    d_attn = attn.shape[2]
    d_pool = pool_out.shape[2]
    return pl.pallas_call(
        functools.partial(_out_kernel, d_model=d, d_attn=d_attn),
        grid=(bsz, s // ts),
        in_specs=[
            pl.BlockSpec((1, ts, d_attn), lambda b, t: (b, t, 0)),
            pl.BlockSpec((1, ts, d_pool), lambda b, t: (b, t, 0)),
            pl.BlockSpec((1, ts, d), lambda b, t: (b, t, 0)),
            pl.BlockSpec((1, 1, mod3.shape[2]), lambda b, t: (b, 0, 0)),
            pl.BlockSpec(w_out_bf.shape, lambda b, t: (0, 0)),
        ],
        out_specs=pl.BlockSpec((1, ts, d), lambda b, t: (b, t, 0)),
        out_shape=jax.ShapeDtypeStruct((bsz, s, d), F32),
        compiler_params=pltpu.CompilerParams(
            dimension_semantics=("arbitrary", "arbitrary"), vmem_limit_bytes=VMEM_LIMIT_BYTES),
        name="outproj_residual",
    )(attn, pool_out, x, mod3, w_out_bf)


def kernel(x, c, w_ada, b_ada, norm_g, w_in, q_norm_g, k_norm_g, w_pool, b_pool, pool_scale, w_out):
    depth = w_ada.shape[0]
    d_attn = w_out.shape[1] // 2
    n_heads = d_attn // HEAD_DIM
    h = x
    for l in range(depth):
        mod = _ada_call(c, w_ada[l], b_ada[l])
        mod3 = mod.reshape(mod.shape[0], 1, mod.shape[1])
        w_in_bf = w_in[l].astype(BF16)
        wv_t = w_in[l][:, 2 * d_attn:3 * d_attn].T.astype(BF16)
        qg = jnp.tile(q_norm_g[l] * (HEAD_DIM ** -0.5 * LOG2E), n_heads).reshape(1, d_attn)
        kg = jnp.tile(k_norm_g[l], n_heads).reshape(1, d_attn)
        qn, kn, vt, sg, pool_out = _proj_call(
            mod3, h, norm_g[l].reshape(1, -1), w_in_bf, wv_t, qg, kg,
            w_pool[l].astype(BF16), b_pool[l].reshape(1, -1), pool_scale[l].reshape(1, -1))
        attn = _attn_call(qn, kn, vt, sg)
        h = _out_call(attn, pool_out, h, mod3, w_out[l].astype(BF16))
    return h
```

```python
import functools
import math

import jax
import jax.numpy as jnp
from jax import lax
from jax.experimental import pallas as pl
from jax.experimental.pallas import tpu as pltpu

F32 = jnp.float32
BF16 = jnp.bfloat16

HEAD_DIM = 64
POOL_WINDOWS = (2, 4, 8, 16)
EPS = 1e-6
LOG2E = 1.4426950408889634

LANES = 128
ATTN_BLOCK = 128
HEADS_PER_STEP = LANES // HEAD_DIM
HALO = 16
VMEM_LIMIT_BYTES = 48 * 1024 * 1024


def _silu(v):
    return v * jax.nn.sigmoid(v)


def _ada_kernel(c_ref, w_ref, b_ref, o_ref):
    c = c_ref[...]
    ca = _silu(c).astype(BF16)
    o_ref[...] = jnp.dot(ca, w_ref[...].astype(BF16), preferred_element_type=F32) + b_ref[...]


def _ada_call(c, w_ada, b_ada, *, tn=512):
    bsz, d = c.shape
    n = w_ada.shape[1]
    return pl.pallas_call(
        _ada_kernel,
        grid=(n // tn,),
        in_specs=[
            pl.BlockSpec((bsz, d), lambda j: (0, 0)),
            pl.BlockSpec((d, tn), lambda j: (0, j)),
            pl.BlockSpec((1, tn), lambda j: (0, j)),
        ],
        out_specs=pl.BlockSpec((bsz, tn), lambda j: (0, j)),
        out_shape=jax.ShapeDtypeStruct((bsz, n), F32),
        compiler_params=pltpu.CompilerParams(
            dimension_semantics=("arbitrary",), vmem_limit_bytes=VMEM_LIMIT_BYTES),
        name="adaln_mod",
    )(c, w_ada, b_ada.reshape(1, n))


def _proj_kernel(mod_ref, x_ref, ng_ref, win_ref, wvt_ref, qg_ref, kg_ref, wp_ref, bp_ref, ps_ref,
                 q_ref, k_ref, vt_ref, sg_ref, po_ref, ubuf, *, ts, d_model, d_attn, d_pool):
    t = pl.program_id(1)
    x = x_ref[0]
    ms = jnp.mean(x * x, axis=-1, keepdims=True)
    shift = mod_ref[0, :, 0:d_model]
    scale = mod_ref[0, :, d_model:2 * d_model]
    a = ng_ref[...] * (1.0 + scale)
    hn = ((x * lax.rsqrt(ms + EPS)) * a + shift).astype(BF16)

    def proj(col, width):
        return jnp.dot(hn, win_ref[:, col:col + width], preferred_element_type=F32)

    r = lax.broadcasted_iota(jnp.int32, (d_attn, d_attn), 0) // HEAD_DIM
    c = lax.broadcasted_iota(jnp.int32, (d_attn, d_attn), 1) // HEAD_DIM
    head_avg = jnp.where(r == c, 1.0 / HEAD_DIM, 0.0).astype(BF16)

    def head_norm(v, g):
        msq = jnp.dot((v * v).astype(BF16), head_avg, preferred_element_type=F32)
        return (v * lax.rsqrt(msq + EPS)) * g

    q = proj(0, d_attn)
    q_ref[0] = head_norm(q, qg_ref[...]).astype(BF16)
    k = proj(d_attn, d_attn)
    k_ref[0] = head_norm(k, kg_ref[...]).astype(BF16)

    vt = lax.dot_general(wvt_ref[...], hn, (((1,), (1,)), ((), ())),
                         preferred_element_type=F32).astype(BF16)
    for jj in range(ts // ATTN_BLOCK):
        vt_ref[0, jj] = vt[:, jj * ATTN_BLOCK:(jj + 1) * ATTN_BLOCK]

    g_attn = proj(3 * d_attn, d_attn)
    sg_ref[0] = _silu(g_attn).astype(BF16)

    u = proj(4 * d_attn, d_pool)
    g_pool = proj(4 * d_attn + d_pool, d_pool)

    @pl.when(t == 0)
    def _():
        ubuf[0:HALO, :] = jnp.zeros((HALO, d_pool), F32)

    @pl.when(t > 0)
    def _():
        ubuf[0:HALO, :] = ubuf[ts:ts + HALO, :]

    ubuf[HALO:HALO + ts, :] = u
    gdim = d_pool // len(POOL_WINDOWS)
    pos = t * ts + lax.broadcasted_iota(jnp.int32, (ts, gdim), 0)
    for g, win in enumerate(POOL_WINDOWS):
        lo, hi = g * gdim, (g + 1) * gdim
        wsum = u[:, lo:hi]
        for dlt in range(1, win):
            wsum = wsum + ubuf[HALO - dlt:HALO - dlt + ts, lo:hi]
        cnt = jnp.minimum(pos + 1, win).astype(F32)
        pooled = wsum / cnt - u[:, lo:hi]
        mixed = jnp.dot(pooled.astype(BF16), wp_ref[g], preferred_element_type=F32) + bp_ref[:, lo:hi]
        po_ref[0, :, lo:hi] = (mixed * ps_ref[:, lo:hi] * _silu(g_pool[:, lo:hi])).astype(BF16)


def _proj_call(mod3, x, norm_g, w_in_bf, wv_t, qg, kg, wp_bf, b_pool, pool_scale, *, ts=512):
    bsz, s, d = x.shape
    d_attn = qg.shape[1]
    d_pool = pool_scale.shape[1]
    nt = s // ts
    nb = s // ATTN_BLOCK
    kern = functools.partial(_proj_kernel, ts=ts, d_model=d, d_attn=d_attn, d_pool=d_pool)
    const = lambda *shape: pl.BlockSpec(shape, lambda b, t: (0,) * len(shape))
    seq_out = lambda width: pl.BlockSpec((1, ts, width), lambda b, t: (b, t, 0))
    return pl.pallas_call(
        kern,
        grid=(bsz, nt),
        in_specs=[
            pl.BlockSpec((1, 1, mod3.shape[2]), lambda b, t: (b, 0, 0)),
            pl.BlockSpec((1, ts, d), lambda b, t: (b, t, 0)),
            const(1, d),
            const(*w_in_bf.shape),
            const(*wv_t.shape),
            const(1, d_attn),
            const(1, d_attn),
            const(*wp_bf.shape),
            const(1, d_pool),
            const(1, d_pool),
        ],
        out_specs=[
            seq_out(d_attn),
            seq_out(d_attn),
            pl.BlockSpec((1, ts // ATTN_BLOCK, d_attn, ATTN_BLOCK), lambda b, t: (b, t, 0, 0)),
            seq_out(d_attn),
            seq_out(d_pool),
        ],
        out_shape=[
            jax.ShapeDtypeStruct((bsz, s, d_attn), BF16),
            jax.ShapeDtypeStruct((bsz, s, d_attn), BF16),
            jax.ShapeDtypeStruct((bsz, nb, d_attn, ATTN_BLOCK), BF16),
            jax.ShapeDtypeStruct((bsz, s, d_attn), BF16),
            jax.ShapeDtypeStruct((bsz, s, d_pool), BF16),
        ],
        scratch_shapes=[pltpu.VMEM((HALO + ts, d_pool), F32)],
        compiler_params=pltpu.CompilerParams(
            dimension_semantics=("arbitrary", "arbitrary"), vmem_limit_bytes=VMEM_LIMIT_BYTES),
        name="norm_inproj_pool",
    )(mod3, x, norm_g, w_in_bf, wv_t, qg, kg, wp_bf, b_pool, pool_scale)


def _attn_kernel(q_ref, k_ref, vt_ref, sg_ref, o_ref, lr_ref, acc_ref, *, nblk, g_q, n_hp):
    tb = ATTN_BLOCK
    row = lax.broadcasted_iota(jnp.int32, (tb, tb), 0)
    col = lax.broadcasted_iota(jnp.int32, (tb, tb), 1)
    later_keys = (col > row).astype(BF16)
    head0_d = (col < HEAD_DIM).astype(BF16)
    head1_d = (col >= HEAD_DIM).astype(BF16)
    head0_rows = row < HEAD_DIM
    causal = row < col
    causal2 = jnp.concatenate([causal, causal], axis=1)

    pw = 2 * tb
    gw = n_hp * pw

    def step(j, qcats, q_lo, diag):
        nq = g_q - q_lo
        rows_j = pl.ds(pl.multiple_of(j * tb, tb), tb)
        zs = []
        for hp in range(n_hp):
            kb = k_ref[0, rows_j, hp * LANES:(hp + 1) * LANES]
            zs.append(lax.dot_general(kb, qcats[hp][q_lo * pw:, :], (((1,), (1,)), ((), ())),
                                      preferred_element_type=F32))
        z = jnp.concatenate([zs[hp][:, g * pw:(g + 1) * pw] for g in range(nq) for hp in range(n_hp)],
                            axis=1)
        e = jnp.exp2(-jnp.abs(z))
        sp = jnp.maximum(z, 0.0) + jnp.log2(1.0 + e)
        lsp = z - sp

        def mask_first(v):
            if not diag:
                return v
            m = jnp.concatenate([causal] * (gw // tb), axis=1)
            first = jnp.where(m, v[:, :gw], 0.0)
            return first if nq == 1 else jnp.concatenate([first, v[:, gw:]], axis=1)

        sp = mask_first(sp)
        later = jnp.dot(later_keys, sp.astype(BF16), preferred_element_type=F32)
        lanes = slice(q_lo * gw, g_q * gw)
        log_rem = lr_ref[:, lanes]
        w = mask_first(jnp.exp2(lsp - later + log_rem)).astype(BF16)
        lr_ref[:, lanes] = log_rem - (later[0:1, :] + sp[0:1, :])
        for hp in range(n_hp):
            w_hp = jnp.concatenate([w[:, (g * n_hp + hp) * pw:(g * n_hp + hp + 1) * pw] for g in range(nq)],
                                   axis=1)
            res = jnp.dot(vt_ref[0, j, hp * LANES:(hp + 1) * LANES, :], w_hp,
                          preferred_element_type=F32)
            for g in range(nq):
                a = ((q_lo + g) * n_hp + hp) * tb
                acc_ref[:, a:a + tb] += jnp.where(head0_rows, res[:, g * pw:g * pw + tb],
                                                  res[:, g * pw + tb:(g + 1) * pw])

    def q_super_block(sb, carry):
        qcats = []
        for hp in range(n_hp):
            parts = []
            for g in range(g_q):
                rows = pl.ds(pl.multiple_of((sb * g_q + g) * tb, tb), tb)
                q = q_ref[0, rows, hp * LANES:(hp + 1) * LANES]
                parts += [q * head0_d, q * head1_d]
            qcats.append(jnp.concatenate(parts, axis=0))
        lr_ref[...] = jnp.zeros_like(lr_ref)
        acc_ref[...] = jnp.zeros_like(acc_ref)
        for m in reversed(range(g_q)):
            step(sb * g_q + m, qcats, m, True)

        def full_step(jj, c):
            step(sb * g_q - 1 - jj, qcats, 0, False)
            return c

        lax.fori_loop(0, sb * g_q, full_step, 0)
        for g in range(g_q):
            rows = pl.ds(pl.multiple_of((sb * g_q + g) * tb, tb), tb)
            for hp in range(n_hp):
                a = (g * n_hp + hp) * tb
                gate = sg_ref[0, rows, hp * LANES:(hp + 1) * LANES].astype(F32)
                o_ref[0, rows, hp * LANES:(hp + 1) * LANES] = (acc_ref[:, a:a + tb].T * gate).astype(BF16)
        return carry

    lax.fori_loop(0, nblk // g_q, q_super_block, 0)


def _attn_call(qn, kn, vt, sg, *, g_q=4, n_hp=4):
    bsz, s, d_attn = qn.shape
    nblk = s // ATTN_BLOCK
    width = n_hp * LANES
    seq = pl.BlockSpec((1, s, width), lambda b, h: (b, 0, h))
    return pl.pallas_call(
        functools.partial(_attn_kernel, nblk=nblk, g_q=g_q, n_hp=n_hp),
        grid=(bsz, d_attn // width),
        in_specs=[seq, seq, pl.BlockSpec((1, nblk, width, ATTN_BLOCK), lambda b, h: (b, 0, h, 0)), seq],
        out_specs=seq,
        out_shape=jax.ShapeDtypeStruct((bsz, s, d_attn), BF16),
        scratch_shapes=[pltpu.VMEM((1, g_q * n_hp * 2 * ATTN_BLOCK), F32),
                        pltpu.VMEM((LANES, g_q * n_hp * ATTN_BLOCK), F32)],
        compiler_params=pltpu.CompilerParams(
            dimension_semantics=("arbitrary", "arbitrary"), vmem_limit_bytes=VMEM_LIMIT_BYTES),
        name="stickbreak_attn",
    )(qn, kn, vt, sg)


def _out_kernel(a_ref, p_ref, x_ref, mod_ref, w_ref, o_ref, *, d_model, d_attn):
    y = jnp.dot(a_ref[0], w_ref[0:d_attn, :], preferred_element_type=F32)
    y = y + jnp.dot(p_ref[0], w_ref[d_attn:, :], preferred_element_type=F32)
    gate = mod_ref[0, :, 2 * d_model:3 * d_model]
    o_ref[0] = x_ref[0] + gate * y


def _out_call(attn, pool_out, x, mod3, w_out_bf, *, ts=512):
    bsz, s, d = x.shape
    d_attn = attn.shape[2]
    d_pool = pool_out.shape[2]
    return pl.pallas_call(
        functools.partial(_out_kernel, d_model=d, d_attn=d_attn),
        grid=(bsz, s // ts),
        in_specs=[
            pl.BlockSpec((1, ts, d_attn), lambda b, t: (b, t, 0)),
            pl.BlockSpec((1, ts, d_pool), lambda b, t: (b, t, 0)),
            pl.BlockSpec((1, ts, d), lambda b, t: (b, t, 0)),
            pl.BlockSpec((1, 1, mod3.shape[2]), lambda b, t: (b, 0, 0)),
            pl.BlockSpec(w_out_bf.shape, lambda b, t: (0, 0)),
        ],
        out_specs=pl.BlockSpec((1, ts, d), lambda b, t: (b, t, 0)),
        out_shape=jax.ShapeDtypeStruct((bsz, s, d), F32),
        compiler_params=pltpu.CompilerParams(
            dimension_semantics=("arbitrary", "arbitrary"), vmem_limit_bytes=VMEM_LIMIT_BYTES),
        name="outproj_residual",
    )(attn, pool_out, x, mod3, w_out_bf)


def kernel(x, c, w_ada, b_ada, norm_g, w_in, q_norm_g, k_norm_g, w_pool, b_pool, pool_scale, w_out):
    depth = w_ada.shape[0]
    d_attn = w_out.shape[1] // 2
    n_heads = d_attn // HEAD_DIM
    h = x
    for l in range(depth):
        mod = _ada_call(c, w_ada[l], b_ada[l])
        mod3 = mod.reshape(mod.shape[0], 1, mod.shape[1])
        w_in_bf = w_in[l].astype(BF16)
        wv_t = w_in[l][:, 2 * d_attn:3 * d_attn].T.astype(BF16)
        qg = jnp.tile(q_norm_g[l] * (HEAD_DIM ** -0.5 * LOG2E), n_heads).reshape(1, d_attn)
        kg = jnp.tile(k_norm_g[l], n_heads).reshape(1, d_attn)
        qn, kn, vt, sg, pool_out = _proj_call(
            mod3, h, norm_g[l].reshape(1, -1), w_in_bf, wv_t, qg, kg,
            w_pool[l].astype(BF16), b_pool[l].reshape(1, -1), pool_scale[l].reshape(1, -1))
        attn = _attn_call(qn, kn, vt, sg)
        h = _out_call(attn, pool_out, h, mod3, w_out[l].astype(BF16))
    return h
```

```python
import functools
import math

import jax
import jax.numpy as jnp
from jax import lax
from jax.experimental import pallas as pl
from jax.experimental.pallas import tpu as pltpu

F32 = jnp.float32
BF16 = jnp.bfloat16

HEAD_DIM = 64
POOL_WINDOWS = (2, 4, 8, 16)
EPS = 1e-6
LOG2E = 1.4426950408889634

LANES = 128
ATTN_BLOCK = 128
HEADS_PER_STEP = LANES // HEAD_DIM
HALO = 16
VMEM_LIMIT_BYTES = 48 * 1024 * 1024


def _silu(v):
    return v * jax.nn.sigmoid(v)


def _ada_kernel(c_ref, w_ref, b_ref, o_ref):
    c = c_ref[...]
    ca = _silu(c).astype(BF16)
    o_ref[...] = jnp.dot(ca, w_ref[...].astype(BF16), preferred_element_type=F32) + b_ref[...]


def _ada_call(c, w_ada, b_ada, *, tn=512):
    bsz, d = c.shape
    n = w_ada.shape[1]
    return pl.pallas_call(
        _ada_kernel,
        grid=(n // tn,),
        in_specs=[
            pl.BlockSpec((bsz, d), lambda j: (0, 0)),
            pl.BlockSpec((d, tn), lambda j: (0, j)),
            pl.BlockSpec((1, tn), lambda j: (0, j)),
        ],
        out_specs=pl.BlockSpec((bsz, tn), lambda j: (0, j)),
        out_shape=jax.ShapeDtypeStruct((bsz, n), F32),
        compiler_params=pltpu.CompilerParams(
            dimension_semantics=("arbitrary",), vmem_limit_bytes=VMEM_LIMIT_BYTES),
        name="adaln_mod",
    )(c, w_ada, b_ada.reshape(1, n))


def _proj_kernel(mod_ref, x_ref, ng_ref, win_ref, wvt_ref, qg_ref, kg_ref, wp_ref, bp_ref, ps_ref,
                 q_ref, k_ref, vt_ref, sg_ref, po_ref, ubuf, *, ts, d_model, d_attn, d_pool):
    t = pl.program_id(1)
    x = x_ref[0]
    ms = jnp.mean(x * x, axis=-1, keepdims=True)
    shift = mod_ref[0, :, 0:d_model]
    scale = mod_ref[0, :, d_model:2 * d_model]
    a = ng_ref[...] * (1.0 + scale)
    hn = ((x * lax.rsqrt(ms + EPS)) * a + shift).astype(BF16)

    def proj(col, width):
        return jnp.dot(hn, win_ref[:, col:col + width], preferred_element_type=F32)

    r = lax.broadcasted_iota(jnp.int32, (d_attn, d_attn), 0) // HEAD_DIM
    c = lax.broadcasted_iota(jnp.int32, (d_attn, d_attn), 1) // HEAD_DIM
    head_avg = jnp.where(r == c, 1.0 / HEAD_DIM, 0.0).astype(BF16)

    def head_norm(v, g):
        msq = jnp.dot((v * v).astype(BF16), head_avg, preferred_element_type=F32)
        return (v * lax.rsqrt(msq + EPS)) * g

    q = proj(0, d_attn)
    q_ref[0] = head_norm(q, qg_ref[...]).astype(BF16)
    k = proj(d_attn, d_attn)
    k_ref[0] = head_norm(k, kg_ref[...]).astype(BF16)

    vt = lax.dot_general(wvt_ref[...], hn, (((1,), (1,)), ((), ())),
                         preferred_element_type=F32).astype(BF16)
    for jj in range(ts // ATTN_BLOCK):
        vt_ref[0, jj] = vt[:, jj * ATTN_BLOCK:(jj + 1) * ATTN_BLOCK]

    g_attn = proj(3 * d_attn, d_attn)
    sg_ref[0] = _silu(g_attn).astype(BF16)

    u = proj(4 * d_attn, d_pool)
    g_pool = proj(4 * d_attn + d_pool, d_pool)

    @pl.when(t == 0)
    def _():
        ubuf[0:HALO, :] = jnp.zeros((HALO, d_pool), F32)

    @pl.when(t > 0)
    def _():
        ubuf[0:HALO, :] = ubuf[ts:ts + HALO, :]

    ubuf[HALO:HALO + ts, :] = u
    gdim = d_pool // len(POOL_WINDOWS)
    pos = t * ts + lax.broadcasted_iota(jnp.int32, (ts, gdim), 0)
    for g, win in enumerate(POOL_WINDOWS):
        lo, hi = g * gdim, (g + 1) * gdim
        wsum = u[:, lo:hi]
        for dlt in range(1, win):
            wsum = wsum + ubuf[HALO - dlt:HALO - dlt + ts, lo:hi]
        cnt = jnp.minimum(pos + 1, win).astype(F32)
        pooled = wsum / cnt - u[:, lo:hi]
        mixed = jnp.dot(pooled.astype(BF16), wp_ref[g], preferred_element_type=F32) + bp_ref[:, lo:hi]
        po_ref[0, :, lo:hi] = (mixed * ps_ref[:, lo:hi] * _silu(g_pool[:, lo:hi])).astype(BF16)


def _proj_call(mod3, x, norm_g, w_in_bf, wv_t, qg, kg, wp_bf, b_pool, pool_scale, *, ts=512):
    bsz, s, d = x.shape
    d_attn = qg.shape[1]
    d_pool = pool_scale.shape[1]
    nt = s // ts
    nb = s // ATTN_BLOCK
    kern = functools.partial(_proj_kernel, ts=ts, d_model=d, d_attn=d_attn, d_pool=d_pool)
    const = lambda *shape: pl.BlockSpec(shape, lambda b, t: (0,) * len(shape))
    seq_out = lambda width: pl.BlockSpec((1, ts, width), lambda b, t: (b, t, 0))
    return pl.pallas_call(
        kern,
        grid=(bsz, nt),
        in_specs=[
            pl.BlockSpec((1, 1, mod3.shape[2]), lambda b, t: (b, 0, 0)),
            pl.BlockSpec((1, ts, d), lambda b, t: (b, t, 0)),
            const(1, d),
            const(*w_in_bf.shape),
            const(*wv_t.shape),
            const(1, d_attn),
            const(1, d_attn),
            const(*wp_bf.shape),
            const(1, d_pool),
            const(1, d_pool),
        ],
        out_specs=[
            seq_out(d_attn),
            seq_out(d_attn),
            pl.BlockSpec((1, ts // ATTN_BLOCK, d_attn, ATTN_BLOCK), lambda b, t: (b, t, 0, 0)),
            seq_out(d_attn),
            seq_out(d_pool),
        ],
        out_shape=[
            jax.ShapeDtypeStruct((bsz, s, d_attn), BF16),
            jax.ShapeDtypeStruct((bsz, s, d_attn), BF16),
            jax.ShapeDtypeStruct((bsz, nb, d_attn, ATTN_BLOCK), BF16),
            jax.ShapeDtypeStruct((bsz, s, d_attn), BF16),
            jax.ShapeDtypeStruct((bsz, s, d_pool), BF16),
        ],
        scratch_shapes=[pltpu.VMEM((HALO + ts, d_pool), F32)],
        compiler_params=pltpu.CompilerParams(
            dimension_semantics=("arbitrary", "arbitrary"), vmem_limit_bytes=VMEM_LIMIT_BYTES),
        name="norm_inproj_pool",
    )(mod3, x, norm_g, w_in_bf, wv_t, qg, kg, wp_bf, b_pool, pool_scale)


def _attn_kernel(q_ref, k_ref, vt_ref, sg_ref, o_ref, lr_ref, acc_ref, *, nblk, g_q, n_hp, k_unroll):
    tb = ATTN_BLOCK
    row = lax.broadcasted_iota(jnp.int32, (tb, tb), 0)
    col = lax.broadcasted_iota(jnp.int32, (tb, tb), 1)
    later_keys = (col >= row).astype(BF16)
    head_d = [(col < HEAD_DIM).astype(BF16), (col >= HEAD_DIM).astype(BF16)]
    causal = row < col
    sign_bit = jnp.int32(-2 ** 31)
    n_grp = 2 * n_hp
    grp_w = g_q * tb

    def step(j, qcats, q_lo, diag):
        nq = g_q - q_lo
        rows_j = pl.ds(pl.multiple_of(j * tb, tb), tb)
        lanes = [slice(grp * grp_w + q_lo * tb, (grp + 1) * grp_w) for grp in range(n_grp)]
        zs = []
        for hp in range(n_hp):
            kb = k_ref[0, rows_j, hp * LANES:(hp + 1) * LANES]
            qc = qcats[hp]
            if q_lo:
                qc = jnp.concatenate([qc[h * grp_w + q_lo * tb:(h + 1) * grp_w] for h in range(2)], axis=0)
            zs.append(lax.dot_general(kb, qc, (((1,), (1,)), ((), ())), preferred_element_type=F32))
        z = jnp.concatenate(zs, axis=1)
        neg_abs = lax.bitcast_convert_type(lax.bitcast_convert_type(z, jnp.int32) | sign_bit, F32)
        sp = jnp.maximum(z, 0.0) + jnp.log(1.0 + jnp.exp2(neg_abs)) * LOG2E

        def mask_diag(v):
            if not diag:
                return v
            parts = []
            for grp in range(n_grp):
                base = grp * nq * tb
                parts.append(jnp.where(causal, v[:, base:base + tb], 0.0))
                if nq > 1:
                    parts.append(v[:, base + tb:base + nq * tb])
            return jnp.concatenate(parts, axis=1)

        sp = mask_diag(sp)
        later = jnp.dot(later_keys, sp.astype(BF16), preferred_element_type=F32)
        log_rem = jnp.concatenate([lr_ref[:, ln] for ln in lanes], axis=1)
        w = mask_diag(jnp.exp2(z - later + log_rem)).astype(BF16)
        new_rem = log_rem - later[0:1, :]
        for grp in range(n_grp):
            hp, h = divmod(grp, 2)
            cols = slice(grp * nq * tb, (grp + 1) * nq * tb)
            lr_ref[:, lanes[grp]] = new_rem[:, cols]
            vt_h = vt_ref[0, j, hp * LANES + h * HEAD_DIM:hp * LANES + (h + 1) * HEAD_DIM, :]
            res = jnp.dot(vt_h, w[:, cols], preferred_element_type=F32)
            acc_ref[h * HEAD_DIM:(h + 1) * HEAD_DIM, hp * grp_w + q_lo * tb:(hp + 1) * grp_w] += res

    def q_super_block(sb, carry):
        qcats = []
        for hp in range(n_hp):
            rows = pl.ds(pl.multiple_of(sb * grp_w, grp_w), grp_w)
            q = q_ref[0, rows, hp * LANES:(hp + 1) * LANES]
            qcats.append(jnp.concatenate([q * jnp.concatenate([head_d[h]] * g_q, axis=0) for h in range(2)],
                                         axis=0))
        lr_ref[...] = jnp.zeros_like(lr_ref)
        acc_ref[...] = jnp.zeros_like(acc_ref)
        for m in reversed(range(g_q)):
            step(sb * g_q + m, qcats, m, True)

        def full_step(jj, c):
            for u in range(k_unroll):
                step(sb * g_q - 1 - (jj * k_unroll + u), qcats, 0, False)
            return c

        lax.fori_loop(0, sb * (g_q // k_unroll), full_step, 0)
        for g in range(g_q):
            rows = pl.ds(pl.multiple_of((sb * g_q + g) * tb, tb), tb)
            for hp in range(n_hp):
                a = hp * grp_w + g * tb
                gate = sg_ref[0, rows, hp * LANES:(hp + 1) * LANES].astype(F32)
                o_ref[0, rows, hp * LANES:(hp + 1) * LANES] = (acc_ref[:, a:a + tb].T * gate).astype(BF16)
        return carry

    lax.fori_loop(0, nblk // g_q, q_super_block, 0)


def _attn_call(qn, kn, vt, sg, *, g_q=4, n_hp=4, k_unroll=2):
    bsz, s, d_attn = qn.shape
    nblk = s // ATTN_BLOCK
    width = n_hp * LANES
    seq = pl.BlockSpec((1, s, width), lambda b, h: (b, 0, h))
    return pl.pallas_call(
        functools.partial(_attn_kernel, nblk=nblk, g_q=g_q, n_hp=n_hp, k_unroll=k_unroll),
        grid=(bsz, d_attn // width),
        in_specs=[seq, seq, pl.BlockSpec((1, nblk, width, ATTN_BLOCK), lambda b, h: (b, 0, h, 0)), seq],
        out_specs=seq,
        out_shape=jax.ShapeDtypeStruct((bsz, s, d_attn), BF16),
        scratch_shapes=[pltpu.VMEM((1, g_q * n_hp * 2 * ATTN_BLOCK), F32),
                        pltpu.VMEM((LANES, g_q * n_hp * ATTN_BLOCK), F32)],
        compiler_params=pltpu.CompilerParams(
            dimension_semantics=("arbitrary", "arbitrary"), vmem_limit_bytes=VMEM_LIMIT_BYTES),
        name="stickbreak_attn",
    )(qn, kn, vt, sg)


def _out_kernel(a_ref, p_ref, x_ref, mod_ref, w_ref, o_ref, *, d_model, d_attn):
    y = jnp.dot(a_ref[0], w_ref[0:d_attn, :], preferred_element_type=F32)
    y = y + jnp.dot(p_ref[0], w_ref[d_attn:, :], preferred_element_type=F32)
    gate = mod_ref[0, :, 2 * d_model:3 * d_model]
    o_ref[0] = x_ref[0] + gate * y


def _out_call(attn, pool_out, x, mod3, w_out_bf, *, ts=512):
    bsz, s, d = x.shape
    d_attn = attn.shape[2]
    d_pool = pool_out.shape[2]
    return pl.pallas_call(
        functools.partial(_out_kernel, d_model=d, d_attn=d_attn),
        grid=(bsz, s // ts),
        in_specs=[
            pl.BlockSpec((1, ts, d_attn), lambda b, t: (b, t, 0)),
            pl.BlockSpec((1, ts, d_pool), lambda b, t: (b, t, 0)),
            pl.BlockSpec((1, ts, d), lambda b, t: (b, t, 0)),
            pl.BlockSpec((1, 1, mod3.shape[2]), lambda b, t: (b, 0, 0)),
            pl.BlockSpec(w_out_bf.shape, lambda b, t: (0, 0)),
        ],
        out_specs=pl.BlockSpec((1, ts, d), lambda b, t: (b, t, 0)),
        out_shape=jax.ShapeDtypeStruct((bsz, s, d), F32),
        compiler_params=pltpu.CompilerParams(
            dimension_semantics=("arbitrary", "arbitrary"), vmem_limit_bytes=VMEM_LIMIT_BYTES),
        name="outproj_residual",
    )(attn, pool_out, x, mod3, w_out_bf)


def kernel(x, c, w_ada, b_ada, norm_g, w_in, q_norm_g, k_norm_g, w_pool, b_pool, pool_scale, w_out):
    depth = w_ada.shape[0]
    d_attn = w_out.shape[1] // 2
    n_heads = d_attn // HEAD_DIM
    h = x
    for l in range(depth):
        mod = _ada_call(c, w_ada[l], b_ada[l])
        mod3 = mod.reshape(mod.shape[0], 1, mod.shape[1])
        w_in_bf = w_in[l].astype(BF16)
        wv_t = w_in[l][:, 2 * d_attn:3 * d_attn].T.astype(BF16)
        qg = jnp.tile(q_norm_g[l] * (HEAD_DIM ** -0.5 * LOG2E), n_heads).reshape(1, d_attn)
        kg = jnp.tile(k_norm_g[l], n_heads).reshape(1, d_attn)
        qn, kn, vt, sg, pool_out = _proj_call(
            mod3, h, norm_g[l].reshape(1, -1), w_in_bf, wv_t, qg, kg,
            w_pool[l].astype(BF16), b_pool[l].reshape(1, -1), pool_scale[l].reshape(1, -1))
        attn = _attn_call(qn, kn, vt, sg)
        h = _out_call(attn, pool_out, h, mod3, w_out[l].astype(BF16))
    return h
```

```python
import functools
import math

import jax
import jax.numpy as jnp
from jax import lax
from jax.experimental import pallas as pl
from jax.experimental.pallas import tpu as pltpu

F32 = jnp.float32
BF16 = jnp.bfloat16

HEAD_DIM = 64
POOL_WINDOWS = (2, 4, 8, 16)
EPS = 1e-6
LOG2E = 1.4426950408889634
SOFTPLUS_CLAMP = 64.0

LANES = 128
ATTN_BLOCK = 128
HEADS_PER_STEP = LANES // HEAD_DIM
HALO = 16
VMEM_LIMIT_BYTES = 48 * 1024 * 1024


def _silu(v):
    return v * jax.nn.sigmoid(v)


def _ada_kernel(c_ref, w_ref, b_ref, o_ref):
    c = c_ref[...]
    ca = _silu(c).astype(BF16)
    o_ref[...] = jnp.dot(ca, w_ref[...].astype(BF16), preferred_element_type=F32) + b_ref[...]


def _ada_call(c, w_ada, b_ada, *, tn=512):
    bsz, d = c.shape
    n = w_ada.shape[1]
    return pl.pallas_call(
        _ada_kernel,
        grid=(n // tn,),
        in_specs=[
            pl.BlockSpec((bsz, d), lambda j: (0, 0)),
            pl.BlockSpec((d, tn), lambda j: (0, j)),
            pl.BlockSpec((1, tn), lambda j: (0, j)),
        ],
        out_specs=pl.BlockSpec((bsz, tn), lambda j: (0, j)),
        out_shape=jax.ShapeDtypeStruct((bsz, n), F32),
        compiler_params=pltpu.CompilerParams(
            dimension_semantics=("arbitrary",), vmem_limit_bytes=VMEM_LIMIT_BYTES),
        name="adaln_mod",
    )(c, w_ada, b_ada.reshape(1, n))


def _proj_kernel(mod_ref, x_ref, ng_ref, win_ref, wvt_ref, qg_ref, kg_ref, wp_ref, bp_ref, ps_ref,
                 q_ref, k_ref, vt_ref, sg_ref, po_ref, ubuf, *, ts, d_model, d_attn, d_pool):
    t = pl.program_id(1)
    x = x_ref[0]
    ms = jnp.mean(x * x, axis=-1, keepdims=True)
    shift = mod_ref[0, :, 0:d_model]
    scale = mod_ref[0, :, d_model:2 * d_model]
    a = ng_ref[...] * (1.0 + scale)
    hn = ((x * lax.rsqrt(ms + EPS)) * a + shift).astype(BF16)

    def proj(col, width):
        return jnp.dot(hn, win_ref[:, col:col + width], preferred_element_type=F32)

    r = lax.broadcasted_iota(jnp.int32, (d_attn, d_attn), 0) // HEAD_DIM
    c = lax.broadcasted_iota(jnp.int32, (d_attn, d_attn), 1) // HEAD_DIM
    head_avg = jnp.where(r == c, 1.0 / HEAD_DIM, 0.0).astype(BF16)

    def head_norm(v, g):
        msq = jnp.dot((v * v).astype(BF16), head_avg, preferred_element_type=F32)
        return (v * lax.rsqrt(msq + EPS)) * g

    q = proj(0, d_attn)
    q_ref[0] = head_norm(q, qg_ref[...]).astype(BF16)
    k = proj(d_attn, d_attn)
    k_ref[0] = head_norm(k, kg_ref[...]).astype(BF16)

    vt = lax.dot_general(wvt_ref[...], hn, (((1,), (1,)), ((), ())),
                         preferred_element_type=F32).astype(BF16)
    for jj in range(ts // ATTN_BLOCK):
        vt_ref[0, jj] = vt[:, jj * ATTN_BLOCK:(jj + 1) * ATTN_BLOCK]

    g_attn = proj(3 * d_attn, d_attn)
    sg_ref[0] = _silu(g_attn).astype(BF16)

    u = proj(4 * d_attn, d_pool)
    g_pool = proj(4 * d_attn + d_pool, d_pool)

    @pl.when(t == 0)
    def _():
        ubuf[0:HALO, :] = jnp.zeros((HALO, d_pool), F32)

    @pl.when(t > 0)
    def _():
        ubuf[0:HALO, :] = ubuf[ts:ts + HALO, :]

    ubuf[HALO:HALO + ts, :] = u
    gdim = d_pool // len(POOL_WINDOWS)
    pos = t * ts + lax.broadcasted_iota(jnp.int32, (ts, gdim), 0)
    for g, win in enumerate(POOL_WINDOWS):
        lo, hi = g * gdim, (g + 1) * gdim
        wsum = u[:, lo:hi]
        for dlt in range(1, win):
            wsum = wsum + ubuf[HALO - dlt:HALO - dlt + ts, lo:hi]
        cnt = jnp.minimum(pos + 1, win).astype(F32)
        pooled = wsum / cnt - u[:, lo:hi]
        mixed = jnp.dot(pooled.astype(BF16), wp_ref[g], preferred_element_type=F32) + bp_ref[:, lo:hi]
        po_ref[0, :, lo:hi] = (mixed * ps_ref[:, lo:hi] * _silu(g_pool[:, lo:hi])).astype(BF16)


def _proj_call(mod3, x, norm_g, w_in_bf, wv_t, qg, kg, wp_bf, b_pool, pool_scale, *, ts=512):
    bsz, s, d = x.shape
    d_attn = qg.shape[1]
    d_pool = pool_scale.shape[1]
    nt = s // ts
    nb = s // ATTN_BLOCK
    kern = functools.partial(_proj_kernel, ts=ts, d_model=d, d_attn=d_attn, d_pool=d_pool)
    const = lambda *shape: pl.BlockSpec(shape, lambda b, t: (0,) * len(shape))
    seq_out = lambda width: pl.BlockSpec((1, ts, width), lambda b, t: (b, t, 0))
    return pl.pallas_call(
        kern,
        grid=(bsz, nt),
        in_specs=[
            pl.BlockSpec((1, 1, mod3.shape[2]), lambda b, t: (b, 0, 0)),
            pl.BlockSpec((1, ts, d), lambda b, t: (b, t, 0)),
            const(1, d),
            const(*w_in_bf.shape),
            const(*wv_t.shape),
            const(1, d_attn),
            const(1, d_attn),
            const(*wp_bf.shape),
            const(1, d_pool),
            const(1, d_pool),
        ],
        out_specs=[
            seq_out(d_attn),
            seq_out(d_attn),
            pl.BlockSpec((1, ts // ATTN_BLOCK, d_attn, ATTN_BLOCK), lambda b, t: (b, t, 0, 0)),
            seq_out(d_attn),
            seq_out(d_pool),
        ],
        out_shape=[
            jax.ShapeDtypeStruct((bsz, s, d_attn), BF16),
            jax.ShapeDtypeStruct((bsz, s, d_attn), BF16),
            jax.ShapeDtypeStruct((bsz, nb, d_attn, ATTN_BLOCK), BF16),
            jax.ShapeDtypeStruct((bsz, s, d_attn), BF16),
            jax.ShapeDtypeStruct((bsz, s, d_pool), BF16),
        ],
        scratch_shapes=[pltpu.VMEM((HALO + ts, d_pool), F32)],
        compiler_params=pltpu.CompilerParams(
            dimension_semantics=("arbitrary", "arbitrary"), vmem_limit_bytes=VMEM_LIMIT_BYTES),
        name="norm_inproj_pool",
    )(mod3, x, norm_g, w_in_bf, wv_t, qg, kg, wp_bf, b_pool, pool_scale)


def _attn_kernel(q_ref, k_ref, vt_ref, sg_ref, o_ref, lr_ref, acc_ref, *, nblk, g_q, n_hp, k_unroll):
    tb = ATTN_BLOCK
    row = lax.broadcasted_iota(jnp.int32, (tb, tb), 0)
    col = lax.broadcasted_iota(jnp.int32, (tb, tb), 1)
    later_keys = (col >= row).astype(BF16)
    head_d = [(col < HEAD_DIM).astype(BF16), (col >= HEAD_DIM).astype(BF16)]
    causal = row < col
    n_grp = 2 * n_hp
    grp_w = g_q * tb

    def step(j, qcats, q_lo, diag):
        nq = g_q - q_lo
        rows_j = pl.ds(pl.multiple_of(j * tb, tb), tb)
        lanes = [slice(grp * grp_w + q_lo * tb, (grp + 1) * grp_w) for grp in range(n_grp)]
        zs = []
        for hp in range(n_hp):
            kb = k_ref[0, rows_j, hp * LANES:(hp + 1) * LANES]
            qc = qcats[hp]
            if q_lo:
                qc = jnp.concatenate([qc[h * grp_w + q_lo * tb:(h + 1) * grp_w] for h in range(2)], axis=0)
            zs.append(lax.dot_general(kb, qc, (((1,), (1,)), ((), ())), preferred_element_type=F32))
        z = jnp.concatenate(zs, axis=1)
        sp = jnp.maximum(jnp.log(1.0 + jnp.exp2(jnp.minimum(z, SOFTPLUS_CLAMP))) * LOG2E, z)

        def mask_diag(v):
            if not diag:
                return v
            parts = []
            for grp in range(n_grp):
                base = grp * nq * tb
                parts.append(jnp.where(causal, v[:, base:base + tb], 0.0))
                if nq > 1:
                    parts.append(v[:, base + tb:base + nq * tb])
            return jnp.concatenate(parts, axis=1)

        sp = mask_diag(sp)
        later = jnp.dot(later_keys, sp.astype(BF16), preferred_element_type=F32)
        log_rem = jnp.concatenate([lr_ref[:, ln] for ln in lanes], axis=1)
        w = mask_diag(jnp.exp2(z - later + log_rem)).astype(BF16)
        new_rem = log_rem - later[0:1, :]
        for grp in range(n_grp):
            hp, h = divmod(grp, 2)
            cols = slice(grp * nq * tb, (grp + 1) * nq * tb)
            lr_ref[:, lanes[grp]] = new_rem[:, cols]
            vt_h = vt_ref[0, j, hp * LANES + h * HEAD_DIM:hp * LANES + (h + 1) * HEAD_DIM, :]
            res = jnp.dot(vt_h, w[:, cols], preferred_element_type=F32)
            acc_ref[h * HEAD_DIM:(h + 1) * HEAD_DIM, hp * grp_w + q_lo * tb:(hp + 1) * grp_w] += res

    def q_super_block(sb, carry):
        qcats = []
        for hp in range(n_hp):
            rows = pl.ds(pl.multiple_of(sb * grp_w, grp_w), grp_w)
            q = q_ref[0, rows, hp * LANES:(hp + 1) * LANES]
            qcats.append(jnp.concatenate([q * jnp.concatenate([head_d[h]] * g_q, axis=0) for h in range(2)],
                                         axis=0))
        lr_ref[...] = jnp.zeros_like(lr_ref)
        acc_ref[...] = jnp.zeros_like(acc_ref)
        for m in reversed(range(g_q)):
            step(sb * g_q + m, qcats, m, True)

        def full_step(jj, c):
            for u in range(k_unroll):
                step(sb * g_q - 1 - (jj * k_unroll + u), qcats, 0, False)
            return c

        lax.fori_loop(0, sb * (g_q // k_unroll), full_step, 0)
        for g in range(g_q):
            rows = pl.ds(pl.multiple_of((sb * g_q + g) * tb, tb), tb)
            for hp in range(n_hp):
                a = hp * grp_w + g * tb
                gate = sg_ref[0, rows, hp * LANES:(hp + 1) * LANES].astype(F32)
                o_ref[0, rows, hp * LANES:(hp + 1) * LANES] = (acc_ref[:, a:a + tb].T * gate).astype(BF16)
        return carry

    lax.fori_loop(0, nblk // g_q, q_super_block, 0)


def _attn_call(qn, kn, vt, sg, *, g_q=4, n_hp=4, k_unroll=2):
    bsz, s, d_attn = qn.shape
    nblk = s // ATTN_BLOCK
    width = n_hp * LANES
    seq = pl.BlockSpec((1, s, width), lambda b, h: (b, 0, h))
    return pl.pallas_call(
        functools.partial(_attn_kernel, nblk=nblk, g_q=g_q, n_hp=n_hp, k_unroll=k_unroll),
        grid=(bsz, d_attn // width),
        in_specs=[seq, seq, pl.BlockSpec((1, nblk, width, ATTN_BLOCK), lambda b, h: (b, 0, h, 0)), seq],
        out_specs=seq,
        out_shape=jax.ShapeDtypeStruct((bsz, s, d_attn), BF16),
        scratch_shapes=[pltpu.VMEM((1, g_q * n_hp * 2 * ATTN_BLOCK), F32),
                        pltpu.VMEM((LANES, g_q * n_hp * ATTN_BLOCK), F32)],
        compiler_params=pltpu.CompilerParams(
            dimension_semantics=("arbitrary", "arbitrary"), vmem_limit_bytes=VMEM_LIMIT_BYTES),
        name="stickbreak_attn",
    )(qn, kn, vt, sg)


def _out_kernel(a_ref, p_ref, x_ref, mod_ref, w_ref, o_ref, *, d_model, d_attn):
    y = jnp.dot(a_ref[0], w_ref[0:d_attn, :], preferred_element_type=F32)
    y = y + jnp.dot(p_ref[0], w_ref[d_attn:, :], preferred_element_type=F32)
    gate = mod_ref[0, :, 2 * d_model:3 * d_model]
    o_ref[0] = x_ref[0] + gate * y


def _out_call(attn, pool_out, x, mod3, w_out_bf, *, ts=512):
    bsz, s, d = x.shape
    d_attn = attn.shape[2]
    d_pool = pool_out.shape[2]
    return pl.pallas_call(
        functools.partial(_out_kernel, d_model=d, d_attn=d_attn),
        grid=(bsz, s // ts),
        in_specs=[
            pl.BlockSpec((1, ts, d_attn), lambda b, t: (b, t, 0)),
            pl.BlockSpec((1, ts, d_pool), lambda b, t: (b, t, 0)),
            pl.BlockSpec((1, ts, d), lambda b, t: (b, t, 0)),
            pl.BlockSpec((1, 1, mod3.shape[2]), lambda b, t: (b, 0, 0)),
            pl.BlockSpec(w_out_bf.shape, lambda b, t: (0, 0)),
        ],
        out_specs=pl.BlockSpec((1, ts, d), lambda b, t: (b, t, 0)),
        out_shape=jax.ShapeDtypeStruct((bsz, s, d), F32),
        compiler_params=pltpu.CompilerParams(
            dimension_semantics=("arbitrary", "arbitrary"), vmem_limit_bytes=VMEM_LIMIT_BYTES),
        name="outproj_residual",
    )(attn, pool_out, x, mod3, w_out_bf)


def kernel(x, c, w_ada, b_ada, norm_g, w_in, q_norm_g, k_norm_g, w_pool, b_pool, pool_scale, w_out):
    depth = w_ada.shape[0]
    d_attn = w_out.shape[1] // 2
    n_heads = d_attn // HEAD_DIM
    h = x
    for l in range(depth):
        mod = _ada_call(c, w_ada[l], b_ada[l])
        mod3 = mod.reshape(mod.shape[0], 1, mod.shape[1])
        w_in_bf = w_in[l].astype(BF16)
        wv_t = w_in[l][:, 2 * d_attn:3 * d_attn].T.astype(BF16)
        qg = jnp.tile(q_norm_g[l] * (HEAD_DIM ** -0.5 * LOG2E), n_heads).reshape(1, d_attn)
        kg = jnp.tile(k_norm_g[l], n_heads).reshape(1, d_attn)
        qn, kn, vt, sg, pool_out = _proj_call(
            mod3, h, norm_g[l].reshape(1, -1), w_in_bf, wv_t, qg, kg,
            w_pool[l].astype(BF16), b_pool[l].reshape(1, -1), pool_scale[l].reshape(1, -1))
        attn = _attn_call(qn, kn, vt, sg)
        h = _out_call(attn, pool_out, h, mod3, w_out[l].astype(BF16))
    return h
```

```python
import functools

import jax
import jax.numpy as jnp
from jax import lax
from jax.experimental import pallas as pl
from jax.experimental.pallas import tpu as pltpu

F32 = jnp.float32
BF16 = jnp.bfloat16

HEAD_DIM = 64
POOL_WINDOWS = (2, 4, 8, 16)
EPS = 1e-6
LOG2E = 1.4426950408889634
SOFTPLUS_CLAMP = 64.0

LANES = 128
SUBLANES = 8
ATTN_BLOCK = 128
POOL_HIST = max(POOL_WINDOWS)
POOL_PAD = 2 * POOL_HIST
VMEM_LIMIT_BYTES = 48 * 1024 * 1024


def _silu(v):
    h = 0.5 * v
    return h + h * jnp.tanh(h)


def _first_grid_step():
    return (pl.program_id(0) == 0) & (pl.program_id(1) == 0)


def _ada_kernel(c_ref, w_ref, b_ref, o_ref):
    c = c_ref[...]
    ca = _silu(c).astype(BF16)
    o_ref[...] = jnp.dot(ca, w_ref[...].astype(BF16), preferred_element_type=F32) + b_ref[...]


def _ada_call(c, w_ada, b_ada, *, tn=512):
    bsz, d = c.shape
    n = w_ada.shape[1]
    return pl.pallas_call(
        _ada_kernel,
        grid=(n // tn,),
        in_specs=[
            pl.BlockSpec((bsz, d), lambda j: (0, 0)),
            pl.BlockSpec((d, tn), lambda j: (0, j)),
            pl.BlockSpec((1, tn), lambda j: (0, j)),
        ],
        out_specs=pl.BlockSpec((bsz, tn), lambda j: (0, j)),
        out_shape=jax.ShapeDtypeStruct((bsz, n), F32),
        compiler_params=pltpu.CompilerParams(
            dimension_semantics=("arbitrary",), vmem_limit_bytes=VMEM_LIMIT_BYTES),
        name="adaln_mod",
    )(c, w_ada, b_ada.reshape(1, n))


def _proj_kernel(mod_ref, x_ref, ng_ref, win_ref, qg_ref, kg_ref, wp_ref, bp_ref, ps_ref,
                 q_ref, k_ref, vt_ref, sg_ref, po_ref, wbf, wpbf, ubuf, sbuf_a, sbuf_b,
                 *, ts, d_model, d_attn, d_pool):
    t = pl.program_id(1)
    hist = slice(POOL_PAD - POOL_HIST, POOL_PAD)

    @pl.when(_first_grid_step())
    def _():
        wbf[...] = win_ref[...].astype(BF16)
        wpbf[...] = wp_ref[...].astype(BF16)

    @pl.when(t == 0)
    def _():
        ubuf[0:POOL_PAD, :] = jnp.zeros((POOL_PAD, d_pool), F32)

    @pl.when(t > 0)
    def _():
        ubuf[hist, :] = ubuf[ts + POOL_PAD - POOL_HIST:ts + POOL_PAD, :]

    x = x_ref[0]
    ms = jnp.mean(x * x, axis=-1, keepdims=True)
    shift = mod_ref[0, :, 0:d_model]
    scale = mod_ref[0, :, d_model:2 * d_model]
    a = ng_ref[...] * (1.0 + scale)
    hn = ((x * lax.rsqrt(ms + EPS)) * a + shift).astype(BF16)

    def proj(col, width):
        return jnp.dot(hn, wbf[:, col:col + width], preferred_element_type=F32)

    u = proj(4 * d_attn, d_pool)
    g_pool = proj(4 * d_attn + d_pool, d_pool)
    ubuf[POOL_PAD:POOL_PAD + ts, :] = u
    gdim = d_pool // len(POOL_WINDOWS)
    n_lvl = len(POOL_WINDOWS)
    end = POOL_PAD + ts
    src = ubuf
    wsums = []
    for lvl in range(n_lvl):
        shift_rows = 2 ** lvl
        start = SUBLANES * (lvl + 1)
        lanes = slice(lvl * gdim, d_pool)
        summed = src[start:end, lanes] + src[start - shift_rows:end - shift_rows, lanes]
        wsums.append(summed[POOL_PAD - start:, 0:gdim])
        if lvl + 1 < n_lvl:
            dst = sbuf_a if lvl % 2 == 0 else sbuf_b
            dst[start:end, lanes] = summed
            src = dst

    row = lax.broadcasted_iota(jnp.int32, (POOL_HIST, gdim), 0)
    for g, win in enumerate(POOL_WINDOWS):
        lo, hi = g * gdim, (g + 1) * gdim
        ug = u[:, lo:hi]
        cnt = jnp.minimum(t * ts + row + 1, win).astype(F32)
        head = wsums[g][0:POOL_HIST] / cnt - ug[0:POOL_HIST]
        rest = wsums[g][POOL_HIST:] * (1.0 / win) - ug[POOL_HIST:]
        pooled = jnp.concatenate([head, rest], axis=0).astype(BF16)
        mixed = jnp.dot(pooled, wpbf[g], preferred_element_type=F32) + bp_ref[:, lo:hi]
        po_ref[0, :, lo:hi] = (mixed * ps_ref[:, lo:hi] * _silu(g_pool[:, lo:hi])).astype(BF16)

    r = lax.broadcasted_iota(jnp.int32, (d_attn, d_attn), 0) // HEAD_DIM
    c = lax.broadcasted_iota(jnp.int32, (d_attn, d_attn), 1) // HEAD_DIM
    head_avg = jnp.where(r == c, 1.0 / HEAD_DIM, 0.0).astype(BF16)

    def head_norm(v, g):
        msq = jnp.dot((v * v).astype(BF16), head_avg, preferred_element_type=F32)
        return (v * lax.rsqrt(msq + EPS)) * g

    q_ref[0] = head_norm(proj(0, d_attn), qg_ref[...]).astype(BF16)
    k_ref[0] = head_norm(proj(d_attn, d_attn), kg_ref[...]).astype(BF16)

    vt = proj(2 * d_attn, d_attn).T.astype(BF16)
    for jj in range(ts // ATTN_BLOCK):
        vt_ref[0, jj] = vt[:, jj * ATTN_BLOCK:(jj + 1) * ATTN_BLOCK]

    sg_ref[0] = _silu(proj(3 * d_attn, d_attn)).astype(BF16)


def _proj_call(mod3, x, norm_g, w_in, qg, kg, w_pool, b_pool, pool_scale, *, ts=512):
    bsz, s, d = x.shape
    d_attn = qg.shape[1]
    d_pool = pool_scale.shape[1]
    nt = s // ts
    nb = s // ATTN_BLOCK
    kern = functools.partial(_proj_kernel, ts=ts, d_model=d, d_attn=d_attn, d_pool=d_pool)
    const = lambda *shape: pl.BlockSpec(shape, lambda b, t: (0,) * len(shape))
    once = lambda *shape: pl.BlockSpec(shape, lambda b, t: (0,) * len(shape), pipeline_mode=pl.Buffered(1))
    seq_out = lambda width: pl.BlockSpec((1, ts, width), lambda b, t: (b, t, 0))
    pool_rows = POOL_PAD + ts
    return pl.pallas_call(
        kern,
        grid=(bsz, nt),
        in_specs=[
            pl.BlockSpec((1, 1, mod3.shape[2]), lambda b, t: (b, 0, 0)),
            pl.BlockSpec((1, ts, d), lambda b, t: (b, t, 0)),
            const(1, d),
            once(*w_in.shape),
            const(1, d_attn),
            const(1, d_attn),
            once(*w_pool.shape),
            const(1, d_pool),
            const(1, d_pool),
        ],
        out_specs=[
            seq_out(d_attn),
            seq_out(d_attn),
            pl.BlockSpec((1, ts // ATTN_BLOCK, d_attn, ATTN_BLOCK), lambda b, t: (b, t, 0, 0)),
            seq_out(d_attn),
            seq_out(d_pool),
        ],
        out_shape=[
            jax.ShapeDtypeStruct((bsz, s, d_attn), BF16),
            jax.ShapeDtypeStruct((bsz, s, d_attn), BF16),
            jax.ShapeDtypeStruct((bsz, nb, d_attn, ATTN_BLOCK), BF16),
            jax.ShapeDtypeStruct((bsz, s, d_attn), BF16),
            jax.ShapeDtypeStruct((bsz, s, d_pool), BF16),
        ],
        scratch_shapes=[pltpu.VMEM(w_in.shape, BF16),
                        pltpu.VMEM(w_pool.shape, BF16),
                        pltpu.VMEM((pool_rows, d_pool), F32),
                        pltpu.VMEM((pool_rows, d_pool), F32),
                        pltpu.VMEM((pool_rows, d_pool), F32)],
        compiler_params=pltpu.CompilerParams(
            dimension_semantics=("arbitrary", "arbitrary"), vmem_limit_bytes=VMEM_LIMIT_BYTES),
        name="norm_inproj_pool",
    )(mod3, x, norm_g, w_in, qg, kg, w_pool, b_pool, pool_scale)


def _attn_kernel(q_ref, k_ref, vt_ref, sg_ref, o_ref, lr_ref, acc_ref, *, nblk, g_q, n_hp, k_unroll):
    tb = ATTN_BLOCK
    row = lax.broadcasted_iota(jnp.int32, (tb, tb), 0)
    col = lax.broadcasted_iota(jnp.int32, (tb, tb), 1)
    later_keys = (col >= row).astype(BF16)
    head_d = [(col < HEAD_DIM).astype(BF16), (col >= HEAD_DIM).astype(BF16)]
    causal = row < col
    n_grp = 2 * n_hp
    grp_w = g_q * tb

    def step(j, qcats, q_lo, diag):
        nq = g_q - q_lo
        rows_j = pl.ds(pl.multiple_of(j * tb, tb), tb)
        lanes = [slice(grp * grp_w + q_lo * tb, (grp + 1) * grp_w) for grp in range(n_grp)]
        zs = []
        for hp in range(n_hp):
            kb = k_ref[0, rows_j, hp * LANES:(hp + 1) * LANES]
            qc = qcats[hp]
            if q_lo:
                qc = jnp.concatenate([qc[h * grp_w + q_lo * tb:(h + 1) * grp_w] for h in range(2)], axis=0)
            zs.append(lax.dot_general(kb, qc, (((1,), (1,)), ((), ())), preferred_element_type=F32))
        z = jnp.concatenate(zs, axis=1)
        sp = jnp.maximum(jnp.log(1.0 + jnp.exp2(jnp.minimum(z, SOFTPLUS_CLAMP))) * LOG2E, z)

        def mask_diag(v):
            if not diag:
                return v
            parts = []
            for grp in range(n_grp):
                base = grp * nq * tb
                parts.append(jnp.where(causal, v[:, base:base + tb], 0.0))
                if nq > 1:
                    parts.append(v[:, base + tb:base + nq * tb])
            return jnp.concatenate(parts, axis=1)

        sp = mask_diag(sp)
        later = jnp.dot(later_keys, sp.astype(BF16), preferred_element_type=F32)
        log_rem = jnp.concatenate([lr_ref[:, ln] for ln in lanes], axis=1)
        w = mask_diag(jnp.exp2(z - later + log_rem)).astype(BF16)
        new_rem = log_rem - later[0:1, :]
        for grp in range(n_grp):
            hp, h = divmod(grp, 2)
            cols = slice(grp * nq * tb, (grp + 1) * nq * tb)
            lr_ref[:, lanes[grp]] = new_rem[:, cols]
            vt_h = vt_ref[0, j, hp * LANES + h * HEAD_DIM:hp * LANES + (h + 1) * HEAD_DIM, :]
            res = jnp.dot(vt_h, w[:, cols], preferred_element_type=F32)
            acc_ref[h * HEAD_DIM:(h + 1) * HEAD_DIM, hp * grp_w + q_lo * tb:(hp + 1) * grp_w] += res

    def q_super_block(sb, carry):
        qcats = []
        for hp in range(n_hp):
            rows = pl.ds(pl.multiple_of(sb * grp_w, grp_w), grp_w)
            q = q_ref[0, rows, hp * LANES:(hp + 1) * LANES]
            qcats.append(jnp.concatenate([q * jnp.concatenate([head_d[h]] * g_q, axis=0) for h in range(2)],
                                         axis=0))
        lr_ref[...] = jnp.zeros_like(lr_ref)
        acc_ref[...] = jnp.zeros_like(acc_ref)
        for m in reversed(range(g_q)):
            step(sb * g_q + m, qcats, m, True)

        def full_step(jj, c):
            for u in range(k_unroll):
                step(sb * g_q - 1 - (jj * k_unroll + u), qcats, 0, False)
            return c

        lax.fori_loop(0, sb * (g_q // k_unroll), full_step, 0)
        for g in range(g_q):
            rows = pl.ds(pl.multiple_of((sb * g_q + g) * tb, tb), tb)
            for hp in range(n_hp):
                a = hp * grp_w + g * tb
                gate = sg_ref[0, rows, hp * LANES:(hp + 1) * LANES].astype(F32)
                o_ref[0, rows, hp * LANES:(hp + 1) * LANES] = (acc_ref[:, a:a + tb].T * gate).astype(BF16)
        return carry

    lax.fori_loop(0, nblk // g_q, q_super_block, 0)


def _attn_call(qn, kn, vt, sg, *, g_q=4, n_hp=4, k_unroll=2):
    bsz, s, d_attn = qn.shape
    nblk = s // ATTN_BLOCK
    width = n_hp * LANES
    seq = pl.BlockSpec((1, s, width), lambda b, h: (b, 0, h))
    return pl.pallas_call(
        functools.partial(_attn_kernel, nblk=nblk, g_q=g_q, n_hp=n_hp, k_unroll=k_unroll),
        grid=(bsz, d_attn // width),
        in_specs=[seq, seq, pl.BlockSpec((1, nblk, width, ATTN_BLOCK), lambda b, h: (b, 0, h, 0)), seq],
        out_specs=seq,
        out_shape=jax.ShapeDtypeStruct((bsz, s, d_attn), BF16),
        scratch_shapes=[pltpu.VMEM((1, g_q * n_hp * 2 * ATTN_BLOCK), F32),
                        pltpu.VMEM((LANES, g_q * n_hp * ATTN_BLOCK), F32)],
        compiler_params=pltpu.CompilerParams(
            dimension_semantics=("arbitrary", "arbitrary"), vmem_limit_bytes=VMEM_LIMIT_BYTES),
        name="stickbreak_attn",
    )(qn, kn, vt, sg)


def _out_kernel(a_ref, p_ref, x_ref, mod_ref, w_ref, o_ref, wbf, *, d_model, d_attn):
    @pl.when(_first_grid_step())
    def _():
        wbf[...] = w_ref[...].astype(BF16)

    y = jnp.dot(a_ref[0], wbf[0:d_attn, :], preferred_element_type=F32)
    y = y + jnp.dot(p_ref[0], wbf[d_attn:, :], preferred_element_type=F32)
    gate = mod_ref[0, :, 2 * d_model:3 * d_model]
    o_ref[0] = x_ref[0] + gate * y


def _out_call(attn, pool_out, x, mod3, w_out, *, ts=1024):
    bsz, s, d = x.shape
    d_attn = attn.shape[2]
    d_pool = pool_out.shape[2]
    return pl.pallas_call(
        functools.partial(_out_kernel, d_model=d, d_attn=d_attn),
        grid=(bsz, s // ts),
        in_specs=[
            pl.BlockSpec((1, ts, d_attn), lambda b, t: (b, t, 0)),
            pl.BlockSpec((1, ts, d_pool), lambda b, t: (b, t, 0)),
            pl.BlockSpec((1, ts, d), lambda b, t: (b, t, 0)),
            pl.BlockSpec((1, 1, mod3.shape[2]), lambda b, t: (b, 0, 0)),
            pl.BlockSpec(w_out.shape, lambda b, t: (0, 0), pipeline_mode=pl.Buffered(1)),
        ],
        out_specs=pl.BlockSpec((1, ts, d), lambda b, t: (b, t, 0)),
        out_shape=jax.ShapeDtypeStruct((bsz, s, d), F32),
        scratch_shapes=[pltpu.VMEM(w_out.shape, BF16)],
        compiler_params=pltpu.CompilerParams(
            dimension_semantics=("arbitrary", "arbitrary"), vmem_limit_bytes=VMEM_LIMIT_BYTES),
        name="outproj_residual",
    )(attn, pool_out, x, mod3, w_out)


def kernel(x, c, w_ada, b_ada, norm_g, w_in, q_norm_g, k_norm_g, w_pool, b_pool, pool_scale, w_out):
    depth = w_ada.shape[0]
    d_attn = w_out.shape[1] // 2
    n_heads = d_attn // HEAD_DIM
    h = x
    for l in range(depth):
        mod = _ada_call(c, w_ada[l], b_ada[l])
        mod3 = mod.reshape(mod.shape[0], 1, mod.shape[1])
        qg = jnp.tile(q_norm_g[l] * (HEAD_DIM ** -0.5 * LOG2E), n_heads).reshape(1, d_attn)
        kg = jnp.tile(k_norm_g[l], n_heads).reshape(1, d_attn)
        qn, kn, vt, sg, pool_out = _proj_call(
            mod3, h, norm_g[l].reshape(1, -1), w_in[l], qg, kg,
            w_pool[l], b_pool[l].reshape(1, -1), pool_scale[l].reshape(1, -1))
        attn = _attn_call(qn, kn, vt, sg)
        h = _out_call(attn, pool_out, h, mod3, w_out[l])
    return h
```

```python
import functools

import jax
import jax.numpy as jnp
from jax import lax
from jax.experimental import pallas as pl
from jax.experimental.pallas import tpu as pltpu

F32 = jnp.float32
BF16 = jnp.bfloat16

HEAD_DIM = 64
POOL_WINDOWS = (2, 4, 8, 16)
EPS = 1e-6
LOG2E = 1.4426950408889634
SOFTPLUS_CLAMP = 64.0

LANES = 128
SUBLANES = 8
ATTN_BLOCK = 128
POOL_HIST = max(POOL_WINDOWS)
POOL_PAD = 2 * POOL_HIST
VMEM_LIMIT_BYTES = 48 * 1024 * 1024


def _silu(v):
    h = 0.5 * v
    return h + h * jnp.tanh(h)


def _first_grid_step():
    return (pl.program_id(0) == 0) & (pl.program_id(1) == 0)


def _ada_kernel(c_ref, w_ref, b_ref, o_ref):
    c = c_ref[...]
    ca = _silu(c).astype(BF16)
    o_ref[...] = jnp.dot(ca, w_ref[...].astype(BF16), preferred_element_type=F32) + b_ref[...]


def _ada_call(c, w_ada, b_ada, *, tn=512):
    bsz, d = c.shape
    n = w_ada.shape[1]
    return pl.pallas_call(
        _ada_kernel,
        grid=(n // tn,),
        in_specs=[
            pl.BlockSpec((bsz, d), lambda j: (0, 0)),
            pl.BlockSpec((d, tn), lambda j: (0, j)),
            pl.BlockSpec((1, tn), lambda j: (0, j)),
        ],
        out_specs=pl.BlockSpec((bsz, tn), lambda j: (0, j)),
        out_shape=jax.ShapeDtypeStruct((bsz, n), F32),
        compiler_params=pltpu.CompilerParams(
            dimension_semantics=("arbitrary",), vmem_limit_bytes=VMEM_LIMIT_BYTES),
        name="adaln_mod",
    )(c, w_ada, b_ada.reshape(1, n))


def _proj_kernel(mod_ref, x_ref, ng_ref, win_ref, qg_ref, kg_ref, wp_ref, bp_ref, ps_ref,
                 q_ref, k_ref, vt_ref, sg_ref, po_ref, wbf, wpbf, ubuf, sbuf_a, sbuf_b,
                 *, ts, d_model, d_attn, d_pool):
    t = pl.program_id(1)
    hist = slice(POOL_PAD - POOL_HIST, POOL_PAD)

    @pl.when(_first_grid_step())
    def _():
        wbf[...] = win_ref[...].astype(BF16)
        wpbf[...] = wp_ref[...].astype(BF16)

    @pl.when(t == 0)
    def _():
        ubuf[0:POOL_PAD, :] = jnp.zeros((POOL_PAD, d_pool), F32)

    @pl.when(t > 0)
    def _():
        ubuf[hist, :] = ubuf[ts + POOL_PAD - POOL_HIST:ts + POOL_PAD, :]

    x = x_ref[0]
    ms = jnp.mean(x * x, axis=-1, keepdims=True)
    shift = mod_ref[0, :, 0:d_model]
    scale = mod_ref[0, :, d_model:2 * d_model]
    a = ng_ref[...] * (1.0 + scale)
    hn = ((x * lax.rsqrt(ms + EPS)) * a + shift).astype(BF16)

    def proj(col, width):
        return jnp.dot(hn, wbf[:, col:col + width], preferred_element_type=F32)

    u = proj(4 * d_attn, d_pool)
    g_pool = proj(4 * d_attn + d_pool, d_pool)
    ubuf[POOL_PAD:POOL_PAD + ts, :] = u
    gdim = d_pool // len(POOL_WINDOWS)
    n_lvl = len(POOL_WINDOWS)
    end = POOL_PAD + ts
    src = ubuf
    wsums = []
    for lvl in range(n_lvl):
        shift_rows = 2 ** lvl
        start = SUBLANES * (lvl + 1)
        lanes = slice(lvl * gdim, d_pool)
        summed = src[start:end, lanes] + src[start - shift_rows:end - shift_rows, lanes]
        wsums.append(summed[POOL_PAD - start:, 0:gdim])
        if lvl + 1 < n_lvl:
            dst = sbuf_a if lvl % 2 == 0 else sbuf_b
            dst[start:end, lanes] = summed
            src = dst

    row = lax.broadcasted_iota(jnp.int32, (POOL_HIST, gdim), 0)
    for g, win in enumerate(POOL_WINDOWS):
        lo, hi = g * gdim, (g + 1) * gdim
        ug = u[:, lo:hi]
        cnt = jnp.minimum(t * ts + row + 1, win).astype(F32)
        head = wsums[g][0:POOL_HIST] / cnt - ug[0:POOL_HIST]
        rest = wsums[g][POOL_HIST:] * (1.0 / win) - ug[POOL_HIST:]
        pooled = jnp.concatenate([head, rest], axis=0).astype(BF16)
        mixed = jnp.dot(pooled, wpbf[g], preferred_element_type=F32) + bp_ref[:, lo:hi]
        po_ref[0, :, lo:hi] = (mixed * ps_ref[:, lo:hi] * _silu(g_pool[:, lo:hi])).astype(BF16)

    r = lax.broadcasted_iota(jnp.int32, (d_attn, d_attn), 0) // HEAD_DIM
    c = lax.broadcasted_iota(jnp.int32, (d_attn, d_attn), 1) // HEAD_DIM
    head_avg = jnp.where(r == c, 1.0 / HEAD_DIM, 0.0).astype(BF16)

    def head_norm(v, g):
        msq = jnp.dot((v * v).astype(BF16), head_avg, preferred_element_type=F32)
        return (v * lax.rsqrt(msq + EPS)) * g

    q_ref[0] = head_norm(proj(0, d_attn), qg_ref[...]).astype(BF16)
    k_ref[0] = head_norm(proj(d_attn, d_attn), kg_ref[...]).astype(BF16)

    vt = proj(2 * d_attn, d_attn).T.astype(BF16)
    for jj in range(ts // ATTN_BLOCK):
        vt_ref[0, jj] = vt[:, jj * ATTN_BLOCK:(jj + 1) * ATTN_BLOCK]

    sg_ref[0] = _silu(proj(3 * d_attn, d_attn)).astype(BF16)


def _proj_call(mod3, x, norm_g, w_in, qg, kg, w_pool, b_pool, pool_scale, *, ts=512):
    bsz, s, d = x.shape
    d_attn = qg.shape[1]
    d_pool = pool_scale.shape[1]
    nt = s // ts
    nb = s // ATTN_BLOCK
    kern = functools.partial(_proj_kernel, ts=ts, d_model=d, d_attn=d_attn, d_pool=d_pool)
    const = lambda *shape: pl.BlockSpec(shape, lambda b, t: (0,) * len(shape))
    once = lambda *shape: pl.BlockSpec(shape, lambda b, t: (0,) * len(shape), pipeline_mode=pl.Buffered(1))
    seq_out = lambda width: pl.BlockSpec((1, ts, width), lambda b, t: (b, t, 0))
    pool_rows = POOL_PAD + ts
    return pl.pallas_call(
        kern,
        grid=(bsz, nt),
        in_specs=[
            pl.BlockSpec((1, 1, mod3.shape[2]), lambda b, t: (b, 0, 0)),
            pl.BlockSpec((1, ts, d), lambda b, t: (b, t, 0)),
            const(1, d),
            once(*w_in.shape),
            const(1, d_attn),
            const(1, d_attn),
            once(*w_pool.shape),
            const(1, d_pool),
            const(1, d_pool),
        ],
        out_specs=[
            seq_out(d_attn),
            seq_out(d_attn),
            pl.BlockSpec((1, ts // ATTN_BLOCK, d_attn, ATTN_BLOCK), lambda b, t: (b, t, 0, 0)),
            seq_out(d_attn),
            seq_out(d_pool),
        ],
        out_shape=[
            jax.ShapeDtypeStruct((bsz, s, d_attn), BF16),
            jax.ShapeDtypeStruct((bsz, s, d_attn), BF16),
            jax.ShapeDtypeStruct((bsz, nb, d_attn, ATTN_BLOCK), BF16),
            jax.ShapeDtypeStruct((bsz, s, d_attn), BF16),
            jax.ShapeDtypeStruct((bsz, s, d_pool), BF16),
        ],
        scratch_shapes=[pltpu.VMEM(w_in.shape, BF16),
                        pltpu.VMEM(w_pool.shape, BF16),
                        pltpu.VMEM((pool_rows, d_pool), F32),
                        pltpu.VMEM((pool_rows, d_pool), F32),
                        pltpu.VMEM((pool_rows, d_pool), F32)],
        compiler_params=pltpu.CompilerParams(
            dimension_semantics=("arbitrary", "arbitrary"), vmem_limit_bytes=VMEM_LIMIT_BYTES),
        name="norm_inproj_pool",
    )(mod3, x, norm_g, w_in, qg, kg, w_pool, b_pool, pool_scale)


def _attn_kernel(q_ref, k_ref, vt_ref, sg_ref, o_ref, lr_ref, acc_ref, z0_ref, z1_ref, sp0_ref, sp1_ref,
                 *, nblk, g_q, n_hp):
    tb = ATTN_BLOCK
    row = lax.broadcasted_iota(jnp.int32, (tb, tb), 0)
    col = lax.broadcasted_iota(jnp.int32, (tb, tb), 1)
    later_keys = (col >= row).astype(BF16)
    head_d = [(col < HEAD_DIM).astype(BF16), (col >= HEAD_DIM).astype(BF16)]
    causal = row < col
    n_grp = 2 * n_hp
    grp_w = g_q * tb
    z_buf = (z0_ref, z1_ref)
    sp_buf = (sp0_ref, sp1_ref)

    def mask_diag(v, q_lo, diag):
        if not diag:
            return v
        nq = g_q - q_lo
        parts = []
        for grp in range(n_grp):
            base = grp * nq * tb
            parts.append(jnp.where(causal, v[:, base:base + tb], 0.0))
            if nq > 1:
                parts.append(v[:, base + tb:base + nq * tb])
        return jnp.concatenate(parts, axis=1)

    def score(j, qcats, q_lo, diag):
        rows_j = pl.ds(pl.multiple_of(j * tb, tb), tb)
        zs = []
        for hp in range(n_hp):
            kb = k_ref[0, rows_j, hp * LANES:(hp + 1) * LANES]
            qc = qcats[hp]
            if q_lo:
                qc = jnp.concatenate([qc[h * grp_w + q_lo * tb:(h + 1) * grp_w] for h in range(2)], axis=0)
            zs.append(lax.dot_general(kb, qc, (((1,), (1,)), ((), ())), preferred_element_type=F32))
        z = jnp.concatenate(zs, axis=1)
        sp = jnp.maximum(jnp.log(1.0 + jnp.exp2(jnp.minimum(z, SOFTPLUS_CLAMP))) * LOG2E, z)
        return z, mask_diag(sp, q_lo, diag).astype(BF16)

    def weigh(j, z, sp, q_lo, diag):
        nq = g_q - q_lo
        lanes = [slice(grp * grp_w + q_lo * tb, (grp + 1) * grp_w) for grp in range(n_grp)]
        later = jnp.dot(later_keys, sp, preferred_element_type=F32)
        log_rem = jnp.concatenate([lr_ref[:, ln] for ln in lanes], axis=1)
        w = mask_diag(jnp.exp2(z - later + log_rem), q_lo, diag).astype(BF16)
        new_rem = log_rem - later[0:1, :]
        for grp in range(n_grp):
            hp, h = divmod(grp, 2)
            cols = slice(grp * nq * tb, (grp + 1) * nq * tb)
            lr_ref[:, lanes[grp]] = new_rem[:, cols]
            vt_h = vt_ref[0, j, hp * LANES + h * HEAD_DIM:hp * LANES + (h + 1) * HEAD_DIM, :]
            res = jnp.dot(vt_h, w[:, cols], preferred_element_type=F32)
            acc_ref[h * HEAD_DIM:(h + 1) * HEAD_DIM, hp * grp_w + q_lo * tb:(hp + 1) * grp_w] += res

    def q_super_block(sb, carry):
        qcats = []
        for hp in range(n_hp):
            rows = pl.ds(pl.multiple_of(sb * grp_w, grp_w), grp_w)
            q = q_ref[0, rows, hp * LANES:(hp + 1) * LANES]
            qcats.append(jnp.concatenate([q * jnp.concatenate([head_d[h]] * g_q, axis=0) for h in range(2)],
                                         axis=0))
        lr_ref[...] = jnp.zeros_like(lr_ref)
        acc_ref[...] = jnp.zeros_like(acc_ref)
        for m in reversed(range(g_q)):
            j = sb * g_q + m
            weigh(j, *score(j, qcats, m, True), m, True)

        n_full = sb * g_q

        def score_to(slot, j):
            z_buf[slot][...], sp_buf[slot][...] = score(j, qcats, 0, False)

        def weigh_from(slot, j):
            weigh(j, z_buf[slot][...], sp_buf[slot][...], 0, False)

        def two_steps(it, c):
            j = n_full - 1 - 2 * it
            score_to(1, j - 1)
            weigh_from(0, j)
            score_to(0, j - 2)
            weigh_from(1, j - 1)
            return c

        @pl.when(sb > 0)
        def _():
            score_to(0, n_full - 1)
            lax.fori_loop(0, n_full // 2 - 1, two_steps, 0)
            score_to(1, 0)
            weigh_from(0, 1)
            weigh_from(1, 0)

        for g in range(g_q):
            rows = pl.ds(pl.multiple_of((sb * g_q + g) * tb, tb), tb)
            for hp in range(n_hp):
                a = hp * grp_w + g * tb
                gate = sg_ref[0, rows, hp * LANES:(hp + 1) * LANES].astype(F32)
                o_ref[0, rows, hp * LANES:(hp + 1) * LANES] = (acc_ref[:, a:a + tb].T * gate).astype(BF16)
        return carry

    lax.fori_loop(0, nblk // g_q, q_super_block, 0)


def _attn_call(qn, kn, vt, sg, *, g_q=4, n_hp=4):
    bsz, s, d_attn = qn.shape
    nblk = s // ATTN_BLOCK
    assert g_q % 2 == 0 and nblk % g_q == 0
    width = n_hp * LANES
    score_lanes = g_q * n_hp * 2 * ATTN_BLOCK
    seq = pl.BlockSpec((1, s, width), lambda b, h: (b, 0, h))
    return pl.pallas_call(
        functools.partial(_attn_kernel, nblk=nblk, g_q=g_q, n_hp=n_hp),
        grid=(bsz, d_attn // width),
        in_specs=[seq, seq, pl.BlockSpec((1, nblk, width, ATTN_BLOCK), lambda b, h: (b, 0, h, 0)), seq],
        out_specs=seq,
        out_shape=jax.ShapeDtypeStruct((bsz, s, d_attn), BF16),
        scratch_shapes=[pltpu.VMEM((1, score_lanes), F32),
                        pltpu.VMEM((LANES, g_q * n_hp * ATTN_BLOCK), F32),
                        pltpu.VMEM((ATTN_BLOCK, score_lanes), F32),
                        pltpu.VMEM((ATTN_BLOCK, score_lanes), F32),
                        pltpu.VMEM((ATTN_BLOCK, score_lanes), BF16),
                        pltpu.VMEM((ATTN_BLOCK, score_lanes), BF16)],
        compiler_params=pltpu.CompilerParams(
            dimension_semantics=("arbitrary", "arbitrary"), vmem_limit_bytes=VMEM_LIMIT_BYTES),
        name="stickbreak_attn",
    )(qn, kn, vt, sg)


def _out_kernel(a_ref, p_ref, x_ref, mod_ref, w_ref, o_ref, wbf, *, d_model, d_attn):
    @pl.when(_first_grid_step())
    def _():
        wbf[...] = w_ref[...].astype(BF16)

    y = jnp.dot(a_ref[0], wbf[0:d_attn, :], preferred_element_type=F32)
    y = y + jnp.dot(p_ref[0], wbf[d_attn:, :], preferred_element_type=F32)
    gate = mod_ref[0, :, 2 * d_model:3 * d_model]
    o_ref[0] = x_ref[0] + gate * y


def _out_call(attn, pool_out, x, mod3, w_out, *, ts=1024):
    bsz, s, d = x.shape
    d_attn = attn.shape[2]
    d_pool = pool_out.shape[2]
    return pl.pallas_call(
        functools.partial(_out_kernel, d_model=d, d_attn=d_attn),
        grid=(bsz, s // ts),
        in_specs=[
            pl.BlockSpec((1, ts, d_attn), lambda b, t: (b, t, 0)),
            pl.BlockSpec((1, ts, d_pool), lambda b, t: (b, t, 0)),
            pl.BlockSpec((1, ts, d), lambda b, t: (b, t, 0)),
            pl.BlockSpec((1, 1, mod3.shape[2]), lambda b, t: (b, 0, 0)),
            pl.BlockSpec(w_out.shape, lambda b, t: (0, 0), pipeline_mode=pl.Buffered(1)),
        ],
        out_specs=pl.BlockSpec((1, ts, d), lambda b, t: (b, t, 0)),
        out_shape=jax.ShapeDtypeStruct((bsz, s, d), F32),
        scratch_shapes=[pltpu.VMEM(w_out.shape, BF16)],
        compiler_params=pltpu.CompilerParams(
            dimension_semantics=("arbitrary", "arbitrary"), vmem_limit_bytes=VMEM_LIMIT_BYTES),
        name="outproj_residual",
    )(attn, pool_out, x, mod3, w_out)


def kernel(x, c, w_ada, b_ada, norm_g, w_in, q_norm_g, k_norm_g, w_pool, b_pool, pool_scale, w_out):
    depth = w_ada.shape[0]
    d_attn = w_out.shape[1] // 2
    n_heads = d_attn // HEAD_DIM
    h = x
    for l in range(depth):
        mod = _ada_call(c, w_ada[l], b_ada[l])
        mod3 = mod.reshape(mod.shape[0], 1, mod.shape[1])
        qg = jnp.tile(q_norm_g[l] * (HEAD_DIM ** -0.5 * LOG2E), n_heads).reshape(1, d_attn)
        kg = jnp.tile(k_norm_g[l], n_heads).reshape(1, d_attn)
        qn, kn, vt, sg, pool_out = _proj_call(
            mod3, h, norm_g[l].reshape(1, -1), w_in[l], qg, kg,
            w_pool[l], b_pool[l].reshape(1, -1), pool_scale[l].reshape(1, -1))
        attn = _attn_call(qn, kn, vt, sg)
        h = _out_call(attn, pool_out, h, mod3, w_out[l])
    return h
```

```python
import functools

import jax
import jax.numpy as jnp
from jax import lax
from jax.experimental import pallas as pl
from jax.experimental.pallas import tpu as pltpu

F32 = jnp.float32
BF16 = jnp.bfloat16

HEAD_DIM = 64
POOL_WINDOWS = (2, 4, 8, 16)
EPS = 1e-6
LOG2E = 1.4426950408889634
SOFTPLUS_CLAMP = 64.0

LANES = 128
SUBLANES = 8
ATTN_BLOCK = 128
POOL_HIST = max(POOL_WINDOWS)
POOL_PAD = 2 * POOL_HIST
VMEM_LIMIT_BYTES = 58 * 1024 * 1024


def _silu(v):
    h = 0.5 * v
    return h + h * jnp.tanh(h)


def _first_grid_step():
    return (pl.program_id(0) == 0) & (pl.program_id(1) == 0)


def _ada_kernel(c_ref, w_ref, b_ref, o_ref):
    c = c_ref[...]
    ca = _silu(c).astype(BF16)
    o_ref[...] = jnp.dot(ca, w_ref[...].astype(BF16), preferred_element_type=F32) + b_ref[...]


def _ada_call(c, w_ada, b_ada, *, tn=512):
    bsz, d = c.shape
    n = w_ada.shape[1]
    return pl.pallas_call(
        _ada_kernel,
        grid=(n // tn,),
        in_specs=[
            pl.BlockSpec((bsz, d), lambda j: (0, 0)),
            pl.BlockSpec((d, tn), lambda j: (0, j)),
            pl.BlockSpec((1, tn), lambda j: (0, j)),
        ],
        out_specs=pl.BlockSpec((bsz, tn), lambda j: (0, j)),
        out_shape=jax.ShapeDtypeStruct((bsz, n), F32),
        compiler_params=pltpu.CompilerParams(
            dimension_semantics=("arbitrary",), vmem_limit_bytes=VMEM_LIMIT_BYTES),
        name="adaln_mod",
    )(c, w_ada, b_ada.reshape(1, n))


def _proj_kernel(mod_ref, x_ref, ng_ref, win_ref, qg_ref, kg_ref, wp_ref, bp_ref, ps_ref,
                 q_ref, k_ref, vt_ref, sg_ref, po_ref, wbf, wpbf, ubuf, sbuf_a, sbuf_b,
                 *, ts, d_model, d_attn, d_pool):
    t = pl.program_id(1)
    hist = slice(POOL_PAD - POOL_HIST, POOL_PAD)

    @pl.when(_first_grid_step())
    def _():
        wbf[...] = win_ref[...].astype(BF16)
        wpbf[...] = wp_ref[...].astype(BF16)

    @pl.when(t == 0)
    def _():
        ubuf[0:POOL_PAD, :] = jnp.zeros((POOL_PAD, d_pool), F32)

    @pl.when(t > 0)
    def _():
        ubuf[hist, :] = ubuf[ts + POOL_PAD - POOL_HIST:ts + POOL_PAD, :]

    x = x_ref[0]
    ms = jnp.mean(x * x, axis=-1, keepdims=True)
    shift = mod_ref[0, :, 0:d_model]
    scale = mod_ref[0, :, d_model:2 * d_model]
    a = ng_ref[...] * (1.0 + scale)
    hn = ((x * lax.rsqrt(ms + EPS)) * a + shift).astype(BF16)

    def proj(col, width):
        return jnp.dot(hn, wbf[:, col:col + width], preferred_element_type=F32)

    u = proj(4 * d_attn, d_pool)
    g_pool = proj(4 * d_attn + d_pool, d_pool)
    ubuf[POOL_PAD:POOL_PAD + ts, :] = u
    gdim = d_pool // len(POOL_WINDOWS)
    n_lvl = len(POOL_WINDOWS)
    end = POOL_PAD + ts
    src = ubuf
    wsums = []
    for lvl in range(n_lvl):
        shift_rows = 2 ** lvl
        start = SUBLANES * (lvl + 1)
        lanes = slice(lvl * gdim, d_pool)
        summed = src[start:end, lanes] + src[start - shift_rows:end - shift_rows, lanes]
        wsums.append(summed[POOL_PAD - start:, 0:gdim])
        if lvl + 1 < n_lvl:
            dst = sbuf_a if lvl % 2 == 0 else sbuf_b
            dst[start:end, lanes] = summed
            src = dst

    row = lax.broadcasted_iota(jnp.int32, (POOL_HIST, gdim), 0)
    for g, win in enumerate(POOL_WINDOWS):
        lo, hi = g * gdim, (g + 1) * gdim
        ug = u[:, lo:hi]
        cnt = jnp.minimum(t * ts + row + 1, win).astype(F32)
        head = wsums[g][0:POOL_HIST] / cnt - ug[0:POOL_HIST]
        rest = wsums[g][POOL_HIST:] * (1.0 / win) - ug[POOL_HIST:]
        pooled = jnp.concatenate([head, rest], axis=0).astype(BF16)
        mixed = jnp.dot(pooled, wpbf[g], preferred_element_type=F32) + bp_ref[:, lo:hi]
        po_ref[0, :, lo:hi] = (mixed * ps_ref[:, lo:hi] * _silu(g_pool[:, lo:hi])).astype(BF16)

    r = lax.broadcasted_iota(jnp.int32, (d_attn, d_attn), 0) // HEAD_DIM
    c = lax.broadcasted_iota(jnp.int32, (d_attn, d_attn), 1) // HEAD_DIM
    head_avg = jnp.where(r == c, 1.0 / HEAD_DIM, 0.0).astype(BF16)

    def head_norm(v, g):
        msq = jnp.dot((v * v).astype(BF16), head_avg, preferred_element_type=F32)
        return (v * lax.rsqrt(msq + EPS)) * g

    q_ref[0] = head_norm(proj(0, d_attn), qg_ref[...]).astype(BF16)
    k_ref[0] = head_norm(proj(d_attn, d_attn), kg_ref[...]).astype(BF16)

    vt = proj(2 * d_attn, d_attn).T.astype(BF16)
    for jj in range(ts // ATTN_BLOCK):
        vt_ref[0, jj] = vt[:, jj * ATTN_BLOCK:(jj + 1) * ATTN_BLOCK]

    sg_ref[0] = _silu(proj(3 * d_attn, d_attn)).astype(BF16)


def _proj_call(mod3, x, norm_g, w_in, qg, kg, w_pool, b_pool, pool_scale, *, ts=1024):
    bsz, s, d = x.shape
    d_attn = qg.shape[1]
    d_pool = pool_scale.shape[1]
    nt = s // ts
    nb = s // ATTN_BLOCK
    kern = functools.partial(_proj_kernel, ts=ts, d_model=d, d_attn=d_attn, d_pool=d_pool)
    const = lambda *shape: pl.BlockSpec(shape, lambda b, t: (0,) * len(shape))
    once = lambda *shape: pl.BlockSpec(shape, lambda b, t: (0,) * len(shape), pipeline_mode=pl.Buffered(1))
    seq_out = lambda width: pl.BlockSpec((1, ts, width), lambda b, t: (b, t, 0))
    pool_rows = POOL_PAD + ts
    return pl.pallas_call(
        kern,
        grid=(bsz, nt),
        in_specs=[
            pl.BlockSpec((1, 1, mod3.shape[2]), lambda b, t: (b, 0, 0)),
            pl.BlockSpec((1, ts, d), lambda b, t: (b, t, 0)),
            const(1, d),
            once(*w_in.shape),
            const(1, d_attn),
            const(1, d_attn),
            once(*w_pool.shape),
            const(1, d_pool),
            const(1, d_pool),
        ],
        out_specs=[
            seq_out(d_attn),
            seq_out(d_attn),
            pl.BlockSpec((1, ts // ATTN_BLOCK, d_attn, ATTN_BLOCK), lambda b, t: (b, t, 0, 0)),
            seq_out(d_attn),
            seq_out(d_pool),
        ],
        out_shape=[
            jax.ShapeDtypeStruct((bsz, s, d_attn), BF16),
            jax.ShapeDtypeStruct((bsz, s, d_attn), BF16),
            jax.ShapeDtypeStruct((bsz, nb, d_attn, ATTN_BLOCK), BF16),
            jax.ShapeDtypeStruct((bsz, s, d_attn), BF16),
            jax.ShapeDtypeStruct((bsz, s, d_pool), BF16),
        ],
        scratch_shapes=[pltpu.VMEM(w_in.shape, BF16),
                        pltpu.VMEM(w_pool.shape, BF16),
                        pltpu.VMEM((pool_rows, d_pool), F32),
                        pltpu.VMEM((pool_rows, d_pool), F32),
                        pltpu.VMEM((pool_rows, d_pool), F32)],
        compiler_params=pltpu.CompilerParams(
            dimension_semantics=("arbitrary", "arbitrary"), vmem_limit_bytes=VMEM_LIMIT_BYTES),
        name="norm_inproj_pool",
    )(mod3, x, norm_g, w_in, qg, kg, w_pool, b_pool, pool_scale)


def _attn_kernel(q_ref, k_ref, vt_ref, sg_ref, o_ref, lr_ref, acc_ref, z0_ref, z1_ref, sp0_ref, sp1_ref,
                 *, nblk, g_q, n_hp):
    tb = ATTN_BLOCK
    row = lax.broadcasted_iota(jnp.int32, (tb, tb), 0)
    col = lax.broadcasted_iota(jnp.int32, (tb, tb), 1)
    later_keys = (col >= row).astype(BF16)
    head_d = [(col < HEAD_DIM).astype(BF16), (col >= HEAD_DIM).astype(BF16)]
    causal = row < col
    n_grp = 2 * n_hp
    grp_w = g_q * tb
    z_buf = (z0_ref, z1_ref)
    sp_buf = (sp0_ref, sp1_ref)

    def mask_diag(v, q_lo, diag):
        if not diag:
            return v
        nq = g_q - q_lo
        parts = []
        for grp in range(n_grp):
            base = grp * nq * tb
            parts.append(jnp.where(causal, v[:, base:base + tb], 0.0))
            if nq > 1:
                parts.append(v[:, base + tb:base + nq * tb])
        return jnp.concatenate(parts, axis=1)

    def score(j, qcats, q_lo, diag):
        rows_j = pl.ds(pl.multiple_of(j * tb, tb), tb)
        zs = []
        for hp in range(n_hp):
            kb = k_ref[0, rows_j, hp * LANES:(hp + 1) * LANES]
            qc = qcats[hp]
            if q_lo:
                qc = jnp.concatenate([qc[h * grp_w + q_lo * tb:(h + 1) * grp_w] for h in range(2)], axis=0)
            zs.append(lax.dot_general(kb, qc, (((1,), (1,)), ((), ())), preferred_element_type=F32))
        z = jnp.concatenate(zs, axis=1)
        sp = jnp.maximum(jnp.log(1.0 + jnp.exp2(jnp.minimum(z, SOFTPLUS_CLAMP))) * LOG2E, z)
        return z, mask_diag(sp, q_lo, diag).astype(BF16)

    def weigh(j, z, sp, q_lo, diag):
        nq = g_q - q_lo
        lanes = [slice(grp * grp_w + q_lo * tb, (grp + 1) * grp_w) for grp in range(n_grp)]
        later = jnp.dot(later_keys, sp, preferred_element_type=F32)
        log_rem = jnp.concatenate([lr_ref[:, ln] for ln in lanes], axis=1)
        w = mask_diag(jnp.exp2(z - later + log_rem), q_lo, diag).astype(BF16)
        new_rem = log_rem - later[0:1, :]
        for grp in range(n_grp):
            hp, h = divmod(grp, 2)
            cols = slice(grp * nq * tb, (grp + 1) * nq * tb)
            lr_ref[:, lanes[grp]] = new_rem[:, cols]
            vt_h = vt_ref[0, j, hp * LANES + h * HEAD_DIM:hp * LANES + (h + 1) * HEAD_DIM, :]
            res = jnp.dot(vt_h, w[:, cols], preferred_element_type=F32)
            acc_ref[h * HEAD_DIM:(h + 1) * HEAD_DIM, hp * grp_w + q_lo * tb:(hp + 1) * grp_w] += res

    def q_super_block(sb, carry):
        qcats = []
        for hp in range(n_hp):
            rows = pl.ds(pl.multiple_of(sb * grp_w, grp_w), grp_w)
            q = q_ref[0, rows, hp * LANES:(hp + 1) * LANES]
            qcats.append(jnp.concatenate([q * jnp.concatenate([head_d[h]] * g_q, axis=0) for h in range(2)],
                                         axis=0))
        lr_ref[...] = jnp.zeros_like(lr_ref)
        acc_ref[...] = jnp.zeros_like(acc_ref)
        for m in reversed(range(g_q)):
            j = sb * g_q + m
            weigh(j, *score(j, qcats, m, True), m, True)

        n_full = sb * g_q

        def score_to(slot, j):
            z_buf[slot][...], sp_buf[slot][...] = score(j, qcats, 0, False)

        def weigh_from(slot, j):
            weigh(j, z_buf[slot][...], sp_buf[slot][...], 0, False)

        def two_steps(it, c):
            j = n_full - 1 - 2 * it
            score_to(1, j - 1)
            weigh_from(0, j)
            score_to(0, j - 2)
            weigh_from(1, j - 1)
            return c

        @pl.when(sb > 0)
        def _():
            score_to(0, n_full - 1)
            lax.fori_loop(0, n_full // 2 - 1, two_steps, 0)
            score_to(1, 0)
            weigh_from(0, 1)
            weigh_from(1, 0)

        for g in range(g_q):
            rows = pl.ds(pl.multiple_of((sb * g_q + g) * tb, tb), tb)
            for hp in range(n_hp):
                a = hp * grp_w + g * tb
                gate = sg_ref[0, rows, hp * LANES:(hp + 1) * LANES].astype(F32)
                o_ref[0, rows, hp * LANES:(hp + 1) * LANES] = (acc_ref[:, a:a + tb].T * gate).astype(BF16)
        return carry

    lax.fori_loop(0, nblk // g_q, q_super_block, 0)


def _attn_call(qn, kn, vt, sg, *, g_q=4, n_hp=4):
    bsz, s, d_attn = qn.shape
    nblk = s // ATTN_BLOCK
    assert g_q % 2 == 0 and nblk % g_q == 0
    width = n_hp * LANES
    score_lanes = g_q * n_hp * 2 * ATTN_BLOCK
    seq = pl.BlockSpec((1, s, width), lambda b, h: (b, 0, h))
    return pl.pallas_call(
        functools.partial(_attn_kernel, nblk=nblk, g_q=g_q, n_hp=n_hp),
        grid=(bsz, d_attn // width),
        in_specs=[seq, seq, pl.BlockSpec((1, nblk, width, ATTN_BLOCK), lambda b, h: (b, 0, h, 0)), seq],
        out_specs=seq,
        out_shape=jax.ShapeDtypeStruct((bsz, s, d_attn), BF16),
        scratch_shapes=[pltpu.VMEM((1, score_lanes), F32),
                        pltpu.VMEM((LANES, g_q * n_hp * ATTN_BLOCK), F32),
                        pltpu.VMEM((ATTN_BLOCK, score_lanes), F32),
                        pltpu.VMEM((ATTN_BLOCK, score_lanes), F32),
                        pltpu.VMEM((ATTN_BLOCK, score_lanes), BF16),
                        pltpu.VMEM((ATTN_BLOCK, score_lanes), BF16)],
        compiler_params=pltpu.CompilerParams(
            dimension_semantics=("arbitrary", "arbitrary"), vmem_limit_bytes=VMEM_LIMIT_BYTES),
        name="stickbreak_attn",
    )(qn, kn, vt, sg)


def _out_kernel(a_ref, p_ref, x_ref, mod_ref, w_ref, o_ref, wbf, *, d_model, d_attn):
    @pl.when(_first_grid_step())
    def _():
        wbf[...] = w_ref[...].astype(BF16)

    y = jnp.dot(a_ref[0], wbf[0:d_attn, :], preferred_element_type=F32)
    y = y + jnp.dot(p_ref[0], wbf[d_attn:, :], preferred_element_type=F32)
    gate = mod_ref[0, :, 2 * d_model:3 * d_model]
    o_ref[0] = x_ref[0] + gate * y


def _out_call(attn, pool_out, x, mod3, w_out, *, ts=2048):
    bsz, s, d = x.shape
    d_attn = attn.shape[2]
    d_pool = pool_out.shape[2]
    return pl.pallas_call(
        functools.partial(_out_kernel, d_model=d, d_attn=d_attn),
        grid=(bsz, s // ts),
        in_specs=[
            pl.BlockSpec((1, ts, d_attn), lambda b, t: (b, t, 0)),
            pl.BlockSpec((1, ts, d_pool), lambda b, t: (b, t, 0)),
            pl.BlockSpec((1, ts, d), lambda b, t: (b, t, 0)),
            pl.BlockSpec((1, 1, mod3.shape[2]), lambda b, t: (b, 0, 0)),
            pl.BlockSpec(w_out.shape, lambda b, t: (0, 0), pipeline_mode=pl.Buffered(1)),
        ],
        out_specs=pl.BlockSpec((1, ts, d), lambda b, t: (b, t, 0)),
        out_shape=jax.ShapeDtypeStruct((bsz, s, d), F32),
        scratch_shapes=[pltpu.VMEM(w_out.shape, BF16)],
        compiler_params=pltpu.CompilerParams(
            dimension_semantics=("arbitrary", "arbitrary"), vmem_limit_bytes=VMEM_LIMIT_BYTES),
        name="outproj_residual",
    )(attn, pool_out, x, mod3, w_out)


def kernel(x, c, w_ada, b_ada, norm_g, w_in, q_norm_g, k_norm_g, w_pool, b_pool, pool_scale, w_out):
    depth = w_ada.shape[0]
    d_attn = w_out.shape[1] // 2
    n_heads = d_attn // HEAD_DIM
    h = x
    for l in range(depth):
        mod = _ada_call(c, w_ada[l], b_ada[l])
        mod3 = mod.reshape(mod.shape[0], 1, mod.shape[1])
        qg = jnp.tile(q_norm_g[l] * (HEAD_DIM ** -0.5 * LOG2E), n_heads).reshape(1, d_attn)
        kg = jnp.tile(k_norm_g[l], n_heads).reshape(1, d_attn)
        qn, kn, vt, sg, pool_out = _proj_call(
            mod3, h, norm_g[l].reshape(1, -1), w_in[l], qg, kg,
            w_pool[l], b_pool[l].reshape(1, -1), pool_scale[l].reshape(1, -1))
        attn = _attn_call(qn, kn, vt, sg)
        h = _out_call(attn, pool_out, h, mod3, w_out[l])
    return h
```

```python
import functools

import jax
import jax.numpy as jnp
from jax import lax
from jax.experimental import pallas as pl
from jax.experimental.pallas import tpu as pltpu

F32 = jnp.float32
BF16 = jnp.bfloat16

HEAD_DIM = 64
POOL_WINDOWS = (2, 4, 8, 16)
EPS = 1e-6
LOG2E = 1.4426950408889634
SOFTPLUS_CLAMP = 64.0

LANES = 128
SUBLANES = 8
ATTN_BLOCK = 128
POOL_HIST = max(POOL_WINDOWS)
POOL_PAD = 2 * POOL_HIST
VMEM_LIMIT_BYTES = 58 * 1024 * 1024


def _silu(v):
    h = 0.5 * v
    return h + h * jnp.tanh(h)


def _first_grid_step():
    return (pl.program_id(0) == 0) & (pl.program_id(1) == 0)


def _ada_kernel(c_ref, w_ref, b_ref, o_ref):
    c = c_ref[...]
    ca = _silu(c).astype(BF16)
    o_ref[...] = jnp.dot(ca, w_ref[...].astype(BF16), preferred_element_type=F32) + b_ref[...]


def _ada_call(c, w_ada, b_ada, *, tn=512):
    bsz, d = c.shape
    n = w_ada.shape[1]
    return pl.pallas_call(
        _ada_kernel,
        grid=(n // tn,),
        in_specs=[
            pl.BlockSpec((bsz, d), lambda j: (0, 0)),
            pl.BlockSpec((d, tn), lambda j: (0, j)),
            pl.BlockSpec((1, tn), lambda j: (0, j)),
        ],
        out_specs=pl.BlockSpec((bsz, tn), lambda j: (0, j)),
        out_shape=jax.ShapeDtypeStruct((bsz, n), F32),
        compiler_params=pltpu.CompilerParams(
            dimension_semantics=("arbitrary",), vmem_limit_bytes=VMEM_LIMIT_BYTES),
        name="adaln_mod",
    )(c, w_ada, b_ada.reshape(1, n))


def _proj_kernel(mod_ref, x_ref, ng_ref, win_ref, qg_ref, kg_ref, wp_ref, bp_ref, ps_ref,
                 q_ref, k_ref, vt_ref, sg_ref, po_ref, wbf, wpbf, ubuf, sbuf_a, sbuf_b,
                 *, ts, d_model, d_attn, d_pool):
    t = pl.program_id(1)
    hist = slice(POOL_PAD - POOL_HIST, POOL_PAD)

    @pl.when(_first_grid_step())
    def _():
        wbf[...] = win_ref[...].astype(BF16)
        wpbf[...] = wp_ref[...].astype(BF16)

    @pl.when(t == 0)
    def _():
        ubuf[0:POOL_PAD, :] = jnp.zeros((POOL_PAD, d_pool), F32)

    @pl.when(t > 0)
    def _():
        ubuf[hist, :] = ubuf[ts + POOL_PAD - POOL_HIST:ts + POOL_PAD, :]

    x = x_ref[0]
    ms = jnp.mean(x * x, axis=-1, keepdims=True)
    shift = mod_ref[0, :, 0:d_model]
    scale = mod_ref[0, :, d_model:2 * d_model]
    a = ng_ref[...] * (1.0 + scale)
    hn = ((x * lax.rsqrt(ms + EPS)) * a + shift).astype(BF16)

    def proj(col, width):
        return jnp.dot(hn, wbf[:, col:col + width], preferred_element_type=F32)

    u = proj(4 * d_attn, d_pool)
    g_pool = proj(4 * d_attn + d_pool, d_pool)
    ubuf[POOL_PAD:POOL_PAD + ts, :] = u
    gdim = d_pool // len(POOL_WINDOWS)
    n_lvl = len(POOL_WINDOWS)
    end = POOL_PAD + ts
    src = ubuf
    wsums = []
    for lvl in range(n_lvl):
        shift_rows = 2 ** lvl
        start = SUBLANES * (lvl + 1)
        lanes = slice(lvl * gdim, d_pool)
        summed = src[start:end, lanes] + src[start - shift_rows:end - shift_rows, lanes]
        wsums.append(summed[POOL_PAD - start:, 0:gdim])
        if lvl + 1 < n_lvl:
            dst = sbuf_a if lvl % 2 == 0 else sbuf_b
            dst[start:end, lanes] = summed
            src = dst

    row = lax.broadcasted_iota(jnp.int32, (POOL_HIST, gdim), 0)
    for g, win in enumerate(POOL_WINDOWS):
        lo, hi = g * gdim, (g + 1) * gdim
        ug = u[:, lo:hi]
        cnt = jnp.minimum(t * ts + row + 1, win).astype(F32)
        head = wsums[g][0:POOL_HIST] / cnt - ug[0:POOL_HIST]
        rest = wsums[g][POOL_HIST:] * (1.0 / win) - ug[POOL_HIST:]
        pooled = jnp.concatenate([head, rest], axis=0).astype(BF16)
        mixed = jnp.dot(pooled, wpbf[g], preferred_element_type=F32) + bp_ref[:, lo:hi]
        po_ref[0, :, lo:hi] = (mixed * ps_ref[:, lo:hi] * _silu(g_pool[:, lo:hi])).astype(BF16)

    r = lax.broadcasted_iota(jnp.int32, (d_attn, d_attn), 0) // HEAD_DIM
    c = lax.broadcasted_iota(jnp.int32, (d_attn, d_attn), 1) // HEAD_DIM
    head_avg = jnp.where(r == c, 1.0 / HEAD_DIM, 0.0).astype(BF16)

    def head_norm(v, g):
        msq = jnp.dot((v * v).astype(BF16), head_avg, preferred_element_type=F32)
        return (v * lax.rsqrt(msq + EPS)) * g

    q_ref[0] = head_norm(proj(0, d_attn), qg_ref[...]).astype(BF16)
    k_ref[0] = head_norm(proj(d_attn, d_attn), kg_ref[...]).astype(BF16)

    vt = proj(2 * d_attn, d_attn).T.astype(BF16)
    for jj in range(ts // ATTN_BLOCK):
        vt_ref[0, jj] = vt[:, jj * ATTN_BLOCK:(jj + 1) * ATTN_BLOCK]

    sg_ref[0] = _silu(proj(3 * d_attn, d_attn)).astype(BF16)


def _proj_call(mod3, x, norm_g, w_in, qg, kg, w_pool, b_pool, pool_scale, *, ts=1024):
    bsz, s, d = x.shape
    d_attn = qg.shape[1]
    d_pool = pool_scale.shape[1]
    nt = s // ts
    nb = s // ATTN_BLOCK
    kern = functools.partial(_proj_kernel, ts=ts, d_model=d, d_attn=d_attn, d_pool=d_pool)
    const = lambda *shape: pl.BlockSpec(shape, lambda b, t: (0,) * len(shape))
    once = lambda *shape: pl.BlockSpec(shape, lambda b, t: (0,) * len(shape), pipeline_mode=pl.Buffered(1))
    seq_out = lambda width: pl.BlockSpec((1, ts, width), lambda b, t: (b, t, 0))
    pool_rows = POOL_PAD + ts
    return pl.pallas_call(
        kern,
        grid=(bsz, nt),
        in_specs=[
            pl.BlockSpec((1, 1, mod3.shape[2]), lambda b, t: (b, 0, 0)),
            pl.BlockSpec((1, ts, d), lambda b, t: (b, t, 0)),
            const(1, d),
            once(*w_in.shape),
            const(1, d_attn),
            const(1, d_attn),
            once(*w_pool.shape),
            const(1, d_pool),
            const(1, d_pool),
        ],
        out_specs=[
            seq_out(d_attn),
            seq_out(d_attn),
            pl.BlockSpec((1, ts // ATTN_BLOCK, d_attn, ATTN_BLOCK), lambda b, t: (b, t, 0, 0)),
            seq_out(d_attn),
            seq_out(d_pool),
        ],
        out_shape=[
            jax.ShapeDtypeStruct((bsz, s, d_attn), BF16),
            jax.ShapeDtypeStruct((bsz, s, d_attn), BF16),
            jax.ShapeDtypeStruct((bsz, nb, d_attn, ATTN_BLOCK), BF16),
            jax.ShapeDtypeStruct((bsz, s, d_attn), BF16),
            jax.ShapeDtypeStruct((bsz, s, d_pool), BF16),
        ],
        scratch_shapes=[pltpu.VMEM(w_in.shape, BF16),
                        pltpu.VMEM(w_pool.shape, BF16),
                        pltpu.VMEM((pool_rows, d_pool), F32),
                        pltpu.VMEM((pool_rows, d_pool), F32),
                        pltpu.VMEM((pool_rows, d_pool), F32)],
        compiler_params=pltpu.CompilerParams(
            dimension_semantics=("arbitrary", "arbitrary"), vmem_limit_bytes=VMEM_LIMIT_BYTES),
        name="norm_inproj_pool",
    )(mod3, x, norm_g, w_in, qg, kg, w_pool, b_pool, pool_scale)


def _attn_kernel(q_ref, k_ref, vt_ref, sg_ref, o_ref, lr_ref, acc_ref, z0_ref, z1_ref, sp0_ref, sp1_ref,
                 *, nblk, g_q, n_hp):
    tb = ATTN_BLOCK
    row = lax.broadcasted_iota(jnp.int32, (tb, tb), 0)
    col = lax.broadcasted_iota(jnp.int32, (tb, tb), 1)
    later_keys = (col >= row).astype(BF16)
    head_d = [(col < HEAD_DIM).astype(BF16), (col >= HEAD_DIM).astype(BF16)]
    causal = row < col
    n_grp = 2 * n_hp
    grp_w = g_q * tb
    z_buf = (z0_ref, z1_ref)
    sp_buf = (sp0_ref, sp1_ref)

    def mask_diag(v, q_lo, diag):
        if not diag:
            return v
        nq = g_q - q_lo
        parts = []
        for grp in range(n_grp):
            base = grp * nq * tb
            parts.append(jnp.where(causal, v[:, base:base + tb], 0.0))
            if nq > 1:
                parts.append(v[:, base + tb:base + nq * tb])
        return jnp.concatenate(parts, axis=1)

    def score(j, qcats, q_lo, diag):
        rows_j = pl.ds(pl.multiple_of(j * tb, tb), tb)
        zs = []
        for hp in range(n_hp):
            kb = k_ref[0, rows_j, hp * LANES:(hp + 1) * LANES]
            qc = qcats[hp]
            if q_lo:
                qc = jnp.concatenate([qc[h * grp_w + q_lo * tb:(h + 1) * grp_w] for h in range(2)], axis=0)
            zs.append(lax.dot_general(kb, qc, (((1,), (1,)), ((), ())), preferred_element_type=F32))
        z = jnp.concatenate(zs, axis=1)
        sp = jnp.maximum(jnp.log(1.0 + jnp.exp2(jnp.minimum(z, SOFTPLUS_CLAMP))) * LOG2E, z)
        return z, mask_diag(sp, q_lo, diag).astype(BF16)

    def weigh(j, z, sp, q_lo, diag):
        nq = g_q - q_lo
        lanes = [slice(grp * grp_w + q_lo * tb, (grp + 1) * grp_w) for grp in range(n_grp)]
        later = jnp.dot(later_keys, sp, preferred_element_type=F32)
        log_rem = jnp.concatenate([lr_ref[:, ln] for ln in lanes], axis=1)
        w = mask_diag(jnp.exp2(z - later + log_rem), q_lo, diag).astype(BF16)
        new_rem = log_rem - later[0:1, :]
        for grp in range(n_grp):
            hp, h = divmod(grp, 2)
            cols = slice(grp * nq * tb, (grp + 1) * nq * tb)
            lr_ref[:, lanes[grp]] = new_rem[:, cols]
            vt_h = vt_ref[0, j, hp * LANES + h * HEAD_DIM:hp * LANES + (h + 1) * HEAD_DIM, :]
            res = jnp.dot(vt_h, w[:, cols], preferred_element_type=F32)
            acc_ref[h * HEAD_DIM:(h + 1) * HEAD_DIM, hp * grp_w + q_lo * tb:(hp + 1) * grp_w] += res

    def q_super_block(sb, carry):
        qcats = []
        for hp in range(n_hp):
            rows = pl.ds(pl.multiple_of(sb * grp_w, grp_w), grp_w)
            q = q_ref[0, rows, hp * LANES:(hp + 1) * LANES]
            qcats.append(jnp.concatenate([q * jnp.concatenate([head_d[h]] * g_q, axis=0) for h in range(2)],
                                         axis=0))
        lr_ref[...] = jnp.zeros_like(lr_ref)
        acc_ref[...] = jnp.zeros_like(acc_ref)
        for m in reversed(range(g_q)):
            j = sb * g_q + m
            weigh(j, *score(j, qcats, m, True), m, True)

        n_full = sb * g_q

        def score_to(slot, j):
            z_buf[slot][...], sp_buf[slot][...] = score(j, qcats, 0, False)

        def weigh_from(slot, j):
            weigh(j, z_buf[slot][...], sp_buf[slot][...], 0, False)

        def two_steps(it, c):
            j = n_full - 1 - 2 * it
            score_to(1, j - 1)
            weigh_from(0, j)
            score_to(0, j - 2)
            weigh_from(1, j - 1)
            return c

        @pl.when(sb > 0)
        def _():
            score_to(0, n_full - 1)
            lax.fori_loop(0, n_full // 2 - 1, two_steps, 0)
            score_to(1, 0)
            weigh_from(0, 1)
            weigh_from(1, 0)

        for g in range(g_q):
            rows = pl.ds(pl.multiple_of((sb * g_q + g) * tb, tb), tb)
            for hp in range(n_hp):
                a = hp * grp_w + g * tb
                gate = sg_ref[0, rows, hp * LANES:(hp + 1) * LANES].astype(F32)
                o_ref[0, rows, hp * LANES:(hp + 1) * LANES] = (acc_ref[:, a:a + tb].T * gate).astype(BF16)
        return carry

    lax.fori_loop(0, nblk // g_q, q_super_block, 0)


def _attn_call(qn, kn, vt, sg, *, g_q=8, n_hp=4):
    bsz, s, d_attn = qn.shape
    nblk = s // ATTN_BLOCK
    assert g_q % 2 == 0 and nblk % g_q == 0
    width = n_hp * LANES
    score_lanes = g_q * n_hp * 2 * ATTN_BLOCK
    seq = pl.BlockSpec((1, s, width), lambda b, h: (b, 0, h))
    return pl.pallas_call(
        functools.partial(_attn_kernel, nblk=nblk, g_q=g_q, n_hp=n_hp),
        grid=(bsz, d_attn // width),
        in_specs=[seq, seq, pl.BlockSpec((1, nblk, width, ATTN_BLOCK), lambda b, h: (b, 0, h, 0)), seq],
        out_specs=seq,
        out_shape=jax.ShapeDtypeStruct((bsz, s, d_attn), BF16),
        scratch_shapes=[pltpu.VMEM((1, score_lanes), F32),
                        pltpu.VMEM((LANES, g_q * n_hp * ATTN_BLOCK), F32),
                        pltpu.VMEM((ATTN_BLOCK, score_lanes), F32),
                        pltpu.VMEM((ATTN_BLOCK, score_lanes), F32),
                        pltpu.VMEM((ATTN_BLOCK, score_lanes), BF16),
                        pltpu.VMEM((ATTN_BLOCK, score_lanes), BF16)],
        compiler_params=pltpu.CompilerParams(
            dimension_semantics=("arbitrary", "arbitrary"), vmem_limit_bytes=VMEM_LIMIT_BYTES),
        name="stickbreak_attn",
    )(qn, kn, vt, sg)


def _out_kernel(a_ref, p_ref, x_ref, mod_ref, w_ref, o_ref, wbf, *, d_model, d_attn):
    @pl.when(_first_grid_step())
    def _():
        wbf[...] = w_ref[...].astype(BF16)

    y = jnp.dot(a_ref[0], wbf[0:d_attn, :], preferred_element_type=F32)
    y = y + jnp.dot(p_ref[0], wbf[d_attn:, :], preferred_element_type=F32)
    gate = mod_ref[0, :, 2 * d_model:3 * d_model]
    o_ref[0] = x_ref[0] + gate * y


def _out_call(attn, pool_out, x, mod3, w_out, *, ts=1024):
    bsz, s, d = x.shape
    d_attn = attn.shape[2]
    d_pool = pool_out.shape[2]
    return pl.pallas_call(
        functools.partial(_out_kernel, d_model=d, d_attn=d_attn),
        grid=(bsz, s // ts),
        in_specs=[
            pl.BlockSpec((1, ts, d_attn), lambda b, t: (b, t, 0)),
            pl.BlockSpec((1, ts, d_pool), lambda b, t: (b, t, 0)),
            pl.BlockSpec((1, ts, d), lambda b, t: (b, t, 0)),
            pl.BlockSpec((1, 1, mod3.shape[2]), lambda b, t: (b, 0, 0)),
            pl.BlockSpec(w_out.shape, lambda b, t: (0, 0), pipeline_mode=pl.Buffered(1)),
        ],
        out_specs=pl.BlockSpec((1, ts, d), lambda b, t: (b, t, 0)),
        out_shape=jax.ShapeDtypeStruct((bsz, s, d), F32),
        scratch_shapes=[pltpu.VMEM(w_out.shape, BF16)],
        compiler_params=pltpu.CompilerParams(
            dimension_semantics=("arbitrary", "arbitrary"), vmem_limit_bytes=VMEM_LIMIT_BYTES),
        name="outproj_residual",
    )(attn, pool_out, x, mod3, w_out)


def kernel(x, c, w_ada, b_ada, norm_g, w_in, q_norm_g, k_norm_g, w_pool, b_pool, pool_scale, w_out):
    depth = w_ada.shape[0]
    d_attn = w_out.shape[1] // 2
    n_heads = d_attn // HEAD_DIM
    h = x
    for l in range(depth):
        mod = _ada_call(c, w_ada[l], b_ada[l])
        mod3 = mod.reshape(mod.shape[0], 1, mod.shape[1])
        qg = jnp.tile(q_norm_g[l] * (HEAD_DIM ** -0.5 * LOG2E), n_heads).reshape(1, d_attn)
        kg = jnp.tile(k_norm_g[l], n_heads).reshape(1, d_attn)
        qn, kn, vt, sg, pool_out = _proj_call(
            mod3, h, norm_g[l].reshape(1, -1), w_in[l], qg, kg,
            w_pool[l], b_pool[l].reshape(1, -1), pool_scale[l].reshape(1, -1))
        attn = _attn_call(qn, kn, vt, sg)
        h = _out_call(attn, pool_out, h, mod3, w_out[l])
    return h
```

```python
import functools

import jax
import jax.numpy as jnp
from jax import lax
from jax.experimental import pallas as pl
from jax.experimental.pallas import tpu as pltpu

F32 = jnp.float32
BF16 = jnp.bfloat16

HEAD_DIM = 64
POOL_WINDOWS = (2, 4, 8, 16)
EPS = 1e-6
LOG2E = 1.4426950408889634
SOFTPLUS_CLAMP = 64.0

LANES = 128
SUBLANES = 8
ATTN_BLOCK = 128
POOL_HIST = max(POOL_WINDOWS)
POOL_PAD = 2 * POOL_HIST
VMEM_LIMIT_BYTES = 58 * 1024 * 1024


def _silu(v):
    h = 0.5 * v
    return h + h * jnp.tanh(h)


def _first_grid_step():
    return (pl.program_id(0) == 0) & (pl.program_id(1) == 0)


def _ada_kernel(c_ref, w_ref, b_ref, o_ref):
    c = c_ref[...]
    ca = _silu(c).astype(BF16)
    o_ref[...] = jnp.dot(ca, w_ref[...].astype(BF16), preferred_element_type=F32) + b_ref[...]


def _ada_call(c, w_ada, b_ada, *, tn=512):
    bsz, d = c.shape
    n = w_ada.shape[1]
    return pl.pallas_call(
        _ada_kernel,
        grid=(n // tn,),
        in_specs=[
            pl.BlockSpec((bsz, d), lambda j: (0, 0)),
            pl.BlockSpec((d, tn), lambda j: (0, j)),
            pl.BlockSpec((1, tn), lambda j: (0, j)),
        ],
        out_specs=pl.BlockSpec((bsz, tn), lambda j: (0, j)),
        out_shape=jax.ShapeDtypeStruct((bsz, n), F32),
        compiler_params=pltpu.CompilerParams(
            dimension_semantics=("arbitrary",), vmem_limit_bytes=VMEM_LIMIT_BYTES),
        name="adaln_mod",
    )(c, w_ada, b_ada.reshape(1, n))


def _proj_kernel(mod_ref, x_ref, ng_ref, win_ref, qg_ref, kg_ref, wp_ref, bp_ref, ps_ref,
                 q_ref, k_ref, vt_ref, sg_ref, po_ref, wbf, wpbf, ubuf, sbuf_a, sbuf_b,
                 *, ts, d_model, d_attn, d_pool):
    t = pl.program_id(1)
    hist = slice(POOL_PAD - POOL_HIST, POOL_PAD)

    @pl.when(_first_grid_step())
    def _():
        wbf[...] = win_ref[...].astype(BF16)
        wpbf[...] = wp_ref[...].astype(BF16)

    @pl.when(t == 0)
    def _():
        ubuf[0:POOL_PAD, :] = jnp.zeros((POOL_PAD, d_pool), F32)

    @pl.when(t > 0)
    def _():
        ubuf[hist, :] = ubuf[ts + POOL_PAD - POOL_HIST:ts + POOL_PAD, :]

    x = x_ref[0]
    ms = jnp.mean(x * x, axis=-1, keepdims=True)
    shift = mod_ref[0, :, 0:d_model]
    scale = mod_ref[0, :, d_model:2 * d_model]
    a = ng_ref[...] * (1.0 + scale)
    hn = ((x * lax.rsqrt(ms + EPS)) * a + shift).astype(BF16)

    def proj(col, width):
        return jnp.dot(hn, wbf[:, col:col + width], preferred_element_type=F32)

    u = proj(4 * d_attn, d_pool)
    g_pool = proj(4 * d_attn + d_pool, d_pool)
    ubuf[POOL_PAD:POOL_PAD + ts, :] = u
    gdim = d_pool // len(POOL_WINDOWS)
    n_lvl = len(POOL_WINDOWS)
    end = POOL_PAD + ts
    src = ubuf
    wsums = []
    for lvl in range(n_lvl):
        shift_rows = 2 ** lvl
        start = SUBLANES * (lvl + 1)
        lanes = slice(lvl * gdim, d_pool)
        summed = src[start:end, lanes] + src[start - shift_rows:end - shift_rows, lanes]
        wsums.append(summed[POOL_PAD - start:, 0:gdim])
        if lvl + 1 < n_lvl:
            dst = sbuf_a if lvl % 2 == 0 else sbuf_b
            dst[start:end, lanes] = summed
            src = dst

    row = lax.broadcasted_iota(jnp.int32, (POOL_HIST, gdim), 0)
    for g, win in enumerate(POOL_WINDOWS):
        lo, hi = g * gdim, (g + 1) * gdim
        ug = u[:, lo:hi]
        cnt = jnp.minimum(t * ts + row + 1, win).astype(F32)
        head = wsums[g][0:POOL_HIST] / cnt - ug[0:POOL_HIST]
        rest = wsums[g][POOL_HIST:] * (1.0 / win) - ug[POOL_HIST:]
        pooled = jnp.concatenate([head, rest], axis=0).astype(BF16)
        mixed = jnp.dot(pooled, wpbf[g], preferred_element_type=F32) + bp_ref[:, lo:hi]
        po_ref[0, :, lo:hi] = (mixed * ps_ref[:, lo:hi] * _silu(g_pool[:, lo:hi])).astype(BF16)

    r = lax.broadcasted_iota(jnp.int32, (d_attn, d_attn), 0) // HEAD_DIM
    c = lax.broadcasted_iota(jnp.int32, (d_attn, d_attn), 1) // HEAD_DIM
    head_avg = jnp.where(r == c, 1.0 / HEAD_DIM, 0.0).astype(BF16)

    def head_norm(v, g):
        msq = jnp.dot((v * v).astype(BF16), head_avg, preferred_element_type=F32)
        return (v * lax.rsqrt(msq + EPS)) * g

    q_ref[0] = head_norm(proj(0, d_attn), qg_ref[...]).astype(BF16)
    k_ref[0] = head_norm(proj(d_attn, d_attn), kg_ref[...]).astype(BF16)

    vt = proj(2 * d_attn, d_attn).T.astype(BF16)
    for jj in range(ts // ATTN_BLOCK):
        vt_ref[0, jj] = vt[:, jj * ATTN_BLOCK:(jj + 1) * ATTN_BLOCK]

    sg_ref[0] = _silu(proj(3 * d_attn, d_attn)).astype(BF16)


def _proj_call(mod3, x, norm_g, w_in, qg, kg, w_pool, b_pool, pool_scale, *, ts=1024):
    bsz, s, d = x.shape
    d_attn = qg.shape[1]
    d_pool = pool_scale.shape[1]
    nt = s // ts
    nb = s // ATTN_BLOCK
    kern = functools.partial(_proj_kernel, ts=ts, d_model=d, d_attn=d_attn, d_pool=d_pool)
    const = lambda *shape: pl.BlockSpec(shape, lambda b, t: (0,) * len(shape))
    once = lambda *shape: pl.BlockSpec(shape, lambda b, t: (0,) * len(shape), pipeline_mode=pl.Buffered(1))
    seq_out = lambda width: pl.BlockSpec((1, ts, width), lambda b, t: (b, t, 0))
    pool_rows = POOL_PAD + ts
    return pl.pallas_call(
        kern,
        grid=(bsz, nt),
        in_specs=[
            pl.BlockSpec((1, 1, mod3.shape[2]), lambda b, t: (b, 0, 0)),
            pl.BlockSpec((1, ts, d), lambda b, t: (b, t, 0)),
            const(1, d),
            once(*w_in.shape),
            const(1, d_attn),
            const(1, d_attn),
            once(*w_pool.shape),
            const(1, d_pool),
            const(1, d_pool),
        ],
        out_specs=[
            seq_out(d_attn),
            seq_out(d_attn),
            pl.BlockSpec((1, ts // ATTN_BLOCK, d_attn, ATTN_BLOCK), lambda b, t: (b, t, 0, 0)),
            seq_out(d_attn),
            seq_out(d_pool),
        ],
        out_shape=[
            jax.ShapeDtypeStruct((bsz, s, d_attn), BF16),
            jax.ShapeDtypeStruct((bsz, s, d_attn), BF16),
            jax.ShapeDtypeStruct((bsz, nb, d_attn, ATTN_BLOCK), BF16),
            jax.ShapeDtypeStruct((bsz, s, d_attn), BF16),
            jax.ShapeDtypeStruct((bsz, s, d_pool), BF16),
        ],
        scratch_shapes=[pltpu.VMEM(w_in.shape, BF16),
                        pltpu.VMEM(w_pool.shape, BF16),
                        pltpu.VMEM((pool_rows, d_pool), F32),
                        pltpu.VMEM((pool_rows, d_pool), F32),
                        pltpu.VMEM((pool_rows, d_pool), F32)],
        compiler_params=pltpu.CompilerParams(
            dimension_semantics=("arbitrary", "arbitrary"), vmem_limit_bytes=VMEM_LIMIT_BYTES),
        name="norm_inproj_pool",
    )(mod3, x, norm_g, w_in, qg, kg, w_pool, b_pool, pool_scale)


def _attn_kernel(q_ref, k_ref, vt_ref, sg_ref, o_ref, lr_ref, acc_ref, z0_ref, z1_ref, sp0_ref, sp1_ref,
                 *, nblk, g_q, n_hp):
    tb = ATTN_BLOCK
    row = lax.broadcasted_iota(jnp.int32, (tb, tb), 0)
    col = lax.broadcasted_iota(jnp.int32, (tb, tb), 1)
    neg_later_keys = jnp.where(col >= row, -1.0, 0.0).astype(BF16)
    causal = row < col
    n_grp = 2 * n_hp
    grp_w = g_q * tb
    hp_w = 2 * grp_w
    z_buf = (z0_ref, z1_ref)
    sp_buf = (sp0_ref, sp1_ref)

    def mask_diag(v, q_lo, diag):
        if not diag:
            return v
        nq = g_q - q_lo
        parts = []
        for grp in range(n_grp):
            base = grp * nq * tb
            parts.append(jnp.where(causal, v[:, base:base + tb], 0.0))
            if nq > 1:
                parts.append(v[:, base + tb:base + nq * tb])
        return jnp.concatenate(parts, axis=1)

    def keys_of(j, hp):
        return k_ref[0, pl.ds(pl.multiple_of(j * tb, tb), tb), hp * LANES:(hp + 1) * LANES]

    def queries_of(qts, hp, q_lo):
        if not q_lo:
            return qts[hp]
        return jnp.concatenate([qts[hp][:, h * grp_w + q_lo * tb:(h + 1) * grp_w] for h in range(2)], axis=1)

    def score(j, qts, q_lo, diag):
        z = jnp.concatenate([jnp.dot(keys_of(j, hp), queries_of(qts, hp, q_lo), preferred_element_type=F32)
                             for hp in range(n_hp)], axis=1)
        sp = jnp.maximum(jnp.log(1.0 + jnp.exp2(jnp.minimum(z, SOFTPLUS_CLAMP))) * LOG2E, z)
        return z[0:1, :], mask_diag(sp, q_lo, diag).astype(BF16)

    def weigh(j, z_first, sp, qts, q_lo, diag):
        nq = g_q - q_lo
        lanes = [slice(grp * grp_w + q_lo * tb, (grp + 1) * grp_w) for grp in range(n_grp)]
        z_less_later = jnp.concatenate([
            jnp.dot(jnp.concatenate([keys_of(j, hp), neg_later_keys], axis=1),
                    jnp.concatenate([queries_of(qts, hp, q_lo), sp[:, hp * 2 * nq * tb:(hp + 1) * 2 * nq * tb]],
                                    axis=0), preferred_element_type=F32)
            for hp in range(n_hp)], axis=1)
        log_rem = jnp.concatenate([lr_ref[:, ln] for ln in lanes], axis=1)
        w = mask_diag(jnp.exp2(z_less_later + log_rem), q_lo, diag).astype(BF16)
        new_rem = log_rem - (z_first - z_less_later[0:1, :])
        for grp in range(n_grp):
            hp, h = divmod(grp, 2)
            cols = slice(grp * nq * tb, (grp + 1) * nq * tb)
            lr_ref[:, lanes[grp]] = new_rem[:, cols]
            vt_h = vt_ref[0, j, hp * LANES + h * HEAD_DIM:hp * LANES + (h + 1) * HEAD_DIM, :]
            res = jnp.dot(vt_h, w[:, cols], preferred_element_type=F32)
            acc_ref[h * HEAD_DIM:(h + 1) * HEAD_DIM, hp * grp_w + q_lo * tb:(hp + 1) * grp_w] += res

    def q_super_block(sb, carry):
        qts = []
        head_of_row = lax.broadcasted_iota(jnp.int32, (LANES, grp_w), 0) // HEAD_DIM
        for hp in range(n_hp):
            rows = pl.ds(pl.multiple_of(sb * grp_w, grp_w), grp_w)
            qt = q_ref[0, rows, hp * LANES:(hp + 1) * LANES].astype(F32).T
            qts.append(jnp.concatenate([jnp.where(head_of_row == h, qt, 0.0) for h in range(2)],
                                       axis=1).astype(BF16))
        lr_ref[...] = jnp.zeros_like(lr_ref)
        acc_ref[...] = jnp.zeros_like(acc_ref)
        for m in reversed(range(g_q)):
            j = sb * g_q + m
            weigh(j, *score(j, qts, m, True), qts, m, True)

        n_full = sb * g_q

        def score_to(slot, j):
            z_buf[slot][...], sp_buf[slot][...] = score(j, qts, 0, False)

        def weigh_from(slot, j):
            weigh(j, z_buf[slot][...], sp_buf[slot][...], qts, 0, False)

        def two_steps(it, c):
            j = n_full - 1 - 2 * it
            score_to(1, j - 1)
            weigh_from(0, j)
            score_to(0, j - 2)
            weigh_from(1, j - 1)
            return c

        @pl.when(sb > 0)
        def _():
            score_to(0, n_full - 1)
            lax.fori_loop(0, n_full // 2 - 1, two_steps, 0)
            score_to(1, 0)
            weigh_from(0, 1)
            weigh_from(1, 0)

        for g in range(g_q):
            rows = pl.ds(pl.multiple_of((sb * g_q + g) * tb, tb), tb)
            for hp in range(n_hp):
                a = hp * grp_w + g * tb
                gate = sg_ref[0, rows, hp * LANES:(hp + 1) * LANES].astype(F32)
                o_ref[0, rows, hp * LANES:(hp + 1) * LANES] = (acc_ref[:, a:a + tb].T * gate).astype(BF16)
        return carry

    lax.fori_loop(0, nblk // g_q, q_super_block, 0)


def _attn_call(qn, kn, vt, sg, *, g_q=8, n_hp=4):
    bsz, s, d_attn = qn.shape
    nblk = s // ATTN_BLOCK
    assert g_q % 2 == 0 and nblk % g_q == 0
    width = n_hp * LANES
    score_lanes = g_q * n_hp * 2 * ATTN_BLOCK
    seq = pl.BlockSpec((1, s, width), lambda b, h: (b, 0, h))
    return pl.pallas_call(
        functools.partial(_attn_kernel, nblk=nblk, g_q=g_q, n_hp=n_hp),
        grid=(bsz, d_attn // width),
        in_specs=[seq, seq, pl.BlockSpec((1, nblk, width, ATTN_BLOCK), lambda b, h: (b, 0, h, 0)), seq],
        out_specs=seq,
        out_shape=jax.ShapeDtypeStruct((bsz, s, d_attn), BF16),
        scratch_shapes=[pltpu.VMEM((1, score_lanes), F32),
                        pltpu.VMEM((LANES, g_q * n_hp * ATTN_BLOCK), F32),
                        pltpu.VMEM((1, score_lanes), F32),
                        pltpu.VMEM((1, score_lanes), F32),
                        pltpu.VMEM((ATTN_BLOCK, score_lanes), BF16),
                        pltpu.VMEM((ATTN_BLOCK, score_lanes), BF16)],
        compiler_params=pltpu.CompilerParams(
            dimension_semantics=("arbitrary", "arbitrary"), vmem_limit_bytes=VMEM_LIMIT_BYTES),
        name="stickbreak_attn",
    )(qn, kn, vt, sg)


def _out_kernel(a_ref, p_ref, x_ref, mod_ref, w_ref, o_ref, wbf, *, d_model, d_attn):
    @pl.when(_first_grid_step())
    def _():
        wbf[...] = w_ref[...].astype(BF16)

    y = jnp.dot(a_ref[0], wbf[0:d_attn, :], preferred_element_type=F32)
    y = y + jnp.dot(p_ref[0], wbf[d_attn:, :], preferred_element_type=F32)
    gate = mod_ref[0, :, 2 * d_model:3 * d_model]
    o_ref[0] = x_ref[0] + gate * y


def _out_call(attn, pool_out, x, mod3, w_out, *, ts=1024):
    bsz, s, d = x.shape
    d_attn = attn.shape[2]
    d_pool = pool_out.shape[2]
    return pl.pallas_call(
        functools.partial(_out_kernel, d_model=d, d_attn=d_attn),
        grid=(bsz, s // ts),
        in_specs=[
            pl.BlockSpec((1, ts, d_attn), lambda b, t: (b, t, 0)),
            pl.BlockSpec((1, ts, d_pool), lambda b, t: (b, t, 0)),
            pl.BlockSpec((1, ts, d), lambda b, t: (b, t, 0)),
            pl.BlockSpec((1, 1, mod3.shape[2]), lambda b, t: (b, 0, 0)),
            pl.BlockSpec(w_out.shape, lambda b, t: (0, 0), pipeline_mode=pl.Buffered(1)),
        ],
        out_specs=pl.BlockSpec((1, ts, d), lambda b, t: (b, t, 0)),
        out_shape=jax.ShapeDtypeStruct((bsz, s, d), F32),
        scratch_shapes=[pltpu.VMEM(w_out.shape, BF16)],
        compiler_params=pltpu.CompilerParams(
            dimension_semantics=("arbitrary", "arbitrary"), vmem_limit_bytes=VMEM_LIMIT_BYTES),
        name="outproj_residual",
    )(attn, pool_out, x, mod3, w_out)


def kernel(x, c, w_ada, b_ada, norm_g, w_in, q_norm_g, k_norm_g, w_pool, b_pool, pool_scale, w_out):
    depth = w_ada.shape[0]
    d_attn = w_out.shape[1] // 2
    n_heads = d_attn // HEAD_DIM
    h = x
    for l in range(depth):
        mod = _ada_call(c, w_ada[l], b_ada[l])
        mod3 = mod.reshape(mod.shape[0], 1, mod.shape[1])
        qg = jnp.tile(q_norm_g[l] * (HEAD_DIM ** -0.5 * LOG2E), n_heads).reshape(1, d_attn)
        kg = jnp.tile(k_norm_g[l], n_heads).reshape(1, d_attn)
        qn, kn, vt, sg, pool_out = _proj_call(
            mod3, h, norm_g[l].reshape(1, -1), w_in[l], qg, kg,
            w_pool[l], b_pool[l].reshape(1, -1), pool_scale[l].reshape(1, -1))
        attn = _attn_call(qn, kn, vt, sg)
        h = _out_call(attn, pool_out, h, mod3, w_out[l])
    return h
```

```python
import functools

import jax
import jax.numpy as jnp
from jax import lax
from jax.experimental import pallas as pl
from jax.experimental.pallas import tpu as pltpu

F32 = jnp.float32
BF16 = jnp.bfloat16

HEAD_DIM = 64
POOL_WINDOWS = (2, 4, 8, 16)
EPS = 1e-6
LOG2E = 1.4426950408889634
SOFTPLUS_CLAMP = 64.0
REM_FLOOR_LOG2 = -160.0

LANES = 128
SUBLANES = 8
ATTN_BLOCK = 128
POOL_HIST = max(POOL_WINDOWS)
POOL_PAD = 2 * POOL_HIST
VMEM_LIMIT_BYTES = 58 * 1024 * 1024


def _silu(v):
    h = 0.5 * v
    return h + h * jnp.tanh(h)


def _first_grid_step():
    return (pl.program_id(0) == 0) & (pl.program_id(1) == 0)


def _ada_kernel(c_ref, w_ref, b_ref, o_ref):
    c = c_ref[...]
    ca = _silu(c).astype(BF16)
    o_ref[...] = jnp.dot(ca, w_ref[...].astype(BF16), preferred_element_type=F32) + b_ref[...]


def _ada_call(c, w_ada, b_ada, *, tn=512):
    bsz, d = c.shape
    n = w_ada.shape[1]
    return pl.pallas_call(
        _ada_kernel,
        grid=(n // tn,),
        in_specs=[
            pl.BlockSpec((bsz, d), lambda j: (0, 0)),
            pl.BlockSpec((d, tn), lambda j: (0, j)),
            pl.BlockSpec((1, tn), lambda j: (0, j)),
        ],
        out_specs=pl.BlockSpec((bsz, tn), lambda j: (0, j)),
        out_shape=jax.ShapeDtypeStruct((bsz, n), F32),
        compiler_params=pltpu.CompilerParams(
            dimension_semantics=("arbitrary",), vmem_limit_bytes=VMEM_LIMIT_BYTES),
        name="adaln_mod",
    )(c, w_ada, b_ada.reshape(1, n))


def _proj_kernel(mod_ref, x_ref, ng_ref, win_ref, qg_ref, kg_ref, wp_ref, bp_ref, ps_ref,
                 q_ref, k_ref, vt_ref, sg_ref, po_ref, wbf, wpbf, ubuf, sbuf_a, sbuf_b,
                 *, ts, d_model, d_attn, d_pool):
    t = pl.program_id(1)
    hist = slice(POOL_PAD - POOL_HIST, POOL_PAD)

    @pl.when(_first_grid_step())
    def _():
        wbf[...] = win_ref[...].astype(BF16)
        wpbf[...] = wp_ref[...].astype(BF16)

    @pl.when(t == 0)
    def _():
        ubuf[0:POOL_PAD, :] = jnp.zeros((POOL_PAD, d_pool), F32)

    @pl.when(t > 0)
    def _():
        ubuf[hist, :] = ubuf[ts + POOL_PAD - POOL_HIST:ts + POOL_PAD, :]

    x = x_ref[0]
    ms = jnp.mean(x * x, axis=-1, keepdims=True)
    shift = mod_ref[0, :, 0:d_model]
    scale = mod_ref[0, :, d_model:2 * d_model]
    a = ng_ref[...] * (1.0 + scale)
    hn = ((x * lax.rsqrt(ms + EPS)) * a + shift).astype(BF16)

    def proj(col, width):
        return jnp.dot(hn, wbf[:, col:col + width], preferred_element_type=F32)

    u = proj(4 * d_attn, d_pool)
    g_pool = proj(4 * d_attn + d_pool, d_pool)
    ubuf[POOL_PAD:POOL_PAD + ts, :] = u
    gdim = d_pool // len(POOL_WINDOWS)
    n_lvl = len(POOL_WINDOWS)
    end = POOL_PAD + ts
    src = ubuf
    wsums = []
    for lvl in range(n_lvl):
        shift_rows = 2 ** lvl
        start = SUBLANES * (lvl + 1)
        lanes = slice(lvl * gdim, d_pool)
        summed = src[start:end, lanes] + src[start - shift_rows:end - shift_rows, lanes]
        wsums.append(summed[POOL_PAD - start:, 0:gdim])
        if lvl + 1 < n_lvl:
            dst = sbuf_a if lvl % 2 == 0 else sbuf_b
            dst[start:end, lanes] = summed
            src = dst

    row = lax.broadcasted_iota(jnp.int32, (POOL_HIST, gdim), 0)
    for g, win in enumerate(POOL_WINDOWS):
        lo, hi = g * gdim, (g + 1) * gdim
        ug = u[:, lo:hi]
        cnt = jnp.minimum(t * ts + row + 1, win).astype(F32)
        head = wsums[g][0:POOL_HIST] / cnt - ug[0:POOL_HIST]
        rest = wsums[g][POOL_HIST:] * (1.0 / win) - ug[POOL_HIST:]
        pooled = jnp.concatenate([head, rest], axis=0).astype(BF16)
        mixed = jnp.dot(pooled, wpbf[g], preferred_element_type=F32) + bp_ref[:, lo:hi]
        po_ref[0, :, lo:hi] = (mixed * ps_ref[:, lo:hi] * _silu(g_pool[:, lo:hi])).astype(BF16)

    r = lax.broadcasted_iota(jnp.int32, (d_attn, d_attn), 0) // HEAD_DIM
    c = lax.broadcasted_iota(jnp.int32, (d_attn, d_attn), 1) // HEAD_DIM
    head_avg = jnp.where(r == c, 1.0 / HEAD_DIM, 0.0).astype(BF16)

    def head_norm(v, g):
        msq = jnp.dot((v * v).astype(BF16), head_avg, preferred_element_type=F32)
        return (v * lax.rsqrt(msq + EPS)) * g

    q_ref[0] = head_norm(proj(0, d_attn), qg_ref[...]).astype(BF16)
    k_ref[0] = head_norm(proj(d_attn, d_attn), kg_ref[...]).astype(BF16)

    vt = proj(2 * d_attn, d_attn).T.astype(BF16)
    for jj in range(ts // ATTN_BLOCK):
        vt_ref[0, jj] = vt[:, jj * ATTN_BLOCK:(jj + 1) * ATTN_BLOCK]

    sg_ref[0] = _silu(proj(3 * d_attn, d_attn)).astype(BF16)


def _proj_call(mod3, x, norm_g, w_in, qg, kg, w_pool, b_pool, pool_scale, *, ts=1024):
    bsz, s, d = x.shape
    d_attn = qg.shape[1]
    d_pool = pool_scale.shape[1]
    nt = s // ts
    nb = s // ATTN_BLOCK
    kern = functools.partial(_proj_kernel, ts=ts, d_model=d, d_attn=d_attn, d_pool=d_pool)
    const = lambda *shape: pl.BlockSpec(shape, lambda b, t: (0,) * len(shape))
    once = lambda *shape: pl.BlockSpec(shape, lambda b, t: (0,) * len(shape), pipeline_mode=pl.Buffered(1))
    seq_out = lambda width: pl.BlockSpec((1, ts, width), lambda b, t: (b, t, 0))
    pool_rows = POOL_PAD + ts
    return pl.pallas_call(
        kern,
        grid=(bsz, nt),
        in_specs=[
            pl.BlockSpec((1, 1, mod3.shape[2]), lambda b, t: (b, 0, 0)),
            pl.BlockSpec((1, ts, d), lambda b, t: (b, t, 0)),
            const(1, d),
            once(*w_in.shape),
            const(1, d_attn),
            const(1, d_attn),
            once(*w_pool.shape),
            const(1, d_pool),
            const(1, d_pool),
        ],
        out_specs=[
            seq_out(d_attn),
            seq_out(d_attn),
            pl.BlockSpec((1, ts // ATTN_BLOCK, d_attn, ATTN_BLOCK), lambda b, t: (b, t, 0, 0)),
            seq_out(d_attn),
            seq_out(d_pool),
        ],
        out_shape=[
            jax.ShapeDtypeStruct((bsz, s, d_attn), BF16),
            jax.ShapeDtypeStruct((bsz, s, d_attn), BF16),
            jax.ShapeDtypeStruct((bsz, nb, d_attn, ATTN_BLOCK), BF16),
            jax.ShapeDtypeStruct((bsz, s, d_attn), BF16),
            jax.ShapeDtypeStruct((bsz, s, d_pool), BF16),
        ],
        scratch_shapes=[pltpu.VMEM(w_in.shape, BF16),
                        pltpu.VMEM(w_pool.shape, BF16),
                        pltpu.VMEM((pool_rows, d_pool), F32),
                        pltpu.VMEM((pool_rows, d_pool), F32),
                        pltpu.VMEM((pool_rows, d_pool), F32)],
        compiler_params=pltpu.CompilerParams(
            dimension_semantics=("arbitrary", "arbitrary"), vmem_limit_bytes=VMEM_LIMIT_BYTES),
        name="norm_inproj_pool",
    )(mod3, x, norm_g, w_in, qg, kg, w_pool, b_pool, pool_scale)


def _attn_kernel(q_ref, k_ref, vt_ref, sg_ref, o_ref, lr_ref, acc_ref, *, nblk, g_q, n_hp):
    tb = ATTN_BLOCK
    row = lax.broadcasted_iota(jnp.int32, (tb, tb), 0)
    col = lax.broadcasted_iota(jnp.int32, (tb, tb), 1)
    neg_later_keys = jnp.where(col >= row, -1.0, 0.0).astype(BF16)
    causal = row < col
    n_grp = 2 * n_hp
    grp_w = g_q * tb

    def mask_diag(v, q_lo, diag):
        if not diag:
            return v
        nq = g_q - q_lo
        parts = []
        for grp in range(n_grp):
            base = grp * nq * tb
            parts.append(jnp.where(causal, v[:, base:base + tb], 0.0))
            if nq > 1:
                parts.append(v[:, base + tb:base + nq * tb])
        return jnp.concatenate(parts, axis=1)

    def keys_of(j, hp):
        return k_ref[0, pl.ds(pl.multiple_of(j * tb, tb), tb), hp * LANES:(hp + 1) * LANES]

    def queries_of(qts, hp, q_lo):
        if not q_lo:
            return qts[hp]
        return jnp.concatenate([qts[hp][:, h * grp_w + q_lo * tb:(h + 1) * grp_w] for h in range(2)], axis=1)

    def score(j, qts, q_lo, diag):
        z = jnp.concatenate([jnp.dot(keys_of(j, hp), queries_of(qts, hp, q_lo), preferred_element_type=F32)
                             for hp in range(n_hp)], axis=1)
        sp = jnp.maximum(jnp.log(1.0 + jnp.exp2(jnp.minimum(z, SOFTPLUS_CLAMP))) * LOG2E, z)
        return z[0:1, :], mask_diag(sp, q_lo, diag).astype(BF16)

    def weigh(j, z_first, sp, qts, q_lo, diag):
        nq = g_q - q_lo
        lanes = [slice(grp * grp_w + q_lo * tb, (grp + 1) * grp_w) for grp in range(n_grp)]
        z_less_later = jnp.concatenate([
            jnp.dot(jnp.concatenate([keys_of(j, hp), neg_later_keys], axis=1),
                    jnp.concatenate([queries_of(qts, hp, q_lo), sp[:, hp * 2 * nq * tb:(hp + 1) * 2 * nq * tb]],
                                    axis=0), preferred_element_type=F32)
            for hp in range(n_hp)], axis=1)
        log_rem = jnp.concatenate([lr_ref[:, ln] for ln in lanes], axis=1)
        w = mask_diag(jnp.exp2(z_less_later + log_rem), q_lo, diag).astype(BF16)
        new_rem = log_rem - (z_first - z_less_later[0:1, :])
        for grp in range(n_grp):
            hp, h = divmod(grp, 2)
            cols = slice(grp * nq * tb, (grp + 1) * nq * tb)
            lr_ref[:, lanes[grp]] = new_rem[:, cols]
            vt_h = vt_ref[0, j, hp * LANES + h * HEAD_DIM:hp * LANES + (h + 1) * HEAD_DIM, :]
            res = jnp.dot(vt_h, w[:, cols], preferred_element_type=F32)
            acc_ref[h * HEAD_DIM:(h + 1) * HEAD_DIM, hp * grp_w + q_lo * tb:(hp + 1) * grp_w] += res
        return jnp.max(new_rem)

    def q_super_block(sb, carry):
        qts = []
        head_of_row = lax.broadcasted_iota(jnp.int32, (LANES, grp_w), 0) // HEAD_DIM
        for hp in range(n_hp):
            rows = pl.ds(pl.multiple_of(sb * grp_w, grp_w), grp_w)
            qt = q_ref[0, rows, hp * LANES:(hp + 1) * LANES].astype(F32).T
            qts.append(jnp.concatenate([jnp.where(head_of_row == h, qt, 0.0) for h in range(2)],
                                       axis=1).astype(BF16))
        lr_ref[...] = jnp.zeros_like(lr_ref)
        acc_ref[...] = jnp.zeros_like(acc_ref)
        for m in reversed(range(g_q)):
            j = sb * g_q + m
            weigh(j, *score(j, qts, m, True), qts, m, True)

        def full_step(state):
            j, _ = state
            return j - 1, weigh(j, *score(j, qts, 0, False), qts, 0, False)

        def more(state):
            j, max_rem = state
            return (j >= 0) & (max_rem > REM_FLOOR_LOG2)

        lax.while_loop(more, full_step, (sb * g_q - 1, jnp.max(lr_ref[...])))

        for g in range(g_q):
            rows = pl.ds(pl.multiple_of((sb * g_q + g) * tb, tb), tb)
            for hp in range(n_hp):
                a = hp * grp_w + g * tb
                gate = sg_ref[0, rows, hp * LANES:(hp + 1) * LANES].astype(F32)
                o_ref[0, rows, hp * LANES:(hp + 1) * LANES] = (acc_ref[:, a:a + tb].T * gate).astype(BF16)
        return carry

    lax.fori_loop(0, nblk // g_q, q_super_block, 0)


def _attn_call(qn, kn, vt, sg, *, g_q=4, n_hp=4):
    bsz, s, d_attn = qn.shape
    nblk = s // ATTN_BLOCK
    assert nblk % g_q == 0
    width = n_hp * LANES
    score_lanes = g_q * n_hp * 2 * ATTN_BLOCK
    seq = pl.BlockSpec((1, s, width), lambda b, h: (b, 0, h))
    return pl.pallas_call(
        functools.partial(_attn_kernel, nblk=nblk, g_q=g_q, n_hp=n_hp),
        grid=(bsz, d_attn // width),
        in_specs=[seq, seq, pl.BlockSpec((1, nblk, width, ATTN_BLOCK), lambda b, h: (b, 0, h, 0)), seq],
        out_specs=seq,
        out_shape=jax.ShapeDtypeStruct((bsz, s, d_attn), BF16),
        scratch_shapes=[pltpu.VMEM((1, score_lanes), F32),
                        pltpu.VMEM((LANES, g_q * n_hp * ATTN_BLOCK), F32)],
        compiler_params=pltpu.CompilerParams(
            dimension_semantics=("arbitrary", "arbitrary"), vmem_limit_bytes=VMEM_LIMIT_BYTES),
        name="stickbreak_attn",
    )(qn, kn, vt, sg)


def _out_kernel(a_ref, p_ref, x_ref, mod_ref, w_ref, o_ref, wbf, *, d_model, d_attn):
    @pl.when(_first_grid_step())
    def _():
        wbf[...] = w_ref[...].astype(BF16)

    y = jnp.dot(a_ref[0], wbf[0:d_attn, :], preferred_element_type=F32)
    y = y + jnp.dot(p_ref[0], wbf[d_attn:, :], preferred_element_type=F32)
    gate = mod_ref[0, :, 2 * d_model:3 * d_model]
    o_ref[0] = x_ref[0] + gate * y


def _out_call(attn, pool_out, x, mod3, w_out, *, ts=1024):
    bsz, s, d = x.shape
    d_attn = attn.shape[2]
    d_pool = pool_out.shape[2]
    return pl.pallas_call(
        functools.partial(_out_kernel, d_model=d, d_attn=d_attn),
        grid=(bsz, s // ts),
        in_specs=[
            pl.BlockSpec((1, ts, d_attn), lambda b, t: (b, t, 0)),
            pl.BlockSpec((1, ts, d_pool), lambda b, t: (b, t, 0)),
            pl.BlockSpec((1, ts, d), lambda b, t: (b, t, 0)),
            pl.BlockSpec((1, 1, mod3.shape[2]), lambda b, t: (b, 0, 0)),
            pl.BlockSpec(w_out.shape, lambda b, t: (0, 0), pipeline_mode=pl.Buffered(1)),
        ],
        out_specs=pl.BlockSpec((1, ts, d), lambda b, t: (b, t, 0)),
        out_shape=jax.ShapeDtypeStruct((bsz, s, d), F32),
        scratch_shapes=[pltpu.VMEM(w_out.shape, BF16)],
        compiler_params=pltpu.CompilerParams(
            dimension_semantics=("arbitrary", "arbitrary"), vmem_limit_bytes=VMEM_LIMIT_BYTES),
        name="outproj_residual",
    )(attn, pool_out, x, mod3, w_out)


def kernel(x, c, w_ada, b_ada, norm_g, w_in, q_norm_g, k_norm_g, w_pool, b_pool, pool_scale, w_out):
    depth = w_ada.shape[0]
    d_attn = w_out.shape[1] // 2
    n_heads = d_attn // HEAD_DIM
    h = x
    for l in range(depth):
        mod = _ada_call(c, w_ada[l], b_ada[l])
        mod3 = mod.reshape(mod.shape[0], 1, mod.shape[1])
        qg = jnp.tile(q_norm_g[l] * (HEAD_DIM ** -0.5 * LOG2E), n_heads).reshape(1, d_attn)
        kg = jnp.tile(k_norm_g[l], n_heads).reshape(1, d_attn)
        qn, kn, vt, sg, pool_out = _proj_call(
            mod3, h, norm_g[l].reshape(1, -1), w_in[l], qg, kg,
            w_pool[l], b_pool[l].reshape(1, -1), pool_scale[l].reshape(1, -1))
        attn = _attn_call(qn, kn, vt, sg)
        h = _out_call(attn, pool_out, h, mod3, w_out[l])
    return h
```

```python
import functools

import jax
import jax.numpy as jnp
from jax import lax
from jax.experimental import pallas as pl
from jax.experimental.pallas import tpu as pltpu

F32 = jnp.float32
BF16 = jnp.bfloat16

HEAD_DIM = 64
POOL_WINDOWS = (2, 4, 8, 16)
EPS = 1e-6
LOG2E = 1.4426950408889634
SOFTPLUS_CLAMP = 64.0
REM_FLOOR_LOG2 = -160.0

LANES = 128
SUBLANES = 8
ATTN_BLOCK = 128
POOL_HIST = max(POOL_WINDOWS)
POOL_PAD = 2 * POOL_HIST
VMEM_LIMIT_BYTES = 58 * 1024 * 1024


def _silu(v):
    h = 0.5 * v
    return h + h * jnp.tanh(h)


def _first_grid_step():
    return (pl.program_id(0) == 0) & (pl.program_id(1) == 0)


def _ada_kernel(c_ref, w_ref, b_ref, o_ref):
    c = c_ref[...]
    ca = _silu(c).astype(BF16)
    o_ref[...] = jnp.dot(ca, w_ref[...].astype(BF16), preferred_element_type=F32) + b_ref[...]


def _ada_call(c, w_ada, b_ada, *, tn=512):
    bsz, d = c.shape
    n = w_ada.shape[1]
    return pl.pallas_call(
        _ada_kernel,
        grid=(n // tn,),
        in_specs=[
            pl.BlockSpec((bsz, d), lambda j: (0, 0)),
            pl.BlockSpec((d, tn), lambda j: (0, j)),
            pl.BlockSpec((1, tn), lambda j: (0, j)),
        ],
        out_specs=pl.BlockSpec((bsz, tn), lambda j: (0, j)),
        out_shape=jax.ShapeDtypeStruct((bsz, n), F32),
        compiler_params=pltpu.CompilerParams(
            dimension_semantics=("arbitrary",), vmem_limit_bytes=VMEM_LIMIT_BYTES),
        name="adaln_mod",
    )(c, w_ada, b_ada.reshape(1, n))


def _proj_kernel(mod_ref, x_ref, ng_ref, win_ref, qg_ref, kg_ref, wp_ref, bp_ref, ps_ref,
                 q_ref, k_ref, vt_ref, sg_ref, po_ref, wbf, wpbf, ubuf, sbuf_a, sbuf_b,
                 *, ts, d_model, d_attn, d_pool):
    t = pl.program_id(1)
    hist = slice(POOL_PAD - POOL_HIST, POOL_PAD)

    @pl.when(_first_grid_step())
    def _():
        wbf[...] = win_ref[...].astype(BF16)
        wpbf[...] = wp_ref[...].astype(BF16)

    @pl.when(t == 0)
    def _():
        ubuf[0:POOL_PAD, :] = jnp.zeros((POOL_PAD, d_pool), F32)

    @pl.when(t > 0)
    def _():
        ubuf[hist, :] = ubuf[ts + POOL_PAD - POOL_HIST:ts + POOL_PAD, :]

    x = x_ref[0]
    ms = jnp.mean(x * x, axis=-1, keepdims=True)
    shift = mod_ref[0, :, 0:d_model]
    scale = mod_ref[0, :, d_model:2 * d_model]
    a = ng_ref[...] * (1.0 + scale)
    hn = ((x * lax.rsqrt(ms + EPS)) * a + shift).astype(BF16)

    def proj(col, width):
        return jnp.dot(hn, wbf[:, col:col + width], preferred_element_type=F32)

    u = proj(4 * d_attn, d_pool)
    g_pool = proj(4 * d_attn + d_pool, d_pool)
    ubuf[POOL_PAD:POOL_PAD + ts, :] = u
    gdim = d_pool // len(POOL_WINDOWS)
    n_lvl = len(POOL_WINDOWS)
    end = POOL_PAD + ts
    src = ubuf
    wsums = []
    for lvl in range(n_lvl):
        shift_rows = 2 ** lvl
        start = SUBLANES * (lvl + 1)
        lanes = slice(lvl * gdim, d_pool)
        summed = src[start:end, lanes] + src[start - shift_rows:end - shift_rows, lanes]
        wsums.append(summed[POOL_PAD - start:, 0:gdim])
        if lvl + 1 < n_lvl:
            dst = sbuf_a if lvl % 2 == 0 else sbuf_b
            dst[start:end, lanes] = summed
            src = dst

    row = lax.broadcasted_iota(jnp.int32, (POOL_HIST, gdim), 0)
    for g, win in enumerate(POOL_WINDOWS):
        lo, hi = g * gdim, (g + 1) * gdim
        ug = u[:, lo:hi]
        cnt = jnp.minimum(t * ts + row + 1, win).astype(F32)
        head = wsums[g][0:POOL_HIST] / cnt - ug[0:POOL_HIST]
        rest = wsums[g][POOL_HIST:] * (1.0 / win) - ug[POOL_HIST:]
        pooled = jnp.concatenate([head, rest], axis=0).astype(BF16)
        mixed = jnp.dot(pooled, wpbf[g], preferred_element_type=F32) + bp_ref[:, lo:hi]
        po_ref[0, :, lo:hi] = (mixed * ps_ref[:, lo:hi] * _silu(g_pool[:, lo:hi])).astype(BF16)

    r = lax.broadcasted_iota(jnp.int32, (d_attn, d_attn), 0) // HEAD_DIM
    c = lax.broadcasted_iota(jnp.int32, (d_attn, d_attn), 1) // HEAD_DIM
    head_avg = jnp.where(r == c, 1.0 / HEAD_DIM, 0.0).astype(BF16)

    def head_norm(v, g):
        msq = jnp.dot((v * v).astype(BF16), head_avg, preferred_element_type=F32)
        return (v * lax.rsqrt(msq + EPS)) * g

    q_ref[0] = head_norm(proj(0, d_attn), qg_ref[...]).astype(BF16)
    k_ref[0] = head_norm(proj(d_attn, d_attn), kg_ref[...]).astype(BF16)

    vt = proj(2 * d_attn, d_attn).T.astype(BF16)
    for jj in range(ts // ATTN_BLOCK):
        vt_ref[0, jj] = vt[:, jj * ATTN_BLOCK:(jj + 1) * ATTN_BLOCK]

    sg_ref[0] = _silu(proj(3 * d_attn, d_attn)).astype(BF16)


def _proj_call(mod3, x, norm_g, w_in, qg, kg, w_pool, b_pool, pool_scale, *, ts=1024):
    bsz, s, d = x.shape
    d_attn = qg.shape[1]
    d_pool = pool_scale.shape[1]
    nt = s // ts
    nb = s // ATTN_BLOCK
    kern = functools.partial(_proj_kernel, ts=ts, d_model=d, d_attn=d_attn, d_pool=d_pool)
    const = lambda *shape: pl.BlockSpec(shape, lambda b, t: (0,) * len(shape))
    once = lambda *shape: pl.BlockSpec(shape, lambda b, t: (0,) * len(shape), pipeline_mode=pl.Buffered(1))
    seq_out = lambda width: pl.BlockSpec((1, ts, width), lambda b, t: (b, t, 0))
    pool_rows = POOL_PAD + ts
    return pl.pallas_call(
        kern,
        grid=(bsz, nt),
        in_specs=[
            pl.BlockSpec((1, 1, mod3.shape[2]), lambda b, t: (b, 0, 0)),
            pl.BlockSpec((1, ts, d), lambda b, t: (b, t, 0)),
            const(1, d),
            once(*w_in.shape),
            const(1, d_attn),
            const(1, d_attn),
            once(*w_pool.shape),
            const(1, d_pool),
            const(1, d_pool),
        ],
        out_specs=[
            seq_out(d_attn),
            seq_out(d_attn),
            pl.BlockSpec((1, ts // ATTN_BLOCK, d_attn, ATTN_BLOCK), lambda b, t: (b, t, 0, 0)),
            seq_out(d_attn),
            seq_out(d_pool),
        ],
        out_shape=[
            jax.ShapeDtypeStruct((bsz, s, d_attn), BF16),
            jax.ShapeDtypeStruct((bsz, s, d_attn), BF16),
            jax.ShapeDtypeStruct((bsz, nb, d_attn, ATTN_BLOCK), BF16),
            jax.ShapeDtypeStruct((bsz, s, d_attn), BF16),
            jax.ShapeDtypeStruct((bsz, s, d_pool), BF16),
        ],
        scratch_shapes=[pltpu.VMEM(w_in.shape, BF16),
                        pltpu.VMEM(w_pool.shape, BF16),
                        pltpu.VMEM((pool_rows, d_pool), F32),
                        pltpu.VMEM((pool_rows, d_pool), F32),
                        pltpu.VMEM((pool_rows, d_pool), F32)],
        compiler_params=pltpu.CompilerParams(
            dimension_semantics=("arbitrary", "arbitrary"), vmem_limit_bytes=VMEM_LIMIT_BYTES),
        name="norm_inproj_pool",
    )(mod3, x, norm_g, w_in, qg, kg, w_pool, b_pool, pool_scale)


def _attn_kernel(q_ref, k_ref, vt_ref, sg_ref, o_ref, lr_ref, acc_ref, *, nblk, g_q, n_hp):
    tb = ATTN_BLOCK
    row = lax.broadcasted_iota(jnp.int32, (tb, tb), 0)
    col = lax.broadcasted_iota(jnp.int32, (tb, tb), 1)
    neg_suffix = jnp.concatenate([jnp.where(col > row, -1.0, 0.0), jnp.where(col == row, -1.0, 0.0)],
                                 axis=1).astype(BF16)
    causal = row < col
    n_grp = 2 * n_hp
    grp_w = g_q * tb

    def mask_diag(v, q_lo, diag):
        if not diag:
            return v
        nq = g_q - q_lo
        keep = causal[0:v.shape[0]]
        parts = []
        for grp in range(n_grp):
            base = grp * nq * tb
            parts.append(jnp.where(keep, v[:, base:base + tb], 0.0))
            if nq > 1:
                parts.append(v[:, base + tb:base + nq * tb])
        return jnp.concatenate(parts, axis=1)

    def step(j, qts, q_lo, diag):
        nq = g_q - q_lo
        lanes = [slice(grp * grp_w + q_lo * tb, (grp + 1) * grp_w) for grp in range(n_grp)]
        rows_j = pl.ds(pl.multiple_of(j * tb, tb), tb)
        zs = []
        for hp in range(n_hp):
            qt = qts[hp]
            if q_lo:
                qt = jnp.concatenate([qt[:, h * grp_w + q_lo * tb:(h + 1) * grp_w] for h in range(2)], axis=1)
            zs.append(jnp.dot(k_ref[0, rows_j, hp * LANES:(hp + 1) * LANES], qt, preferred_element_type=F32))
        z = jnp.concatenate(zs, axis=1)
        sp = jnp.maximum(jnp.log(1.0 + jnp.exp2(jnp.minimum(z, SOFTPLUS_CLAMP))) * LOG2E, z)
        neg_log_beta = mask_diag(sp - z, q_lo, diag)
        sp = mask_diag(sp, q_lo, diag)
        terms = jnp.concatenate([sp.astype(BF16), neg_log_beta.astype(BF16)], axis=0)
        log_w = jnp.dot(neg_suffix, terms, preferred_element_type=F32)
        log_rem = jnp.concatenate([lr_ref[:, ln] for ln in lanes], axis=1)
        w = mask_diag(jnp.exp2(log_w + log_rem), q_lo, diag).astype(BF16)
        later_0 = -(log_w[0:1, :] + terms[tb:tb + 1, :].astype(F32))
        new_rem = log_rem - (sp[0:1, :] + later_0)
        for grp in range(n_grp):
            hp, h = divmod(grp, 2)
            cols = slice(grp * nq * tb, (grp + 1) * nq * tb)
            lr_ref[:, lanes[grp]] = new_rem[:, cols]
            vt_h = vt_ref[0, j, hp * LANES + h * HEAD_DIM:hp * LANES + (h + 1) * HEAD_DIM, :]
            res = jnp.dot(vt_h, w[:, cols], preferred_element_type=F32)
            acc_ref[h * HEAD_DIM:(h + 1) * HEAD_DIM, hp * grp_w + q_lo * tb:(hp + 1) * grp_w] += res
        return jnp.max(new_rem)

    def q_super_block(sb, carry):
        qts = []
        head_of_row = lax.broadcasted_iota(jnp.int32, (LANES, grp_w), 0) // HEAD_DIM
        for hp in range(n_hp):
            rows = pl.ds(pl.multiple_of(sb * grp_w, grp_w), grp_w)
            qt = q_ref[0, rows, hp * LANES:(hp + 1) * LANES].astype(F32).T
            qts.append(jnp.concatenate([jnp.where(head_of_row == h, qt, 0.0) for h in range(2)],
                                       axis=1).astype(BF16))
        lr_ref[...] = jnp.zeros_like(lr_ref)
        acc_ref[...] = jnp.zeros_like(acc_ref)
        for m in reversed(range(g_q)):
            step(sb * g_q + m, qts, m, True)

        def full_step(state):
            j, _ = state
            return j - 1, step(j, qts, 0, False)

        def more(state):
            j, max_rem = state
            return (j >= 0) & (max_rem > REM_FLOOR_LOG2)

        lax.while_loop(more, full_step, (sb * g_q - 1, jnp.max(lr_ref[...])))

        for g in range(g_q):
            rows = pl.ds(pl.multiple_of((sb * g_q + g) * tb, tb), tb)
            for hp in range(n_hp):
                a = hp * grp_w + g * tb
                gate = sg_ref[0, rows, hp * LANES:(hp + 1) * LANES].astype(F32)
                o_ref[0, rows, hp * LANES:(hp + 1) * LANES] = (acc_ref[:, a:a + tb].T * gate).astype(BF16)
        return carry

    lax.fori_loop(0, nblk // g_q, q_super_block, 0)


def _attn_call(qn, kn, vt, sg, *, g_q=4, n_hp=4):
    bsz, s, d_attn = qn.shape
    nblk = s // ATTN_BLOCK
    assert nblk % g_q == 0
    width = n_hp * LANES
    score_lanes = g_q * n_hp * 2 * ATTN_BLOCK
    seq = pl.BlockSpec((1, s, width), lambda b, h: (b, 0, h))
    return pl.pallas_call(
        functools.partial(_attn_kernel, nblk=nblk, g_q=g_q, n_hp=n_hp),
        grid=(bsz, d_attn // width),
        in_specs=[seq, seq, pl.BlockSpec((1, nblk, width, ATTN_BLOCK), lambda b, h: (b, 0, h, 0)), seq],
        out_specs=seq,
        out_shape=jax.ShapeDtypeStruct((bsz, s, d_attn), BF16),
        scratch_shapes=[pltpu.VMEM((1, score_lanes), F32),
                        pltpu.VMEM((LANES, g_q * n_hp * ATTN_BLOCK), F32)],
        compiler_params=pltpu.CompilerParams(
            dimension_semantics=("arbitrary", "arbitrary"), vmem_limit_bytes=VMEM_LIMIT_BYTES),
        name="stickbreak_attn",
    )(qn, kn, vt, sg)


def _out_kernel(a_ref, p_ref, x_ref, mod_ref, w_ref, o_ref, wbf, *, d_model, d_attn):
    @pl.when(_first_grid_step())
    def _():
        wbf[...] = w_ref[...].astype(BF16)

    y = jnp.dot(a_ref[0], wbf[0:d_attn, :], preferred_element_type=F32)
    y = y + jnp.dot(p_ref[0], wbf[d_attn:, :], preferred_element_type=F32)
    gate = mod_ref[0, :, 2 * d_model:3 * d_model]
    o_ref[0] = x_ref[0] + gate * y


def _out_call(attn, pool_out, x, mod3, w_out, *, ts=1024):
    bsz, s, d = x.shape
    d_attn = attn.shape[2]
    d_pool = pool_out.shape[2]
    return pl.pallas_call(
        functools.partial(_out_kernel, d_model=d, d_attn=d_attn),
        grid=(bsz, s // ts),
        in_specs=[
            pl.BlockSpec((1, ts, d_attn), lambda b, t: (b, t, 0)),
            pl.BlockSpec((1, ts, d_pool), lambda b, t: (b, t, 0)),
            pl.BlockSpec((1, ts, d), lambda b, t: (b, t, 0)),
            pl.BlockSpec((1, 1, mod3.shape[2]), lambda b, t: (b, 0, 0)),
            pl.BlockSpec(w_out.shape, lambda b, t: (0, 0), pipeline_mode=pl.Buffered(1)),
        ],
        out_specs=pl.BlockSpec((1, ts, d), lambda b, t: (b, t, 0)),
        out_shape=jax.ShapeDtypeStruct((bsz, s, d), F32),
        scratch_shapes=[pltpu.VMEM(w_out.shape, BF16)],
        compiler_params=pltpu.CompilerParams(
            dimension_semantics=("arbitrary", "arbitrary"), vmem_limit_bytes=VMEM_LIMIT_BYTES),
        name="outproj_residual",
    )(attn, pool_out, x, mod3, w_out)


def kernel(x, c, w_ada, b_ada, norm_g, w_in, q_norm_g, k_norm_g, w_pool, b_pool, pool_scale, w_out):
    depth = w_ada.shape[0]
    d_attn = w_out.shape[1] // 2
    n_heads = d_attn // HEAD_DIM
    h = x
    for l in range(depth):
        mod = _ada_call(c, w_ada[l], b_ada[l])
        mod3 = mod.reshape(mod.shape[0], 1, mod.shape[1])
        qg = jnp.tile(q_norm_g[l] * (HEAD_DIM ** -0.5 * LOG2E), n_heads).reshape(1, d_attn)
        kg = jnp.tile(k_norm_g[l], n_heads).reshape(1, d_attn)
        qn, kn, vt, sg, pool_out = _proj_call(
            mod3, h, norm_g[l].reshape(1, -1), w_in[l], qg, kg,
            w_pool[l], b_pool[l].reshape(1, -1), pool_scale[l].reshape(1, -1))
        attn = _attn_call(qn, kn, vt, sg)
        h = _out_call(attn, pool_out, h, mod3, w_out[l])
    return h
```

```python
import functools

import jax
import jax.numpy as jnp
from jax import lax
from jax.experimental import pallas as pl
from jax.experimental.pallas import tpu as pltpu

F32 = jnp.float32
BF16 = jnp.bfloat16

HEAD_DIM = 64
POOL_WINDOWS = (2, 4, 8, 16)
EPS = 1e-6
LOG2E = 1.4426950408889634
SOFTPLUS_CLAMP = 64.0
REM_FLOOR_LOG2 = -160.0

LANES = 128
SUBLANES = 8
ATTN_BLOCK = 128
POOL_HIST = max(POOL_WINDOWS)
POOL_PAD = 2 * POOL_HIST
VMEM_LIMIT_BYTES = 58 * 1024 * 1024


def _silu(v):
    h = 0.5 * v
    return h + h * jnp.tanh(h)


def _first_grid_step():
    return (pl.program_id(0) == 0) & (pl.program_id(1) == 0)


def _ada_kernel(c_ref, w_ref, b_ref, o_ref):
    c = c_ref[...]
    ca = _silu(c).astype(BF16)
    o_ref[...] = jnp.dot(ca, w_ref[...].astype(BF16), preferred_element_type=F32) + b_ref[...]


def _ada_call(c, w_ada, b_ada, *, tn=512):
    bsz, d = c.shape
    n = w_ada.shape[1]
    return pl.pallas_call(
        _ada_kernel,
        grid=(n // tn,),
        in_specs=[
            pl.BlockSpec((bsz, d), lambda j: (0, 0)),
            pl.BlockSpec((d, tn), lambda j: (0, j)),
            pl.BlockSpec((1, tn), lambda j: (0, j)),
        ],
        out_specs=pl.BlockSpec((bsz, tn), lambda j: (0, j)),
        out_shape=jax.ShapeDtypeStruct((bsz, n), F32),
        compiler_params=pltpu.CompilerParams(
            dimension_semantics=("arbitrary",), vmem_limit_bytes=VMEM_LIMIT_BYTES),
        name="adaln_mod",
    )(c, w_ada, b_ada.reshape(1, n))


def _proj_kernel(mod_ref, x_ref, ng_ref, win_ref, qg_ref, kg_ref, wp_ref, bp_ref, ps_ref,
                 q_ref, k_ref, vt_ref, sg_ref, po_ref, wbf, wpbf, ubuf, sbuf_a, sbuf_b,
                 *, ts, d_model, d_attn, d_pool):
    t = pl.program_id(1)
    hist = slice(POOL_PAD - POOL_HIST, POOL_PAD)

    @pl.when(_first_grid_step())
    def _():
        wbf[...] = win_ref[...].astype(BF16)
        wpbf[...] = wp_ref[...].astype(BF16)

    @pl.when(t == 0)
    def _():
        ubuf[0:POOL_PAD, :] = jnp.zeros((POOL_PAD, d_pool), F32)

    @pl.when(t > 0)
    def _():
        ubuf[hist, :] = ubuf[ts + POOL_PAD - POOL_HIST:ts + POOL_PAD, :]

    x = x_ref[0]
    ms = jnp.mean(x * x, axis=-1, keepdims=True)
    shift = mod_ref[0, :, 0:d_model]
    scale = mod_ref[0, :, d_model:2 * d_model]
    a = ng_ref[...] * (1.0 + scale)
    hn = ((x * lax.rsqrt(ms + EPS)) * a + shift).astype(BF16)

    def proj(col, width):
        return jnp.dot(hn, wbf[:, col:col + width], preferred_element_type=F32)

    u = proj(4 * d_attn, d_pool)
    g_pool = proj(4 * d_attn + d_pool, d_pool)
    ubuf[POOL_PAD:POOL_PAD + ts, :] = u
    gdim = d_pool // len(POOL_WINDOWS)
    n_lvl = len(POOL_WINDOWS)
    end = POOL_PAD + ts
    src = ubuf
    wsums = []
    for lvl in range(n_lvl):
        shift_rows = 2 ** lvl
        start = SUBLANES * (lvl + 1)
        lanes = slice(lvl * gdim, d_pool)
        summed = src[start:end, lanes] + src[start - shift_rows:end - shift_rows, lanes]
        wsums.append(summed[POOL_PAD - start:, 0:gdim])
        if lvl + 1 < n_lvl:
            dst = sbuf_a if lvl % 2 == 0 else sbuf_b
            dst[start:end, lanes] = summed
            src = dst

    row = lax.broadcasted_iota(jnp.int32, (POOL_HIST, gdim), 0)
    for g, win in enumerate(POOL_WINDOWS):
        lo, hi = g * gdim, (g + 1) * gdim
        ug = u[:, lo:hi]
        cnt = jnp.minimum(t * ts + row + 1, win).astype(F32)
        head = wsums[g][0:POOL_HIST] / cnt - ug[0:POOL_HIST]
        rest = wsums[g][POOL_HIST:] * (1.0 / win) - ug[POOL_HIST:]
        pooled = jnp.concatenate([head, rest], axis=0).astype(BF16)
        mixed = jnp.dot(pooled, wpbf[g], preferred_element_type=F32) + bp_ref[:, lo:hi]
        po_ref[0, :, lo:hi] = (mixed * ps_ref[:, lo:hi] * _silu(g_pool[:, lo:hi])).astype(BF16)

    r = lax.broadcasted_iota(jnp.int32, (d_attn, d_attn), 0) // HEAD_DIM
    c = lax.broadcasted_iota(jnp.int32, (d_attn, d_attn), 1) // HEAD_DIM
    head_avg = jnp.where(r == c, 1.0 / HEAD_DIM, 0.0).astype(BF16)

    def head_norm(v, g):
        msq = jnp.dot((v * v).astype(BF16), head_avg, preferred_element_type=F32)
        return (v * lax.rsqrt(msq + EPS)) * g

    q_ref[0] = head_norm(proj(0, d_attn), qg_ref[...]).astype(BF16)
    k_ref[0] = head_norm(proj(d_attn, d_attn), kg_ref[...]).astype(BF16)

    vt = proj(2 * d_attn, d_attn).T.astype(BF16)
    for jj in range(ts // ATTN_BLOCK):
        vt_ref[0, jj] = vt[:, jj * ATTN_BLOCK:(jj + 1) * ATTN_BLOCK]

    sg_ref[0] = _silu(proj(3 * d_attn, d_attn)).astype(BF16)


def _proj_call(mod3, x, norm_g, w_in, qg, kg, w_pool, b_pool, pool_scale, *, ts=1024):
    bsz, s, d = x.shape
    d_attn = qg.shape[1]
    d_pool = pool_scale.shape[1]
    nt = s // ts
    nb = s // ATTN_BLOCK
    kern = functools.partial(_proj_kernel, ts=ts, d_model=d, d_attn=d_attn, d_pool=d_pool)
    const = lambda *shape: pl.BlockSpec(shape, lambda b, t: (0,) * len(shape))
    once = lambda *shape: pl.BlockSpec(shape, lambda b, t: (0,) * len(shape), pipeline_mode=pl.Buffered(1))
    seq_out = lambda width: pl.BlockSpec((1, ts, width), lambda b, t: (b, t, 0))
    pool_rows = POOL_PAD + ts
    return pl.pallas_call(
        kern,
        grid=(bsz, nt),
        in_specs=[
            pl.BlockSpec((1, 1, mod3.shape[2]), lambda b, t: (b, 0, 0)),
            pl.BlockSpec((1, ts, d), lambda b, t: (b, t, 0)),
            const(1, d),
            once(*w_in.shape),
            const(1, d_attn),
            const(1, d_attn),
            once(*w_pool.shape),
            const(1, d_pool),
            const(1, d_pool),
        ],
        out_specs=[
            seq_out(d_attn),
            seq_out(d_attn),
            pl.BlockSpec((1, ts // ATTN_BLOCK, d_attn, ATTN_BLOCK), lambda b, t: (b, t, 0, 0)),
            seq_out(d_attn),
            seq_out(d_pool),
        ],
        out_shape=[
            jax.ShapeDtypeStruct((bsz, s, d_attn), BF16),
            jax.ShapeDtypeStruct((bsz, s, d_attn), BF16),
            jax.ShapeDtypeStruct((bsz, nb, d_attn, ATTN_BLOCK), BF16),
            jax.ShapeDtypeStruct((bsz, s, d_attn), BF16),
            jax.ShapeDtypeStruct((bsz, s, d_pool), BF16),
        ],
        scratch_shapes=[pltpu.VMEM(w_in.shape, BF16),
                        pltpu.VMEM(w_pool.shape, BF16),
                        pltpu.VMEM((pool_rows, d_pool), F32),
                        pltpu.VMEM((pool_rows, d_pool), F32),
                        pltpu.VMEM((pool_rows, d_pool), F32)],
        compiler_params=pltpu.CompilerParams(
            dimension_semantics=("arbitrary", "arbitrary"), vmem_limit_bytes=VMEM_LIMIT_BYTES),
        name="norm_inproj_pool",
    )(mod3, x, norm_g, w_in, qg, kg, w_pool, b_pool, pool_scale)


def _attn_kernel(q_ref, k_ref, vt_ref, sg_ref, o_ref, lr_ref, acc_ref, *, nblk, g_q, n_hp):
    tb = ATTN_BLOCK
    row = lax.broadcasted_iota(jnp.int32, (tb, tb), 0)
    col = lax.broadcasted_iota(jnp.int32, (tb, tb), 1)
    neg_suffix = jnp.concatenate([jnp.where(col > row, -1.0, 0.0), jnp.where(col == row, -1.0, 0.0)],
                                 axis=1).astype(BF16)
    causal = row < col
    n_grp = 2 * n_hp
    grp_w = g_q * tb

    def mask_diag(v, diag):
        if not diag:
            return v
        nq = v.shape[1] // (n_grp * tb)
        keep = causal[0:v.shape[0]]
        parts = []
        for grp in range(n_grp):
            base = grp * nq * tb
            parts.append(jnp.where(keep, v[:, base:base + tb], 0.0))
            if nq > 1:
                parts.append(v[:, base + tb:base + nq * tb])
        return jnp.concatenate(parts, axis=1)

    def blocks_alive(rem, q_lo, nq):
        n_alive = jnp.int32(0)
        for g in range(nq):
            blk = jnp.concatenate([rem[:, (grp * nq + g) * tb:(grp * nq + g + 1) * tb] for grp in range(n_grp)],
                                  axis=1)
            n_alive = jnp.where(jnp.max(blk) > REM_FLOOR_LOG2, q_lo + g + 1, n_alive)
        return n_alive

    def step(j, qts, q_lo, q_hi, diag):
        nq = q_hi - q_lo
        lanes = [slice(grp * grp_w + q_lo * tb, grp * grp_w + q_hi * tb) for grp in range(n_grp)]
        rows_j = pl.ds(pl.multiple_of(j * tb, tb), tb)
        zs = []
        for hp in range(n_hp):
            qt = qts[hp]
            if nq < g_q:
                qt = jnp.concatenate([qt[:, h * grp_w + q_lo * tb:h * grp_w + q_hi * tb] for h in range(2)],
                                     axis=1)
            zs.append(jnp.dot(k_ref[0, rows_j, hp * LANES:(hp + 1) * LANES], qt, preferred_element_type=F32))
        z = jnp.concatenate(zs, axis=1)
        sp = jnp.maximum(jnp.log(1.0 + jnp.exp2(jnp.minimum(z, SOFTPLUS_CLAMP))) * LOG2E, z)
        neg_log_beta = mask_diag(sp - z, diag)
        sp = mask_diag(sp, diag)
        terms = jnp.concatenate([sp.astype(BF16), neg_log_beta.astype(BF16)], axis=0)
        log_w = jnp.dot(neg_suffix, terms, preferred_element_type=F32)
        log_rem = jnp.concatenate([lr_ref[:, ln] for ln in lanes], axis=1)
        w = mask_diag(jnp.exp2(log_w + log_rem), diag).astype(BF16)
        later_0 = -(log_w[0:1, :] + terms[tb:tb + 1, :].astype(F32))
        new_rem = log_rem - (sp[0:1, :] + later_0)
        for grp in range(n_grp):
            hp, h = divmod(grp, 2)
            cols = slice(grp * nq * tb, (grp + 1) * nq * tb)
            lr_ref[:, lanes[grp]] = new_rem[:, cols]
            vt_h = vt_ref[0, j, hp * LANES + h * HEAD_DIM:hp * LANES + (h + 1) * HEAD_DIM, :]
            res = jnp.dot(vt_h, w[:, cols], preferred_element_type=F32)
            acc_ref[h * HEAD_DIM:(h + 1) * HEAD_DIM, hp * grp_w + q_lo * tb:hp * grp_w + q_hi * tb] += res
        return blocks_alive(new_rem, q_lo, nq)

    def q_super_block(sb, carry):
        qts = []
        head_of_row = lax.broadcasted_iota(jnp.int32, (LANES, grp_w), 0) // HEAD_DIM
        for hp in range(n_hp):
            rows = pl.ds(pl.multiple_of(sb * grp_w, grp_w), grp_w)
            qt = q_ref[0, rows, hp * LANES:(hp + 1) * LANES].astype(F32).T
            qts.append(jnp.concatenate([jnp.where(head_of_row == h, qt, 0.0) for h in range(2)],
                                       axis=1).astype(BF16))
        lr_ref[...] = jnp.zeros_like(lr_ref)
        acc_ref[...] = jnp.zeros_like(acc_ref)
        for m in reversed(range(g_q)):
            step(sb * g_q + m, qts, m, g_q, True)

        def full_step(state):
            j, n_alive = state
            widths = [functools.partial(step, j, qts, 0, q_hi, False) for q_hi in range(1, g_q + 1)]
            return j - 1, lax.switch(n_alive - 1, widths)

        def more(state):
            j, n_alive = state
            return (j >= 0) & (n_alive > 0)

        lax.while_loop(more, full_step, (sb * g_q - 1, blocks_alive(lr_ref[...], 0, g_q)))

        for g in range(g_q):
            rows = pl.ds(pl.multiple_of((sb * g_q + g) * tb, tb), tb)
            for hp in range(n_hp):
                a = hp * grp_w + g * tb
                gate = sg_ref[0, rows, hp * LANES:(hp + 1) * LANES].astype(F32)
                o_ref[0, rows, hp * LANES:(hp + 1) * LANES] = (acc_ref[:, a:a + tb].T * gate).astype(BF16)
        return carry

    lax.fori_loop(0, nblk // g_q, q_super_block, 0)


def _attn_call(qn, kn, vt, sg, *, g_q=4, n_hp=4):
    bsz, s, d_attn = qn.shape
    nblk = s // ATTN_BLOCK
    assert nblk % g_q == 0
    width = n_hp * LANES
    score_lanes = g_q * n_hp * 2 * ATTN_BLOCK
    seq = pl.BlockSpec((1, s, width), lambda b, h: (b, 0, h))
    return pl.pallas_call(
        functools.partial(_attn_kernel, nblk=nblk, g_q=g_q, n_hp=n_hp),
        grid=(bsz, d_attn // width),
        in_specs=[seq, seq, pl.BlockSpec((1, nblk, width, ATTN_BLOCK), lambda b, h: (b, 0, h, 0)), seq],
        out_specs=seq,
        out_shape=jax.ShapeDtypeStruct((bsz, s, d_attn), BF16),
        scratch_shapes=[pltpu.VMEM((1, score_lanes), F32),
                        pltpu.VMEM((LANES, g_q * n_hp * ATTN_BLOCK), F32)],
        compiler_params=pltpu.CompilerParams(
            dimension_semantics=("arbitrary", "arbitrary"), vmem_limit_bytes=VMEM_LIMIT_BYTES),
        name="stickbreak_attn",
    )(qn, kn, vt, sg)


def _out_kernel(a_ref, p_ref, x_ref, mod_ref, w_ref, o_ref, wbf, *, d_model, d_attn):
    @pl.when(_first_grid_step())
    def _():
        wbf[...] = w_ref[...].astype(BF16)

    y = jnp.dot(a_ref[0], wbf[0:d_attn, :], preferred_element_type=F32)
    y = y + jnp.dot(p_ref[0], wbf[d_attn:, :], preferred_element_type=F32)
    gate = mod_ref[0, :, 2 * d_model:3 * d_model]
    o_ref[0] = x_ref[0] + gate * y


def _out_call(attn, pool_out, x, mod3, w_out, *, ts=1024):
    bsz, s, d = x.shape
    d_attn = attn.shape[2]
    d_pool = pool_out.shape[2]
    return pl.pallas_call(
        functools.partial(_out_kernel, d_model=d, d_attn=d_attn),
        grid=(bsz, s // ts),
        in_specs=[
            pl.BlockSpec((1, ts, d_attn), lambda b, t: (b, t, 0)),
            pl.BlockSpec((1, ts, d_pool), lambda b, t: (b, t, 0)),
            pl.BlockSpec((1, ts, d), lambda b, t: (b, t, 0)),
            pl.BlockSpec((1, 1, mod3.shape[2]), lambda b, t: (b, 0, 0)),
            pl.BlockSpec(w_out.shape, lambda b, t: (0, 0), pipeline_mode=pl.Buffered(1)),
        ],
        out_specs=pl.BlockSpec((1, ts, d), lambda b, t: (b, t, 0)),
        out_shape=jax.ShapeDtypeStruct((bsz, s, d), F32),
        scratch_shapes=[pltpu.VMEM(w_out.shape, BF16)],
        compiler_params=pltpu.CompilerParams(
            dimension_semantics=("arbitrary", "arbitrary"), vmem_limit_bytes=VMEM_LIMIT_BYTES),
        name="outproj_residual",
    )(attn, pool_out, x, mod3, w_out)


def kernel(x, c, w_ada, b_ada, norm_g, w_in, q_norm_g, k_norm_g, w_pool, b_pool, pool_scale, w_out):
    depth = w_ada.shape[0]
    d_attn = w_out.shape[1] // 2
    n_heads = d_attn // HEAD_DIM
    h = x
    for l in range(depth):
        mod = _ada_call(c, w_ada[l], b_ada[l])
        mod3 = mod.reshape(mod.shape[0], 1, mod.shape[1])
        qg = jnp.tile(q_norm_g[l] * (HEAD_DIM ** -0.5 * LOG2E), n_heads).reshape(1, d_attn)
        kg = jnp.tile(k_norm_g[l], n_heads).reshape(1, d_attn)
        qn, kn, vt, sg, pool_out = _proj_call(
            mod3, h, norm_g[l].reshape(1, -1), w_in[l], qg, kg,
            w_pool[l], b_pool[l].reshape(1, -1), pool_scale[l].reshape(1, -1))
        attn = _attn_call(qn, kn, vt, sg)
        h = _out_call(attn, pool_out, h, mod3, w_out[l])
    return h
```

```python
import functools

import jax
import jax.numpy as jnp
from jax import lax
from jax.experimental import pallas as pl
from jax.experimental.pallas import tpu as pltpu

F32 = jnp.float32
BF16 = jnp.bfloat16

HEAD_DIM = 64
POOL_WINDOWS = (2, 4, 8, 16)
EPS = 1e-6
LOG2E = 1.4426950408889634
SOFTPLUS_CLAMP = 64.0
REM_FLOOR_LOG2 = -160.0
DIAG_LOOKAHEAD = 1

LANES = 128
SUBLANES = 8
ATTN_BLOCK = 128
POOL_HIST = max(POOL_WINDOWS)
POOL_PAD = 2 * POOL_HIST
VMEM_LIMIT_BYTES = 58 * 1024 * 1024


def _silu(v):
    h = 0.5 * v
    return h + h * jnp.tanh(h)


def _first_grid_step():
    return (pl.program_id(0) == 0) & (pl.program_id(1) == 0)


def _ada_kernel(c_ref, w_ref, b_ref, o_ref):
    c = c_ref[...]
    ca = _silu(c).astype(BF16)
    o_ref[...] = jnp.dot(ca, w_ref[...].astype(BF16), preferred_element_type=F32) + b_ref[...]


def _ada_call(c, w_ada, b_ada, *, tn=512):
    bsz, d = c.shape
    n = w_ada.shape[1]
    return pl.pallas_call(
        _ada_kernel,
        grid=(n // tn,),
        in_specs=[
            pl.BlockSpec((bsz, d), lambda j: (0, 0)),
            pl.BlockSpec((d, tn), lambda j: (0, j)),
            pl.BlockSpec((1, tn), lambda j: (0, j)),
        ],
        out_specs=pl.BlockSpec((bsz, tn), lambda j: (0, j)),
        out_shape=jax.ShapeDtypeStruct((bsz, n), F32),
        compiler_params=pltpu.CompilerParams(
            dimension_semantics=("arbitrary",), vmem_limit_bytes=VMEM_LIMIT_BYTES),
        name="adaln_mod",
    )(c, w_ada, b_ada.reshape(1, n))


def _proj_kernel(mod_ref, x_ref, ng_ref, win_ref, qg_ref, kg_ref, wp_ref, bp_ref, ps_ref,
                 q_ref, k_ref, vt_ref, sg_ref, po_ref, wbf, wpbf, ubuf, sbuf_a, sbuf_b,
                 *, ts, d_model, d_attn, d_pool):
    t = pl.program_id(1)
    hist = slice(POOL_PAD - POOL_HIST, POOL_PAD)

    @pl.when(_first_grid_step())
    def _():
        wbf[...] = win_ref[...].astype(BF16)
        wpbf[...] = wp_ref[...].astype(BF16)

    @pl.when(t == 0)
    def _():
        ubuf[0:POOL_PAD, :] = jnp.zeros((POOL_PAD, d_pool), F32)

    @pl.when(t > 0)
    def _():
        ubuf[hist, :] = ubuf[ts + POOL_PAD - POOL_HIST:ts + POOL_PAD, :]

    x = x_ref[0]
    ms = jnp.mean(x * x, axis=-1, keepdims=True)
    shift = mod_ref[0, :, 0:d_model]
    scale = mod_ref[0, :, d_model:2 * d_model]
    a = ng_ref[...] * (1.0 + scale)
    hn = ((x * lax.rsqrt(ms + EPS)) * a + shift).astype(BF16)

    def proj(col, width):
        return jnp.dot(hn, wbf[:, col:col + width], preferred_element_type=F32)

    u = proj(4 * d_attn, d_pool)
    g_pool = proj(4 * d_attn + d_pool, d_pool)
    ubuf[POOL_PAD:POOL_PAD + ts, :] = u
    gdim = d_pool // len(POOL_WINDOWS)
    n_lvl = len(POOL_WINDOWS)
    end = POOL_PAD + ts
    src = ubuf
    wsums = []
    for lvl in range(n_lvl):
        shift_rows = 2 ** lvl
        start = SUBLANES * (lvl + 1)
        lanes = slice(lvl * gdim, d_pool)
        summed = src[start:end, lanes] + src[start - shift_rows:end - shift_rows, lanes]
        wsums.append(summed[POOL_PAD - start:, 0:gdim])
        if lvl + 1 < n_lvl:
            dst = sbuf_a if lvl % 2 == 0 else sbuf_b
            dst[start:end, lanes] = summed
            src = dst

    row = lax.broadcasted_iota(jnp.int32, (POOL_HIST, gdim), 0)
    for g, win in enumerate(POOL_WINDOWS):
        lo, hi = g * gdim, (g + 1) * gdim
        ug = u[:, lo:hi]
        cnt = jnp.minimum(t * ts + row + 1, win).astype(F32)
        head = wsums[g][0:POOL_HIST] / cnt - ug[0:POOL_HIST]
        rest = wsums[g][POOL_HIST:] * (1.0 / win) - ug[POOL_HIST:]
        pooled = jnp.concatenate([head, rest], axis=0).astype(BF16)
        mixed = jnp.dot(pooled, wpbf[g], preferred_element_type=F32) + bp_ref[:, lo:hi]
        po_ref[0, :, lo:hi] = (mixed * ps_ref[:, lo:hi] * _silu(g_pool[:, lo:hi])).astype(BF16)

    r = lax.broadcasted_iota(jnp.int32, (d_attn, d_attn), 0) // HEAD_DIM
    c = lax.broadcasted_iota(jnp.int32, (d_attn, d_attn), 1) // HEAD_DIM
    head_avg = jnp.where(r == c, 1.0 / HEAD_DIM, 0.0).astype(BF16)

    def head_norm(v, g):
        msq = jnp.dot((v * v).astype(BF16), head_avg, preferred_element_type=F32)
        return (v * lax.rsqrt(msq + EPS)) * g

    q_ref[0] = head_norm(proj(0, d_attn), qg_ref[...]).astype(BF16)
    k_ref[0] = head_norm(proj(d_attn, d_attn), kg_ref[...]).astype(BF16)

    vt = proj(2 * d_attn, d_attn).T.astype(BF16)
    for jj in range(ts // ATTN_BLOCK):
        vt_ref[0, jj] = vt[:, jj * ATTN_BLOCK:(jj + 1) * ATTN_BLOCK]

    sg_ref[0] = _silu(proj(3 * d_attn, d_attn)).astype(BF16)


def _proj_call(mod3, x, norm_g, w_in, qg, kg, w_pool, b_pool, pool_scale, *, ts=1024):
    bsz, s, d = x.shape
    d_attn = qg.shape[1]
    d_pool = pool_scale.shape[1]
    nt = s // ts
    nb = s // ATTN_BLOCK
    kern = functools.partial(_proj_kernel, ts=ts, d_model=d, d_attn=d_attn, d_pool=d_pool)
    const = lambda *shape: pl.BlockSpec(shape, lambda b, t: (0,) * len(shape))
    once = lambda *shape: pl.BlockSpec(shape, lambda b, t: (0,) * len(shape), pipeline_mode=pl.Buffered(1))
    seq_out = lambda width: pl.BlockSpec((1, ts, width), lambda b, t: (b, t, 0))
    pool_rows = POOL_PAD + ts
    return pl.pallas_call(
        kern,
        grid=(bsz, nt),
        in_specs=[
            pl.BlockSpec((1, 1, mod3.shape[2]), lambda b, t: (b, 0, 0)),
            pl.BlockSpec((1, ts, d), lambda b, t: (b, t, 0)),
            const(1, d),
            once(*w_in.shape),
            const(1, d_attn),
            const(1, d_attn),
            once(*w_pool.shape),
            const(1, d_pool),
            const(1, d_pool),
        ],
        out_specs=[
            seq_out(d_attn),
            seq_out(d_attn),
            pl.BlockSpec((1, ts // ATTN_BLOCK, d_attn, ATTN_BLOCK), lambda b, t: (b, t, 0, 0)),
            seq_out(d_attn),
            seq_out(d_pool),
        ],
        out_shape=[
            jax.ShapeDtypeStruct((bsz, s, d_attn), BF16),
            jax.ShapeDtypeStruct((bsz, s, d_attn), BF16),
            jax.ShapeDtypeStruct((bsz, nb, d_attn, ATTN_BLOCK), BF16),
            jax.ShapeDtypeStruct((bsz, s, d_attn), BF16),
            jax.ShapeDtypeStruct((bsz, s, d_pool), BF16),
        ],
        scratch_shapes=[pltpu.VMEM(w_in.shape, BF16),
                        pltpu.VMEM(w_pool.shape, BF16),
                        pltpu.VMEM((pool_rows, d_pool), F32),
                        pltpu.VMEM((pool_rows, d_pool), F32),
                        pltpu.VMEM((pool_rows, d_pool), F32)],
        compiler_params=pltpu.CompilerParams(
            dimension_semantics=("arbitrary", "arbitrary"), vmem_limit_bytes=VMEM_LIMIT_BYTES),
        name="norm_inproj_pool",
    )(mod3, x, norm_g, w_in, qg, kg, w_pool, b_pool, pool_scale)


def _attn_kernel(q_ref, k_ref, vt_ref, sg_ref, o_ref, lr_ref, acc_ref, *, nblk, g_q, n_hp):
    tb = ATTN_BLOCK
    row = lax.broadcasted_iota(jnp.int32, (tb, tb), 0)
    col = lax.broadcasted_iota(jnp.int32, (tb, tb), 1)
    neg_suffix = jnp.concatenate([jnp.where(col > row, -1.0, 0.0), jnp.where(col == row, -1.0, 0.0)],
                                 axis=1).astype(BF16)
    causal = row < col
    n_grp = 2 * n_hp
    grp_w = g_q * tb

    def mask_diag(v, diag):
        if not diag:
            return v
        nq = v.shape[1] // (n_grp * tb)
        keep = causal[0:v.shape[0]]
        parts = []
        for grp in range(n_grp):
            base = grp * nq * tb
            parts.append(jnp.where(keep, v[:, base:base + tb], 0.0))
            if nq > 1:
                parts.append(v[:, base + tb:base + nq * tb])
        return jnp.concatenate(parts, axis=1)

    def blocks_alive(rem, q_lo, nq):
        n_alive = jnp.int32(0)
        for g in range(nq):
            blk = jnp.concatenate([rem[:, (grp * nq + g) * tb:(grp * nq + g + 1) * tb] for grp in range(n_grp)],
                                  axis=1)
            n_alive = jnp.where(jnp.max(blk) > REM_FLOOR_LOG2, q_lo + g + 1, n_alive)
        return n_alive

    def score(j, qts, q_lo, q_hi, diag):
        nq = q_hi - q_lo
        rows_j = pl.ds(pl.multiple_of(j * tb, tb), tb)
        zs = []
        for hp in range(n_hp):
            qt = qts[hp]
            if nq < g_q:
                qt = jnp.concatenate([qt[:, h * grp_w + q_lo * tb:h * grp_w + q_hi * tb] for h in range(2)],
                                     axis=1)
            zs.append(jnp.dot(k_ref[0, rows_j, hp * LANES:(hp + 1) * LANES], qt, preferred_element_type=F32))
        z = jnp.concatenate(zs, axis=1)
        sp = jnp.maximum(jnp.log(1.0 + jnp.exp2(jnp.minimum(z, SOFTPLUS_CLAMP))) * LOG2E, z)
        neg_log_beta = mask_diag(sp - z, diag)
        sp = mask_diag(sp, diag)
        return sp[0:1, :], jnp.concatenate([sp.astype(BF16), neg_log_beta.astype(BF16)], axis=0)

    def weigh(j, sp_0, terms, q_lo, q_hi, diag):
        nq = q_hi - q_lo
        lanes = [slice(grp * grp_w + q_lo * tb, grp * grp_w + q_hi * tb) for grp in range(n_grp)]
        log_w = jnp.dot(neg_suffix, terms, preferred_element_type=F32)
        log_rem = jnp.concatenate([lr_ref[:, ln] for ln in lanes], axis=1)
        w = mask_diag(jnp.exp2(log_w + log_rem), diag).astype(BF16)
        later_0 = -(log_w[0:1, :] + terms[tb:tb + 1, :].astype(F32))
        new_rem = log_rem - (sp_0 + later_0)
        for grp in range(n_grp):
            hp, h = divmod(grp, 2)
            cols = slice(grp * nq * tb, (grp + 1) * nq * tb)
            lr_ref[:, lanes[grp]] = new_rem[:, cols]
            vt_h = vt_ref[0, j, hp * LANES + h * HEAD_DIM:hp * LANES + (h + 1) * HEAD_DIM, :]
            res = jnp.dot(vt_h, w[:, cols], preferred_element_type=F32)
            acc_ref[h * HEAD_DIM:(h + 1) * HEAD_DIM, hp * grp_w + q_lo * tb:hp * grp_w + q_hi * tb] += res
        return blocks_alive(new_rem, q_lo, nq)

    def step(j, qts, q_lo, q_hi, diag):
        return weigh(j, *score(j, qts, q_lo, q_hi, diag), q_lo, q_hi, diag)

    def q_super_block(sb, carry):
        qts = []
        head_of_row = lax.broadcasted_iota(jnp.int32, (LANES, grp_w), 0) // HEAD_DIM
        for hp in range(n_hp):
            rows = pl.ds(pl.multiple_of(sb * grp_w, grp_w), grp_w)
            qt = q_ref[0, rows, hp * LANES:(hp + 1) * LANES].astype(F32).T
            qts.append(jnp.concatenate([jnp.where(head_of_row == h, qt, 0.0) for h in range(2)],
                                       axis=1).astype(BF16))
        lr_ref[...] = jnp.zeros_like(lr_ref)
        acc_ref[...] = jnp.zeros_like(acc_ref)
        scored = {}
        for m in reversed(range(-DIAG_LOOKAHEAD, g_q)):
            if m >= 0:
                scored[m] = score(sb * g_q + m, qts, m, g_q, True)
            done = m + DIAG_LOOKAHEAD
            if done < g_q:
                weigh(sb * g_q + done, *scored.pop(done), done, g_q, True)

        def full_step(state):
            j, n_alive = state
            widths = [functools.partial(step, j, qts, 0, q_hi, False) for q_hi in range(1, g_q + 1)]
            return j - 1, lax.switch(n_alive - 1, widths)

        def more(state):
            j, n_alive = state
            return (j >= 0) & (n_alive > 0)

        lax.while_loop(more, full_step, (sb * g_q - 1, blocks_alive(lr_ref[...], 0, g_q)))

        for g in range(g_q):
            rows = pl.ds(pl.multiple_of((sb * g_q + g) * tb, tb), tb)
            for hp in range(n_hp):
                a = hp * grp_w + g * tb
                gate = sg_ref[0, rows, hp * LANES:(hp + 1) * LANES].astype(F32)
                o_ref[0, rows, hp * LANES:(hp + 1) * LANES] = (acc_ref[:, a:a + tb].T * gate).astype(BF16)
        return carry

    lax.fori_loop(0, nblk // g_q, q_super_block, 0)


def _attn_call(qn, kn, vt, sg, *, g_q=4, n_hp=4):
    bsz, s, d_attn = qn.shape
    nblk = s // ATTN_BLOCK
    assert nblk % g_q == 0
    width = n_hp * LANES
    score_lanes = g_q * n_hp * 2 * ATTN_BLOCK
    seq = pl.BlockSpec((1, s, width), lambda b, h: (b, 0, h))
    return pl.pallas_call(
        functools.partial(_attn_kernel, nblk=nblk, g_q=g_q, n_hp=n_hp),
        grid=(bsz, d_attn // width),
        in_specs=[seq, seq, pl.BlockSpec((1, nblk, width, ATTN_BLOCK), lambda b, h: (b, 0, h, 0)), seq],
        out_specs=seq,
        out_shape=jax.ShapeDtypeStruct((bsz, s, d_attn), BF16),
        scratch_shapes=[pltpu.VMEM((1, score_lanes), F32),
                        pltpu.VMEM((LANES, g_q * n_hp * ATTN_BLOCK), F32)],
        compiler_params=pltpu.CompilerParams(
            dimension_semantics=("arbitrary", "arbitrary"), vmem_limit_bytes=VMEM_LIMIT_BYTES),
        name="stickbreak_attn",
    )(qn, kn, vt, sg)


def _out_kernel(a_ref, p_ref, x_ref, mod_ref, w_ref, o_ref, wbf, *, d_model, d_attn):
    @pl.when(_first_grid_step())
    def _():
        wbf[...] = w_ref[...].astype(BF16)

    y = jnp.dot(a_ref[0], wbf[0:d_attn, :], preferred_element_type=F32)
    y = y + jnp.dot(p_ref[0], wbf[d_attn:, :], preferred_element_type=F32)
    gate = mod_ref[0, :, 2 * d_model:3 * d_model]
    o_ref[0] = x_ref[0] + gate * y


def _out_call(attn, pool_out, x, mod3, w_out, *, ts=1024):
    bsz, s, d = x.shape
    d_attn = attn.shape[2]
    d_pool = pool_out.shape[2]
    return pl.pallas_call(
        functools.partial(_out_kernel, d_model=d, d_attn=d_attn),
        grid=(bsz, s // ts),
        in_specs=[
            pl.BlockSpec((1, ts, d_attn), lambda b, t: (b, t, 0)),
            pl.BlockSpec((1, ts, d_pool), lambda b, t: (b, t, 0)),
            pl.BlockSpec((1, ts, d), lambda b, t: (b, t, 0)),
            pl.BlockSpec((1, 1, mod3.shape[2]), lambda b, t: (b, 0, 0)),
            pl.BlockSpec(w_out.shape, lambda b, t: (0, 0), pipeline_mode=pl.Buffered(1)),
        ],
        out_specs=pl.BlockSpec((1, ts, d), lambda b, t: (b, t, 0)),
        out_shape=jax.ShapeDtypeStruct((bsz, s, d), F32),
        scratch_shapes=[pltpu.VMEM(w_out.shape, BF16)],
        compiler_params=pltpu.CompilerParams(
            dimension_semantics=("arbitrary", "arbitrary"), vmem_limit_bytes=VMEM_LIMIT_BYTES),
        name="outproj_residual",
    )(attn, pool_out, x, mod3, w_out)


def kernel(x, c, w_ada, b_ada, norm_g, w_in, q_norm_g, k_norm_g, w_pool, b_pool, pool_scale, w_out):
    depth = w_ada.shape[0]
    d_attn = w_out.shape[1] // 2
    n_heads = d_attn // HEAD_DIM
    h = x
    for l in range(depth):
        mod = _ada_call(c, w_ada[l], b_ada[l])
        mod3 = mod.reshape(mod.shape[0], 1, mod.shape[1])
        qg = jnp.tile(q_norm_g[l] * (HEAD_DIM ** -0.5 * LOG2E), n_heads).reshape(1, d_attn)
        kg = jnp.tile(k_norm_g[l], n_heads).reshape(1, d_attn)
        qn, kn, vt, sg, pool_out = _proj_call(
            mod3, h, norm_g[l].reshape(1, -1), w_in[l], qg, kg,
            w_pool[l], b_pool[l].reshape(1, -1), pool_scale[l].reshape(1, -1))
        attn = _attn_call(qn, kn, vt, sg)
        h = _out_call(attn, pool_out, h, mod3, w_out[l])
    return h
```

```python
import functools

import jax
import jax.numpy as jnp
from jax import lax
from jax.experimental import pallas as pl
from jax.experimental.pallas import tpu as pltpu

F32 = jnp.float32
BF16 = jnp.bfloat16

HEAD_DIM = 64
POOL_WINDOWS = (2, 4, 8, 16)
EPS = 1e-6
LOG2E = 1.4426950408889634
SOFTPLUS_CLAMP = 64.0
REM_FLOOR_LOG2 = -160.0
DIAG_LOOKAHEAD = 1

LANES = 128
SUBLANES = 8
MXU_WIDTH = 256
ATTN_BLOCK = 128
POOL_HIST = max(POOL_WINDOWS)
POOL_PAD = 2 * POOL_HIST
VMEM_LIMIT_BYTES = 58 * 1024 * 1024


def _silu(v):
    h = 0.5 * v
    return h + h * jnp.tanh(h)


def _first_grid_step():
    return (pl.program_id(0) == 0) & (pl.program_id(1) == 0)


def _ada_kernel(c_ref, w_ref, b_ref, o_ref):
    c = c_ref[...]
    ca = _silu(c).astype(BF16)
    o_ref[...] = jnp.dot(ca, w_ref[...].astype(BF16), preferred_element_type=F32) + b_ref[...]


def _ada_call(c, w_ada, b_ada, *, tn=1024):
    bsz, d = c.shape
    n = w_ada.shape[1]
    return pl.pallas_call(
        _ada_kernel,
        grid=(n // tn,),
        in_specs=[
            pl.BlockSpec((bsz, d), lambda j: (0, 0)),
            pl.BlockSpec((d, tn), lambda j: (0, j)),
            pl.BlockSpec((1, tn), lambda j: (0, j)),
        ],
        out_specs=pl.BlockSpec((bsz, tn), lambda j: (0, j)),
        out_shape=jax.ShapeDtypeStruct((bsz, n), F32),
        compiler_params=pltpu.CompilerParams(
            dimension_semantics=("arbitrary",), vmem_limit_bytes=VMEM_LIMIT_BYTES),
        name="adaln_mod",
    )(c, w_ada, b_ada.reshape(1, n))


def _proj_kernel(mod_ref, x_ref, ng_ref, win_ref, qg_ref, kg_ref, wp_ref, bp_ref, ps_ref,
                 q_ref, k_ref, vt_ref, sg_ref, po_ref, wbf, wpbf, ubuf, sbuf_a, sbuf_b,
                 *, ts, d_model, d_attn, d_pool):
    t = pl.program_id(1)
    hist = slice(POOL_PAD - POOL_HIST, POOL_PAD)
    gdim = d_pool // len(POOL_WINDOWS)
    pool_pack = MXU_WIDTH // gdim

    @pl.when(_first_grid_step())
    def _():
        wbf[...] = win_ref[...].astype(BF16)
        wpbf[...] = jnp.zeros_like(wpbf)
        for g in range(len(POOL_WINDOWS)):
            p, i = divmod(g, pool_pack)
            wpbf[p, i * gdim:(i + 1) * gdim, i * gdim:(i + 1) * gdim] = wp_ref[g].astype(BF16)

    @pl.when(t == 0)
    def _():
        ubuf[0:POOL_PAD, :] = jnp.zeros((POOL_PAD, d_pool), F32)

    @pl.when(t > 0)
    def _():
        ubuf[hist, :] = ubuf[ts + POOL_PAD - POOL_HIST:ts + POOL_PAD, :]

    x = x_ref[0]
    ms = jnp.mean(x * x, axis=-1, keepdims=True)
    shift = mod_ref[0, :, 0:d_model]
    scale = mod_ref[0, :, d_model:2 * d_model]
    a = ng_ref[...] * (1.0 + scale)
    hn = ((x * lax.rsqrt(ms + EPS)) * a + shift).astype(BF16)

    def proj(col, width):
        return jnp.dot(hn, wbf[:, col:col + width], preferred_element_type=F32)

    u = proj(4 * d_attn, d_pool)
    g_pool = proj(4 * d_attn + d_pool, d_pool)
    ubuf[POOL_PAD:POOL_PAD + ts, :] = u
    n_lvl = len(POOL_WINDOWS)
    end = POOL_PAD + ts
    src = ubuf
    wsums = []
    for lvl in range(n_lvl):
        shift_rows = 2 ** lvl
        start = SUBLANES * (lvl + 1)
        lanes = slice(lvl * gdim, d_pool)
        summed = src[start:end, lanes] + src[start - shift_rows:end - shift_rows, lanes]
        wsums.append(summed[POOL_PAD - start:, 0:gdim])
        if lvl + 1 < n_lvl:
            dst = sbuf_a if lvl % 2 == 0 else sbuf_b
            dst[start:end, lanes] = summed
            src = dst

    row = lax.broadcasted_iota(jnp.int32, (POOL_HIST, gdim), 0)
    pooled = []
    for g, win in enumerate(POOL_WINDOWS):
        ug = u[:, g * gdim:(g + 1) * gdim]
        cnt = jnp.minimum(t * ts + row + 1, win).astype(F32)
        head = wsums[g][0:POOL_HIST] / cnt - ug[0:POOL_HIST]
        rest = wsums[g][POOL_HIST:] * (1.0 / win) - ug[POOL_HIST:]
        pooled.append(jnp.concatenate([head, rest], axis=0).astype(BF16))
    for p in range(len(POOL_WINDOWS) // pool_pack):
        lo, hi = p * MXU_WIDTH, (p + 1) * MXU_WIDTH
        mixed = jnp.dot(jnp.concatenate(pooled[p * pool_pack:(p + 1) * pool_pack], axis=1), wpbf[p],
                        preferred_element_type=F32) + bp_ref[:, lo:hi]
        po_ref[0, :, lo:hi] = (mixed * ps_ref[:, lo:hi] * _silu(g_pool[:, lo:hi])).astype(BF16)

    r = lax.broadcasted_iota(jnp.int32, (MXU_WIDTH, MXU_WIDTH), 0) // HEAD_DIM
    c = lax.broadcasted_iota(jnp.int32, (MXU_WIDTH, MXU_WIDTH), 1) // HEAD_DIM
    head_avg = jnp.where(r == c, 1.0 / HEAD_DIM, 0.0).astype(BF16)

    def head_norm(v, g):
        sq = (v * v).astype(BF16)
        msq = jnp.concatenate([jnp.dot(sq[:, lo:lo + MXU_WIDTH], head_avg, preferred_element_type=F32)
                               for lo in range(0, d_attn, MXU_WIDTH)], axis=1)
        return (v * lax.rsqrt(msq + EPS)) * g

    q_ref[0] = head_norm(proj(0, d_attn), qg_ref[...]).astype(BF16)
    k_ref[0] = head_norm(proj(d_attn, d_attn), kg_ref[...]).astype(BF16)

    vt = proj(2 * d_attn, d_attn).T.astype(BF16)
    for jj in range(ts // ATTN_BLOCK):
        vt_ref[0, jj] = vt[:, jj * ATTN_BLOCK:(jj + 1) * ATTN_BLOCK]

    sg_ref[0] = _silu(proj(3 * d_attn, d_attn)).astype(BF16)


def _proj_call(mod3, x, norm_g, w_in, qg, kg, w_pool, b_pool, pool_scale, *, ts=1024):
    bsz, s, d = x.shape
    d_attn = qg.shape[1]
    d_pool = pool_scale.shape[1]
    nt = s // ts
    nb = s // ATTN_BLOCK
    kern = functools.partial(_proj_kernel, ts=ts, d_model=d, d_attn=d_attn, d_pool=d_pool)
    const = lambda *shape: pl.BlockSpec(shape, lambda b, t: (0,) * len(shape))
    once = lambda *shape: pl.BlockSpec(shape, lambda b, t: (0,) * len(shape), pipeline_mode=pl.Buffered(1))
    seq_out = lambda width: pl.BlockSpec((1, ts, width), lambda b, t: (b, t, 0))
    pool_rows = POOL_PAD + ts
    return pl.pallas_call(
        kern,
        grid=(bsz, nt),
        in_specs=[
            pl.BlockSpec((1, 1, mod3.shape[2]), lambda b, t: (b, 0, 0)),
            pl.BlockSpec((1, ts, d), lambda b, t: (b, t, 0)),
            const(1, d),
            once(*w_in.shape),
            const(1, d_attn),
            const(1, d_attn),
            once(*w_pool.shape),
            const(1, d_pool),
            const(1, d_pool),
        ],
        out_specs=[
            seq_out(d_attn),
            seq_out(d_attn),
            pl.BlockSpec((1, ts // ATTN_BLOCK, d_attn, ATTN_BLOCK), lambda b, t: (b, t, 0, 0)),
            seq_out(d_attn),
            seq_out(d_pool),
        ],
        out_shape=[
            jax.ShapeDtypeStruct((bsz, s, d_attn), BF16),
            jax.ShapeDtypeStruct((bsz, s, d_attn), BF16),
            jax.ShapeDtypeStruct((bsz, nb, d_attn, ATTN_BLOCK), BF16),
            jax.ShapeDtypeStruct((bsz, s, d_attn), BF16),
            jax.ShapeDtypeStruct((bsz, s, d_pool), BF16),
        ],
        scratch_shapes=[pltpu.VMEM(w_in.shape, BF16),
                        pltpu.VMEM((d_pool // MXU_WIDTH, MXU_WIDTH, MXU_WIDTH), BF16),
                        pltpu.VMEM((pool_rows, d_pool), F32),
                        pltpu.VMEM((pool_rows, d_pool), F32),
                        pltpu.VMEM((pool_rows, d_pool), F32)],
        compiler_params=pltpu.CompilerParams(
            dimension_semantics=("arbitrary", "arbitrary"), vmem_limit_bytes=VMEM_LIMIT_BYTES),
        name="norm_inproj_pool",
    )(mod3, x, norm_g, w_in, qg, kg, w_pool, b_pool, pool_scale)


def _attn_kernel(q_ref, k_ref, vt_ref, sg_ref, o_ref, lr_ref, acc_ref, *, nblk, g_q, n_hp):
    tb = ATTN_BLOCK
    row = lax.broadcasted_iota(jnp.int32, (tb, tb), 0)
    col = lax.broadcasted_iota(jnp.int32, (tb, tb), 1)
    neg_suffix = jnp.concatenate([jnp.where(col > row, -1.0, 0.0), jnp.where(col == row, -1.0, 0.0)],
                                 axis=1).astype(BF16)
    causal = row < col
    n_grp = 2 * n_hp
    grp_w = g_q * tb

    def mask_diag(v, diag):
        if not diag:
            return v
        nq = v.shape[1] // (n_grp * tb)
        keep = causal[0:v.shape[0]]
        parts = []
        for grp in range(n_grp):
            base = grp * nq * tb
            parts.append(jnp.where(keep, v[:, base:base + tb], 0.0))
            if nq > 1:
                parts.append(v[:, base + tb:base + nq * tb])
        return jnp.concatenate(parts, axis=1)

    def blocks_alive(rem, q_lo, nq):
        n_alive = jnp.int32(0)
        for g in range(nq):
            blk = jnp.concatenate([rem[:, (grp * nq + g) * tb:(grp * nq + g + 1) * tb] for grp in range(n_grp)],
                                  axis=1)
            n_alive = jnp.where(jnp.max(blk) > REM_FLOOR_LOG2, q_lo + g + 1, n_alive)
        return n_alive

    def score(j, qts, q_lo, q_hi, diag):
        nq = q_hi - q_lo
        rows_j = pl.ds(pl.multiple_of(j * tb, tb), tb)
        zs = []
        for hp in range(n_hp):
            qt = qts[hp]
            if nq < g_q:
                qt = jnp.concatenate([qt[:, h * grp_w + q_lo * tb:h * grp_w + q_hi * tb] for h in range(2)],
                                     axis=1)
            zs.append(jnp.dot(k_ref[0, rows_j, hp * LANES:(hp + 1) * LANES], qt, preferred_element_type=F32))
        z = jnp.concatenate(zs, axis=1)
        sp = jnp.maximum(jnp.log(1.0 + jnp.exp2(jnp.minimum(z, SOFTPLUS_CLAMP))) * LOG2E, z)
        neg_log_beta = mask_diag(sp - z, diag)
        sp = mask_diag(sp, diag)
        return sp[0:1, :], jnp.concatenate([sp.astype(BF16), neg_log_beta.astype(BF16)], axis=0)

    def weigh(j, sp_0, terms, q_lo, q_hi, diag):
        nq = q_hi - q_lo
        lanes = [slice(grp * grp_w + q_lo * tb, grp * grp_w + q_hi * tb) for grp in range(n_grp)]
        log_w = jnp.dot(neg_suffix, terms, preferred_element_type=F32)
        log_rem = jnp.concatenate([lr_ref[:, ln] for ln in lanes], axis=1)
        w = mask_diag(jnp.exp2(log_w + log_rem), diag).astype(BF16)
        later_0 = -(log_w[0:1, :] + terms[tb:tb + 1, :].astype(F32))
        new_rem = log_rem - (sp_0 + later_0)
        for grp in range(n_grp):
            hp, h = divmod(grp, 2)
            cols = slice(grp * nq * tb, (grp + 1) * nq * tb)
            lr_ref[:, lanes[grp]] = new_rem[:, cols]
            vt_h = vt_ref[0, j, hp * LANES + h * HEAD_DIM:hp * LANES + (h + 1) * HEAD_DIM, :]
            res = jnp.dot(vt_h, w[:, cols], preferred_element_type=F32)
            acc_ref[h * HEAD_DIM:(h + 1) * HEAD_DIM, hp * grp_w + q_lo * tb:hp * grp_w + q_hi * tb] += res
        return blocks_alive(new_rem, q_lo, nq)

    def step(j, qts, q_lo, q_hi, diag):
        return weigh(j, *score(j, qts, q_lo, q_hi, diag), q_lo, q_hi, diag)

    def q_super_block(sb, carry):
        qts = []
        head_of_row = lax.broadcasted_iota(jnp.int32, (LANES, grp_w), 0) // HEAD_DIM
        for hp in range(n_hp):
            rows = pl.ds(pl.multiple_of(sb * grp_w, grp_w), grp_w)
            qt = q_ref[0, rows, hp * LANES:(hp + 1) * LANES].astype(F32).T
            qts.append(jnp.concatenate([jnp.where(head_of_row == h, qt, 0.0) for h in range(2)],
                                       axis=1).astype(BF16))
        lr_ref[...] = jnp.zeros_like(lr_ref)
        acc_ref[...] = jnp.zeros_like(acc_ref)
        scored = {}
        for m in reversed(range(-DIAG_LOOKAHEAD, g_q)):
            if m >= 0:
                scored[m] = score(sb * g_q + m, qts, m, g_q, True)
            done = m + DIAG_LOOKAHEAD
            if done < g_q:
                weigh(sb * g_q + done, *scored.pop(done), done, g_q, True)

        def full_step(state):
            j, n_alive = state
            widths = [functools.partial(step, j, qts, 0, q_hi, False) for q_hi in range(1, g_q + 1)]
            return j - 1, lax.switch(n_alive - 1, widths)

        def more(state):
            j, n_alive = state
            return (j >= 0) & (n_alive > 0)

        lax.while_loop(more, full_step, (sb * g_q - 1, blocks_alive(lr_ref[...], 0, g_q)))

        for g in range(g_q):
            rows = pl.ds(pl.multiple_of((sb * g_q + g) * tb, tb), tb)
            for hp in range(n_hp):
                a = hp * grp_w + g * tb
                gate = sg_ref[0, rows, hp * LANES:(hp + 1) * LANES].astype(F32)
                o_ref[0, rows, hp * LANES:(hp + 1) * LANES] = (acc_ref[:, a:a + tb].T * gate).astype(BF16)
        return carry

    lax.fori_loop(0, nblk // g_q, q_super_block, 0)


def _attn_call(qn, kn, vt, sg, *, g_q=4, n_hp=4):
    bsz, s, d_attn = qn.shape
    nblk = s // ATTN_BLOCK
    assert nblk % g_q == 0
    width = n_hp * LANES
    score_lanes = g_q * n_hp * 2 * ATTN_BLOCK
    seq = pl.BlockSpec((1, s, width), lambda b, h: (b, 0, h))
    return pl.pallas_call(
        functools.partial(_attn_kernel, nblk=nblk, g_q=g_q, n_hp=n_hp),
        grid=(bsz, d_attn // width),
        in_specs=[seq, seq, pl.BlockSpec((1, nblk, width, ATTN_BLOCK), lambda b, h: (b, 0, h, 0)), seq],
        out_specs=seq,
        out_shape=jax.ShapeDtypeStruct((bsz, s, d_attn), BF16),
        scratch_shapes=[pltpu.VMEM((1, score_lanes), F32),
                        pltpu.VMEM((LANES, g_q * n_hp * ATTN_BLOCK), F32)],
        compiler_params=pltpu.CompilerParams(
            dimension_semantics=("arbitrary", "arbitrary"), vmem_limit_bytes=VMEM_LIMIT_BYTES),
        name="stickbreak_attn",
    )(qn, kn, vt, sg)


def _out_kernel(a_ref, p_ref, x_ref, mod_ref, w_ref, o_ref, wbf, *, d_model, d_attn):
    @pl.when(_first_grid_step())
    def _():
        wbf[...] = w_ref[...].astype(BF16)

    y = jnp.dot(a_ref[0], wbf[0:d_attn, :], preferred_element_type=F32)
    y = y + jnp.dot(p_ref[0], wbf[d_attn:, :], preferred_element_type=F32)
    gate = mod_ref[0, :, 2 * d_model:3 * d_model]
    o_ref[0] = x_ref[0] + gate * y


def _out_call(attn, pool_out, x, mod3, w_out, *, ts=1024):
    bsz, s, d = x.shape
    d_attn = attn.shape[2]
    d_pool = pool_out.shape[2]
    return pl.pallas_call(
        functools.partial(_out_kernel, d_model=d, d_attn=d_attn),
        grid=(bsz, s // ts),
        in_specs=[
            pl.BlockSpec((1, ts, d_attn), lambda b, t: (b, t, 0)),
            pl.BlockSpec((1, ts, d_pool), lambda b, t: (b, t, 0)),
            pl.BlockSpec((1, ts, d), lambda b, t: (b, t, 0)),
            pl.BlockSpec((1, 1, mod3.shape[2]), lambda b, t: (b, 0, 0)),
            pl.BlockSpec(w_out.shape, lambda b, t: (0, 0), pipeline_mode=pl.Buffered(1)),
        ],
        out_specs=pl.BlockSpec((1, ts, d), lambda b, t: (b, t, 0)),
        out_shape=jax.ShapeDtypeStruct((bsz, s, d), F32),
        scratch_shapes=[pltpu.VMEM(w_out.shape, BF16)],
        compiler_params=pltpu.CompilerParams(
            dimension_semantics=("arbitrary", "arbitrary"), vmem_limit_bytes=VMEM_LIMIT_BYTES),
        name="outproj_residual",
    )(attn, pool_out, x, mod3, w_out)


def kernel(x, c, w_ada, b_ada, norm_g, w_in, q_norm_g, k_norm_g, w_pool, b_pool, pool_scale, w_out):
    depth = w_ada.shape[0]
    d_attn = w_out.shape[1] // 2
    n_heads = d_attn // HEAD_DIM
    h = x
    for l in range(depth):
        mod = _ada_call(c, w_ada[l], b_ada[l])
        mod3 = mod.reshape(mod.shape[0], 1, mod.shape[1])
        qg = jnp.tile(q_norm_g[l] * (HEAD_DIM ** -0.5 * LOG2E), n_heads).reshape(1, d_attn)
        kg = jnp.tile(k_norm_g[l], n_heads).reshape(1, d_attn)
        qn, kn, vt, sg, pool_out = _proj_call(
            mod3, h, norm_g[l].reshape(1, -1), w_in[l], qg, kg,
            w_pool[l], b_pool[l].reshape(1, -1), pool_scale[l].reshape(1, -1))
        attn = _attn_call(qn, kn, vt, sg)
        h = _out_call(attn, pool_out, h, mod3, w_out[l])
    return h
```

```python
import functools

import jax
import jax.numpy as jnp
from jax import lax
from jax.experimental import pallas as pl
from jax.experimental.pallas import tpu as pltpu

F32 = jnp.float32
BF16 = jnp.bfloat16

HEAD_DIM = 64
POOL_WINDOWS = (2, 4, 8, 16)
EPS = 1e-6
LOG2E = 1.4426950408889634
SOFTPLUS_CLAMP = 64.0
REM_FLOOR_LOG2 = -160.0
DIAG_LOOKAHEAD = 1

LANES = 128
SUBLANES = 8
MXU_WIDTH = 256
ATTN_BLOCK = 128
POOL_HIST = max(POOL_WINDOWS)
POOL_PAD = 2 * POOL_HIST
VMEM_LIMIT_BYTES = 58 * 1024 * 1024


def _silu(v):
    h = 0.5 * v
    return h + h * jnp.tanh(h)


def _first_grid_step():
    return (pl.program_id(0) == 0) & (pl.program_id(1) == 0)


def _ada_kernel(c_ref, w_ref, b_ref, o_ref):
    c = c_ref[...]
    ca = _silu(c).astype(BF16)
    o_ref[...] = jnp.dot(ca, w_ref[...].astype(BF16), preferred_element_type=F32) + b_ref[...]


def _ada_call(c, w_ada, b_ada, *, tn=1024):
    bsz, d = c.shape
    n = w_ada.shape[1]
    return pl.pallas_call(
        _ada_kernel,
        grid=(n // tn,),
        in_specs=[
            pl.BlockSpec((bsz, d), lambda j: (0, 0)),
            pl.BlockSpec((d, tn), lambda j: (0, j)),
            pl.BlockSpec((1, tn), lambda j: (0, j)),
        ],
        out_specs=pl.BlockSpec((bsz, tn), lambda j: (0, j)),
        out_shape=jax.ShapeDtypeStruct((bsz, n), F32),
        compiler_params=pltpu.CompilerParams(
            dimension_semantics=("arbitrary",), vmem_limit_bytes=VMEM_LIMIT_BYTES),
        name="adaln_mod",
    )(c, w_ada, b_ada.reshape(1, n))


def _proj_kernel(mod_ref, x_ref, ng_ref, win_ref, qg_ref, kg_ref, wp_ref, bp_ref, ps_ref,
                 q_ref, k_ref, vt_ref, sg_ref, po_ref, wbf, wpbf, ubuf, sbuf_a, sbuf_b,
                 *, ts, d_model, d_attn, d_pool):
    t = pl.program_id(1)
    hist = slice(POOL_PAD - POOL_HIST, POOL_PAD)
    gdim = d_pool // len(POOL_WINDOWS)
    pool_pack = MXU_WIDTH // gdim

    @pl.when(_first_grid_step())
    def _():
        wbf[...] = win_ref[...].astype(BF16)
        wpbf[...] = jnp.zeros_like(wpbf)
        for g in range(len(POOL_WINDOWS)):
            p, i = divmod(g, pool_pack)
            wpbf[p, i * gdim:(i + 1) * gdim, i * gdim:(i + 1) * gdim] = wp_ref[g].astype(BF16)

    @pl.when(t == 0)
    def _():
        ubuf[0:POOL_PAD, :] = jnp.zeros((POOL_PAD, d_pool), F32)

    @pl.when(t > 0)
    def _():
        ubuf[hist, :] = ubuf[ts + POOL_PAD - POOL_HIST:ts + POOL_PAD, :]

    x = x_ref[0]
    ms = jnp.mean(x * x, axis=-1, keepdims=True)
    shift = mod_ref[0, :, 0:d_model]
    scale = mod_ref[0, :, d_model:2 * d_model]
    a = ng_ref[...] * (1.0 + scale)
    hn = ((x * lax.rsqrt(ms + EPS)) * a + shift).astype(BF16)

    def proj(col, width):
        return jnp.dot(hn, wbf[:, col:col + width], preferred_element_type=F32)

    u = proj(4 * d_attn, d_pool)
    g_pool = proj(4 * d_attn + d_pool, d_pool)
    ubuf[POOL_PAD:POOL_PAD + ts, :] = u
    n_lvl = len(POOL_WINDOWS)
    end = POOL_PAD + ts
    src = ubuf
    wsums = []
    for lvl in range(n_lvl):
        shift_rows = 2 ** lvl
        start = SUBLANES * (lvl + 1)
        lanes = slice(lvl * gdim, d_pool)
        summed = src[start:end, lanes] + src[start - shift_rows:end - shift_rows, lanes]
        wsums.append(summed[POOL_PAD - start:, 0:gdim])
        if lvl + 1 < n_lvl:
            dst = sbuf_a if lvl % 2 == 0 else sbuf_b
            dst[start:end, lanes] = summed
            src = dst

    row = lax.broadcasted_iota(jnp.int32, (POOL_HIST, gdim), 0)
    pooled = []
    for g, win in enumerate(POOL_WINDOWS):
        ug = u[:, g * gdim:(g + 1) * gdim]
        cnt = jnp.minimum(t * ts + row + 1, win).astype(F32)
        head = wsums[g][0:POOL_HIST] / cnt - ug[0:POOL_HIST]
        rest = wsums[g][POOL_HIST:] * (1.0 / win) - ug[POOL_HIST:]
        pooled.append(jnp.concatenate([head, rest], axis=0).astype(BF16))
    for p in range(len(POOL_WINDOWS) // pool_pack):
        lo, hi = p * MXU_WIDTH, (p + 1) * MXU_WIDTH
        mixed = jnp.dot(jnp.concatenate(pooled[p * pool_pack:(p + 1) * pool_pack], axis=1), wpbf[p],
                        preferred_element_type=F32) + bp_ref[:, lo:hi]
        po_ref[0, :, lo:hi] = (mixed * ps_ref[:, lo:hi] * _silu(g_pool[:, lo:hi])).astype(BF16)

    r = lax.broadcasted_iota(jnp.int32, (MXU_WIDTH, MXU_WIDTH), 0) // HEAD_DIM
    c = lax.broadcasted_iota(jnp.int32, (MXU_WIDTH, MXU_WIDTH), 1) // HEAD_DIM
    head_avg = jnp.where(r == c, 1.0 / HEAD_DIM, 0.0).astype(BF16)

    def head_norm(v, g):
        sq = (v * v).astype(BF16)
        msq = jnp.concatenate([jnp.dot(sq[:, lo:lo + MXU_WIDTH], head_avg, preferred_element_type=F32)
                               for lo in range(0, d_attn, MXU_WIDTH)], axis=1)
        return (v * lax.rsqrt(msq + EPS)) * g

    q_ref[0] = head_norm(proj(0, d_attn), qg_ref[...]).astype(BF16)
    k_ref[0] = head_norm(proj(d_attn, d_attn), kg_ref[...]).astype(BF16)

    vt = proj(2 * d_attn, d_attn).T.astype(BF16)
    for jj in range(ts // ATTN_BLOCK):
        vt_ref[0, jj] = vt[:, jj * ATTN_BLOCK:(jj + 1) * ATTN_BLOCK]

    sg_ref[0] = _silu(proj(3 * d_attn, d_attn)).astype(BF16)


def _proj_call(mod3, x, norm_g, w_in, qg, kg, w_pool, b_pool, pool_scale, *, ts=1024):
    bsz, s, d = x.shape
    d_attn = qg.shape[1]
    d_pool = pool_scale.shape[1]
    nt = s // ts
    nb = s // ATTN_BLOCK
    kern = functools.partial(_proj_kernel, ts=ts, d_model=d, d_attn=d_attn, d_pool=d_pool)
    const = lambda *shape: pl.BlockSpec(shape, lambda b, t: (0,) * len(shape))
    once = lambda *shape: pl.BlockSpec(shape, lambda b, t: (0,) * len(shape), pipeline_mode=pl.Buffered(1))
    seq_out = lambda width: pl.BlockSpec((1, ts, width), lambda b, t: (b, t, 0))
    pool_rows = POOL_PAD + ts
    return pl.pallas_call(
        kern,
        grid=(bsz, nt),
        in_specs=[
            pl.BlockSpec((1, 1, mod3.shape[2]), lambda b, t: (b, 0, 0)),
            pl.BlockSpec((1, ts, d), lambda b, t: (b, t, 0)),
            const(1, d),
            once(*w_in.shape),
            const(1, d_attn),
            const(1, d_attn),
            once(*w_pool.shape),
            const(1, d_pool),
            const(1, d_pool),
        ],
        out_specs=[
            seq_out(d_attn),
            seq_out(d_attn),
            pl.BlockSpec((1, ts // ATTN_BLOCK, d_attn, ATTN_BLOCK), lambda b, t: (b, t, 0, 0)),
            seq_out(d_attn),
            seq_out(d_pool),
        ],
        out_shape=[
            jax.ShapeDtypeStruct((bsz, s, d_attn), BF16),
            jax.ShapeDtypeStruct((bsz, s, d_attn), BF16),
            jax.ShapeDtypeStruct((bsz, nb, d_attn, ATTN_BLOCK), BF16),
            jax.ShapeDtypeStruct((bsz, s, d_attn), BF16),
            jax.ShapeDtypeStruct((bsz, s, d_pool), BF16),
        ],
        scratch_shapes=[pltpu.VMEM(w_in.shape, BF16),
                        pltpu.VMEM((d_pool // MXU_WIDTH, MXU_WIDTH, MXU_WIDTH), BF16),
                        pltpu.VMEM((pool_rows, d_pool), F32),
                        pltpu.VMEM((pool_rows, d_pool), F32),
                        pltpu.VMEM((pool_rows, d_pool), F32)],
        compiler_params=pltpu.CompilerParams(
            dimension_semantics=("arbitrary", "arbitrary"), vmem_limit_bytes=VMEM_LIMIT_BYTES),
        name="norm_inproj_pool",
    )(mod3, x, norm_g, w_in, qg, kg, w_pool, b_pool, pool_scale)


def _attn_kernel(q_ref, k_ref, vt_ref, sg_ref, o_ref, lr_ref, acc_ref, *, nblk, g_q, n_hp):
    tb = ATTN_BLOCK
    row = lax.broadcasted_iota(jnp.int32, (tb, tb), 0)
    col = lax.broadcasted_iota(jnp.int32, (tb, tb), 1)
    neg_suffix = jnp.concatenate([jnp.where(col > row, -1.0, 0.0), jnp.where(col == row, -1.0, 0.0)],
                                 axis=1).astype(BF16)
    causal = row < col
    n_grp = 2 * n_hp
    grp_w = g_q * tb

    def mask_diag(v, diag):
        if not diag:
            return v
        nq = v.shape[1] // (n_grp * tb)
        keep = causal[0:v.shape[0]]
        parts = []
        for grp in range(n_grp):
            base = grp * nq * tb
            parts.append(jnp.where(keep, v[:, base:base + tb], 0.0))
            if nq > 1:
                parts.append(v[:, base + tb:base + nq * tb])
        return jnp.concatenate(parts, axis=1)

    def blocks_alive(rem, q_lo, nq):
        n_alive = jnp.int32(0)
        for g in range(nq):
            blk = jnp.concatenate([rem[:, (grp * nq + g) * tb:(grp * nq + g + 1) * tb] for grp in range(n_grp)],
                                  axis=1)
            n_alive = jnp.where(jnp.max(blk) > REM_FLOOR_LOG2, q_lo + g + 1, n_alive)
        return n_alive

    def score(j, qts, q_lo, q_hi, diag):
        nq = q_hi - q_lo
        rows_j = pl.ds(pl.multiple_of(j * tb, tb), tb)
        zs = []
        for hp in range(n_hp):
            qt = qts[hp]
            if nq < g_q:
                qt = jnp.concatenate([qt[:, h * grp_w + q_lo * tb:h * grp_w + q_hi * tb] for h in range(2)],
                                     axis=1)
            zs.append(jnp.dot(k_ref[0, rows_j, hp * LANES:(hp + 1) * LANES], qt, preferred_element_type=F32))
        z = jnp.concatenate(zs, axis=1)
        sp = jnp.maximum(jnp.log(1.0 + jnp.exp2(jnp.minimum(z, SOFTPLUS_CLAMP))) * LOG2E, z)
        neg_log_beta = mask_diag(sp - z, diag)
        sp = mask_diag(sp, diag)
        return sp[0:1, :], jnp.concatenate([sp.astype(BF16), neg_log_beta.astype(BF16)], axis=0)

    def weigh(j, sp_0, terms, q_lo, q_hi, diag):
        nq = q_hi - q_lo
        lanes = [slice(grp * grp_w + q_lo * tb, grp * grp_w + q_hi * tb) for grp in range(n_grp)]
        log_w = jnp.dot(neg_suffix, terms, preferred_element_type=F32)
        log_rem = jnp.concatenate([lr_ref[:, ln] for ln in lanes], axis=1)
        w = mask_diag(jnp.exp2(log_w + log_rem), diag).astype(BF16)
        later_0 = -(log_w[0:1, :] + terms[tb:tb + 1, :].astype(F32))
        new_rem = log_rem - (sp_0 + later_0)
        for grp in range(n_grp):
            hp, h = divmod(grp, 2)
            cols = slice(grp * nq * tb, (grp + 1) * nq * tb)
            lr_ref[:, lanes[grp]] = new_rem[:, cols]
            vt_h = vt_ref[0, j, hp * LANES + h * HEAD_DIM:hp * LANES + (h + 1) * HEAD_DIM, :]
            res = jnp.dot(vt_h, w[:, cols], preferred_element_type=F32)
            acc_ref[h * HEAD_DIM:(h + 1) * HEAD_DIM, hp * grp_w + q_lo * tb:hp * grp_w + q_hi * tb] += res
        return blocks_alive(new_rem, q_lo, nq)

    def step(j, qts, q_lo, q_hi, diag):
        return weigh(j, *score(j, qts, q_lo, q_hi, diag), q_lo, q_hi, diag)

    def q_super_block(sb, carry):
        qts = []
        head_of_row = lax.broadcasted_iota(jnp.int32, (LANES, grp_w), 0) // HEAD_DIM
        for hp in range(n_hp):
            rows = pl.ds(pl.multiple_of(sb * grp_w, grp_w), grp_w)
            qt = q_ref[0, rows, hp * LANES:(hp + 1) * LANES].astype(F32).T
            qts.append(jnp.concatenate([jnp.where(head_of_row == h, qt, 0.0) for h in range(2)],
                                       axis=1).astype(BF16))
        lr_ref[...] = jnp.zeros_like(lr_ref)
        acc_ref[...] = jnp.zeros_like(acc_ref)
        scored = {}
        for m in reversed(range(-DIAG_LOOKAHEAD, g_q)):
            if m >= 0:
                scored[m] = score(sb * g_q + m, qts, m, g_q, True)
            done = m + DIAG_LOOKAHEAD
            if done < g_q:
                weigh(sb * g_q + done, *scored.pop(done), done, g_q, True)

        def full_step(state):
            j, n_alive = state
            widths = [functools.partial(step, j, qts, 0, q_hi, False) for q_hi in range(1, g_q + 1)]
            return j - 1, lax.switch(n_alive - 1, widths)

        def more(state):
            j, n_alive = state
            return (j >= 0) & (n_alive > 0)

        lax.while_loop(more, full_step, (sb * g_q - 1, blocks_alive(lr_ref[...], 0, g_q)))

        for g in range(g_q):
            rows = pl.ds(pl.multiple_of((sb * g_q + g) * tb, tb), tb)
            for hp in range(n_hp):
                a = hp * grp_w + g * tb
                gate = sg_ref[0, rows, hp * LANES:(hp + 1) * LANES].astype(F32)
                o_ref[0, rows, hp * LANES:(hp + 1) * LANES] = (acc_ref[:, a:a + tb].T * gate).astype(BF16)
        return carry

    lax.fori_loop(0, nblk // g_q, q_super_block, 0)


def _attn_call(qn, kn, vt, sg, *, g_q=4, n_hp=4):
    bsz, s, d_attn = qn.shape
    nblk = s // ATTN_BLOCK
    assert nblk % g_q == 0
    width = n_hp * LANES
    score_lanes = g_q * n_hp * 2 * ATTN_BLOCK
    seq = pl.BlockSpec((1, s, width), lambda b, h: (b, 0, h))
    return pl.pallas_call(
        functools.partial(_attn_kernel, nblk=nblk, g_q=g_q, n_hp=n_hp),
        grid=(bsz, d_attn // width),
        in_specs=[seq, seq, pl.BlockSpec((1, nblk, width, ATTN_BLOCK), lambda b, h: (b, 0, h, 0)), seq],
        out_specs=seq,
        out_shape=jax.ShapeDtypeStruct((bsz, s, d_attn), BF16),
        scratch_shapes=[pltpu.VMEM((1, score_lanes), F32),
                        pltpu.VMEM((LANES, g_q * n_hp * ATTN_BLOCK), F32)],
        compiler_params=pltpu.CompilerParams(
            dimension_semantics=("arbitrary", "arbitrary"), vmem_limit_bytes=VMEM_LIMIT_BYTES),
        name="stickbreak_attn",
    )(qn, kn, vt, sg)


X_RING = 3


def _out_kernel(a_ref, p_ref, x_hbm, mod_ref, w_ref, o_ref, wbf, xbuf, sem, *, ts, d_model, d_attn):
    n_t = pl.num_programs(1)
    step = pl.program_id(0) * n_t + pl.program_id(1)
    n_steps = pl.num_programs(0) * n_t

    def x_copy(n):
        slot = n % X_RING
        rows = pl.ds(pl.multiple_of((n % n_t) * ts, ts), ts)
        return pltpu.make_async_copy(x_hbm.at[n // n_t, rows, :], xbuf.at[slot], sem.at[slot])

    @pl.when(step == 0)
    def _():
        wbf[...] = w_ref[...].astype(BF16)
        for n in range(X_RING - 1):
            x_copy(n).start()

    @pl.when(step + (X_RING - 1) < n_steps)
    def _():
        x_copy(step + (X_RING - 1)).start()

    y = jnp.dot(a_ref[0], wbf[0:d_attn, :], preferred_element_type=F32)
    y = y + jnp.dot(p_ref[0], wbf[d_attn:, :], preferred_element_type=F32)
    gate = mod_ref[0, :, 2 * d_model:3 * d_model]
    x_copy(step).wait()
    o_ref[0] = xbuf[step % X_RING] + gate * y


def _out_call(attn, pool_out, x, mod3, w_out, *, ts=1024):
    bsz, s, d = x.shape
    d_attn = attn.shape[2]
    d_pool = pool_out.shape[2]
    assert bsz * (s // ts) >= X_RING - 1
    return pl.pallas_call(
        functools.partial(_out_kernel, ts=ts, d_model=d, d_attn=d_attn),
        grid=(bsz, s // ts),
        in_specs=[
            pl.BlockSpec((1, ts, d_attn), lambda b, t: (b, t, 0)),
            pl.BlockSpec((1, ts, d_pool), lambda b, t: (b, t, 0)),
            pl.BlockSpec(memory_space=pl.ANY),
            pl.BlockSpec((1, 1, mod3.shape[2]), lambda b, t: (b, 0, 0)),
            pl.BlockSpec(w_out.shape, lambda b, t: (0, 0), pipeline_mode=pl.Buffered(1)),
        ],
        out_specs=pl.BlockSpec((1, ts, d), lambda b, t: (b, t, 0)),
        out_shape=jax.ShapeDtypeStruct((bsz, s, d), F32),
        scratch_shapes=[pltpu.VMEM(w_out.shape, BF16),
                        pltpu.VMEM((X_RING, ts, d), F32),
                        pltpu.SemaphoreType.DMA((X_RING,))],
        compiler_params=pltpu.CompilerParams(
            dimension_semantics=("arbitrary", "arbitrary"), vmem_limit_bytes=VMEM_LIMIT_BYTES),
        name="outproj_residual",
    )(attn, pool_out, x, mod3, w_out)


def kernel(x, c, w_ada, b_ada, norm_g, w_in, q_norm_g, k_norm_g, w_pool, b_pool, pool_scale, w_out):
    depth = w_ada.shape[0]
    d_attn = w_out.shape[1] // 2
    n_heads = d_attn // HEAD_DIM
    h = x
    for l in range(depth):
        mod = _ada_call(c, w_ada[l], b_ada[l])
        mod3 = mod.reshape(mod.shape[0], 1, mod.shape[1])
        qg = jnp.tile(q_norm_g[l] * (HEAD_DIM ** -0.5 * LOG2E), n_heads).reshape(1, d_attn)
        kg = jnp.tile(k_norm_g[l], n_heads).reshape(1, d_attn)
        qn, kn, vt, sg, pool_out = _proj_call(
            mod3, h, norm_g[l].reshape(1, -1), w_in[l], qg, kg,
            w_pool[l], b_pool[l].reshape(1, -1), pool_scale[l].reshape(1, -1))
        attn = _attn_call(qn, kn, vt, sg)
        h = _out_call(attn, pool_out, h, mod3, w_out[l])
    return h
```

```python
import functools

import jax
import jax.numpy as jnp
from jax import lax
from jax.experimental import pallas as pl
from jax.experimental.pallas import tpu as pltpu

F32 = jnp.float32
BF16 = jnp.bfloat16

HEAD_DIM = 64
POOL_WINDOWS = (2, 4, 8, 16)
EPS = 1e-6
LOG2E = 1.4426950408889634
SOFTPLUS_CLAMP = 64.0
REM_FLOOR_LOG2 = -160.0
DIAG_LOOKAHEAD = 1

LANES = 128
SUBLANES = 8
MXU_WIDTH = 256
ATTN_BLOCK = 128
POOL_HIST = max(POOL_WINDOWS)
POOL_PAD = 2 * POOL_HIST
VMEM_LIMIT_BYTES = 58 * 1024 * 1024


def _silu(v):
    h = 0.5 * v
    return h + h * jnp.tanh(h)


def _first_grid_step():
    return (pl.program_id(0) == 0) & (pl.program_id(1) == 0)


def _ada_kernel(c_ref, w_ref, b_ref, o_ref):
    c = c_ref[...]
    ca = _silu(c).astype(BF16)
    o_ref[...] = jnp.dot(ca, w_ref[...].astype(BF16), preferred_element_type=F32) + b_ref[...]


def _ada_call(c, w_ada, b_ada, *, tn=1024):
    bsz, d = c.shape
    n = w_ada.shape[1]
    return pl.pallas_call(
        _ada_kernel,
        grid=(n // tn,),
        in_specs=[
            pl.BlockSpec((bsz, d), lambda j: (0, 0)),
            pl.BlockSpec((d, tn), lambda j: (0, j)),
            pl.BlockSpec((1, tn), lambda j: (0, j)),
        ],
        out_specs=pl.BlockSpec((bsz, tn), lambda j: (0, j)),
        out_shape=jax.ShapeDtypeStruct((bsz, n), F32),
        compiler_params=pltpu.CompilerParams(
            dimension_semantics=("arbitrary",), vmem_limit_bytes=VMEM_LIMIT_BYTES),
        name="adaln_mod",
    )(c, w_ada, b_ada.reshape(1, n))


def _proj_kernel(mod_ref, x_ref, ng_ref, win_ref, qg_ref, kg_ref, wp_ref, bp_ref, ps_ref,
                 q_ref, k_ref, vt_ref, sg_ref, po_ref, wbf, wpbf, ubuf, sbuf_a, sbuf_b,
                 *, ts, d_model, d_attn, d_pool):
    t = pl.program_id(1)
    hist = slice(POOL_PAD - POOL_HIST, POOL_PAD)
    gdim = d_pool // len(POOL_WINDOWS)
    pool_pack = MXU_WIDTH // gdim

    @pl.when(_first_grid_step())
    def _():
        wbf[...] = win_ref[...].astype(BF16)
        wpbf[...] = jnp.zeros_like(wpbf)
        for g in range(len(POOL_WINDOWS)):
            p, i = divmod(g, pool_pack)
            wpbf[p, i * gdim:(i + 1) * gdim, i * gdim:(i + 1) * gdim] = wp_ref[g].astype(BF16)

    @pl.when(t == 0)
    def _():
        ubuf[0:POOL_PAD, :] = jnp.zeros((POOL_PAD, d_pool), F32)

    @pl.when(t > 0)
    def _():
        ubuf[hist, :] = ubuf[ts + POOL_PAD - POOL_HIST:ts + POOL_PAD, :]

    x = x_ref[0]
    ms = jnp.mean(x * x, axis=-1, keepdims=True)
    shift = mod_ref[0, :, 0:d_model]
    scale = mod_ref[0, :, d_model:2 * d_model]
    a = ng_ref[...] * (1.0 + scale)
    hn = ((x * lax.rsqrt(ms + EPS)) * a + shift).astype(BF16)

    def proj(col, width):
        return jnp.dot(hn, wbf[:, col:col + width], preferred_element_type=F32)

    u = proj(4 * d_attn, d_pool)
    g_pool = proj(4 * d_attn + d_pool, d_pool)
    ubuf[POOL_PAD:POOL_PAD + ts, :] = u
    n_lvl = len(POOL_WINDOWS)
    end = POOL_PAD + ts
    src = ubuf
    wsums = []
    for lvl in range(n_lvl):
        shift_rows = 2 ** lvl
        start = SUBLANES * (lvl + 1)
        lanes = slice(lvl * gdim, d_pool)
        summed = src[start:end, lanes] + src[start - shift_rows:end - shift_rows, lanes]
        wsums.append(summed[POOL_PAD - start:, 0:gdim])
        if lvl + 1 < n_lvl:
            dst = sbuf_a if lvl % 2 == 0 else sbuf_b
            dst[start:end, lanes] = summed
            src = dst

    row = lax.broadcasted_iota(jnp.int32, (POOL_HIST, gdim), 0)
    pooled = []
    for g, win in enumerate(POOL_WINDOWS):
        ug = u[:, g * gdim:(g + 1) * gdim]
        cnt = jnp.minimum(t * ts + row + 1, win).astype(F32)
        head = wsums[g][0:POOL_HIST] / cnt - ug[0:POOL_HIST]
        rest = wsums[g][POOL_HIST:] * (1.0 / win) - ug[POOL_HIST:]
        pooled.append(jnp.concatenate([head, rest], axis=0).astype(BF16))
    for p in range(len(POOL_WINDOWS) // pool_pack):
        lo, hi = p * MXU_WIDTH, (p + 1) * MXU_WIDTH
        mixed = jnp.dot(jnp.concatenate(pooled[p * pool_pack:(p + 1) * pool_pack], axis=1), wpbf[p],
                        preferred_element_type=F32) + bp_ref[:, lo:hi]
        po_ref[0, :, lo:hi] = (mixed * ps_ref[:, lo:hi] * _silu(g_pool[:, lo:hi])).astype(BF16)

    r = lax.broadcasted_iota(jnp.int32, (MXU_WIDTH, MXU_WIDTH), 0) // HEAD_DIM
    c = lax.broadcasted_iota(jnp.int32, (MXU_WIDTH, MXU_WIDTH), 1) // HEAD_DIM
    head_avg = jnp.where(r == c, 1.0 / HEAD_DIM, 0.0).astype(BF16)

    def head_norm(v, g):
        sq = (v * v).astype(BF16)
        msq = jnp.concatenate([jnp.dot(sq[:, lo:lo + MXU_WIDTH], head_avg, preferred_element_type=F32)
                               for lo in range(0, d_attn, MXU_WIDTH)], axis=1)
        return (v * lax.rsqrt(msq + EPS)) * g

    q_ref[0] = head_norm(proj(0, d_attn), qg_ref[...]).astype(BF16)
    k_ref[0] = head_norm(proj(d_attn, d_attn), kg_ref[...]).astype(BF16)

    vt = proj(2 * d_attn, d_attn).T.astype(BF16)
    for jj in range(ts // ATTN_BLOCK):
        vt_ref[0, jj] = vt[:, jj * ATTN_BLOCK:(jj + 1) * ATTN_BLOCK]

    sg_ref[0] = _silu(proj(3 * d_attn, d_attn)).astype(BF16)


def _proj_call(mod3, x, norm_g, w_in, qg, kg, w_pool, b_pool, pool_scale, *, ts=1024):
    bsz, s, d = x.shape
    d_attn = qg.shape[1]
    d_pool = pool_scale.shape[1]
    nt = s // ts
    nb = s // ATTN_BLOCK
    kern = functools.partial(_proj_kernel, ts=ts, d_model=d, d_attn=d_attn, d_pool=d_pool)
    const = lambda *shape: pl.BlockSpec(shape, lambda b, t: (0,) * len(shape))
    once = lambda *shape: pl.BlockSpec(shape, lambda b, t: (0,) * len(shape), pipeline_mode=pl.Buffered(1))
    seq_out = lambda width: pl.BlockSpec((1, ts, width), lambda b, t: (b, t, 0))
    pool_rows = POOL_PAD + ts
    return pl.pallas_call(
        kern,
        grid=(bsz, nt),
        in_specs=[
            pl.BlockSpec((1, 1, mod3.shape[2]), lambda b, t: (b, 0, 0)),
            pl.BlockSpec((1, ts, d), lambda b, t: (b, t, 0)),
            const(1, d),
            once(*w_in.shape),
            const(1, d_attn),
            const(1, d_attn),
            once(*w_pool.shape),
            const(1, d_pool),
            const(1, d_pool),
        ],
        out_specs=[
            seq_out(d_attn),
            seq_out(d_attn),
            pl.BlockSpec((1, ts // ATTN_BLOCK, d_attn, ATTN_BLOCK), lambda b, t: (b, t, 0, 0)),
            seq_out(d_attn),
            seq_out(d_pool),
        ],
        out_shape=[
            jax.ShapeDtypeStruct((bsz, s, d_attn), BF16),
            jax.ShapeDtypeStruct((bsz, s, d_attn), BF16),
            jax.ShapeDtypeStruct((bsz, nb, d_attn, ATTN_BLOCK), BF16),
            jax.ShapeDtypeStruct((bsz, s, d_attn), BF16),
            jax.ShapeDtypeStruct((bsz, s, d_pool), BF16),
        ],
        scratch_shapes=[pltpu.VMEM(w_in.shape, BF16),
                        pltpu.VMEM((d_pool // MXU_WIDTH, MXU_WIDTH, MXU_WIDTH), BF16),
                        pltpu.VMEM((pool_rows, d_pool), F32),
                        pltpu.VMEM((pool_rows, d_pool), F32),
                        pltpu.VMEM((pool_rows, d_pool), F32)],
        compiler_params=pltpu.CompilerParams(
            dimension_semantics=("arbitrary", "arbitrary"), vmem_limit_bytes=VMEM_LIMIT_BYTES),
        name="norm_inproj_pool",
    )(mod3, x, norm_g, w_in, qg, kg, w_pool, b_pool, pool_scale)


def _attn_kernel(q_ref, k_ref, vt_ref, sg_ref, o_ref, lr_ref, acc_ref, *, nblk, g_q, n_hp):
    tb = ATTN_BLOCK
    row = lax.broadcasted_iota(jnp.int32, (tb, tb), 0)
    col = lax.broadcasted_iota(jnp.int32, (tb, tb), 1)
    neg_suffix = jnp.concatenate([jnp.where(col > row, -1.0, 0.0), jnp.where(col == row, -1.0, 0.0)],
                                 axis=1).astype(BF16)
    causal = row < col
    n_grp = 2 * n_hp
    grp_w = g_q * tb

    def mask_diag(v, diag):
        if not diag:
            return v
        nq = v.shape[1] // (n_grp * tb)
        keep = causal[0:v.shape[0]]
        parts = []
        for grp in range(n_grp):
            base = grp * nq * tb
            parts.append(jnp.where(keep, v[:, base:base + tb], 0.0))
            if nq > 1:
                parts.append(v[:, base + tb:base + nq * tb])
        return jnp.concatenate(parts, axis=1)

    def blocks_alive(rem, q_lo, nq):
        n_alive = jnp.int32(0)
        for g in range(nq):
            blk = jnp.concatenate([rem[:, (grp * nq + g) * tb:(grp * nq + g + 1) * tb] for grp in range(n_grp)],
                                  axis=1)
            n_alive = jnp.where(jnp.max(blk) > REM_FLOOR_LOG2, q_lo + g + 1, n_alive)
        return n_alive

    def score(j, qts, q_lo, q_hi, diag):
        nq = q_hi - q_lo
        rows_j = pl.ds(pl.multiple_of(j * tb, tb), tb)
        zs = []
        for hp in range(n_hp):
            qt = qts[hp]
            if nq < g_q:
                qt = jnp.concatenate([qt[:, h * grp_w + q_lo * tb:h * grp_w + q_hi * tb] for h in range(2)],
                                     axis=1)
            zs.append(jnp.dot(k_ref[0, rows_j, hp * LANES:(hp + 1) * LANES], qt, preferred_element_type=F32))
        z = jnp.concatenate(zs, axis=1)
        sp = jnp.maximum(jnp.log(1.0 + jnp.exp2(jnp.minimum(z, SOFTPLUS_CLAMP))) * LOG2E, z)
        neg_log_beta = mask_diag(sp - z, diag)
        sp = mask_diag(sp, diag)
        return sp[0:1, :], jnp.concatenate([sp.astype(BF16), neg_log_beta.astype(BF16)], axis=0)

    def weigh(j, sp_0, terms, q_lo, q_hi, diag):
        nq = q_hi - q_lo
        lanes = [slice(grp * grp_w + q_lo * tb, grp * grp_w + q_hi * tb) for grp in range(n_grp)]
        log_w = jnp.dot(neg_suffix, terms, preferred_element_type=F32)
        log_rem = jnp.concatenate([lr_ref[:, ln] for ln in lanes], axis=1)
        w = mask_diag(jnp.exp2(log_w + log_rem), diag).astype(BF16)
        later_0 = -(log_w[0:1, :] + terms[tb:tb + 1, :].astype(F32))
        new_rem = log_rem - (sp_0 + later_0)
        for grp in range(n_grp):
            hp, h = divmod(grp, 2)
            cols = slice(grp * nq * tb, (grp + 1) * nq * tb)
            lr_ref[:, lanes[grp]] = new_rem[:, cols]
            vt_h = vt_ref[0, j, hp * LANES + h * HEAD_DIM:hp * LANES + (h + 1) * HEAD_DIM, :]
            res = jnp.dot(vt_h, w[:, cols], preferred_element_type=F32)
            acc_ref[h * HEAD_DIM:(h + 1) * HEAD_DIM, hp * grp_w + q_lo * tb:hp * grp_w + q_hi * tb] += res
        return blocks_alive(new_rem, q_lo, nq)

    def step(j, qts, q_lo, q_hi, diag):
        return weigh(j, *score(j, qts, q_lo, q_hi, diag), q_lo, q_hi, diag)

    def q_super_block(sb, carry):
        qts = []
        head_of_row = lax.broadcasted_iota(jnp.int32, (LANES, grp_w), 0) // HEAD_DIM
        for hp in range(n_hp):
            rows = pl.ds(pl.multiple_of(sb * grp_w, grp_w), grp_w)
            qt = q_ref[0, rows, hp * LANES:(hp + 1) * LANES].astype(F32).T
            qts.append(jnp.concatenate([jnp.where(head_of_row == h, qt, 0.0) for h in range(2)],
                                       axis=1).astype(BF16))
        lr_ref[...] = jnp.zeros_like(lr_ref)
        acc_ref[...] = jnp.zeros_like(acc_ref)
        scored = {}
        for m in reversed(range(-DIAG_LOOKAHEAD, g_q)):
            if m >= 0:
                scored[m] = score(sb * g_q + m, qts, m, g_q, True)
            done = m + DIAG_LOOKAHEAD
            if done < g_q:
                weigh(sb * g_q + done, *scored.pop(done), done, g_q, True)

        def full_step(state):
            j, n_alive = state
            widths = [functools.partial(step, j, qts, 0, q_hi, False) for q_hi in range(1, g_q + 1)]
            return j - 1, lax.switch(n_alive - 1, widths)

        def more(state):
            j, n_alive = state
            return (j >= 0) & (n_alive > 0)

        lax.while_loop(more, full_step, (sb * g_q - 1, blocks_alive(lr_ref[...], 0, g_q)))

        for g in range(g_q):
            rows = pl.ds(pl.multiple_of((sb * g_q + g) * tb, tb), tb)
            for hp in range(n_hp):
                a = hp * grp_w + g * tb
                gate = sg_ref[0, rows, hp * LANES:(hp + 1) * LANES].astype(F32)
                o_ref[0, rows, hp * LANES:(hp + 1) * LANES] = (acc_ref[:, a:a + tb].T * gate).astype(BF16)
        return carry

    lax.fori_loop(0, nblk // g_q, q_super_block, 0)


def _attn_call(qn, kn, vt, sg, *, g_q=4, n_hp=4):
    bsz, s, d_attn = qn.shape
    nblk = s // ATTN_BLOCK
    assert nblk % g_q == 0
    width = n_hp * LANES
    score_lanes = g_q * n_hp * 2 * ATTN_BLOCK
    seq = pl.BlockSpec((1, s, width), lambda b, h: (b, 0, h))
    return pl.pallas_call(
        functools.partial(_attn_kernel, nblk=nblk, g_q=g_q, n_hp=n_hp),
        grid=(bsz, d_attn // width),
        in_specs=[seq, seq, pl.BlockSpec((1, nblk, width, ATTN_BLOCK), lambda b, h: (b, 0, h, 0)), seq],
        out_specs=seq,
        out_shape=jax.ShapeDtypeStruct((bsz, s, d_attn), BF16),
        scratch_shapes=[pltpu.VMEM((1, score_lanes), F32),
                        pltpu.VMEM((LANES, g_q * n_hp * ATTN_BLOCK), F32)],
        compiler_params=pltpu.CompilerParams(
            dimension_semantics=("arbitrary", "arbitrary"), vmem_limit_bytes=VMEM_LIMIT_BYTES),
        name="stickbreak_attn",
    )(qn, kn, vt, sg)


X_RING = 3


def _out_kernel(a_ref, p_ref, x_hbm, mod_ref, w_ref, o_ref, wbf, xbuf, sem, *, ts, d_model, d_attn):
    n_t = pl.num_programs(1)
    step = pl.program_id(0) * n_t + pl.program_id(1)
    n_steps = pl.num_programs(0) * n_t

    def x_copy(n):
        slot = n % X_RING
        rows = pl.ds(pl.multiple_of((n % n_t) * ts, ts), ts)
        return pltpu.make_async_copy(x_hbm.at[n // n_t, rows, :], xbuf.at[slot], sem.at[slot])

    @pl.when(step == 0)
    def _():
        wbf[...] = w_ref[...].astype(BF16)
        for n in range(X_RING - 1):
            x_copy(n).start()

    @pl.when(step + (X_RING - 1) < n_steps)
    def _():
        x_copy(step + (X_RING - 1)).start()

    x_copy(step).wait()
    y = jnp.dot(a_ref[0], wbf[0:d_attn, :], preferred_element_type=F32)
    y = y + jnp.dot(p_ref[0], wbf[d_attn:, :], preferred_element_type=F32)
    gate = mod_ref[0, :, 2 * d_model:3 * d_model]
    o_ref[0] = xbuf[step % X_RING] + gate * y


def _out_call(attn, pool_out, x, mod3, w_out, *, ts=1024):
    bsz, s, d = x.shape
    d_attn = attn.shape[2]
    d_pool = pool_out.shape[2]
    assert bsz * (s // ts) >= X_RING - 1
    return pl.pallas_call(
        functools.partial(_out_kernel, ts=ts, d_model=d, d_attn=d_attn),
        grid=(bsz, s // ts),
        in_specs=[
            pl.BlockSpec((1, ts, d_attn), lambda b, t: (b, t, 0)),
            pl.BlockSpec((1, ts, d_pool), lambda b, t: (b, t, 0)),
            pl.BlockSpec(memory_space=pl.ANY),
            pl.BlockSpec((1, 1, mod3.shape[2]), lambda b, t: (b, 0, 0)),
            pl.BlockSpec(w_out.shape, lambda b, t: (0, 0), pipeline_mode=pl.Buffered(1)),
        ],
        out_specs=pl.BlockSpec((1, ts, d), lambda b, t: (b, t, 0)),
        out_shape=jax.ShapeDtypeStruct((bsz, s, d), F32),
        scratch_shapes=[pltpu.VMEM(w_out.shape, BF16),
                        pltpu.VMEM((X_RING, ts, d), F32),
                        pltpu.SemaphoreType.DMA((X_RING,))],
        compiler_params=pltpu.CompilerParams(
            dimension_semantics=("arbitrary", "arbitrary"), vmem_limit_bytes=VMEM_LIMIT_BYTES),
        name="outproj_residual",
    )(attn, pool_out, x, mod3, w_out)


def kernel(x, c, w_ada, b_ada, norm_g, w_in, q_norm_g, k_norm_g, w_pool, b_pool, pool_scale, w_out):
    depth = w_ada.shape[0]
    d_attn = w_out.shape[1] // 2
    n_heads = d_attn // HEAD_DIM
    h = x
    for l in range(depth):
        mod = _ada_call(c, w_ada[l], b_ada[l])
        mod3 = mod.reshape(mod.shape[0], 1, mod.shape[1])
        qg = jnp.tile(q_norm_g[l] * (HEAD_DIM ** -0.5 * LOG2E), n_heads).reshape(1, d_attn)
        kg = jnp.tile(k_norm_g[l], n_heads).reshape(1, d_attn)
        qn, kn, vt, sg, pool_out = _proj_call(
            mod3, h, norm_g[l].reshape(1, -1), w_in[l], qg, kg,
            w_pool[l], b_pool[l].reshape(1, -1), pool_scale[l].reshape(1, -1))
        attn = _attn_call(qn, kn, vt, sg)
        h = _out_call(attn, pool_out, h, mod3, w_out[l])
    return h
```

```python
import functools

import jax
import jax.numpy as jnp
from jax import lax
from jax.experimental import pallas as pl
from jax.experimental.pallas import tpu as pltpu

F32 = jnp.float32
BF16 = jnp.bfloat16

HEAD_DIM = 64
POOL_WINDOWS = (2, 4, 8, 16)
EPS = 1e-6
LOG2E = 1.4426950408889634
SOFTPLUS_CLAMP = 64.0
REM_FLOOR_LOG2 = -160.0
DIAG_LOOKAHEAD = 1
BAND = 3

LANES = 128
SUBLANES = 8
MXU_WIDTH = 256
ATTN_BLOCK = 128
POOL_HIST = max(POOL_WINDOWS)
POOL_PAD = 2 * POOL_HIST
VMEM_LIMIT_BYTES = 58 * 1024 * 1024


def _silu(v):
    h = 0.5 * v
    return h + h * jnp.tanh(h)


def _first_grid_step():
    return (pl.program_id(0) == 0) & (pl.program_id(1) == 0)


def _ada_kernel(c_ref, w_ref, b_ref, o_ref):
    c = c_ref[...]
    ca = _silu(c).astype(BF16)
    o_ref[...] = jnp.dot(ca, w_ref[...].astype(BF16), preferred_element_type=F32) + b_ref[...]


def _ada_call(c, w_ada, b_ada, *, tn=1024):
    bsz, d = c.shape
    n = w_ada.shape[1]
    return pl.pallas_call(
        _ada_kernel,
        grid=(n // tn,),
        in_specs=[
            pl.BlockSpec((bsz, d), lambda j: (0, 0)),
            pl.BlockSpec((d, tn), lambda j: (0, j)),
            pl.BlockSpec((1, tn), lambda j: (0, j)),
        ],
        out_specs=pl.BlockSpec((bsz, tn), lambda j: (0, j)),
        out_shape=jax.ShapeDtypeStruct((bsz, n), F32),
        compiler_params=pltpu.CompilerParams(
            dimension_semantics=("arbitrary",), vmem_limit_bytes=VMEM_LIMIT_BYTES),
        name="adaln_mod",
    )(c, w_ada, b_ada.reshape(1, n))


def _proj_kernel(mod_ref, x_ref, ng_ref, win_ref, qg_ref, kg_ref, wp_ref, bp_ref, ps_ref,
                 q_ref, k_ref, vt_ref, sg_ref, po_ref, wbf, wpbf, ubuf, sbuf_a, sbuf_b,
                 *, ts, d_model, d_attn, d_pool):
    t = pl.program_id(1)
    hist = slice(POOL_PAD - POOL_HIST, POOL_PAD)
    gdim = d_pool // len(POOL_WINDOWS)
    pool_pack = MXU_WIDTH // gdim

    @pl.when(_first_grid_step())
    def _():
        wbf[...] = win_ref[...].astype(BF16)
        wpbf[...] = jnp.zeros_like(wpbf)
        for g in range(len(POOL_WINDOWS)):
            p, i = divmod(g, pool_pack)
            wpbf[p, i * gdim:(i + 1) * gdim, i * gdim:(i + 1) * gdim] = wp_ref[g].astype(BF16)

    @pl.when(t == 0)
    def _():
        ubuf[0:POOL_PAD, :] = jnp.zeros((POOL_PAD, d_pool), F32)

    @pl.when(t > 0)
    def _():
        ubuf[hist, :] = ubuf[ts + POOL_PAD - POOL_HIST:ts + POOL_PAD, :]

    x = x_ref[0]
    ms = jnp.mean(x * x, axis=-1, keepdims=True)
    shift = mod_ref[0, :, 0:d_model]
    scale = mod_ref[0, :, d_model:2 * d_model]
    a = ng_ref[...] * (1.0 + scale)
    hn = ((x * lax.rsqrt(ms + EPS)) * a + shift).astype(BF16)

    def proj(col, width):
        return jnp.dot(hn, wbf[:, col:col + width], preferred_element_type=F32)

    u = proj(4 * d_attn, d_pool)
    g_pool = proj(4 * d_attn + d_pool, d_pool)
    ubuf[POOL_PAD:POOL_PAD + ts, :] = u
    n_lvl = len(POOL_WINDOWS)
    end = POOL_PAD + ts
    src = ubuf
    wsums = []
    for lvl in range(n_lvl):
        shift_rows = 2 ** lvl
        start = SUBLANES * (lvl + 1)
        lanes = slice(lvl * gdim, d_pool)
        summed = src[start:end, lanes] + src[start - shift_rows:end - shift_rows, lanes]
        wsums.append(summed[POOL_PAD - start:, 0:gdim])
        if lvl + 1 < n_lvl:
            dst = sbuf_a if lvl % 2 == 0 else sbuf_b
            dst[start:end, lanes] = summed
            src = dst

    row = lax.broadcasted_iota(jnp.int32, (POOL_HIST, gdim), 0)
    pooled = []
    for g, win in enumerate(POOL_WINDOWS):
        ug = u[:, g * gdim:(g + 1) * gdim]
        cnt = jnp.minimum(t * ts + row + 1, win).astype(F32)
        head = wsums[g][0:POOL_HIST] / cnt - ug[0:POOL_HIST]
        rest = wsums[g][POOL_HIST:] * (1.0 / win) - ug[POOL_HIST:]
        pooled.append(jnp.concatenate([head, rest], axis=0).astype(BF16))
    for p in range(len(POOL_WINDOWS) // pool_pack):
        lo, hi = p * MXU_WIDTH, (p + 1) * MXU_WIDTH
        mixed = jnp.dot(jnp.concatenate(pooled[p * pool_pack:(p + 1) * pool_pack], axis=1), wpbf[p],
                        preferred_element_type=F32) + bp_ref[:, lo:hi]
        po_ref[0, :, lo:hi] = (mixed * ps_ref[:, lo:hi] * _silu(g_pool[:, lo:hi])).astype(BF16)

    r = lax.broadcasted_iota(jnp.int32, (MXU_WIDTH, MXU_WIDTH), 0) // HEAD_DIM
    c = lax.broadcasted_iota(jnp.int32, (MXU_WIDTH, MXU_WIDTH), 1) // HEAD_DIM
    head_avg = jnp.where(r == c, 1.0 / HEAD_DIM, 0.0).astype(BF16)

    def head_norm(v, g):
        sq = (v * v).astype(BF16)
        msq = jnp.concatenate([jnp.dot(sq[:, lo:lo + MXU_WIDTH], head_avg, preferred_element_type=F32)
                               for lo in range(0, d_attn, MXU_WIDTH)], axis=1)
        return (v * lax.rsqrt(msq + EPS)) * g

    q_ref[0] = head_norm(proj(0, d_attn), qg_ref[...]).astype(BF16)
    k_ref[0] = head_norm(proj(d_attn, d_attn), kg_ref[...]).astype(BF16)

    vt = proj(2 * d_attn, d_attn).T.astype(BF16)
    for jj in range(ts // ATTN_BLOCK):
        vt_ref[0, jj] = vt[:, jj * ATTN_BLOCK:(jj + 1) * ATTN_BLOCK]

    sg_ref[0] = _silu(proj(3 * d_attn, d_attn)).astype(BF16)


def _proj_call(mod3, x, norm_g, w_in, qg, kg, w_pool, b_pool, pool_scale, *, ts=1024):
    bsz, s, d = x.shape
    d_attn = qg.shape[1]
    d_pool = pool_scale.shape[1]
    nt = s // ts
    nb = s // ATTN_BLOCK
    kern = functools.partial(_proj_kernel, ts=ts, d_model=d, d_attn=d_attn, d_pool=d_pool)
    const = lambda *shape: pl.BlockSpec(shape, lambda b, t: (0,) * len(shape))
    once = lambda *shape: pl.BlockSpec(shape, lambda b, t: (0,) * len(shape), pipeline_mode=pl.Buffered(1))
    seq_out = lambda width: pl.BlockSpec((1, ts, width), lambda b, t: (b, t, 0))
    pool_rows = POOL_PAD + ts
    return pl.pallas_call(
        kern,
        grid=(bsz, nt),
        in_specs=[
            pl.BlockSpec((1, 1, mod3.shape[2]), lambda b, t: (b, 0, 0)),
            pl.BlockSpec((1, ts, d), lambda b, t: (b, t, 0)),
            const(1, d),
            once(*w_in.shape),
            const(1, d_attn),
            const(1, d_attn),
            once(*w_pool.shape),
            const(1, d_pool),
            const(1, d_pool),
        ],
        out_specs=[
            seq_out(d_attn),
            seq_out(d_attn),
            pl.BlockSpec((1, ts // ATTN_BLOCK, d_attn, ATTN_BLOCK), lambda b, t: (b, t, 0, 0)),
            seq_out(d_attn),
            seq_out(d_pool),
        ],
        out_shape=[
            jax.ShapeDtypeStruct((bsz, s, d_attn), BF16),
            jax.ShapeDtypeStruct((bsz, s, d_attn), BF16),
            jax.ShapeDtypeStruct((bsz, nb, d_attn, ATTN_BLOCK), BF16),
            jax.ShapeDtypeStruct((bsz, s, d_attn), BF16),
            jax.ShapeDtypeStruct((bsz, s, d_pool), BF16),
        ],
        scratch_shapes=[pltpu.VMEM(w_in.shape, BF16),
                        pltpu.VMEM((d_pool // MXU_WIDTH, MXU_WIDTH, MXU_WIDTH), BF16),
                        pltpu.VMEM((pool_rows, d_pool), F32),
                        pltpu.VMEM((pool_rows, d_pool), F32),
                        pltpu.VMEM((pool_rows, d_pool), F32)],
        compiler_params=pltpu.CompilerParams(
            dimension_semantics=("arbitrary", "arbitrary"), vmem_limit_bytes=VMEM_LIMIT_BYTES),
        name="norm_inproj_pool",
    )(mod3, x, norm_g, w_in, qg, kg, w_pool, b_pool, pool_scale)


def _attn_kernel(q_ref, k_ref, vt_ref, sg_ref, o_ref, lr_ref, acc_ref, *, nblk, g_q, n_hp):
    tb = ATTN_BLOCK
    row = lax.broadcasted_iota(jnp.int32, (tb, tb), 0)
    col = lax.broadcasted_iota(jnp.int32, (tb, tb), 1)
    neg_suffix = jnp.concatenate([jnp.where(col > row, -1.0, 0.0), jnp.where(col == row, -1.0, 0.0)],
                                 axis=1).astype(BF16)
    causal = row < col
    n_grp = 2 * n_hp
    grp_w = g_q * tb

    def mask_diag(v, diag):
        if not diag:
            return v
        nq = v.shape[1] // (n_grp * tb)
        keep = causal[0:v.shape[0]]
        parts = []
        for grp in range(n_grp):
            base = grp * nq * tb
            parts.append(jnp.where(keep, v[:, base:base + tb], 0.0))
            if nq > 1:
                parts.append(v[:, base + tb:base + nq * tb])
        return jnp.concatenate(parts, axis=1)

    def blocks_alive(rem, q_lo, nq):
        n_alive = jnp.int32(0)
        for g in range(nq):
            blk = jnp.concatenate([rem[:, (grp * nq + g) * tb:(grp * nq + g + 1) * tb] for grp in range(n_grp)],
                                  axis=1)
            n_alive = jnp.where(jnp.max(blk) > REM_FLOOR_LOG2, q_lo + g + 1, n_alive)
        return n_alive

    def score(j, qts, q_lo, q_hi, diag):
        nq = q_hi - q_lo
        rows_j = pl.ds(pl.multiple_of(j * tb, tb), tb)
        zs = []
        for hp in range(n_hp):
            qt = qts[hp]
            if nq < g_q:
                qt = jnp.concatenate([qt[:, h * grp_w + q_lo * tb:h * grp_w + q_hi * tb] for h in range(2)],
                                     axis=1)
            zs.append(jnp.dot(k_ref[0, rows_j, hp * LANES:(hp + 1) * LANES], qt, preferred_element_type=F32))
        z = jnp.concatenate(zs, axis=1)
        sp = jnp.maximum(jnp.log(1.0 + jnp.exp2(jnp.minimum(z, SOFTPLUS_CLAMP))) * LOG2E, z)
        neg_log_beta = mask_diag(sp - z, diag)
        sp = mask_diag(sp, diag)
        return sp[0:1, :], jnp.concatenate([sp.astype(BF16), neg_log_beta.astype(BF16)], axis=0)

    def weigh(j, sp_0, terms, q_lo, q_hi, diag):
        nq = q_hi - q_lo
        lanes = [slice(grp * grp_w + q_lo * tb, grp * grp_w + q_hi * tb) for grp in range(n_grp)]
        log_w = jnp.dot(neg_suffix, terms, preferred_element_type=F32)
        log_rem = jnp.concatenate([lr_ref[:, ln] for ln in lanes], axis=1)
        w = mask_diag(jnp.exp2(log_w + log_rem), diag).astype(BF16)
        later_0 = -(log_w[0:1, :] + terms[tb:tb + 1, :].astype(F32))
        new_rem = log_rem - (sp_0 + later_0)
        for grp in range(n_grp):
            hp, h = divmod(grp, 2)
            cols = slice(grp * nq * tb, (grp + 1) * nq * tb)
            lr_ref[:, lanes[grp]] = new_rem[:, cols]
            vt_h = vt_ref[0, j, hp * LANES + h * HEAD_DIM:hp * LANES + (h + 1) * HEAD_DIM, :]
            res = jnp.dot(vt_h, w[:, cols], preferred_element_type=F32)
            acc_ref[h * HEAD_DIM:(h + 1) * HEAD_DIM, hp * grp_w + q_lo * tb:hp * grp_w + q_hi * tb] += res
        return blocks_alive(new_rem, q_lo, nq)

    def step(j, qts, q_lo, q_hi, diag):
        return weigh(j, *score(j, qts, q_lo, q_hi, diag), q_lo, q_hi, diag)

    def q_super_block(sb, carry):
        qts = []
        head_of_row = lax.broadcasted_iota(jnp.int32, (LANES, grp_w), 0) // HEAD_DIM
        for hp in range(n_hp):
            rows = pl.ds(pl.multiple_of(sb * grp_w, grp_w), grp_w)
            qt = q_ref[0, rows, hp * LANES:(hp + 1) * LANES].astype(F32).T
            qts.append(jnp.concatenate([jnp.where(head_of_row == h, qt, 0.0) for h in range(2)],
                                       axis=1).astype(BF16))
        lr_ref[...] = jnp.zeros_like(lr_ref)
        acc_ref[...] = jnp.zeros_like(acc_ref)
        base = sb * g_q

        def band(offsets):
            spans = [(m, max(m, 0), min(m + BAND, g_q), m >= 0) for m in offsets]
            scored = {}
            for i in range(len(spans) + DIAG_LOOKAHEAD):
                if i < len(spans):
                    m, lo, hi, diag = spans[i]
                    scored[m] = score(base + m, qts, lo, hi, diag)
                if i >= DIAG_LOOKAHEAD:
                    m, lo, hi, diag = spans[i - DIAG_LOOKAHEAD]
                    weigh(base + m, *scored.pop(m), lo, hi, diag)

        band(range(g_q - 1, -1, -1))

        @pl.when(sb > 0)
        def _():
            band(range(-1, -BAND, -1))

        def more(state):
            j, alive = state
            return (j >= 0) & (alive > 0)

        alive = [blocks_alive(jnp.concatenate([lr_ref[:, grp * grp_w + g * tb:grp * grp_w + (g + 1) * tb]
                                               for grp in range(n_grp)], axis=1), g, 1) for g in range(g_q)]
        for g in range(g_q):
            def one_block_step(state, g=g):
                j, _ = state
                return j - 1, step(j, qts, g, g + 1, False)

            lax.while_loop(more, one_block_step, (base + g - BAND, alive[g]))

        for g in range(g_q):
            rows = pl.ds(pl.multiple_of((sb * g_q + g) * tb, tb), tb)
            for hp in range(n_hp):
                a = hp * grp_w + g * tb
                gate = sg_ref[0, rows, hp * LANES:(hp + 1) * LANES].astype(F32)
                o_ref[0, rows, hp * LANES:(hp + 1) * LANES] = (acc_ref[:, a:a + tb].T * gate).astype(BF16)
        return carry

    lax.fori_loop(0, nblk // g_q, q_super_block, 0)


def _attn_call(qn, kn, vt, sg, *, g_q=4, n_hp=4):
    bsz, s, d_attn = qn.shape
    nblk = s // ATTN_BLOCK
    assert nblk % g_q == 0
    width = n_hp * LANES
    score_lanes = g_q * n_hp * 2 * ATTN_BLOCK
    seq = pl.BlockSpec((1, s, width), lambda b, h: (b, 0, h))
    return pl.pallas_call(
        functools.partial(_attn_kernel, nblk=nblk, g_q=g_q, n_hp=n_hp),
        grid=(bsz, d_attn // width),
        in_specs=[seq, seq, pl.BlockSpec((1, nblk, width, ATTN_BLOCK), lambda b, h: (b, 0, h, 0)), seq],
        out_specs=seq,
        out_shape=jax.ShapeDtypeStruct((bsz, s, d_attn), BF16),
        scratch_shapes=[pltpu.VMEM((1, score_lanes), F32),
                        pltpu.VMEM((LANES, g_q * n_hp * ATTN_BLOCK), F32)],
        compiler_params=pltpu.CompilerParams(
            dimension_semantics=("arbitrary", "arbitrary"), vmem_limit_bytes=VMEM_LIMIT_BYTES),
        name="stickbreak_attn",
    )(qn, kn, vt, sg)


X_RING = 3


def _out_kernel(a_ref, p_ref, x_hbm, mod_ref, w_ref, o_ref, wbf, xbuf, sem, *, ts, d_model, d_attn):
    n_t = pl.num_programs(1)
    step = pl.program_id(0) * n_t + pl.program_id(1)
    n_steps = pl.num_programs(0) * n_t

    def x_copy(n):
        slot = n % X_RING
        rows = pl.ds(pl.multiple_of((n % n_t) * ts, ts), ts)
        return pltpu.make_async_copy(x_hbm.at[n // n_t, rows, :], xbuf.at[slot], sem.at[slot])

    @pl.when(step == 0)
    def _():
        wbf[...] = w_ref[...].astype(BF16)
        for n in range(X_RING - 1):
            x_copy(n).start()

    @pl.when(step + (X_RING - 1) < n_steps)
    def _():
        x_copy(step + (X_RING - 1)).start()

    x_copy(step).wait()
    y = jnp.dot(a_ref[0], wbf[0:d_attn, :], preferred_element_type=F32)
    y = y + jnp.dot(p_ref[0], wbf[d_attn:, :], preferred_element_type=F32)
    gate = mod_ref[0, :, 2 * d_model:3 * d_model]
    o_ref[0] = xbuf[step % X_RING] + gate * y


def _out_call(attn, pool_out, x, mod3, w_out, *, ts=1024):
    bsz, s, d = x.shape
    d_attn = attn.shape[2]
    d_pool = pool_out.shape[2]
    assert bsz * (s // ts) >= X_RING - 1
    return pl.pallas_call(
        functools.partial(_out_kernel, ts=ts, d_model=d, d_attn=d_attn),
        grid=(bsz, s // ts),
        in_specs=[
            pl.BlockSpec((1, ts, d_attn), lambda b, t: (b, t, 0)),
            pl.BlockSpec((1, ts, d_pool), lambda b, t: (b, t, 0)),
            pl.BlockSpec(memory_space=pl.ANY),
            pl.BlockSpec((1, 1, mod3.shape[2]), lambda b, t: (b, 0, 0)),
            pl.BlockSpec(w_out.shape, lambda b, t: (0, 0), pipeline_mode=pl.Buffered(1)),
        ],
        out_specs=pl.BlockSpec((1, ts, d), lambda b, t: (b, t, 0)),
        out_shape=jax.ShapeDtypeStruct((bsz, s, d), F32),
        scratch_shapes=[pltpu.VMEM(w_out.shape, BF16),
                        pltpu.VMEM((X_RING, ts, d), F32),
                        pltpu.SemaphoreType.DMA((X_RING,))],
        compiler_params=pltpu.CompilerParams(
            dimension_semantics=("arbitrary", "arbitrary"), vmem_limit_bytes=VMEM_LIMIT_BYTES),
        name="outproj_residual",
    )(attn, pool_out, x, mod3, w_out)


def kernel(x, c, w_ada, b_ada, norm_g, w_in, q_norm_g, k_norm_g, w_pool, b_pool, pool_scale, w_out):
    depth = w_ada.shape[0]
    d_attn = w_out.shape[1] // 2
    n_heads = d_attn // HEAD_DIM
    h = x
    for l in range(depth):
        mod = _ada_call(c, w_ada[l], b_ada[l])
        mod3 = mod.reshape(mod.shape[0], 1, mod.shape[1])
        qg = jnp.tile(q_norm_g[l] * (HEAD_DIM ** -0.5 * LOG2E), n_heads).reshape(1, d_attn)
        kg = jnp.tile(k_norm_g[l], n_heads).reshape(1, d_attn)
        qn, kn, vt, sg, pool_out = _proj_call(
            mod3, h, norm_g[l].reshape(1, -1), w_in[l], qg, kg,
            w_pool[l], b_pool[l].reshape(1, -1), pool_scale[l].reshape(1, -1))
        attn = _attn_call(qn, kn, vt, sg)
        h = _out_call(attn, pool_out, h, mod3, w_out[l])
    return h
```

```python
import functools

import jax
import jax.numpy as jnp
from jax import lax
from jax.experimental import pallas as pl
from jax.experimental.pallas import tpu as pltpu

F32 = jnp.float32
BF16 = jnp.bfloat16

HEAD_DIM = 64
POOL_WINDOWS = (2, 4, 8, 16)
EPS = 1e-6
LOG2E = 1.4426950408889634
SOFTPLUS_CLAMP = 64.0
REM_FLOOR_LOG2 = -160.0
DIAG_LOOKAHEAD = 1
BAND = 3

LANES = 128
SUBLANES = 8
MXU_WIDTH = 256
ATTN_BLOCK = 128
POOL_HIST = max(POOL_WINDOWS)
POOL_PAD = 2 * POOL_HIST
VMEM_LIMIT_BYTES = 58 * 1024 * 1024


def _silu(v):
    h = 0.5 * v
    return h + h * jnp.tanh(h)


def _first_grid_step():
    return (pl.program_id(0) == 0) & (pl.program_id(1) == 0)


def _ada_kernel(c_ref, w_ref, b_ref, o_ref):
    c = c_ref[...]
    ca = _silu(c).astype(BF16)
    o_ref[...] = jnp.dot(ca, w_ref[...].astype(BF16), preferred_element_type=F32) + b_ref[...]


def _ada_call(c, w_ada, b_ada, *, tn=1024):
    bsz, d = c.shape
    n = w_ada.shape[1]
    return pl.pallas_call(
        _ada_kernel,
        grid=(n // tn,),
        in_specs=[
            pl.BlockSpec((bsz, d), lambda j: (0, 0)),
            pl.BlockSpec((d, tn), lambda j: (0, j)),
            pl.BlockSpec((1, tn), lambda j: (0, j)),
        ],
        out_specs=pl.BlockSpec((bsz, tn), lambda j: (0, j)),
        out_shape=jax.ShapeDtypeStruct((bsz, n), F32),
        compiler_params=pltpu.CompilerParams(
            dimension_semantics=("arbitrary",), vmem_limit_bytes=VMEM_LIMIT_BYTES),
        name="adaln_mod",
    )(c, w_ada, b_ada.reshape(1, n))


def _proj_kernel(mod_ref, x_ref, ng_ref, win_ref, qg_ref, kg_ref, wp_ref, bp_ref, ps_ref,
                 q_ref, k_ref, vt_ref, sg_ref, po_ref, wbf, wpbf, ubuf, sbuf_a, sbuf_b,
                 *, ts, d_model, d_attn, d_pool):
    t = pl.program_id(1)
    hist = slice(POOL_PAD - POOL_HIST, POOL_PAD)
    gdim = d_pool // len(POOL_WINDOWS)
    pool_pack = MXU_WIDTH // gdim

    @pl.when(_first_grid_step())
    def _():
        wbf[...] = win_ref[...].astype(BF16)
        wpbf[...] = jnp.zeros_like(wpbf)
        for g in range(len(POOL_WINDOWS)):
            p, i = divmod(g, pool_pack)
            wpbf[p, i * gdim:(i + 1) * gdim, i * gdim:(i + 1) * gdim] = wp_ref[g].astype(BF16)

    @pl.when(t == 0)
    def _():
        ubuf[0:POOL_PAD, :] = jnp.zeros((POOL_PAD, d_pool), F32)

    @pl.when(t > 0)
    def _():
        ubuf[hist, :] = ubuf[ts + POOL_PAD - POOL_HIST:ts + POOL_PAD, :]

    x = x_ref[0]
    ms = jnp.mean(x * x, axis=-1, keepdims=True)
    shift = mod_ref[0, :, 0:d_model]
    scale = mod_ref[0, :, d_model:2 * d_model]
    a = ng_ref[...] * (1.0 + scale)
    hn = ((x * lax.rsqrt(ms + EPS)) * a + shift).astype(BF16)

    def proj(col, width):
        return jnp.dot(hn, wbf[:, col:col + width], preferred_element_type=F32)

    u = proj(4 * d_attn, d_pool)
    g_pool = proj(4 * d_attn + d_pool, d_pool)
    ubuf[POOL_PAD:POOL_PAD + ts, :] = u
    n_lvl = len(POOL_WINDOWS)
    end = POOL_PAD + ts
    src = ubuf
    wsums = []
    for lvl in range(n_lvl):
        shift_rows = 2 ** lvl
        start = SUBLANES * (lvl + 1)
        lanes = slice(lvl * gdim, d_pool)
        summed = src[start:end, lanes] + src[start - shift_rows:end - shift_rows, lanes]
        wsums.append(summed[POOL_PAD - start:, 0:gdim])
        if lvl + 1 < n_lvl:
            dst = sbuf_a if lvl % 2 == 0 else sbuf_b
            dst[start:end, lanes] = summed
            src = dst

    row = lax.broadcasted_iota(jnp.int32, (POOL_HIST, gdim), 0)
    pooled = []
    for g, win in enumerate(POOL_WINDOWS):
        ug = u[:, g * gdim:(g + 1) * gdim]
        cnt = jnp.minimum(t * ts + row + 1, win).astype(F32)
        head = wsums[g][0:POOL_HIST] / cnt - ug[0:POOL_HIST]
        rest = wsums[g][POOL_HIST:] * (1.0 / win) - ug[POOL_HIST:]
        pooled.append(jnp.concatenate([head, rest], axis=0).astype(BF16))
    for p in range(len(POOL_WINDOWS) // pool_pack):
        lo, hi = p * MXU_WIDTH, (p + 1) * MXU_WIDTH
        mixed = jnp.dot(jnp.concatenate(pooled[p * pool_pack:(p + 1) * pool_pack], axis=1), wpbf[p],
                        preferred_element_type=F32) + bp_ref[:, lo:hi]
        po_ref[0, :, lo:hi] = (mixed * ps_ref[:, lo:hi] * _silu(g_pool[:, lo:hi])).astype(BF16)

    r = lax.broadcasted_iota(jnp.int32, (MXU_WIDTH, MXU_WIDTH), 0) // HEAD_DIM
    c = lax.broadcasted_iota(jnp.int32, (MXU_WIDTH, MXU_WIDTH), 1) // HEAD_DIM
    head_avg = jnp.where(r == c, 1.0 / HEAD_DIM, 0.0).astype(BF16)

    def head_norm(v, g):
        sq = (v * v).astype(BF16)
        msq = jnp.concatenate([jnp.dot(sq[:, lo:lo + MXU_WIDTH], head_avg, preferred_element_type=F32)
                               for lo in range(0, d_attn, MXU_WIDTH)], axis=1)
        return (v * lax.rsqrt(msq + EPS)) * g

    q_ref[0] = head_norm(proj(0, d_attn), qg_ref[...]).astype(BF16)
    k_ref[0] = head_norm(proj(d_attn, d_attn), kg_ref[...]).astype(BF16)

    vt = proj(2 * d_attn, d_attn).T.astype(BF16)
    for jj in range(ts // ATTN_BLOCK):
        vt_ref[0, jj] = vt[:, jj * ATTN_BLOCK:(jj + 1) * ATTN_BLOCK]

    sg_ref[0] = _silu(proj(3 * d_attn, d_attn)).astype(BF16)


def _proj_call(mod3, x, norm_g, w_in, qg, kg, w_pool, b_pool, pool_scale, *, ts=1024):
    bsz, s, d = x.shape
    d_attn = qg.shape[1]
    d_pool = pool_scale.shape[1]
    nt = s // ts
    nb = s // ATTN_BLOCK
    kern = functools.partial(_proj_kernel, ts=ts, d_model=d, d_attn=d_attn, d_pool=d_pool)
    const = lambda *shape: pl.BlockSpec(shape, lambda b, t: (0,) * len(shape))
    once = lambda *shape: pl.BlockSpec(shape, lambda b, t: (0,) * len(shape), pipeline_mode=pl.Buffered(1))
    seq_out = lambda width: pl.BlockSpec((1, ts, width), lambda b, t: (b, t, 0))
    pool_rows = POOL_PAD + ts
    return pl.pallas_call(
        kern,
        grid=(bsz, nt),
        in_specs=[
            pl.BlockSpec((1, 1, mod3.shape[2]), lambda b, t: (b, 0, 0)),
            pl.BlockSpec((1, ts, d), lambda b, t: (b, t, 0)),
            const(1, d),
            once(*w_in.shape),
            const(1, d_attn),
            const(1, d_attn),
            once(*w_pool.shape),
            const(1, d_pool),
            const(1, d_pool),
        ],
        out_specs=[
            seq_out(d_attn),
            seq_out(d_attn),
            pl.BlockSpec((1, ts // ATTN_BLOCK, d_attn, ATTN_BLOCK), lambda b, t: (b, t, 0, 0)),
            seq_out(d_attn),
            seq_out(d_pool),
        ],
        out_shape=[
            jax.ShapeDtypeStruct((bsz, s, d_attn), BF16),
            jax.ShapeDtypeStruct((bsz, s, d_attn), BF16),
            jax.ShapeDtypeStruct((bsz, nb, d_attn, ATTN_BLOCK), BF16),
            jax.ShapeDtypeStruct((bsz, s, d_attn), BF16),
            jax.ShapeDtypeStruct((bsz, s, d_pool), BF16),
        ],
        scratch_shapes=[pltpu.VMEM(w_in.shape, BF16),
                        pltpu.VMEM((d_pool // MXU_WIDTH, MXU_WIDTH, MXU_WIDTH), BF16),
                        pltpu.VMEM((pool_rows, d_pool), F32),
                        pltpu.VMEM((pool_rows, d_pool), F32),
                        pltpu.VMEM((pool_rows, d_pool), F32)],
        compiler_params=pltpu.CompilerParams(
            dimension_semantics=("arbitrary", "arbitrary"), vmem_limit_bytes=VMEM_LIMIT_BYTES),
        name="norm_inproj_pool",
    )(mod3, x, norm_g, w_in, qg, kg, w_pool, b_pool, pool_scale)


def _attn_kernel(q_ref, k_ref, vt_ref, sg_ref, o_ref, lr_ref, acc_ref, *, nblk, g_q, n_hp):
    tb = ATTN_BLOCK
    row = lax.broadcasted_iota(jnp.int32, (tb, tb), 0)
    col = lax.broadcasted_iota(jnp.int32, (tb, tb), 1)
    neg_suffix = jnp.concatenate([jnp.where(col > row, -1.0, 0.0), jnp.where(col == row, -1.0, 0.0)],
                                 axis=1).astype(BF16)
    causal = row < col
    n_grp = 2 * n_hp
    grp_w = g_q * tb

    def mask_diag(v, diag):
        if not diag:
            return v
        nq = v.shape[1] // (n_grp * tb)
        keep = causal[0:v.shape[0]]
        parts = []
        for grp in range(n_grp):
            base = grp * nq * tb
            parts.append(jnp.where(keep, v[:, base:base + tb], 0.0))
            if nq > 1:
                parts.append(v[:, base + tb:base + nq * tb])
        return jnp.concatenate(parts, axis=1)

    def blocks_alive(rem, q_lo, nq):
        n_alive = jnp.int32(0)
        for g in range(nq):
            blk = jnp.concatenate([rem[:, (grp * nq + g) * tb:(grp * nq + g + 1) * tb] for grp in range(n_grp)],
                                  axis=1)
            n_alive = jnp.where(jnp.max(blk) > REM_FLOOR_LOG2, q_lo + g + 1, n_alive)
        return n_alive

    def score(j, qts, q_lo, q_hi, diag):
        nq = q_hi - q_lo
        rows_j = pl.ds(pl.multiple_of(j * tb, tb), tb)
        zs = []
        for hp in range(n_hp):
            qt = qts[hp]
            if nq < g_q:
                qt = jnp.concatenate([qt[:, h * grp_w + q_lo * tb:h * grp_w + q_hi * tb] for h in range(2)],
                                     axis=1)
            zs.append(jnp.dot(k_ref[0, rows_j, hp * LANES:(hp + 1) * LANES], qt, preferred_element_type=F32))
        z = jnp.concatenate(zs, axis=1)
        sp = jnp.maximum(jnp.log(1.0 + jnp.exp2(jnp.minimum(z, SOFTPLUS_CLAMP))) * LOG2E, z)
        neg_log_beta = mask_diag(sp - z, diag)
        sp = mask_diag(sp, diag)
        return sp[0:1, :], jnp.concatenate([sp.astype(BF16), neg_log_beta.astype(BF16)], axis=0)

    def weigh(j, sp_0, terms, q_lo, q_hi, diag):
        nq = q_hi - q_lo
        lanes = [slice(grp * grp_w + q_lo * tb, grp * grp_w + q_hi * tb) for grp in range(n_grp)]
        log_w = jnp.dot(neg_suffix, terms, preferred_element_type=F32)
        log_rem = jnp.concatenate([lr_ref[:, ln] for ln in lanes], axis=1)
        w = mask_diag(jnp.exp2(log_w + log_rem), diag).astype(BF16)
        later_0 = -(log_w[0:1, :] + terms[tb:tb + 1, :].astype(F32))
        new_rem = log_rem - (sp_0 + later_0)
        for grp in range(n_grp):
            hp, h = divmod(grp, 2)
            cols = slice(grp * nq * tb, (grp + 1) * nq * tb)
            lr_ref[:, lanes[grp]] = new_rem[:, cols]
            vt_h = vt_ref[0, j, hp * LANES + h * HEAD_DIM:hp * LANES + (h + 1) * HEAD_DIM, :]
            res = jnp.dot(vt_h, w[:, cols], preferred_element_type=F32)
            acc_ref[h * HEAD_DIM:(h + 1) * HEAD_DIM, hp * grp_w + q_lo * tb:hp * grp_w + q_hi * tb] += res
        return blocks_alive(new_rem, q_lo, nq)

    def step(j, qts, q_lo, q_hi, diag):
        return weigh(j, *score(j, qts, q_lo, q_hi, diag), q_lo, q_hi, diag)

    def q_super_block(sb, carry):
        qts = []
        head_of_row = lax.broadcasted_iota(jnp.int32, (LANES, grp_w), 0) // HEAD_DIM
        for hp in range(n_hp):
            rows = pl.ds(pl.multiple_of(sb * grp_w, grp_w), grp_w)
            qt = q_ref[0, rows, hp * LANES:(hp + 1) * LANES].astype(F32).T
            qts.append(jnp.concatenate([jnp.where(head_of_row == h, qt, 0.0) for h in range(2)],
                                       axis=1).astype(BF16))
        lr_ref[...] = jnp.zeros_like(lr_ref)
        acc_ref[...] = jnp.zeros_like(acc_ref)
        base = sb * g_q

        def band(offsets):
            spans = [(m, max(m, 0), min(m + BAND, g_q), m >= 0) for m in offsets]
            scored = {}
            for i in range(len(spans) + DIAG_LOOKAHEAD):
                if i < len(spans):
                    m, lo, hi, diag = spans[i]
                    scored[m] = score(base + m, qts, lo, hi, diag)
                if i >= DIAG_LOOKAHEAD:
                    m, lo, hi, diag = spans[i - DIAG_LOOKAHEAD]
                    weigh(base + m, *scored.pop(m), lo, hi, diag)

        band(range(g_q - 1, -1, -1))

        @pl.when(sb > 0)
        def _():
            band(range(-1, -BAND, -1))

        def more(state):
            j, alive = state
            return (j >= 0) & (alive > 0)

        alive = [blocks_alive(jnp.concatenate([lr_ref[:, grp * grp_w + g * tb:grp * grp_w + (g + 1) * tb]
                                               for grp in range(n_grp)], axis=1), g, 1) for g in range(g_q)]
        for g in range(g_q):
            def one_block_step(state, g=g):
                j, _ = state
                return j - 1, step(j, qts, g, g + 1, False)

            lax.while_loop(more, one_block_step, (base + g - BAND, alive[g]))

        for g in range(g_q):
            rows = pl.ds(pl.multiple_of((sb * g_q + g) * tb, tb), tb)
            for hp in range(n_hp):
                a = hp * grp_w + g * tb
                gate = sg_ref[0, rows, hp * LANES:(hp + 1) * LANES].astype(F32)
                o_ref[0, rows, hp * LANES:(hp + 1) * LANES] = (acc_ref[:, a:a + tb].T * gate).astype(BF16)
        return carry

    lax.fori_loop(0, nblk // g_q, q_super_block, 0)


def _attn_call(qn, kn, vt, sg, *, g_q=8, n_hp=4):
    bsz, s, d_attn = qn.shape
    nblk = s // ATTN_BLOCK
    assert nblk % g_q == 0
    width = n_hp * LANES
    score_lanes = g_q * n_hp * 2 * ATTN_BLOCK
    seq = pl.BlockSpec((1, s, width), lambda b, h: (b, 0, h))
    return pl.pallas_call(
        functools.partial(_attn_kernel, nblk=nblk, g_q=g_q, n_hp=n_hp),
        grid=(bsz, d_attn // width),
        in_specs=[seq, seq, pl.BlockSpec((1, nblk, width, ATTN_BLOCK), lambda b, h: (b, 0, h, 0)), seq],
        out_specs=seq,
        out_shape=jax.ShapeDtypeStruct((bsz, s, d_attn), BF16),
        scratch_shapes=[pltpu.VMEM((1, score_lanes), F32),
                        pltpu.VMEM((LANES, g_q * n_hp * ATTN_BLOCK), F32)],
        compiler_params=pltpu.CompilerParams(
            dimension_semantics=("arbitrary", "arbitrary"), vmem_limit_bytes=VMEM_LIMIT_BYTES),
        name="stickbreak_attn",
    )(qn, kn, vt, sg)


X_RING = 3


def _out_kernel(a_ref, p_ref, x_hbm, mod_ref, w_ref, o_ref, wbf, xbuf, sem, *, ts, d_model, d_attn):
    n_t = pl.num_programs(1)
    step = pl.program_id(0) * n_t + pl.program_id(1)
    n_steps = pl.num_programs(0) * n_t

    def x_copy(n):
        slot = n % X_RING
        rows = pl.ds(pl.multiple_of((n % n_t) * ts, ts), ts)
        return pltpu.make_async_copy(x_hbm.at[n // n_t, rows, :], xbuf.at[slot], sem.at[slot])

    @pl.when(step == 0)
    def _():
        wbf[...] = w_ref[...].astype(BF16)
        for n in range(X_RING - 1):
            x_copy(n).start()

    @pl.when(step + (X_RING - 1) < n_steps)
    def _():
        x_copy(step + (X_RING - 1)).start()

    x_copy(step).wait()
    y = jnp.dot(a_ref[0], wbf[0:d_attn, :], preferred_element_type=F32)
    y = y + jnp.dot(p_ref[0], wbf[d_attn:, :], preferred_element_type=F32)
    gate = mod_ref[0, :, 2 * d_model:3 * d_model]
    o_ref[0] = xbuf[step % X_RING] + gate * y


def _out_call(attn, pool_out, x, mod3, w_out, *, ts=1024):
    bsz, s, d = x.shape
    d_attn = attn.shape[2]
    d_pool = pool_out.shape[2]
    assert bsz * (s // ts) >= X_RING - 1
    return pl.pallas_call(
        functools.partial(_out_kernel, ts=ts, d_model=d, d_attn=d_attn),
        grid=(bsz, s // ts),
        in_specs=[
            pl.BlockSpec((1, ts, d_attn), lambda b, t: (b, t, 0)),
            pl.BlockSpec((1, ts, d_pool), lambda b, t: (b, t, 0)),
            pl.BlockSpec(memory_space=pl.ANY),
            pl.BlockSpec((1, 1, mod3.shape[2]), lambda b, t: (b, 0, 0)),
            pl.BlockSpec(w_out.shape, lambda b, t: (0, 0), pipeline_mode=pl.Buffered(1)),
        ],
        out_specs=pl.BlockSpec((1, ts, d), lambda b, t: (b, t, 0)),
        out_shape=jax.ShapeDtypeStruct((bsz, s, d), F32),
        scratch_shapes=[pltpu.VMEM(w_out.shape, BF16),
                        pltpu.VMEM((X_RING, ts, d), F32),
                        pltpu.SemaphoreType.DMA((X_RING,))],
        compiler_params=pltpu.CompilerParams(
            dimension_semantics=("arbitrary", "arbitrary"), vmem_limit_bytes=VMEM_LIMIT_BYTES),
        name="outproj_residual",
    )(attn, pool_out, x, mod3, w_out)


def kernel(x, c, w_ada, b_ada, norm_g, w_in, q_norm_g, k_norm_g, w_pool, b_pool, pool_scale, w_out):
    depth = w_ada.shape[0]
    d_attn = w_out.shape[1] // 2
    n_heads = d_attn // HEAD_DIM
    h = x
    for l in range(depth):
        mod = _ada_call(c, w_ada[l], b_ada[l])
        mod3 = mod.reshape(mod.shape[0], 1, mod.shape[1])
        qg = jnp.tile(q_norm_g[l] * (HEAD_DIM ** -0.5 * LOG2E), n_heads).reshape(1, d_attn)
        kg = jnp.tile(k_norm_g[l], n_heads).reshape(1, d_attn)
        qn, kn, vt, sg, pool_out = _proj_call(
            mod3, h, norm_g[l].reshape(1, -1), w_in[l], qg, kg,
            w_pool[l], b_pool[l].reshape(1, -1), pool_scale[l].reshape(1, -1))
        attn = _attn_call(qn, kn, vt, sg)
        h = _out_call(attn, pool_out, h, mod3, w_out[l])
    return h
```

```python
import functools

import jax
import jax.numpy as jnp
from jax import lax
from jax.experimental import pallas as pl
from jax.experimental.pallas import tpu as pltpu

F32 = jnp.float32
BF16 = jnp.bfloat16

HEAD_DIM = 64
POOL_WINDOWS = (2, 4, 8, 16)
EPS = 1e-6
LOG2E = 1.4426950408889634
SOFTPLUS_CLAMP = 64.0
REM_FLOOR_LOG2 = -160.0
DIAG_LOOKAHEAD = 1
BAND = 3

LANES = 128
SUBLANES = 8
MXU_WIDTH = 256
ATTN_BLOCK = 128
POOL_HIST = max(POOL_WINDOWS)
POOL_PAD = 2 * POOL_HIST
VMEM_LIMIT_BYTES = 58 * 1024 * 1024


def _silu(v):
    h = 0.5 * v
    return h + h * jnp.tanh(h)


def _first_grid_step():
    return (pl.program_id(0) == 0) & (pl.program_id(1) == 0)


def _ada_kernel(c_ref, w_ref, b_ref, o_ref):
    c = c_ref[...]
    ca = _silu(c).astype(BF16)
    o_ref[...] = jnp.dot(ca, w_ref[...].astype(BF16), preferred_element_type=F32) + b_ref[...]


def _ada_call(c, w_ada, b_ada, *, tn=1024):
    bsz, d = c.shape
    n = w_ada.shape[1]
    return pl.pallas_call(
        _ada_kernel,
        grid=(n // tn,),
        in_specs=[
            pl.BlockSpec((bsz, d), lambda j: (0, 0)),
            pl.BlockSpec((d, tn), lambda j: (0, j)),
            pl.BlockSpec((1, tn), lambda j: (0, j)),
        ],
        out_specs=pl.BlockSpec((bsz, tn), lambda j: (0, j)),
        out_shape=jax.ShapeDtypeStruct((bsz, n), F32),
        compiler_params=pltpu.CompilerParams(
            dimension_semantics=("arbitrary",), vmem_limit_bytes=VMEM_LIMIT_BYTES),
        name="adaln_mod",
    )(c, w_ada, b_ada.reshape(1, n))


def _proj_kernel(mod_ref, x_ref, ng_ref, win_ref, qg_ref, kg_ref, wp_ref, bp_ref, ps_ref,
                 q_ref, k_ref, vt_ref, sg_ref, po_ref, wbf, wpbf, ubuf, sbuf_a, sbuf_b,
                 *, ts, d_model, d_attn, d_pool):
    t = pl.program_id(1)
    hist = slice(POOL_PAD - POOL_HIST, POOL_PAD)
    gdim = d_pool // len(POOL_WINDOWS)
    pool_pack = MXU_WIDTH // gdim

    @pl.when(_first_grid_step())
    def _():
        wbf[...] = win_ref[...].astype(BF16)
        wpbf[...] = jnp.zeros_like(wpbf)
        for g in range(len(POOL_WINDOWS)):
            p, i = divmod(g, pool_pack)
            wpbf[p, i * gdim:(i + 1) * gdim, i * gdim:(i + 1) * gdim] = wp_ref[g].astype(BF16)

    @pl.when(t == 0)
    def _():
        ubuf[0:POOL_PAD, :] = jnp.zeros((POOL_PAD, d_pool), F32)

    @pl.when(t > 0)
    def _():
        ubuf[hist, :] = ubuf[ts + POOL_PAD - POOL_HIST:ts + POOL_PAD, :]

    x = x_ref[0]
    ms = jnp.mean(x * x, axis=-1, keepdims=True)
    shift = mod_ref[0, :, 0:d_model]
    scale = mod_ref[0, :, d_model:2 * d_model]
    a = ng_ref[...] * (1.0 + scale)
    hn = ((x * lax.rsqrt(ms + EPS)) * a + shift).astype(BF16)

    def proj(col, width):
        return jnp.dot(hn, wbf[:, col:col + width], preferred_element_type=F32)

    u = proj(4 * d_attn, d_pool)
    g_pool = proj(4 * d_attn + d_pool, d_pool)
    ubuf[POOL_PAD:POOL_PAD + ts, :] = u
    n_lvl = len(POOL_WINDOWS)
    end = POOL_PAD + ts
    src = ubuf
    wsums = []
    for lvl in range(n_lvl):
        shift_rows = 2 ** lvl
        start = SUBLANES * (lvl + 1)
        lanes = slice(lvl * gdim, d_pool)
        summed = src[start:end, lanes] + src[start - shift_rows:end - shift_rows, lanes]
        wsums.append(summed[POOL_PAD - start:, 0:gdim])
        if lvl + 1 < n_lvl:
            dst = sbuf_a if lvl % 2 == 0 else sbuf_b
            dst[start:end, lanes] = summed
            src = dst

    row = lax.broadcasted_iota(jnp.int32, (POOL_HIST, gdim), 0)
    pooled = []
    for g, win in enumerate(POOL_WINDOWS):
        ug = u[:, g * gdim:(g + 1) * gdim]
        cnt = jnp.minimum(t * ts + row + 1, win).astype(F32)
        head = wsums[g][0:POOL_HIST] / cnt - ug[0:POOL_HIST]
        rest = wsums[g][POOL_HIST:] * (1.0 / win) - ug[POOL_HIST:]
        pooled.append(jnp.concatenate([head, rest], axis=0).astype(BF16))
    for p in range(len(POOL_WINDOWS) // pool_pack):
        lo, hi = p * MXU_WIDTH, (p + 1) * MXU_WIDTH
        mixed = jnp.dot(jnp.concatenate(pooled[p * pool_pack:(p + 1) * pool_pack], axis=1), wpbf[p],
                        preferred_element_type=F32) + bp_ref[:, lo:hi]
        po_ref[0, :, lo:hi] = (mixed * ps_ref[:, lo:hi] * _silu(g_pool[:, lo:hi])).astype(BF16)

    r = lax.broadcasted_iota(jnp.int32, (MXU_WIDTH, MXU_WIDTH), 0) // HEAD_DIM
    c = lax.broadcasted_iota(jnp.int32, (MXU_WIDTH, MXU_WIDTH), 1) // HEAD_DIM
    head_avg = jnp.where(r == c, 1.0 / HEAD_DIM, 0.0).astype(BF16)

    def head_norm(v, g):
        sq = (v * v).astype(BF16)
        msq = jnp.concatenate([jnp.dot(sq[:, lo:lo + MXU_WIDTH], head_avg, preferred_element_type=F32)
                               for lo in range(0, d_attn, MXU_WIDTH)], axis=1)
        return (v * lax.rsqrt(msq + EPS)) * g

    q_ref[0] = head_norm(proj(0, d_attn), qg_ref[...]).astype(BF16)
    k_ref[0] = head_norm(proj(d_attn, d_attn), kg_ref[...]).astype(BF16)

    vt = proj(2 * d_attn, d_attn).T.astype(BF16)
    for jj in range(ts // ATTN_BLOCK):
        vt_ref[0, jj] = vt[:, jj * ATTN_BLOCK:(jj + 1) * ATTN_BLOCK]

    sg_ref[0] = _silu(proj(3 * d_attn, d_attn)).astype(BF16)


def _proj_call(mod3, x, norm_g, w_in, qg, kg, w_pool, b_pool, pool_scale, *, ts=1024):
    bsz, s, d = x.shape
    d_attn = qg.shape[1]
    d_pool = pool_scale.shape[1]
    nt = s // ts
    nb = s // ATTN_BLOCK
    kern = functools.partial(_proj_kernel, ts=ts, d_model=d, d_attn=d_attn, d_pool=d_pool)
    const = lambda *shape: pl.BlockSpec(shape, lambda b, t: (0,) * len(shape))
    once = lambda *shape: pl.BlockSpec(shape, lambda b, t: (0,) * len(shape), pipeline_mode=pl.Buffered(1))
    seq_out = lambda width: pl.BlockSpec((1, ts, width), lambda b, t: (b, t, 0))
    pool_rows = POOL_PAD + ts
    return pl.pallas_call(
        kern,
        grid=(bsz, nt),
        in_specs=[
            pl.BlockSpec((1, 1, mod3.shape[2]), lambda b, t: (b, 0, 0)),
            pl.BlockSpec((1, ts, d), lambda b, t: (b, t, 0)),
            const(1, d),
            once(*w_in.shape),
            const(1, d_attn),
            const(1, d_attn),
            once(*w_pool.shape),
            const(1, d_pool),
            const(1, d_pool),
        ],
        out_specs=[
            seq_out(d_attn),
            seq_out(d_attn),
            pl.BlockSpec((1, ts // ATTN_BLOCK, d_attn, ATTN_BLOCK), lambda b, t: (b, t, 0, 0)),
            seq_out(d_attn),
            seq_out(d_pool),
        ],
        out_shape=[
            jax.ShapeDtypeStruct((bsz, s, d_attn), BF16),
            jax.ShapeDtypeStruct((bsz, s, d_attn), BF16),
            jax.ShapeDtypeStruct((bsz, nb, d_attn, ATTN_BLOCK), BF16),
            jax.ShapeDtypeStruct((bsz, s, d_attn), BF16),
            jax.ShapeDtypeStruct((bsz, s, d_pool), BF16),
        ],
        scratch_shapes=[pltpu.VMEM(w_in.shape, BF16),
                        pltpu.VMEM((d_pool // MXU_WIDTH, MXU_WIDTH, MXU_WIDTH), BF16),
                        pltpu.VMEM((pool_rows, d_pool), F32),
                        pltpu.VMEM((pool_rows, d_pool), F32),
                        pltpu.VMEM((pool_rows, d_pool), F32)],
        compiler_params=pltpu.CompilerParams(
            dimension_semantics=("arbitrary", "arbitrary"), vmem_limit_bytes=VMEM_LIMIT_BYTES),
        name="norm_inproj_pool",
    )(mod3, x, norm_g, w_in, qg, kg, w_pool, b_pool, pool_scale)


def _attn_kernel(q_ref, k_ref, vt_ref, sg_ref, o_ref, lr_ref, acc_ref, *, nblk, g_q, n_hp):
    tb = ATTN_BLOCK
    row = lax.broadcasted_iota(jnp.int32, (tb, tb), 0)
    col = lax.broadcasted_iota(jnp.int32, (tb, tb), 1)
    neg_suffix = jnp.concatenate([jnp.where(col > row, -1.0, 0.0), jnp.where(col == row, -1.0, 0.0)],
                                 axis=1).astype(BF16)
    causal = row < col
    n_grp = 2 * n_hp
    grp_w = g_q * tb

    def mask_diag(v, diag):
        if not diag:
            return v
        nq = v.shape[1] // (n_grp * tb)
        keep = causal[0:v.shape[0]]
        parts = []
        for grp in range(n_grp):
            base = grp * nq * tb
            parts.append(jnp.where(keep, v[:, base:base + tb], 0.0))
            if nq > 1:
                parts.append(v[:, base + tb:base + nq * tb])
        return jnp.concatenate(parts, axis=1)

    def blocks_alive(rem, q_lo, nq):
        n_alive = jnp.int32(0)
        for g in range(nq):
            blk = jnp.concatenate([rem[:, (grp * nq + g) * tb:(grp * nq + g + 1) * tb] for grp in range(n_grp)],
                                  axis=1)
            n_alive = jnp.where(jnp.max(blk) > REM_FLOOR_LOG2, q_lo + g + 1, n_alive)
        return n_alive

    def score(j, qts, q_lo, q_hi, diag):
        nq = q_hi - q_lo
        rows_j = pl.ds(pl.multiple_of(j * tb, tb), tb)
        zs = []
        for hp in range(n_hp):
            qt = qts[hp]
            if nq < g_q:
                qt = jnp.concatenate([qt[:, h * grp_w + q_lo * tb:h * grp_w + q_hi * tb] for h in range(2)],
                                     axis=1)
            zs.append(jnp.dot(k_ref[0, rows_j, hp * LANES:(hp + 1) * LANES], qt, preferred_element_type=F32))
        z = jnp.concatenate(zs, axis=1)
        sp = jnp.maximum(jnp.log(1.0 + jnp.exp2(jnp.minimum(z, SOFTPLUS_CLAMP))) * LOG2E, z)
        neg_log_beta = mask_diag(sp - z, diag)
        sp = mask_diag(sp, diag)
        return sp[0:1, :], jnp.concatenate([sp.astype(BF16), neg_log_beta.astype(BF16)], axis=0)

    def weigh(j, sp_0, terms, q_lo, q_hi, diag):
        nq = q_hi - q_lo
        lanes = [slice(grp * grp_w + q_lo * tb, grp * grp_w + q_hi * tb) for grp in range(n_grp)]
        log_w = jnp.dot(neg_suffix, terms, preferred_element_type=F32)
        log_rem = jnp.concatenate([lr_ref[:, ln] for ln in lanes], axis=1)
        w = mask_diag(jnp.exp2(log_w + log_rem), diag).astype(BF16)
        later_0 = -(log_w[0:1, :] + terms[tb:tb + 1, :].astype(F32))
        new_rem = log_rem - (sp_0 + later_0)
        for grp in range(n_grp):
            hp, h = divmod(grp, 2)
            cols = slice(grp * nq * tb, (grp + 1) * nq * tb)
            lr_ref[:, lanes[grp]] = new_rem[:, cols]
            vt_h = vt_ref[0, j, hp * LANES + h * HEAD_DIM:hp * LANES + (h + 1) * HEAD_DIM, :]
            res = jnp.dot(vt_h, w[:, cols], preferred_element_type=F32)
            acc_ref[h * HEAD_DIM:(h + 1) * HEAD_DIM, hp * grp_w + q_lo * tb:hp * grp_w + q_hi * tb] += res
        return blocks_alive(new_rem, q_lo, nq)

    def step(j, qts, q_lo, q_hi, diag):
        return weigh(j, *score(j, qts, q_lo, q_hi, diag), q_lo, q_hi, diag)

    def q_super_block(sb, carry):
        qts = []
        head_of_row = lax.broadcasted_iota(jnp.int32, (LANES, grp_w), 0) // HEAD_DIM
        for hp in range(n_hp):
            rows = pl.ds(pl.multiple_of(sb * grp_w, grp_w), grp_w)
            qt = q_ref[0, rows, hp * LANES:(hp + 1) * LANES].astype(F32).T
            qts.append(jnp.concatenate([jnp.where(head_of_row == h, qt, 0.0) for h in range(2)],
                                       axis=1).astype(BF16))
        lr_ref[...] = jnp.zeros_like(lr_ref)
        acc_ref[...] = jnp.zeros_like(acc_ref)
        base = sb * g_q

        def band(offsets):
            spans = [(m, max(m, 0), min(m + BAND, g_q), m >= 0) for m in offsets]
            scored = {}
            for i in range(len(spans) + DIAG_LOOKAHEAD):
                if i < len(spans):
                    m, lo, hi, diag = spans[i]
                    scored[m] = score(base + m, qts, lo, hi, diag)
                if i >= DIAG_LOOKAHEAD:
                    m, lo, hi, diag = spans[i - DIAG_LOOKAHEAD]
                    weigh(base + m, *scored.pop(m), lo, hi, diag)

        band(range(g_q - 1, -1, -1))

        @pl.when(sb > 0)
        def _():
            band(range(-1, -BAND, -1))

        def more(state):
            j, alive = state
            return (j >= 0) & (alive > 0)

        alive = [blocks_alive(jnp.concatenate([lr_ref[:, grp * grp_w + g * tb:grp * grp_w + (g + 1) * tb]
                                               for grp in range(n_grp)], axis=1), g, 1) for g in range(g_q)]
        for g in range(g_q):
            def one_block_step(state, g=g):
                j, _ = state
                return j - 1, step(j, qts, g, g + 1, False)

            lax.while_loop(more, one_block_step, (base + g - BAND, alive[g]))

        for g in range(g_q):
            rows = pl.ds(pl.multiple_of((sb * g_q + g) * tb, tb), tb)
            for hp in range(n_hp):
                a = hp * grp_w + g * tb
                gate = sg_ref[0, rows, hp * LANES:(hp + 1) * LANES].astype(F32)
                o_ref[0, rows, hp * LANES:(hp + 1) * LANES] = (acc_ref[:, a:a + tb].T * gate).astype(BF16)
        return carry

    lax.fori_loop(0, nblk // g_q, q_super_block, 0)


def _attn_call(qn, kn, vt, sg, *, g_q=16, n_hp=4):
    bsz, s, d_attn = qn.shape
    nblk = s // ATTN_BLOCK
    assert nblk % g_q == 0
    width = n_hp * LANES
    score_lanes = g_q * n_hp * 2 * ATTN_BLOCK
    seq = pl.BlockSpec((1, s, width), lambda b, h: (b, 0, h))
    return pl.pallas_call(
        functools.partial(_attn_kernel, nblk=nblk, g_q=g_q, n_hp=n_hp),
        grid=(bsz, d_attn // width),
        in_specs=[seq, seq, pl.BlockSpec((1, nblk, width, ATTN_BLOCK), lambda b, h: (b, 0, h, 0)), seq],
        out_specs=seq,
        out_shape=jax.ShapeDtypeStruct((bsz, s, d_attn), BF16),
        scratch_shapes=[pltpu.VMEM((1, score_lanes), F32),
                        pltpu.VMEM((LANES, g_q * n_hp * ATTN_BLOCK), F32)],
        compiler_params=pltpu.CompilerParams(
            dimension_semantics=("arbitrary", "arbitrary"), vmem_limit_bytes=VMEM_LIMIT_BYTES),
        name="stickbreak_attn",
    )(qn, kn, vt, sg)


X_RING = 3


def _out_kernel(a_ref, p_ref, x_hbm, mod_ref, w_ref, o_ref, wbf, xbuf, sem, *, ts, d_model, d_attn):
    n_t = pl.num_programs(1)
    step = pl.program_id(0) * n_t + pl.program_id(1)
    n_steps = pl.num_programs(0) * n_t

    def x_copy(n):
        slot = n % X_RING
        rows = pl.ds(pl.multiple_of((n % n_t) * ts, ts), ts)
        return pltpu.make_async_copy(x_hbm.at[n // n_t, rows, :], xbuf.at[slot], sem.at[slot])

    @pl.when(step == 0)
    def _():
        wbf[...] = w_ref[...].astype(BF16)
        for n in range(X_RING - 1):
            x_copy(n).start()

    @pl.when(step + (X_RING - 1) < n_steps)
    def _():
        x_copy(step + (X_RING - 1)).start()

    x_copy(step).wait()
    y = jnp.dot(a_ref[0], wbf[0:d_attn, :], preferred_element_type=F32)
    y = y + jnp.dot(p_ref[0], wbf[d_attn:, :], preferred_element_type=F32)
    gate = mod_ref[0, :, 2 * d_model:3 * d_model]
    o_ref[0] = xbuf[step % X_RING] + gate * y


def _out_call(attn, pool_out, x, mod3, w_out, *, ts=1024):
    bsz, s, d = x.shape
    d_attn = attn.shape[2]
    d_pool = pool_out.shape[2]
    assert bsz * (s // ts) >= X_RING - 1
    return pl.pallas_call(
        functools.partial(_out_kernel, ts=ts, d_model=d, d_attn=d_attn),
        grid=(bsz, s // ts),
        in_specs=[
            pl.BlockSpec((1, ts, d_attn), lambda b, t: (b, t, 0)),
            pl.BlockSpec((1, ts, d_pool), lambda b, t: (b, t, 0)),
            pl.BlockSpec(memory_space=pl.ANY),
            pl.BlockSpec((1, 1, mod3.shape[2]), lambda b, t: (b, 0, 0)),
            pl.BlockSpec(w_out.shape, lambda b, t: (0, 0), pipeline_mode=pl.Buffered(1)),
        ],
        out_specs=pl.BlockSpec((1, ts, d), lambda b, t: (b, t, 0)),
        out_shape=jax.ShapeDtypeStruct((bsz, s, d), F32),
        scratch_shapes=[pltpu.VMEM(w_out.shape, BF16),
                        pltpu.VMEM((X_RING, ts, d), F32),
                        pltpu.SemaphoreType.DMA((X_RING,))],
        compiler_params=pltpu.CompilerParams(
            dimension_semantics=("arbitrary", "arbitrary"), vmem_limit_bytes=VMEM_LIMIT_BYTES),
        name="outproj_residual",
    )(attn, pool_out, x, mod3, w_out)


def kernel(x, c, w_ada, b_ada, norm_g, w_in, q_norm_g, k_norm_g, w_pool, b_pool, pool_scale, w_out):
    depth = w_ada.shape[0]
    d_attn = w_out.shape[1] // 2
    n_heads = d_attn // HEAD_DIM
    h = x
    for l in range(depth):
        mod = _ada_call(c, w_ada[l], b_ada[l])
        mod3 = mod.reshape(mod.shape[0], 1, mod.shape[1])
        qg = jnp.tile(q_norm_g[l] * (HEAD_DIM ** -0.5 * LOG2E), n_heads).reshape(1, d_attn)
        kg = jnp.tile(k_norm_g[l], n_heads).reshape(1, d_attn)
        qn, kn, vt, sg, pool_out = _proj_call(
            mod3, h, norm_g[l].reshape(1, -1), w_in[l], qg, kg,
            w_pool[l], b_pool[l].reshape(1, -1), pool_scale[l].reshape(1, -1))
        attn = _attn_call(qn, kn, vt, sg)
        h = _out_call(attn, pool_out, h, mod3, w_out[l])
    return h
```

```python
import functools

import jax
import jax.numpy as jnp
from jax import lax
from jax.experimental import pallas as pl
from jax.experimental.pallas import tpu as pltpu

F32 = jnp.float32
BF16 = jnp.bfloat16

HEAD_DIM = 64
POOL_WINDOWS = (2, 4, 8, 16)
EPS = 1e-6
LOG2E = 1.4426950408889634
SOFTPLUS_CLAMP = 64.0
REM_FLOOR_LOG2 = -160.0
DIAG_LOOKAHEAD = 1
BAND = 3
BAND_TAIL_KEYS = 64

LANES = 128
SUBLANES = 8
MXU_WIDTH = 256
ATTN_BLOCK = 128
POOL_HIST = max(POOL_WINDOWS)
POOL_PAD = 2 * POOL_HIST
VMEM_LIMIT_BYTES = 58 * 1024 * 1024


def _silu(v):
    h = 0.5 * v
    return h + h * jnp.tanh(h)


def _first_grid_step():
    return (pl.program_id(0) == 0) & (pl.program_id(1) == 0)


def _ada_kernel(c_ref, w_ref, b_ref, o_ref):
    c = c_ref[...]
    ca = _silu(c).astype(BF16)
    o_ref[...] = jnp.dot(ca, w_ref[...].astype(BF16), preferred_element_type=F32) + b_ref[...]


def _ada_call(c, w_ada, b_ada, *, tn=1024):
    bsz, d = c.shape
    n = w_ada.shape[1]
    return pl.pallas_call(
        _ada_kernel,
        grid=(n // tn,),
        in_specs=[
            pl.BlockSpec((bsz, d), lambda j: (0, 0)),
            pl.BlockSpec((d, tn), lambda j: (0, j)),
            pl.BlockSpec((1, tn), lambda j: (0, j)),
        ],
        out_specs=pl.BlockSpec((bsz, tn), lambda j: (0, j)),
        out_shape=jax.ShapeDtypeStruct((bsz, n), F32),
        compiler_params=pltpu.CompilerParams(
            dimension_semantics=("arbitrary",), vmem_limit_bytes=VMEM_LIMIT_BYTES),
        name="adaln_mod",
    )(c, w_ada, b_ada.reshape(1, n))


def _proj_kernel(mod_ref, x_ref, ng_ref, win_ref, qg_ref, kg_ref, wp_ref, bp_ref, ps_ref,
                 q_ref, k_ref, vt_ref, sg_ref, po_ref, wbf, wpbf, ubuf, sbuf_a, sbuf_b,
                 *, ts, d_model, d_attn, d_pool):
    t = pl.program_id(1)
    hist = slice(POOL_PAD - POOL_HIST, POOL_PAD)
    gdim = d_pool // len(POOL_WINDOWS)
    pool_pack = MXU_WIDTH // gdim

    @pl.when(_first_grid_step())
    def _():
        wbf[...] = win_ref[...].astype(BF16)
        wpbf[...] = jnp.zeros_like(wpbf)
        for g in range(len(POOL_WINDOWS)):
            p, i = divmod(g, pool_pack)
            wpbf[p, i * gdim:(i + 1) * gdim, i * gdim:(i + 1) * gdim] = wp_ref[g].astype(BF16)

    @pl.when(t == 0)
    def _():
        ubuf[0:POOL_PAD, :] = jnp.zeros((POOL_PAD, d_pool), F32)

    @pl.when(t > 0)
    def _():
        ubuf[hist, :] = ubuf[ts + POOL_PAD - POOL_HIST:ts + POOL_PAD, :]

    x = x_ref[0]
    ms = jnp.mean(x * x, axis=-1, keepdims=True)
    shift = mod_ref[0, :, 0:d_model]
    scale = mod_ref[0, :, d_model:2 * d_model]
    a = ng_ref[...] * (1.0 + scale)
    hn = ((x * lax.rsqrt(ms + EPS)) * a + shift).astype(BF16)

    def proj(col, width):
        return jnp.dot(hn, wbf[:, col:col + width], preferred_element_type=F32)

    u = proj(4 * d_attn, d_pool)
    g_pool = proj(4 * d_attn + d_pool, d_pool)
    ubuf[POOL_PAD:POOL_PAD + ts, :] = u
    n_lvl = len(POOL_WINDOWS)
    end = POOL_PAD + ts
    src = ubuf
    wsums = []
    for lvl in range(n_lvl):
        shift_rows = 2 ** lvl
        start = SUBLANES * (lvl + 1)
        lanes = slice(lvl * gdim, d_pool)
        summed = src[start:end, lanes] + src[start - shift_rows:end - shift_rows, lanes]
        wsums.append(summed[POOL_PAD - start:, 0:gdim])
        if lvl + 1 < n_lvl:
            dst = sbuf_a if lvl % 2 == 0 else sbuf_b
            dst[start:end, lanes] = summed
            src = dst

    row = lax.broadcasted_iota(jnp.int32, (POOL_HIST, gdim), 0)
    pooled = []
    for g, win in enumerate(POOL_WINDOWS):
        ug = u[:, g * gdim:(g + 1) * gdim]
        cnt = jnp.minimum(t * ts + row + 1, win).astype(F32)
        head = wsums[g][0:POOL_HIST] / cnt - ug[0:POOL_HIST]
        rest = wsums[g][POOL_HIST:] * (1.0 / win) - ug[POOL_HIST:]
        pooled.append(jnp.concatenate([head, rest], axis=0).astype(BF16))
    for p in range(len(POOL_WINDOWS) // pool_pack):
        lo, hi = p * MXU_WIDTH, (p + 1) * MXU_WIDTH
        mixed = jnp.dot(jnp.concatenate(pooled[p * pool_pack:(p + 1) * pool_pack], axis=1), wpbf[p],
                        preferred_element_type=F32) + bp_ref[:, lo:hi]
        po_ref[0, :, lo:hi] = (mixed * ps_ref[:, lo:hi] * _silu(g_pool[:, lo:hi])).astype(BF16)

    r = lax.broadcasted_iota(jnp.int32, (MXU_WIDTH, MXU_WIDTH), 0) // HEAD_DIM
    c = lax.broadcasted_iota(jnp.int32, (MXU_WIDTH, MXU_WIDTH), 1) // HEAD_DIM
    head_avg = jnp.where(r == c, 1.0 / HEAD_DIM, 0.0).astype(BF16)

    def head_norm(v, g):
        sq = (v * v).astype(BF16)
        msq = jnp.concatenate([jnp.dot(sq[:, lo:lo + MXU_WIDTH], head_avg, preferred_element_type=F32)
                               for lo in range(0, d_attn, MXU_WIDTH)], axis=1)
        return (v * lax.rsqrt(msq + EPS)) * g

    q_ref[0] = head_norm(proj(0, d_attn), qg_ref[...]).astype(BF16)
    k_ref[0] = head_norm(proj(d_attn, d_attn), kg_ref[...]).astype(BF16)

    vt = proj(2 * d_attn, d_attn).T.astype(BF16)
    for jj in range(ts // ATTN_BLOCK):
        vt_ref[0, jj] = vt[:, jj * ATTN_BLOCK:(jj + 1) * ATTN_BLOCK]

    sg_ref[0] = _silu(proj(3 * d_attn, d_attn)).astype(BF16)


def _proj_call(mod3, x, norm_g, w_in, qg, kg, w_pool, b_pool, pool_scale, *, ts=1024):
    bsz, s, d = x.shape
    d_attn = qg.shape[1]
    d_pool = pool_scale.shape[1]
    nt = s // ts
    nb = s // ATTN_BLOCK
    kern = functools.partial(_proj_kernel, ts=ts, d_model=d, d_attn=d_attn, d_pool=d_pool)
    const = lambda *shape: pl.BlockSpec(shape, lambda b, t: (0,) * len(shape))
    once = lambda *shape: pl.BlockSpec(shape, lambda b, t: (0,) * len(shape), pipeline_mode=pl.Buffered(1))
    seq_out = lambda width: pl.BlockSpec((1, ts, width), lambda b, t: (b, t, 0))
    pool_rows = POOL_PAD + ts
    return pl.pallas_call(
        kern,
        grid=(bsz, nt),
        in_specs=[
            pl.BlockSpec((1, 1, mod3.shape[2]), lambda b, t: (b, 0, 0)),
            pl.BlockSpec((1, ts, d), lambda b, t: (b, t, 0)),
            const(1, d),
            once(*w_in.shape),
            const(1, d_attn),
            const(1, d_attn),
            once(*w_pool.shape),
            const(1, d_pool),
            const(1, d_pool),
        ],
        out_specs=[
            seq_out(d_attn),
            seq_out(d_attn),
            pl.BlockSpec((1, ts // ATTN_BLOCK, d_attn, ATTN_BLOCK), lambda b, t: (b, t, 0, 0)),
            seq_out(d_attn),
            seq_out(d_pool),
        ],
        out_shape=[
            jax.ShapeDtypeStruct((bsz, s, d_attn), BF16),
            jax.ShapeDtypeStruct((bsz, s, d_attn), BF16),
            jax.ShapeDtypeStruct((bsz, nb, d_attn, ATTN_BLOCK), BF16),
            jax.ShapeDtypeStruct((bsz, s, d_attn), BF16),
            jax.ShapeDtypeStruct((bsz, s, d_pool), BF16),
        ],
        scratch_shapes=[pltpu.VMEM(w_in.shape, BF16),
                        pltpu.VMEM((d_pool // MXU_WIDTH, MXU_WIDTH, MXU_WIDTH), BF16),
                        pltpu.VMEM((pool_rows, d_pool), F32),
                        pltpu.VMEM((pool_rows, d_pool), F32),
                        pltpu.VMEM((pool_rows, d_pool), F32)],
        compiler_params=pltpu.CompilerParams(
            dimension_semantics=("arbitrary", "arbitrary"), vmem_limit_bytes=VMEM_LIMIT_BYTES),
        name="norm_inproj_pool",
    )(mod3, x, norm_g, w_in, qg, kg, w_pool, b_pool, pool_scale)


def _attn_kernel(q_ref, k_ref, vt_ref, sg_ref, o_ref, lr_ref, acc_ref, *, nblk, g_q, n_hp):
    tb = ATTN_BLOCK
    row = lax.broadcasted_iota(jnp.int32, (tb, tb), 0)
    col = lax.broadcasted_iota(jnp.int32, (tb, tb), 1)
    def neg_suffix(nk):
        return jnp.concatenate([jnp.where(col > row, -1.0, 0.0)[0:nk, 0:nk],
                                jnp.where(col == row, -1.0, 0.0)[0:nk, 0:nk]], axis=1).astype(BF16)
    causal = row < col
    n_grp = 2 * n_hp
    grp_w = g_q * tb

    def mask_diag(v, diag):
        if not diag:
            return v
        nq = v.shape[1] // (n_grp * tb)
        keep = causal[0:v.shape[0]]
        parts = []
        for grp in range(n_grp):
            base = grp * nq * tb
            parts.append(jnp.where(keep, v[:, base:base + tb], 0.0))
            if nq > 1:
                parts.append(v[:, base + tb:base + nq * tb])
        return jnp.concatenate(parts, axis=1)

    def blocks_alive(rem, q_lo, nq):
        n_alive = jnp.int32(0)
        for g in range(nq):
            blk = jnp.concatenate([rem[:, (grp * nq + g) * tb:(grp * nq + g + 1) * tb] for grp in range(n_grp)],
                                  axis=1)
            n_alive = jnp.where(jnp.max(blk) > REM_FLOOR_LOG2, q_lo + g + 1, n_alive)
        return n_alive

    def score(j, keys, qts, q_lo, q_hi, diag):
        nq = q_hi - q_lo
        rows_j = pl.ds(pl.multiple_of(j * tb + keys[0], SUBLANES), keys[1] - keys[0])
        zs = []
        for hp in range(n_hp):
            qt = qts[hp]
            if nq < g_q:
                qt = jnp.concatenate([qt[:, h * grp_w + q_lo * tb:h * grp_w + q_hi * tb] for h in range(2)],
                                     axis=1)
            zs.append(jnp.dot(k_ref[0, rows_j, hp * LANES:(hp + 1) * LANES], qt, preferred_element_type=F32))
        z = jnp.concatenate(zs, axis=1)
        sp = jnp.maximum(jnp.log(1.0 + jnp.exp2(jnp.minimum(z, SOFTPLUS_CLAMP))) * LOG2E, z)
        neg_log_beta = mask_diag(sp - z, diag)
        sp = mask_diag(sp, diag)
        return sp[0:1, :], jnp.concatenate([sp.astype(BF16), neg_log_beta.astype(BF16)], axis=0)

    def weigh(j, keys, sp_0, terms, q_lo, q_hi, diag):
        nq = q_hi - q_lo
        nk = keys[1] - keys[0]
        lanes = [slice(grp * grp_w + q_lo * tb, grp * grp_w + q_hi * tb) for grp in range(n_grp)]
        log_w = jnp.dot(neg_suffix(nk), terms, preferred_element_type=F32)
        log_rem = jnp.concatenate([lr_ref[:, ln] for ln in lanes], axis=1)
        w = mask_diag(jnp.exp2(log_w + log_rem), diag).astype(BF16)
        later_0 = -(log_w[0:1, :] + terms[nk:nk + 1, :].astype(F32))
        new_rem = log_rem - (sp_0 + later_0)
        for grp in range(n_grp):
            hp, h = divmod(grp, 2)
            cols = slice(grp * nq * tb, (grp + 1) * nq * tb)
            lr_ref[:, lanes[grp]] = new_rem[:, cols]
            vt_h = vt_ref[0, j, hp * LANES + h * HEAD_DIM:hp * LANES + (h + 1) * HEAD_DIM, keys[0]:keys[1]]
            res = jnp.dot(vt_h, w[:, cols], preferred_element_type=F32)
            acc_ref[h * HEAD_DIM:(h + 1) * HEAD_DIM, hp * grp_w + q_lo * tb:hp * grp_w + q_hi * tb] += res
        return blocks_alive(new_rem, q_lo, nq)

    def step(j, keys, qts, q_lo, q_hi, diag):
        return weigh(j, keys, *score(j, keys, qts, q_lo, q_hi, diag), q_lo, q_hi, diag)

    def q_super_block(sb, carry):
        qts = []
        head_of_row = lax.broadcasted_iota(jnp.int32, (LANES, grp_w), 0) // HEAD_DIM
        for hp in range(n_hp):
            rows = pl.ds(pl.multiple_of(sb * grp_w, grp_w), grp_w)
            qt = q_ref[0, rows, hp * LANES:(hp + 1) * LANES].astype(F32).T
            qts.append(jnp.concatenate([jnp.where(head_of_row == h, qt, 0.0) for h in range(2)],
                                       axis=1).astype(BF16))
        lr_ref[...] = jnp.zeros_like(lr_ref)
        acc_ref[...] = jnp.zeros_like(acc_ref)
        base = sb * g_q

        all_keys = (0, tb)
        tail_keys = (tb - BAND_TAIL_KEYS, tb)
        rest_keys = (0, tb - BAND_TAIL_KEYS)

        def band(offsets):
            spans = []
            for m in offsets:
                lo, hi = max(m, 0), min(m + BAND - 1, g_q)
                if lo < hi:
                    spans.append((m, all_keys, lo, hi, m >= 0))
                if 0 <= m + BAND - 1 < g_q:
                    spans.append((m, tail_keys, m + BAND - 1, m + BAND, False))
            scored = {}
            for i in range(len(spans) + DIAG_LOOKAHEAD):
                if i < len(spans):
                    m, keys, lo, hi, diag = spans[i]
                    scored[i] = score(base + m, keys, qts, lo, hi, diag)
                if i >= DIAG_LOOKAHEAD:
                    m, keys, lo, hi, diag = spans[i - DIAG_LOOKAHEAD]
                    weigh(base + m, keys, *scored.pop(i - DIAG_LOOKAHEAD), lo, hi, diag)

        band(range(g_q - 1, -1, -1))

        @pl.when(sb > 0)
        def _():
            band(range(-1, -BAND, -1))

        def more(state):
            j, alive = state
            return (j >= 0) & (alive > 0)

        alive = [blocks_alive(jnp.concatenate([lr_ref[:, grp * grp_w + g * tb:grp * grp_w + (g + 1) * tb]
                                               for grp in range(n_grp)], axis=1), g, 1) for g in range(g_q)]
        for g in range(g_q):
            def rest_of_band_block(state, g=g):
                j, _ = state
                return j - 1, step(j, rest_keys, qts, g, g + 1, False)

            def one_block_step(state, g=g):
                j, _ = state
                return j - 1, step(j, all_keys, qts, g, g + 1, False)

            oldest = base + g - (BAND - 1)
            state = lax.cond(more((oldest, alive[g])), rest_of_band_block, lambda st: st, (oldest, alive[g]))
            lax.while_loop(more, one_block_step, state)

        for g in range(g_q):
            rows = pl.ds(pl.multiple_of((sb * g_q + g) * tb, tb), tb)
            for hp in range(n_hp):
                a = hp * grp_w + g * tb
                gate = sg_ref[0, rows, hp * LANES:(hp + 1) * LANES].astype(F32)
                o_ref[0, rows, hp * LANES:(hp + 1) * LANES] = (acc_ref[:, a:a + tb].T * gate).astype(BF16)
        return carry

    lax.fori_loop(0, nblk // g_q, q_super_block, 0)


def _attn_call(qn, kn, vt, sg, *, g_q=16, n_hp=4):
    bsz, s, d_attn = qn.shape
    nblk = s // ATTN_BLOCK
    assert nblk % g_q == 0
    width = n_hp * LANES
    score_lanes = g_q * n_hp * 2 * ATTN_BLOCK
    seq = pl.BlockSpec((1, s, width), lambda b, h: (b, 0, h))
    return pl.pallas_call(
        functools.partial(_attn_kernel, nblk=nblk, g_q=g_q, n_hp=n_hp),
        grid=(bsz, d_attn // width),
        in_specs=[seq, seq, pl.BlockSpec((1, nblk, width, ATTN_BLOCK), lambda b, h: (b, 0, h, 0)), seq],
        out_specs=seq,
        out_shape=jax.ShapeDtypeStruct((bsz, s, d_attn), BF16),
        scratch_shapes=[pltpu.VMEM((1, score_lanes), F32),
                        pltpu.VMEM((LANES, g_q * n_hp * ATTN_BLOCK), F32)],
        compiler_params=pltpu.CompilerParams(
            dimension_semantics=("arbitrary", "arbitrary"), vmem_limit_bytes=VMEM_LIMIT_BYTES),
        name="stickbreak_attn",
    )(qn, kn, vt, sg)


X_RING = 3


def _out_kernel(a_ref, p_ref, x_hbm, mod_ref, w_ref, o_ref, wbf, xbuf, sem, *, ts, d_model, d_attn):
    n_t = pl.num_programs(1)
    step = pl.program_id(0) * n_t + pl.program_id(1)
    n_steps = pl.num_programs(0) * n_t

    def x_copy(n):
        slot = n % X_RING
        rows = pl.ds(pl.multiple_of((n % n_t) * ts, ts), ts)
        return pltpu.make_async_copy(x_hbm.at[n // n_t, rows, :], xbuf.at[slot], sem.at[slot])

    @pl.when(step == 0)
    def _():
        wbf[...] = w_ref[...].astype(BF16)
        for n in range(X_RING - 1):
            x_copy(n).start()

    @pl.when(step + (X_RING - 1) < n_steps)
    def _():
        x_copy(step + (X_RING - 1)).start()

    x_copy(step).wait()
    y = jnp.dot(a_ref[0], wbf[0:d_attn, :], preferred_element_type=F32)
    y = y + jnp.dot(p_ref[0], wbf[d_attn:, :], preferred_element_type=F32)
    gate = mod_ref[0, :, 2 * d_model:3 * d_model]
    o_ref[0] = xbuf[step % X_RING] + gate * y


def _out_call(attn, pool_out, x, mod3, w_out, *, ts=1024):
    bsz, s, d = x.shape
    d_attn = attn.shape[2]
    d_pool = pool_out.shape[2]
    assert bsz * (s // ts) >= X_RING - 1
    return pl.pallas_call(
        functools.partial(_out_kernel, ts=ts, d_model=d, d_attn=d_attn),
        grid=(bsz, s // ts),
        in_specs=[
            pl.BlockSpec((1, ts, d_attn), lambda b, t: (b, t, 0)),
            pl.BlockSpec((1, ts, d_pool), lambda b, t: (b, t, 0)),
            pl.BlockSpec(memory_space=pl.ANY),
            pl.BlockSpec((1, 1, mod3.shape[2]), lambda b, t: (b, 0, 0)),
            pl.BlockSpec(w_out.shape, lambda b, t: (0, 0), pipeline_mode=pl.Buffered(1)),
        ],
        out_specs=pl.BlockSpec((1, ts, d), lambda b, t: (b, t, 0)),
        out_shape=jax.ShapeDtypeStruct((bsz, s, d), F32),
        scratch_shapes=[pltpu.VMEM(w_out.shape, BF16),
                        pltpu.VMEM((X_RING, ts, d), F32),
                        pltpu.SemaphoreType.DMA((X_RING,))],
        compiler_params=pltpu.CompilerParams(
            dimension_semantics=("arbitrary", "arbitrary"), vmem_limit_bytes=VMEM_LIMIT_BYTES),
        name="outproj_residual",
    )(attn, pool_out, x, mod3, w_out)


def kernel(x, c, w_ada, b_ada, norm_g, w_in, q_norm_g, k_norm_g, w_pool, b_pool, pool_scale, w_out):
    depth = w_ada.shape[0]
    d_attn = w_out.shape[1] // 2
    n_heads = d_attn // HEAD_DIM
    h = x
    for l in range(depth):
        mod = _ada_call(c, w_ada[l], b_ada[l])
        mod3 = mod.reshape(mod.shape[0], 1, mod.shape[1])
        qg = jnp.tile(q_norm_g[l] * (HEAD_DIM ** -0.5 * LOG2E), n_heads).reshape(1, d_attn)
        kg = jnp.tile(k_norm_g[l], n_heads).reshape(1, d_attn)
        qn, kn, vt, sg, pool_out = _proj_call(
            mod3, h, norm_g[l].reshape(1, -1), w_in[l], qg, kg,
            w_pool[l], b_pool[l].reshape(1, -1), pool_scale[l].reshape(1, -1))
        attn = _attn_call(qn, kn, vt, sg)
        h = _out_call(attn, pool_out, h, mod3, w_out[l])
    return h
```

```python
import functools

import jax
import jax.numpy as jnp
from jax import lax
from jax.experimental import pallas as pl
from jax.experimental.pallas import tpu as pltpu

F32 = jnp.float32
BF16 = jnp.bfloat16

HEAD_DIM = 64
POOL_WINDOWS = (2, 4, 8, 16)
EPS = 1e-6
LOG2E = 1.4426950408889634
SOFTPLUS_CLAMP = 64.0
REM_FLOOR_LOG2 = -160.0
DIAG_LOOKAHEAD = 1
BAND = 3

LANES = 128
SUBLANES = 8
MXU_WIDTH = 256
ATTN_BLOCK = 128
POOL_HIST = max(POOL_WINDOWS)
POOL_PAD = 2 * POOL_HIST
VMEM_LIMIT_BYTES = 58 * 1024 * 1024


def _silu(v):
    h = 0.5 * v
    return h + h * jnp.tanh(h)


def _first_grid_step():
    return (pl.program_id(0) == 0) & (pl.program_id(1) == 0)


def _ada_kernel(c_ref, w_ref, b_ref, o_ref):
    c = c_ref[...]
    ca = _silu(c).astype(BF16)
    o_ref[...] = jnp.dot(ca, w_ref[...].astype(BF16), preferred_element_type=F32) + b_ref[...]


def _ada_call(c, w_ada, b_ada, *, tn=1024):
    bsz, d = c.shape
    n = w_ada.shape[1]
    return pl.pallas_call(
        _ada_kernel,
        grid=(n // tn,),
        in_specs=[
            pl.BlockSpec((bsz, d), lambda j: (0, 0)),
            pl.BlockSpec((d, tn), lambda j: (0, j)),
            pl.BlockSpec((1, tn), lambda j: (0, j)),
        ],
        out_specs=pl.BlockSpec((bsz, tn), lambda j: (0, j)),
        out_shape=jax.ShapeDtypeStruct((bsz, n), F32),
        compiler_params=pltpu.CompilerParams(
            dimension_semantics=("arbitrary",), vmem_limit_bytes=VMEM_LIMIT_BYTES),
        name="adaln_mod",
    )(c, w_ada, b_ada.reshape(1, n))


def _proj_kernel(mod_ref, x_ref, ng_ref, win_ref, qg_ref, kg_ref, wp_ref, bp_ref, ps_ref,
                 q_ref, k_ref, vt_ref, sg_ref, po_ref, wbf, wpbf, ubuf, sbuf_a, sbuf_b,
                 *, ts, d_model, d_attn, d_pool):
    t = pl.program_id(1)
    hist = slice(POOL_PAD - POOL_HIST, POOL_PAD)
    gdim = d_pool // len(POOL_WINDOWS)
    pool_pack = MXU_WIDTH // gdim

    @pl.when(_first_grid_step())
    def _():
        wbf[...] = win_ref[...].astype(BF16)
        wpbf[...] = jnp.zeros_like(wpbf)
        for g in range(len(POOL_WINDOWS)):
            p, i = divmod(g, pool_pack)
            wpbf[p, i * gdim:(i + 1) * gdim, i * gdim:(i + 1) * gdim] = wp_ref[g].astype(BF16)

    @pl.when(t == 0)
    def _():
        ubuf[0:POOL_PAD, :] = jnp.zeros((POOL_PAD, d_pool), F32)

    @pl.when(t > 0)
    def _():
        ubuf[hist, :] = ubuf[ts + POOL_PAD - POOL_HIST:ts + POOL_PAD, :]

    x = x_ref[0]
    ms = jnp.mean(x * x, axis=-1, keepdims=True)
    shift = mod_ref[0, :, 0:d_model]
    scale = mod_ref[0, :, d_model:2 * d_model]
    a = ng_ref[...] * (1.0 + scale)
    hn = ((x * lax.rsqrt(ms + EPS)) * a + shift).astype(BF16)

    def proj(col, width):
        return jnp.dot(hn, wbf[:, col:col + width], preferred_element_type=F32)

    u = proj(4 * d_attn, d_pool)
    g_pool = proj(4 * d_attn + d_pool, d_pool)
    ubuf[POOL_PAD:POOL_PAD + ts, :] = u
    n_lvl = len(POOL_WINDOWS)
    end = POOL_PAD + ts
    src = ubuf
    wsums = []
    for lvl in range(n_lvl):
        shift_rows = 2 ** lvl
        start = SUBLANES * (lvl + 1)
        lanes = slice(lvl * gdim, d_pool)
        summed = src[start:end, lanes] + src[start - shift_rows:end - shift_rows, lanes]
        wsums.append(summed[POOL_PAD - start:, 0:gdim])
        if lvl + 1 < n_lvl:
            dst = sbuf_a if lvl % 2 == 0 else sbuf_b
            dst[start:end, lanes] = summed
            src = dst

    row = lax.broadcasted_iota(jnp.int32, (POOL_HIST, gdim), 0)
    pooled = []
    for g, win in enumerate(POOL_WINDOWS):
        ug = u[:, g * gdim:(g + 1) * gdim]
        cnt = jnp.minimum(t * ts + row + 1, win).astype(F32)
        head = wsums[g][0:POOL_HIST] / cnt - ug[0:POOL_HIST]
        rest = wsums[g][POOL_HIST:] * (1.0 / win) - ug[POOL_HIST:]
        pooled.append(jnp.concatenate([head, rest], axis=0).astype(BF16))
    for p in range(len(POOL_WINDOWS) // pool_pack):
        lo, hi = p * MXU_WIDTH, (p + 1) * MXU_WIDTH
        mixed = jnp.dot(jnp.concatenate(pooled[p * pool_pack:(p + 1) * pool_pack], axis=1), wpbf[p],
                        preferred_element_type=F32) + bp_ref[:, lo:hi]
        po_ref[0, :, lo:hi] = (mixed * ps_ref[:, lo:hi] * _silu(g_pool[:, lo:hi])).astype(BF16)

    r = lax.broadcasted_iota(jnp.int32, (MXU_WIDTH, MXU_WIDTH), 0) // HEAD_DIM
    c = lax.broadcasted_iota(jnp.int32, (MXU_WIDTH, MXU_WIDTH), 1) // HEAD_DIM
    head_avg = jnp.where(r == c, 1.0 / HEAD_DIM, 0.0).astype(BF16)

    def head_norm(v, g):
        sq = (v * v).astype(BF16)
        msq = jnp.concatenate([jnp.dot(sq[:, lo:lo + MXU_WIDTH], head_avg, preferred_element_type=F32)
                               for lo in range(0, d_attn, MXU_WIDTH)], axis=1)
        return (v * lax.rsqrt(msq + EPS)) * g

    q_ref[0] = head_norm(proj(0, d_attn), qg_ref[...]).astype(BF16)
    k_ref[0] = head_norm(proj(d_attn, d_attn), kg_ref[...]).astype(BF16)

    vt = proj(2 * d_attn, d_attn).T.astype(BF16)
    for jj in range(ts // ATTN_BLOCK):
        vt_ref[0, jj] = vt[:, jj * ATTN_BLOCK:(jj + 1) * ATTN_BLOCK]

    sg_ref[0] = _silu(proj(3 * d_attn, d_attn)).astype(BF16)


def _proj_call(mod3, x, norm_g, w_in, qg, kg, w_pool, b_pool, pool_scale, *, ts=1024):
    bsz, s, d = x.shape
    d_attn = qg.shape[1]
    d_pool = pool_scale.shape[1]
    nt = s // ts
    nb = s // ATTN_BLOCK
    kern = functools.partial(_proj_kernel, ts=ts, d_model=d, d_attn=d_attn, d_pool=d_pool)
    const = lambda *shape: pl.BlockSpec(shape, lambda b, t: (0,) * len(shape))
    once = lambda *shape: pl.BlockSpec(shape, lambda b, t: (0,) * len(shape), pipeline_mode=pl.Buffered(1))
    seq_out = lambda width: pl.BlockSpec((1, ts, width), lambda b, t: (b, t, 0))
    pool_rows = POOL_PAD + ts
    return pl.pallas_call(
        kern,
        grid=(bsz, nt),
        in_specs=[
            pl.BlockSpec((1, 1, mod3.shape[2]), lambda b, t: (b, 0, 0)),
            pl.BlockSpec((1, ts, d), lambda b, t: (b, t, 0)),
            const(1, d),
            once(*w_in.shape),
            const(1, d_attn),
            const(1, d_attn),
            once(*w_pool.shape),
            const(1, d_pool),
            const(1, d_pool),
        ],
        out_specs=[
            seq_out(d_attn),
            seq_out(d_attn),
            pl.BlockSpec((1, ts // ATTN_BLOCK, d_attn, ATTN_BLOCK), lambda b, t: (b, t, 0, 0)),
            seq_out(d_attn),
            seq_out(d_pool),
        ],
        out_shape=[
            jax.ShapeDtypeStruct((bsz, s, d_attn), BF16),
            jax.ShapeDtypeStruct((bsz, s, d_attn), BF16),
            jax.ShapeDtypeStruct((bsz, nb, d_attn, ATTN_BLOCK), BF16),
            jax.ShapeDtypeStruct((bsz, s, d_attn), BF16),
            jax.ShapeDtypeStruct((bsz, s, d_pool), BF16),
        ],
        scratch_shapes=[pltpu.VMEM(w_in.shape, BF16),
                        pltpu.VMEM((d_pool // MXU_WIDTH, MXU_WIDTH, MXU_WIDTH), BF16),
                        pltpu.VMEM((pool_rows, d_pool), F32),
                        pltpu.VMEM((pool_rows, d_pool), F32),
                        pltpu.VMEM((pool_rows, d_pool), F32)],
        compiler_params=pltpu.CompilerParams(
            dimension_semantics=("arbitrary", "arbitrary"), vmem_limit_bytes=VMEM_LIMIT_BYTES),
        name="norm_inproj_pool",
    )(mod3, x, norm_g, w_in, qg, kg, w_pool, b_pool, pool_scale)


def _attn_kernel(q_ref, k_ref, vt_ref, sg_ref, o_ref, lr_ref, acc_ref, *, nblk, g_q, n_hp):
    tb = ATTN_BLOCK
    row = lax.broadcasted_iota(jnp.int32, (tb, tb), 0)
    col = lax.broadcasted_iota(jnp.int32, (tb, tb), 1)
    neg_suffix = jnp.concatenate([jnp.where(col > row, -1.0, 0.0), jnp.where(col == row, -1.0, 0.0)],
                                 axis=1).astype(BF16)
    causal = row < col
    n_grp = 2 * n_hp
    grp_w = g_q * tb

    def mask_diag(v, diag):
        if not diag:
            return v
        nq = v.shape[1] // (n_grp * tb)
        keep = causal[0:v.shape[0]]
        parts = []
        for grp in range(n_grp):
            base = grp * nq * tb
            parts.append(jnp.where(keep, v[:, base:base + tb], 0.0))
            if nq > 1:
                parts.append(v[:, base + tb:base + nq * tb])
        return jnp.concatenate(parts, axis=1)

    def blocks_alive(rem, q_lo, nq):
        n_alive = jnp.int32(0)
        for g in range(nq):
            blk = jnp.concatenate([rem[:, (grp * nq + g) * tb:(grp * nq + g + 1) * tb] for grp in range(n_grp)],
                                  axis=1)
            n_alive = jnp.where(jnp.max(blk) > REM_FLOOR_LOG2, q_lo + g + 1, n_alive)
        return n_alive

    def score(j, qts, q_lo, q_hi, diag):
        nq = q_hi - q_lo
        rows_j = pl.ds(pl.multiple_of(j * tb, tb), tb)
        zs = []
        for hp in range(n_hp):
            qt = qts[hp]
            if nq < g_q:
                qt = jnp.concatenate([qt[:, h * grp_w + q_lo * tb:h * grp_w + q_hi * tb] for h in range(2)],
                                     axis=1)
            zs.append(jnp.dot(k_ref[0, rows_j, hp * LANES:(hp + 1) * LANES], qt, preferred_element_type=F32))
        z = jnp.concatenate(zs, axis=1)
        sp = jnp.maximum(jnp.log(1.0 + jnp.exp2(jnp.minimum(z, SOFTPLUS_CLAMP))) * LOG2E, z)
        neg_log_beta = mask_diag(sp - z, diag)
        sp = mask_diag(sp, diag)
        return sp[0:1, :], jnp.concatenate([sp.astype(BF16), neg_log_beta.astype(BF16)], axis=0)

    def weigh(j, sp_0, terms, q_lo, q_hi, diag):
        nq = q_hi - q_lo
        lanes = [slice(grp * grp_w + q_lo * tb, grp * grp_w + q_hi * tb) for grp in range(n_grp)]
        log_w = jnp.dot(neg_suffix, terms, preferred_element_type=F32)
        log_rem = jnp.concatenate([lr_ref[:, ln] for ln in lanes], axis=1)
        w = mask_diag(jnp.exp2(log_w + log_rem), diag).astype(BF16)
        later_0 = -(log_w[0:1, :] + terms[tb:tb + 1, :].astype(F32))
        new_rem = log_rem - (sp_0 + later_0)
        for grp in range(n_grp):
            hp, h = divmod(grp, 2)
            cols = slice(grp * nq * tb, (grp + 1) * nq * tb)
            lr_ref[:, lanes[grp]] = new_rem[:, cols]
            vt_h = vt_ref[0, j, hp * LANES + h * HEAD_DIM:hp * LANES + (h + 1) * HEAD_DIM, :]
            res = jnp.dot(vt_h, w[:, cols], preferred_element_type=F32)
            acc_ref[h * HEAD_DIM:(h + 1) * HEAD_DIM, hp * grp_w + q_lo * tb:hp * grp_w + q_hi * tb] += res
        return blocks_alive(new_rem, q_lo, nq)

    def step(j, qts, q_lo, q_hi, diag):
        return weigh(j, *score(j, qts, q_lo, q_hi, diag), q_lo, q_hi, diag)

    def emit(sb, g):
        rows = pl.ds(pl.multiple_of((sb * g_q + g) * tb, tb), tb)
        for hp in range(n_hp):
            a = hp * grp_w + g * tb
            gate = sg_ref[0, rows, hp * LANES:(hp + 1) * LANES].astype(F32)
            o_ref[0, rows, hp * LANES:(hp + 1) * LANES] = (acc_ref[:, a:a + tb].T * gate).astype(BF16)

    def q_super_block(sb, carry):
        qts = []
        head_of_row = lax.broadcasted_iota(jnp.int32, (LANES, grp_w), 0) // HEAD_DIM
        for hp in range(n_hp):
            rows = pl.ds(pl.multiple_of(sb * grp_w, grp_w), grp_w)
            qt = q_ref[0, rows, hp * LANES:(hp + 1) * LANES].astype(F32).T
            qts.append(jnp.concatenate([jnp.where(head_of_row == h, qt, 0.0) for h in range(2)],
                                       axis=1).astype(BF16))
        lr_ref[...] = jnp.zeros_like(lr_ref)
        acc_ref[...] = jnp.zeros_like(acc_ref)
        base = sb * g_q

        def band(offsets):
            spans = [(m, max(m, 0), min(m + BAND, g_q), m >= 0) for m in offsets]
            scored = {}
            for i in range(len(spans) + DIAG_LOOKAHEAD):
                if i < len(spans):
                    m, lo, hi, diag = spans[i]
                    scored[m] = score(base + m, qts, lo, hi, diag)
                if i >= DIAG_LOOKAHEAD:
                    m, lo, hi, diag = spans[i - DIAG_LOOKAHEAD]
                    weigh(base + m, *scored.pop(m), lo, hi, diag)
                    if 0 <= m + BAND - 1 < g_q:
                        emit(sb, m + BAND - 1)

        first_blocks = range(min(BAND - 1, g_q))
        band(range(g_q - 1, -1, -1))
        for g in first_blocks:
            emit(sb, g)

        @pl.when(sb > 0)
        def _():
            band(range(-1, -BAND, -1))
            for g in first_blocks:
                emit(sb, g)

        def more(state):
            j, alive = state
            return (j >= 0) & (alive > 0)

        alive = [blocks_alive(jnp.concatenate([lr_ref[:, grp * grp_w + g * tb:grp * grp_w + (g + 1) * tb]
                                               for grp in range(n_grp)], axis=1), g, 1) for g in range(g_q)]
        for g in range(g_q):
            def one_block_step(state, g=g):
                j, _ = state
                return j - 1, step(j, qts, g, g + 1, False)

            start = (base + g - BAND, alive[g])

            @pl.when(more(start))
            def _(g=g, start=start, one_block_step=one_block_step):
                lax.while_loop(more, one_block_step, start)
                emit(sb, g)
        return carry

    lax.fori_loop(0, nblk // g_q, q_super_block, 0)


def _attn_call(qn, kn, vt, sg, *, g_q=16, n_hp=4):
    bsz, s, d_attn = qn.shape
    nblk = s // ATTN_BLOCK
    assert nblk % g_q == 0
    width = n_hp * LANES
    score_lanes = g_q * n_hp * 2 * ATTN_BLOCK
    seq = pl.BlockSpec((1, s, width), lambda b, h: (b, 0, h))
    return pl.pallas_call(
        functools.partial(_attn_kernel, nblk=nblk, g_q=g_q, n_hp=n_hp),
        grid=(bsz, d_attn // width),
        in_specs=[seq, seq, pl.BlockSpec((1, nblk, width, ATTN_BLOCK), lambda b, h: (b, 0, h, 0)), seq],
        out_specs=seq,
        out_shape=jax.ShapeDtypeStruct((bsz, s, d_attn), BF16),
        scratch_shapes=[pltpu.VMEM((1, score_lanes), F32),
                        pltpu.VMEM((LANES, g_q * n_hp * ATTN_BLOCK), F32)],
        compiler_params=pltpu.CompilerParams(
            dimension_semantics=("arbitrary", "arbitrary"), vmem_limit_bytes=VMEM_LIMIT_BYTES),
        name="stickbreak_attn",
    )(qn, kn, vt, sg)


X_RING = 3


def _out_kernel(a_ref, p_ref, x_hbm, mod_ref, w_ref, o_ref, wbf, xbuf, sem, *, ts, d_model, d_attn):
    n_t = pl.num_programs(1)
    step = pl.program_id(0) * n_t + pl.program_id(1)
    n_steps = pl.num_programs(0) * n_t

    def x_copy(n):
        slot = n % X_RING
        rows = pl.ds(pl.multiple_of((n % n_t) * ts, ts), ts)
        return pltpu.make_async_copy(x_hbm.at[n // n_t, rows, :], xbuf.at[slot], sem.at[slot])

    @pl.when(step == 0)
    def _():
        wbf[...] = w_ref[...].astype(BF16)
        for n in range(X_RING - 1):
            x_copy(n).start()

    @pl.when(step + (X_RING - 1) < n_steps)
    def _():
        x_copy(step + (X_RING - 1)).start()

    x_copy(step).wait()
    y = jnp.dot(a_ref[0], wbf[0:d_attn, :], preferred_element_type=F32)
    y = y + jnp.dot(p_ref[0], wbf[d_attn:, :], preferred_element_type=F32)
    gate = mod_ref[0, :, 2 * d_model:3 * d_model]
    o_ref[0] = xbuf[step % X_RING] + gate * y


def _out_call(attn, pool_out, x, mod3, w_out, *, ts=1024):
    bsz, s, d = x.shape
    d_attn = attn.shape[2]
    d_pool = pool_out.shape[2]
    assert bsz * (s // ts) >= X_RING - 1
    return pl.pallas_call(
        functools.partial(_out_kernel, ts=ts, d_model=d, d_attn=d_attn),
        grid=(bsz, s // ts),
        in_specs=[
            pl.BlockSpec((1, ts, d_attn), lambda b, t: (b, t, 0)),
            pl.BlockSpec((1, ts, d_pool), lambda b, t: (b, t, 0)),
            pl.BlockSpec(memory_space=pl.ANY),
            pl.BlockSpec((1, 1, mod3.shape[2]), lambda b, t: (b, 0, 0)),
            pl.BlockSpec(w_out.shape, lambda b, t: (0, 0), pipeline_mode=pl.Buffered(1)),
        ],
        out_specs=pl.BlockSpec((1, ts, d), lambda b, t: (b, t, 0)),
        out_shape=jax.ShapeDtypeStruct((bsz, s, d), F32),
        scratch_shapes=[pltpu.VMEM(w_out.shape, BF16),
                        pltpu.VMEM((X_RING, ts, d), F32),
                        pltpu.SemaphoreType.DMA((X_RING,))],
        compiler_params=pltpu.CompilerParams(
            dimension_semantics=("arbitrary", "arbitrary"), vmem_limit_bytes=VMEM_LIMIT_BYTES),
        name="outproj_residual",
    )(attn, pool_out, x, mod3, w_out)


def kernel(x, c, w_ada, b_ada, norm_g, w_in, q_norm_g, k_norm_g, w_pool, b_pool, pool_scale, w_out):
    depth = w_ada.shape[0]
    d_attn = w_out.shape[1] // 2
    n_heads = d_attn // HEAD_DIM
    h = x
    for l in range(depth):
        mod = _ada_call(c, w_ada[l], b_ada[l])
        mod3 = mod.reshape(mod.shape[0], 1, mod.shape[1])
        qg = jnp.tile(q_norm_g[l] * (HEAD_DIM ** -0.5 * LOG2E), n_heads).reshape(1, d_attn)
        kg = jnp.tile(k_norm_g[l], n_heads).reshape(1, d_attn)
        qn, kn, vt, sg, pool_out = _proj_call(
            mod3, h, norm_g[l].reshape(1, -1), w_in[l], qg, kg,
            w_pool[l], b_pool[l].reshape(1, -1), pool_scale[l].reshape(1, -1))
        attn = _attn_call(qn, kn, vt, sg)
        h = _out_call(attn, pool_out, h, mod3, w_out[l])
    return h
```

```python
import functools

import jax
import jax.numpy as jnp
from jax import lax
from jax.experimental import pallas as pl
from jax.experimental.pallas import tpu as pltpu

F32 = jnp.float32
BF16 = jnp.bfloat16

HEAD_DIM = 64
POOL_WINDOWS = (2, 4, 8, 16)
EPS = 1e-6
LOG2E = 1.4426950408889634
SOFTPLUS_CLAMP = 64.0
REM_FLOOR_LOG2 = -160.0
DIAG_LOOKAHEAD = 1
BAND = 3

LANES = 128
SUBLANES = 8
MXU_WIDTH = 256
ATTN_BLOCK = 128
POOL_HIST = max(POOL_WINDOWS)
POOL_PAD = 2 * POOL_HIST
VMEM_LIMIT_BYTES = 58 * 1024 * 1024


def _silu(v):
    h = 0.5 * v
    return h + h * jnp.tanh(h)


def _first_grid_step():
    return (pl.program_id(0) == 0) & (pl.program_id(1) == 0)


def _ada_kernel(c_ref, w_ref, b_ref, o_ref):
    c = c_ref[...]
    ca = _silu(c).astype(BF16)
    o_ref[...] = jnp.dot(ca, w_ref[...].astype(BF16), preferred_element_type=F32) + b_ref[...]


def _ada_call(c, w_ada, b_ada, *, tn=1024):
    bsz, d = c.shape
    n = w_ada.shape[1]
    return pl.pallas_call(
        _ada_kernel,
        grid=(n // tn,),
        in_specs=[
            pl.BlockSpec((bsz, d), lambda j: (0, 0)),
            pl.BlockSpec((d, tn), lambda j: (0, j)),
            pl.BlockSpec((1, tn), lambda j: (0, j)),
        ],
        out_specs=pl.BlockSpec((bsz, tn), lambda j: (0, j)),
        out_shape=jax.ShapeDtypeStruct((bsz, n), F32),
        compiler_params=pltpu.CompilerParams(
            dimension_semantics=("arbitrary",), vmem_limit_bytes=VMEM_LIMIT_BYTES),
        name="adaln_mod",
    )(c, w_ada, b_ada.reshape(1, n))


def _proj_kernel(mod_ref, x_ref, ng_ref, win_ref, qg_ref, kg_ref, wp_ref, bp_ref, ps_ref,
                 q_ref, k_ref, vt_ref, sg_ref, po_ref, wbf, wpbf, ubuf, sbuf_a, sbuf_b,
                 *, ts, d_model, d_attn, d_pool):
    t = pl.program_id(1)
    hist = slice(POOL_PAD - POOL_HIST, POOL_PAD)
    gdim = d_pool // len(POOL_WINDOWS)
    pool_pack = MXU_WIDTH // gdim

    @pl.when(_first_grid_step())
    def _():
        wbf[...] = win_ref[...].astype(BF16)
        wpbf[...] = jnp.zeros_like(wpbf)
        for g in range(len(POOL_WINDOWS)):
            p, i = divmod(g, pool_pack)
            wpbf[p, i * gdim:(i + 1) * gdim, i * gdim:(i + 1) * gdim] = wp_ref[g].astype(BF16)

    @pl.when(t == 0)
    def _():
        ubuf[0:POOL_PAD, :] = jnp.zeros((POOL_PAD, d_pool), F32)

    @pl.when(t > 0)
    def _():
        ubuf[hist, :] = ubuf[ts + POOL_PAD - POOL_HIST:ts + POOL_PAD, :]

    x = x_ref[0]
    ms = jnp.mean(x * x, axis=-1, keepdims=True)
    shift = mod_ref[0, :, 0:d_model]
    scale = mod_ref[0, :, d_model:2 * d_model]
    a = ng_ref[...] * (1.0 + scale)
    hn = ((x * lax.rsqrt(ms + EPS)) * a + shift).astype(BF16)

    def proj(col, width):
        return jnp.dot(hn, wbf[:, col:col + width], preferred_element_type=F32)

    u = proj(4 * d_attn, d_pool)
    g_pool = proj(4 * d_attn + d_pool, d_pool)
    ubuf[POOL_PAD:POOL_PAD + ts, :] = u
    n_lvl = len(POOL_WINDOWS)
    end = POOL_PAD + ts
    src = ubuf
    wsums = []
    for lvl in range(n_lvl):
        shift_rows = 2 ** lvl
        start = SUBLANES * (lvl + 1)
        lanes = slice(lvl * gdim, d_pool)
        summed = src[start:end, lanes] + src[start - shift_rows:end - shift_rows, lanes]
        wsums.append(summed[POOL_PAD - start:, 0:gdim])
        if lvl + 1 < n_lvl:
            dst = sbuf_a if lvl % 2 == 0 else sbuf_b
            dst[start:end, lanes] = summed
            src = dst

    row = lax.broadcasted_iota(jnp.int32, (POOL_HIST, gdim), 0)
    pooled = []
    for g, win in enumerate(POOL_WINDOWS):
        ug = u[:, g * gdim:(g + 1) * gdim]
        cnt = jnp.minimum(t * ts + row + 1, win).astype(F32)
        head = wsums[g][0:POOL_HIST] / cnt - ug[0:POOL_HIST]
        rest = wsums[g][POOL_HIST:] * (1.0 / win) - ug[POOL_HIST:]
        pooled.append(jnp.concatenate([head, rest], axis=0).astype(BF16))
    for p in range(len(POOL_WINDOWS) // pool_pack):
        lo, hi = p * MXU_WIDTH, (p + 1) * MXU_WIDTH
        mixed = jnp.dot(jnp.concatenate(pooled[p * pool_pack:(p + 1) * pool_pack], axis=1), wpbf[p],
                        preferred_element_type=F32) + bp_ref[:, lo:hi]
        po_ref[0, :, lo:hi] = (mixed * ps_ref[:, lo:hi] * _silu(g_pool[:, lo:hi])).astype(BF16)

    r = lax.broadcasted_iota(jnp.int32, (MXU_WIDTH, MXU_WIDTH), 0) // HEAD_DIM
    c = lax.broadcasted_iota(jnp.int32, (MXU_WIDTH, MXU_WIDTH), 1) // HEAD_DIM
    head_avg = jnp.where(r == c, 1.0 / HEAD_DIM, 0.0).astype(BF16)

    def head_norm(v, g):
        sq = (v * v).astype(BF16)
        msq = jnp.concatenate([jnp.dot(sq[:, lo:lo + MXU_WIDTH], head_avg, preferred_element_type=F32)
                               for lo in range(0, d_attn, MXU_WIDTH)], axis=1)
        return (v * lax.rsqrt(msq + EPS)) * g

    q_ref[0] = head_norm(proj(0, d_attn), qg_ref[...]).astype(BF16)
    k_ref[0] = head_norm(proj(d_attn, d_attn), kg_ref[...]).astype(BF16)

    vt = proj(2 * d_attn, d_attn).T.astype(BF16)
    for jj in range(ts // ATTN_BLOCK):
        vt_ref[0, jj] = vt[:, jj * ATTN_BLOCK:(jj + 1) * ATTN_BLOCK]

    sg_ref[0] = _silu(proj(3 * d_attn, d_attn)).astype(BF16)


def _proj_call(mod3, x, norm_g, w_in, qg, kg, w_pool, b_pool, pool_scale, *, ts=1024):
    bsz, s, d = x.shape
    d_attn = qg.shape[1]
    d_pool = pool_scale.shape[1]
    nt = s // ts
    nb = s // ATTN_BLOCK
    kern = functools.partial(_proj_kernel, ts=ts, d_model=d, d_attn=d_attn, d_pool=d_pool)
    const = lambda *shape: pl.BlockSpec(shape, lambda b, t: (0,) * len(shape))
    once = lambda *shape: pl.BlockSpec(shape, lambda b, t: (0,) * len(shape), pipeline_mode=pl.Buffered(1))
    seq_out = lambda width: pl.BlockSpec((1, ts, width), lambda b, t: (b, t, 0))
    pool_rows = POOL_PAD + ts
    return pl.pallas_call(
        kern,
        grid=(bsz, nt),
        in_specs=[
            pl.BlockSpec((1, 1, mod3.shape[2]), lambda b, t: (b, 0, 0)),
            pl.BlockSpec((1, ts, d), lambda b, t: (b, t, 0)),
            const(1, d),
            once(*w_in.shape),
            const(1, d_attn),
            const(1, d_attn),
            once(*w_pool.shape),
            const(1, d_pool),
            const(1, d_pool),
        ],
        out_specs=[
            seq_out(d_attn),
            seq_out(d_attn),
            pl.BlockSpec((1, ts // ATTN_BLOCK, d_attn, ATTN_BLOCK), lambda b, t: (b, t, 0, 0)),
            seq_out(d_attn),
            seq_out(d_pool),
        ],
        out_shape=[
            jax.ShapeDtypeStruct((bsz, s, d_attn), BF16),
            jax.ShapeDtypeStruct((bsz, s, d_attn), BF16),
            jax.ShapeDtypeStruct((bsz, nb, d_attn, ATTN_BLOCK), BF16),
            jax.ShapeDtypeStruct((bsz, s, d_attn), BF16),
            jax.ShapeDtypeStruct((bsz, s, d_pool), BF16),
        ],
        scratch_shapes=[pltpu.VMEM(w_in.shape, BF16),
                        pltpu.VMEM((d_pool // MXU_WIDTH, MXU_WIDTH, MXU_WIDTH), BF16),
                        pltpu.VMEM((pool_rows, d_pool), F32),
                        pltpu.VMEM((pool_rows, d_pool), F32),
                        pltpu.VMEM((pool_rows, d_pool), F32)],
        compiler_params=pltpu.CompilerParams(
            dimension_semantics=("arbitrary", "arbitrary"), vmem_limit_bytes=VMEM_LIMIT_BYTES),
        name="norm_inproj_pool",
    )(mod3, x, norm_g, w_in, qg, kg, w_pool, b_pool, pool_scale)


def _attn_out_kernel(q_ref, k_ref, vt_ref, sg_ref, p_ref, x_hbm, mod_ref, w_ref, out_hbm,
                     lr_ref, acc_ref, o_buf, wbf, xbuf, ybuf, x_sem, y_sem, *, nblk, g_q, n_hp, out_rows):
    tb = ATTN_BLOCK
    b = pl.program_id(0)
    d_attn = n_hp * LANES
    d_model = xbuf.shape[1]
    n_out = nblk * tb // out_rows

    def x_copy(c):
        rows = pl.ds(c * out_rows, out_rows)
        return pltpu.make_async_copy(x_hbm.at[b, rows, :], xbuf.at[rows, :], x_sem.at[c])

    def out_copy(bb, c):
        rows = pl.ds(c * out_rows, out_rows)
        return pltpu.make_async_copy(ybuf.at[rows, :], out_hbm.at[bb, rows, :], y_sem.at[c])

    @pl.when(b == 0)
    def _():
        wbf[...] = w_ref[...].astype(BF16)

    for c in range(n_out):
        x_copy(c).start()

    row = lax.broadcasted_iota(jnp.int32, (tb, tb), 0)
    col = lax.broadcasted_iota(jnp.int32, (tb, tb), 1)
    neg_suffix = jnp.concatenate([jnp.where(col > row, -1.0, 0.0), jnp.where(col == row, -1.0, 0.0)],
                                 axis=1).astype(BF16)
    causal = row < col
    n_grp = 2 * n_hp
    grp_w = g_q * tb

    def mask_diag(v, diag):
        if not diag:
            return v
        nq = v.shape[1] // (n_grp * tb)
        keep = causal[0:v.shape[0]]
        parts = []
        for grp in range(n_grp):
            base = grp * nq * tb
            parts.append(jnp.where(keep, v[:, base:base + tb], 0.0))
            if nq > 1:
                parts.append(v[:, base + tb:base + nq * tb])
        return jnp.concatenate(parts, axis=1)

    def blocks_alive(rem, q_lo, nq):
        n_alive = jnp.int32(0)
        for g in range(nq):
            blk = jnp.concatenate([rem[:, (grp * nq + g) * tb:(grp * nq + g + 1) * tb] for grp in range(n_grp)],
                                  axis=1)
            n_alive = jnp.where(jnp.max(blk) > REM_FLOOR_LOG2, q_lo + g + 1, n_alive)
        return n_alive

    def score(j, qts, q_lo, q_hi, diag):
        nq = q_hi - q_lo
        rows_j = pl.ds(pl.multiple_of(j * tb, tb), tb)
        zs = []
        for hp in range(n_hp):
            qt = qts[hp]
            if nq < g_q:
                qt = jnp.concatenate([qt[:, h * grp_w + q_lo * tb:h * grp_w + q_hi * tb] for h in range(2)],
                                     axis=1)
            zs.append(jnp.dot(k_ref[0, rows_j, hp * LANES:(hp + 1) * LANES], qt, preferred_element_type=F32))
        z = jnp.concatenate(zs, axis=1)
        sp = jnp.maximum(jnp.log(1.0 + jnp.exp2(jnp.minimum(z, SOFTPLUS_CLAMP))) * LOG2E, z)
        neg_log_beta = mask_diag(sp - z, diag)
        sp = mask_diag(sp, diag)
        return sp[0:1, :], jnp.concatenate([sp.astype(BF16), neg_log_beta.astype(BF16)], axis=0)

    def weigh(j, sp_0, terms, q_lo, q_hi, diag):
        nq = q_hi - q_lo
        lanes = [slice(grp * grp_w + q_lo * tb, grp * grp_w + q_hi * tb) for grp in range(n_grp)]
        log_w = jnp.dot(neg_suffix, terms, preferred_element_type=F32)
        log_rem = jnp.concatenate([lr_ref[:, ln] for ln in lanes], axis=1)
        w = mask_diag(jnp.exp2(log_w + log_rem), diag).astype(BF16)
        later_0 = -(log_w[0:1, :] + terms[tb:tb + 1, :].astype(F32))
        new_rem = log_rem - (sp_0 + later_0)
        for grp in range(n_grp):
            hp, h = divmod(grp, 2)
            cols = slice(grp * nq * tb, (grp + 1) * nq * tb)
            lr_ref[:, lanes[grp]] = new_rem[:, cols]
            vt_h = vt_ref[0, j, hp * LANES + h * HEAD_DIM:hp * LANES + (h + 1) * HEAD_DIM, :]
            res = jnp.dot(vt_h, w[:, cols], preferred_element_type=F32)
            acc_ref[h * HEAD_DIM:(h + 1) * HEAD_DIM, hp * grp_w + q_lo * tb:hp * grp_w + q_hi * tb] += res
        return blocks_alive(new_rem, q_lo, nq)

    def step(j, qts, q_lo, q_hi, diag):
        return weigh(j, *score(j, qts, q_lo, q_hi, diag), q_lo, q_hi, diag)

    def emit(sb, g):
        rows = pl.ds(pl.multiple_of((sb * g_q + g) * tb, tb), tb)
        for hp in range(n_hp):
            a = hp * grp_w + g * tb
            gate = sg_ref[0, rows, hp * LANES:(hp + 1) * LANES].astype(F32)
            o_buf[rows, hp * LANES:(hp + 1) * LANES] = (acc_ref[:, a:a + tb].T * gate).astype(BF16)

    def q_super_block(sb, carry):
        qts = []
        head_of_row = lax.broadcasted_iota(jnp.int32, (LANES, grp_w), 0) // HEAD_DIM
        for hp in range(n_hp):
            rows = pl.ds(pl.multiple_of(sb * grp_w, grp_w), grp_w)
            qt = q_ref[0, rows, hp * LANES:(hp + 1) * LANES].astype(F32).T
            qts.append(jnp.concatenate([jnp.where(head_of_row == h, qt, 0.0) for h in range(2)],
                                       axis=1).astype(BF16))
        lr_ref[...] = jnp.zeros_like(lr_ref)
        acc_ref[...] = jnp.zeros_like(acc_ref)
        base = sb * g_q

        def band(offsets):
            spans = [(m, max(m, 0), min(m + BAND, g_q), m >= 0) for m in offsets]
            scored = {}
            for i in range(len(spans) + DIAG_LOOKAHEAD):
                if i < len(spans):
                    m, lo, hi, diag = spans[i]
                    scored[m] = score(base + m, qts, lo, hi, diag)
                if i >= DIAG_LOOKAHEAD:
                    m, lo, hi, diag = spans[i - DIAG_LOOKAHEAD]
                    weigh(base + m, *scored.pop(m), lo, hi, diag)
                    if 0 <= m + BAND - 1 < g_q:
                        emit(sb, m + BAND - 1)

        first_blocks = range(min(BAND - 1, g_q))
        band(range(g_q - 1, -1, -1))
        for g in first_blocks:
            emit(sb, g)

        @pl.when(sb > 0)
        def _():
            band(range(-1, -BAND, -1))
            for g in first_blocks:
                emit(sb, g)

        def more(state):
            j, alive = state
            return (j >= 0) & (alive > 0)

        alive = [blocks_alive(jnp.concatenate([lr_ref[:, grp * grp_w + g * tb:grp * grp_w + (g + 1) * tb]
                                               for grp in range(n_grp)], axis=1), g, 1) for g in range(g_q)]
        for g in range(g_q):
            def one_block_step(state, g=g):
                j, _ = state
                return j - 1, step(j, qts, g, g + 1, False)

            start = (base + g - BAND, alive[g])

            @pl.when(more(start))
            def _(g=g, start=start, one_block_step=one_block_step):
                lax.while_loop(more, one_block_step, start)
                emit(sb, g)
        return carry

    lax.fori_loop(0, nblk // g_q, q_super_block, 0)

    @pl.when(b > 0)
    def _():
        for c in range(n_out):
            out_copy(b - 1, c).wait()

    gate = mod_ref[0, :, 2 * d_model:3 * d_model]
    for c in range(n_out):
        rows = slice(c * out_rows, (c + 1) * out_rows)
        x_copy(c).wait()
        y = jnp.dot(o_buf[rows, :], wbf[0:d_attn, :], preferred_element_type=F32)
        y = y + jnp.dot(p_ref[0, rows, :], wbf[d_attn:, :], preferred_element_type=F32)
        ybuf[rows, :] = xbuf[rows, :] + gate * y
        out_copy(b, c).start()

    @pl.when(b == pl.num_programs(0) - 1)
    def _():
        for c in range(n_out):
            out_copy(b, c).wait()


def _attn_out_call(qn, kn, vt, sg, pool_out, x, mod3, w_out, *, g_q=16, out_rows=1024):
    bsz, s, d = x.shape
    d_attn = qn.shape[2]
    nblk = s // ATTN_BLOCK
    assert nblk % g_q == 0 and s % out_rows == 0 and d_attn % LANES == 0
    assert w_out.shape == (d_attn + pool_out.shape[2], d)
    n_hp = d_attn // LANES
    score_lanes = g_q * n_hp * 2 * ATTN_BLOCK
    seq = pl.BlockSpec((1, s, d_attn), lambda b: (b, 0, 0))
    return pl.pallas_call(
        functools.partial(_attn_out_kernel, nblk=nblk, g_q=g_q, n_hp=n_hp, out_rows=out_rows),
        grid=(bsz,),
        in_specs=[seq, seq, pl.BlockSpec((1, nblk, d_attn, ATTN_BLOCK), lambda b: (b, 0, 0, 0)), seq,
                  pl.BlockSpec((1, s, pool_out.shape[2]), lambda b: (b, 0, 0)),
                  pl.BlockSpec(memory_space=pl.ANY),
                  pl.BlockSpec((1, 1, mod3.shape[2]), lambda b: (b, 0, 0)),
                  pl.BlockSpec(w_out.shape, lambda b: (0, 0), pipeline_mode=pl.Buffered(1))],
        out_specs=pl.BlockSpec(memory_space=pl.ANY),
        out_shape=jax.ShapeDtypeStruct((bsz, s, d), F32),
        scratch_shapes=[pltpu.VMEM((1, score_lanes), F32),
                        pltpu.VMEM((LANES, g_q * n_hp * ATTN_BLOCK), F32),
                        pltpu.VMEM((s, d_attn), BF16),
                        pltpu.VMEM(w_out.shape, BF16),
                        pltpu.VMEM((s, d), F32),
                        pltpu.VMEM((s, d), F32),
                        pltpu.SemaphoreType.DMA((s // out_rows,)),
                        pltpu.SemaphoreType.DMA((s // out_rows,))],
        compiler_params=pltpu.CompilerParams(
            dimension_semantics=("arbitrary",), vmem_limit_bytes=VMEM_LIMIT_BYTES),
        name="stickbreak_attn_outproj",
    )(qn, kn, vt, sg, pool_out, x, mod3, w_out)


def kernel(x, c, w_ada, b_ada, norm_g, w_in, q_norm_g, k_norm_g, w_pool, b_pool, pool_scale, w_out):
    depth = w_ada.shape[0]
    d_attn = w_out.shape[1] // 2
    n_heads = d_attn // HEAD_DIM
    h = x
    for l in range(depth):
        mod = _ada_call(c, w_ada[l], b_ada[l])
        mod3 = mod.reshape(mod.shape[0], 1, mod.shape[1])
        qg = jnp.tile(q_norm_g[l] * (HEAD_DIM ** -0.5 * LOG2E), n_heads).reshape(1, d_attn)
        kg = jnp.tile(k_norm_g[l], n_heads).reshape(1, d_attn)
        qn, kn, vt, sg, pool_out = _proj_call(
            mod3, h, norm_g[l].reshape(1, -1), w_in[l], qg, kg,
            w_pool[l], b_pool[l].reshape(1, -1), pool_scale[l].reshape(1, -1))
        h = _attn_out_call(qn, kn, vt, sg, pool_out, h, mod3, w_out[l])
    return h
```

```python
import functools

import jax
import jax.numpy as jnp
from jax import lax
from jax.experimental import pallas as pl
from jax.experimental.pallas import tpu as pltpu

F32 = jnp.float32
BF16 = jnp.bfloat16

HEAD_DIM = 64
POOL_WINDOWS = (2, 4, 8, 16)
EPS = 1e-6
LOG2E = 1.4426950408889634
SOFTPLUS_CLAMP = 64.0
REM_FLOOR_LOG2 = -160.0
DIAG_LOOKAHEAD = 1
BAND = 3

LANES = 128
SUBLANES = 8
MXU_WIDTH = 256
ATTN_BLOCK = 128
POOL_HIST = max(POOL_WINDOWS)
POOL_PAD = 2 * POOL_HIST
VMEM_LIMIT_BYTES = 58 * 1024 * 1024


def _silu(v):
    h = 0.5 * v
    return h + h * jnp.tanh(h)


def _first_grid_step():
    return (pl.program_id(0) == 0) & (pl.program_id(1) == 0)


def _ada_kernel(c_ref, w_ref, b_ref, o_ref):
    c = c_ref[...]
    ca = _silu(c).astype(BF16)
    mod = jnp.dot(ca, w_ref[...].astype(BF16), preferred_element_type=F32) + b_ref[...]
    for i in range(mod.shape[0]):
        o_ref[i] = mod[i:i + 1, :]


def _ada_call(c, w_ada, b_ada, *, tn=1024):
    bsz, d = c.shape
    n = w_ada.shape[1]
    return pl.pallas_call(
        _ada_kernel,
        grid=(n // tn,),
        in_specs=[
            pl.BlockSpec((bsz, d), lambda j: (0, 0)),
            pl.BlockSpec((d, tn), lambda j: (0, j)),
            pl.BlockSpec((1, tn), lambda j: (0, j)),
        ],
        out_specs=pl.BlockSpec((bsz, 1, tn), lambda j: (0, 0, j)),
        out_shape=jax.ShapeDtypeStruct((bsz, 1, n), F32),
        compiler_params=pltpu.CompilerParams(
            dimension_semantics=("arbitrary",), vmem_limit_bytes=VMEM_LIMIT_BYTES),
        name="adaln_mod",
    )(c, w_ada, b_ada.reshape(1, n))


def _proj_kernel(mod_ref, x_ref, ng_ref, win_ref, qg_ref, kg_ref, wp_ref, bp_ref, ps_ref,
                 q_ref, k_ref, vt_ref, sg_ref, po_ref, wbf, wpbf, ubuf, sbuf_a, sbuf_b,
                 *, ts, d_model, d_attn, d_pool):
    t = pl.program_id(1)
    hist = slice(POOL_PAD - POOL_HIST, POOL_PAD)
    gdim = d_pool // len(POOL_WINDOWS)
    pool_pack = MXU_WIDTH // gdim

    @pl.when(_first_grid_step())
    def _():
        wbf[...] = win_ref[...].astype(BF16)
        wpbf[...] = jnp.zeros_like(wpbf)
        for g in range(len(POOL_WINDOWS)):
            p, i = divmod(g, pool_pack)
            wpbf[p, i * gdim:(i + 1) * gdim, i * gdim:(i + 1) * gdim] = wp_ref[g].astype(BF16)

    @pl.when(t == 0)
    def _():
        ubuf[0:POOL_PAD, :] = jnp.zeros((POOL_PAD, d_pool), F32)

    @pl.when(t > 0)
    def _():
        ubuf[hist, :] = ubuf[ts + POOL_PAD - POOL_HIST:ts + POOL_PAD, :]

    x = x_ref[0]
    ms = jnp.mean(x * x, axis=-1, keepdims=True)
    shift = mod_ref[0, :, 0:d_model]
    scale = mod_ref[0, :, d_model:2 * d_model]
    a = ng_ref[...] * (1.0 + scale)
    hn = ((x * lax.rsqrt(ms + EPS)) * a + shift).astype(BF16)

    def proj(col, width):
        return jnp.dot(hn, wbf[:, col:col + width], preferred_element_type=F32)

    u = proj(4 * d_attn, d_pool)
    g_pool = proj(4 * d_attn + d_pool, d_pool)
    ubuf[POOL_PAD:POOL_PAD + ts, :] = u
    n_lvl = len(POOL_WINDOWS)
    end = POOL_PAD + ts
    src = ubuf
    wsums = []
    for lvl in range(n_lvl):
        shift_rows = 2 ** lvl
        start = SUBLANES * (lvl + 1)
        lanes = slice(lvl * gdim, d_pool)
        summed = src[start:end, lanes] + src[start - shift_rows:end - shift_rows, lanes]
        wsums.append(summed[POOL_PAD - start:, 0:gdim])
        if lvl + 1 < n_lvl:
            dst = sbuf_a if lvl % 2 == 0 else sbuf_b
            dst[start:end, lanes] = summed
            src = dst

    row = lax.broadcasted_iota(jnp.int32, (POOL_HIST, gdim), 0)
    pooled = []
    for g, win in enumerate(POOL_WINDOWS):
        ug = u[:, g * gdim:(g + 1) * gdim]
        cnt = jnp.minimum(t * ts + row + 1, win).astype(F32)
        head = wsums[g][0:POOL_HIST] / cnt - ug[0:POOL_HIST]
        rest = wsums[g][POOL_HIST:] * (1.0 / win) - ug[POOL_HIST:]
        pooled.append(jnp.concatenate([head, rest], axis=0).astype(BF16))
    for p in range(len(POOL_WINDOWS) // pool_pack):
        lo, hi = p * MXU_WIDTH, (p + 1) * MXU_WIDTH
        mixed = jnp.dot(jnp.concatenate(pooled[p * pool_pack:(p + 1) * pool_pack], axis=1), wpbf[p],
                        preferred_element_type=F32) + bp_ref[:, lo:hi]
        po_ref[0, :, lo:hi] = (mixed * ps_ref[:, lo:hi] * _silu(g_pool[:, lo:hi])).astype(BF16)

    r = lax.broadcasted_iota(jnp.int32, (MXU_WIDTH, MXU_WIDTH), 0) // HEAD_DIM
    c = lax.broadcasted_iota(jnp.int32, (MXU_WIDTH, MXU_WIDTH), 1) // HEAD_DIM
    head_avg = jnp.where(r == c, 1.0 / HEAD_DIM, 0.0).astype(BF16)

    def head_norm(v, g):
        sq = (v * v).astype(BF16)
        msq = jnp.concatenate([jnp.dot(sq[:, lo:lo + MXU_WIDTH], head_avg, preferred_element_type=F32)
                               for lo in range(0, d_attn, MXU_WIDTH)], axis=1)
        return (v * lax.rsqrt(msq + EPS)) * g

    q_ref[0] = head_norm(proj(0, d_attn), qg_ref[...]).astype(BF16)
    k_ref[0] = head_norm(proj(d_attn, d_attn), kg_ref[...]).astype(BF16)

    vt = proj(2 * d_attn, d_attn).T.astype(BF16)
    for jj in range(ts // ATTN_BLOCK):
        vt_ref[0, jj] = vt[:, jj * ATTN_BLOCK:(jj + 1) * ATTN_BLOCK]

    sg_ref[0] = _silu(proj(3 * d_attn, d_attn)).astype(BF16)


def _proj_call(mod3, x, norm_g, w_in, qg, kg, w_pool, b_pool, pool_scale, *, ts=1024):
    bsz, s, d = x.shape
    d_attn = qg.shape[1]
    d_pool = pool_scale.shape[1]
    nt = s // ts
    nb = s // ATTN_BLOCK
    kern = functools.partial(_proj_kernel, ts=ts, d_model=d, d_attn=d_attn, d_pool=d_pool)
    const = lambda *shape: pl.BlockSpec(shape, lambda b, t: (0,) * len(shape))
    once = lambda *shape: pl.BlockSpec(shape, lambda b, t: (0,) * len(shape), pipeline_mode=pl.Buffered(1))
    seq_out = lambda width: pl.BlockSpec((1, ts, width), lambda b, t: (b, t, 0))
    pool_rows = POOL_PAD + ts
    return pl.pallas_call(
        kern,
        grid=(bsz, nt),
        in_specs=[
            pl.BlockSpec((1, 1, mod3.shape[2]), lambda b, t: (b, 0, 0)),
            pl.BlockSpec((1, ts, d), lambda b, t: (b, t, 0)),
            const(1, d),
            once(*w_in.shape),
            const(1, d_attn),
            const(1, d_attn),
            once(*w_pool.shape),
            const(1, d_pool),
            const(1, d_pool),
        ],
        out_specs=[
            seq_out(d_attn),
            seq_out(d_attn),
            pl.BlockSpec((1, ts // ATTN_BLOCK, d_attn, ATTN_BLOCK), lambda b, t: (b, t, 0, 0)),
            seq_out(d_attn),
            seq_out(d_pool),
        ],
        out_shape=[
            jax.ShapeDtypeStruct((bsz, s, d_attn), BF16),
            jax.ShapeDtypeStruct((bsz, s, d_attn), BF16),
            jax.ShapeDtypeStruct((bsz, nb, d_attn, ATTN_BLOCK), BF16),
            jax.ShapeDtypeStruct((bsz, s, d_attn), BF16),
            jax.ShapeDtypeStruct((bsz, s, d_pool), BF16),
        ],
        scratch_shapes=[pltpu.VMEM(w_in.shape, BF16),
                        pltpu.VMEM((d_pool // MXU_WIDTH, MXU_WIDTH, MXU_WIDTH), BF16),
                        pltpu.VMEM((pool_rows, d_pool), F32),
                        pltpu.VMEM((pool_rows, d_pool), F32),
                        pltpu.VMEM((pool_rows, d_pool), F32)],
        compiler_params=pltpu.CompilerParams(
            dimension_semantics=("arbitrary", "arbitrary"), vmem_limit_bytes=VMEM_LIMIT_BYTES),
        name="norm_inproj_pool",
    )(mod3, x, norm_g, w_in, qg, kg, w_pool, b_pool, pool_scale)


def _attn_out_kernel(q_ref, k_ref, vt_ref, sg_ref, p_ref, x_hbm, mod_ref, w_hbm, out_hbm,
                     lr_ref, acc_ref, o_buf, wbf, xbuf, ybuf, x_sem, y_sem, w_sem,
                     *, nblk, g_q, n_hp, out_rows):
    tb = ATTN_BLOCK
    b = pl.program_id(0)
    d_attn = n_hp * LANES
    d_model = xbuf.shape[1]
    n_out = nblk * tb // out_rows

    def x_copy(c):
        rows = pl.ds(c * out_rows, out_rows)
        return pltpu.make_async_copy(x_hbm.at[b, rows, :], xbuf.at[rows, :], x_sem.at[c])

    def out_copy(bb, c):
        rows = pl.ds(c * out_rows, out_rows)
        return pltpu.make_async_copy(ybuf.at[rows, :], out_hbm.at[bb, rows, :], y_sem.at[c])

    def w_copy():
        return pltpu.make_async_copy(w_hbm, ybuf.at[0:wbf.shape[0], :], w_sem.at[0])

    @pl.when(b == 0)
    def _():
        w_copy().start()

    for c in range(n_out):
        x_copy(c).start()

    row = lax.broadcasted_iota(jnp.int32, (tb, tb), 0)
    col = lax.broadcasted_iota(jnp.int32, (tb, tb), 1)
    neg_suffix = jnp.concatenate([jnp.where(col > row, -1.0, 0.0), jnp.where(col == row, -1.0, 0.0)],
                                 axis=1).astype(BF16)
    causal = row < col
    n_grp = 2 * n_hp
    grp_w = g_q * tb

    def mask_diag(v, diag):
        if not diag:
            return v
        nq = v.shape[1] // (n_grp * tb)
        keep = causal[0:v.shape[0]]
        parts = []
        for grp in range(n_grp):
            base = grp * nq * tb
            parts.append(jnp.where(keep, v[:, base:base + tb], 0.0))
            if nq > 1:
                parts.append(v[:, base + tb:base + nq * tb])
        return jnp.concatenate(parts, axis=1)

    def blocks_alive(rem, q_lo, nq):
        n_alive = jnp.int32(0)
        for g in range(nq):
            blk = jnp.concatenate([rem[:, (grp * nq + g) * tb:(grp * nq + g + 1) * tb] for grp in range(n_grp)],
                                  axis=1)
            n_alive = jnp.where(jnp.max(blk) > REM_FLOOR_LOG2, q_lo + g + 1, n_alive)
        return n_alive

    def score(j, qts, q_lo, q_hi, diag):
        nq = q_hi - q_lo
        rows_j = pl.ds(pl.multiple_of(j * tb, tb), tb)
        zs = []
        for hp in range(n_hp):
            qt = qts[hp]
            if nq < g_q:
                qt = jnp.concatenate([qt[:, h * grp_w + q_lo * tb:h * grp_w + q_hi * tb] for h in range(2)],
                                     axis=1)
            zs.append(jnp.dot(k_ref[0, rows_j, hp * LANES:(hp + 1) * LANES], qt, preferred_element_type=F32))
        z = jnp.concatenate(zs, axis=1)
        sp = jnp.maximum(jnp.log(1.0 + jnp.exp2(jnp.minimum(z, SOFTPLUS_CLAMP))) * LOG2E, z)
        neg_log_beta = mask_diag(sp - z, diag)
        sp = mask_diag(sp, diag)
        return sp[0:1, :], jnp.concatenate([sp.astype(BF16), neg_log_beta.astype(BF16)], axis=0)

    def weigh(j, sp_0, terms, q_lo, q_hi, diag):
        nq = q_hi - q_lo
        lanes = [slice(grp * grp_w + q_lo * tb, grp * grp_w + q_hi * tb) for grp in range(n_grp)]
        log_w = jnp.dot(neg_suffix, terms, preferred_element_type=F32)
        log_rem = jnp.concatenate([lr_ref[:, ln] for ln in lanes], axis=1)
        w = mask_diag(jnp.exp2(log_w + log_rem), diag).astype(BF16)
        later_0 = -(log_w[0:1, :] + terms[tb:tb + 1, :].astype(F32))
        new_rem = log_rem - (sp_0 + later_0)
        for grp in range(n_grp):
            hp, h = divmod(grp, 2)
            cols = slice(grp * nq * tb, (grp + 1) * nq * tb)
            lr_ref[:, lanes[grp]] = new_rem[:, cols]
            vt_h = vt_ref[0, j, hp * LANES + h * HEAD_DIM:hp * LANES + (h + 1) * HEAD_DIM, :]
            res = jnp.dot(vt_h, w[:, cols], preferred_element_type=F32)
            acc_ref[h * HEAD_DIM:(h + 1) * HEAD_DIM, hp * grp_w + q_lo * tb:hp * grp_w + q_hi * tb] += res
        return blocks_alive(new_rem, q_lo, nq)

    def step(j, qts, q_lo, q_hi, diag):
        return weigh(j, *score(j, qts, q_lo, q_hi, diag), q_lo, q_hi, diag)

    def emit(sb, g):
        rows = pl.ds(pl.multiple_of((sb * g_q + g) * tb, tb), tb)
        for hp in range(n_hp):
            a = hp * grp_w + g * tb
            gate = sg_ref[0, rows, hp * LANES:(hp + 1) * LANES].astype(F32)
            o_buf[rows, hp * LANES:(hp + 1) * LANES] = (acc_ref[:, a:a + tb].T * gate).astype(BF16)

    def q_super_block(sb, carry):
        qts = []
        head_of_row = lax.broadcasted_iota(jnp.int32, (LANES, grp_w), 0) // HEAD_DIM
        for hp in range(n_hp):
            rows = pl.ds(pl.multiple_of(sb * grp_w, grp_w), grp_w)
            qt = q_ref[0, rows, hp * LANES:(hp + 1) * LANES].astype(F32).T
            qts.append(jnp.concatenate([jnp.where(head_of_row == h, qt, 0.0) for h in range(2)],
                                       axis=1).astype(BF16))
        lr_ref[...] = jnp.zeros_like(lr_ref)
        acc_ref[...] = jnp.zeros_like(acc_ref)
        base = sb * g_q

        def band(offsets):
            spans = [(m, max(m, 0), min(m + BAND, g_q), m >= 0) for m in offsets]
            scored = {}
            for i in range(len(spans) + DIAG_LOOKAHEAD):
                if i < len(spans):
                    m, lo, hi, diag = spans[i]
                    scored[m] = score(base + m, qts, lo, hi, diag)
                if i >= DIAG_LOOKAHEAD:
                    m, lo, hi, diag = spans[i - DIAG_LOOKAHEAD]
                    weigh(base + m, *scored.pop(m), lo, hi, diag)
                    if 0 <= m + BAND - 1 < g_q:
                        emit(sb, m + BAND - 1)

        first_blocks = range(min(BAND - 1, g_q))
        band(range(g_q - 1, -1, -1))
        for g in first_blocks:
            emit(sb, g)

        @pl.when(sb > 0)
        def _():
            band(range(-1, -BAND, -1))
            for g in first_blocks:
                emit(sb, g)

        def more(state):
            j, alive = state
            return (j >= 0) & (alive > 0)

        alive = [blocks_alive(jnp.concatenate([lr_ref[:, grp * grp_w + g * tb:grp * grp_w + (g + 1) * tb]
                                               for grp in range(n_grp)], axis=1), g, 1) for g in range(g_q)]
        for g in range(g_q):
            def one_block_step(state, g=g):
                j, _ = state
                return j - 1, step(j, qts, g, g + 1, False)

            start = (base + g - BAND, alive[g])

            @pl.when(more(start))
            def _(g=g, start=start, one_block_step=one_block_step):
                lax.while_loop(more, one_block_step, start)
                emit(sb, g)
        return carry

    lax.fori_loop(0, nblk // g_q, q_super_block, 0)

    @pl.when(b > 0)
    def _():
        for c in range(n_out):
            out_copy(b - 1, c).wait()

    @pl.when(b == 0)
    def _():
        w_copy().wait()
        wbf[...] = ybuf[0:wbf.shape[0], :].astype(BF16)

    gate = mod_ref[0, :, 2 * d_model:3 * d_model]
    for c in range(n_out):
        rows = slice(c * out_rows, (c + 1) * out_rows)
        x_copy(c).wait()
        y = jnp.dot(o_buf[rows, :], wbf[0:d_attn, :], preferred_element_type=F32)
        y = y + jnp.dot(p_ref[0, rows, :], wbf[d_attn:, :], preferred_element_type=F32)
        ybuf[rows, :] = xbuf[rows, :] + gate * y
        out_copy(b, c).start()

    @pl.when(b == pl.num_programs(0) - 1)
    def _():
        for c in range(n_out):
            out_copy(b, c).wait()


def _attn_out_call(qn, kn, vt, sg, pool_out, x, mod3, w_out, *, g_q=16, out_rows=1024):
    bsz, s, d = x.shape
    d_attn = qn.shape[2]
    nblk = s // ATTN_BLOCK
    assert nblk % g_q == 0 and s % out_rows == 0 and d_attn % LANES == 0
    assert w_out.shape == (d_attn + pool_out.shape[2], d) and w_out.shape[0] <= s
    n_hp = d_attn // LANES
    score_lanes = g_q * n_hp * 2 * ATTN_BLOCK
    seq = pl.BlockSpec((1, s, d_attn), lambda b: (b, 0, 0))
    return pl.pallas_call(
        functools.partial(_attn_out_kernel, nblk=nblk, g_q=g_q, n_hp=n_hp, out_rows=out_rows),
        grid=(bsz,),
        in_specs=[seq, seq, pl.BlockSpec((1, nblk, d_attn, ATTN_BLOCK), lambda b: (b, 0, 0, 0)), seq,
                  pl.BlockSpec((1, s, pool_out.shape[2]), lambda b: (b, 0, 0)),
                  pl.BlockSpec(memory_space=pl.ANY),
                  pl.BlockSpec((1, 1, mod3.shape[2]), lambda b: (b, 0, 0)),
                  pl.BlockSpec(memory_space=pl.ANY)],
        out_specs=pl.BlockSpec(memory_space=pl.ANY),
        out_shape=jax.ShapeDtypeStruct((bsz, s, d), F32),
        scratch_shapes=[pltpu.VMEM((1, score_lanes), F32),
                        pltpu.VMEM((LANES, g_q * n_hp * ATTN_BLOCK), F32),
                        pltpu.VMEM((s, d_attn), BF16),
                        pltpu.VMEM(w_out.shape, BF16),
                        pltpu.VMEM((s, d), F32),
                        pltpu.VMEM((s, d), F32),
                        pltpu.SemaphoreType.DMA((s // out_rows,)),
                        pltpu.SemaphoreType.DMA((s // out_rows,)),
                        pltpu.SemaphoreType.DMA((1,))],
        compiler_params=pltpu.CompilerParams(
            dimension_semantics=("arbitrary",), vmem_limit_bytes=VMEM_LIMIT_BYTES),
        name="stickbreak_attn_outproj",
    )(qn, kn, vt, sg, pool_out, x, mod3, w_out)


def kernel(x, c, w_ada, b_ada, norm_g, w_in, q_norm_g, k_norm_g, w_pool, b_pool, pool_scale, w_out):
    depth = w_ada.shape[0]
    d_attn = w_out.shape[1] // 2
    n_heads = d_attn // HEAD_DIM
    h = x
    for l in range(depth):
        mod3 = _ada_call(c, w_ada[l], b_ada[l])
        qg = jnp.tile(q_norm_g[l] * (HEAD_DIM ** -0.5 * LOG2E), n_heads).reshape(1, d_attn)
        kg = jnp.tile(k_norm_g[l], n_heads).reshape(1, d_attn)
        qn, kn, vt, sg, pool_out = _proj_call(
            mod3, h, norm_g[l].reshape(1, -1), w_in[l], qg, kg,
            w_pool[l], b_pool[l].reshape(1, -1), pool_scale[l].reshape(1, -1))
        h = _attn_out_call(qn, kn, vt, sg, pool_out, h, mod3, w_out[l])
    return h
```

```python
import functools

import jax
import jax.numpy as jnp
from jax import lax
from jax.experimental import pallas as pl
from jax.experimental.pallas import tpu as pltpu

F32 = jnp.float32
BF16 = jnp.bfloat16

HEAD_DIM = 64
POOL_WINDOWS = (2, 4, 8, 16)
EPS = 1e-6
LOG2E = 1.4426950408889634
SOFTPLUS_CLAMP = 64.0
REM_FLOOR_LOG2 = -160.0
DIAG_LOOKAHEAD = 1
BAND = 3

LANES = 128
SUBLANES = 8
MXU_WIDTH = 256
ATTN_BLOCK = 128
POOL_HIST = max(POOL_WINDOWS)
POOL_PAD = 2 * POOL_HIST
VMEM_LIMIT_BYTES = 58 * 1024 * 1024


def _silu(v):
    h = 0.5 * v
    return h + h * jnp.tanh(h)


def _first_grid_step():
    return (pl.program_id(0) == 0) & (pl.program_id(1) == 0)


def _ada_kernel(c_ref, w_ref, b_ref, o_ref):
    c = c_ref[...]
    ca = _silu(c).astype(BF16)
    mod = jnp.dot(ca, w_ref[...].astype(BF16), preferred_element_type=F32) + b_ref[...]
    for i in range(mod.shape[0]):
        o_ref[i] = mod[i:i + 1, :]


def _ada_call(c, w_ada, b_ada, *, tn=512):
    bsz, d = c.shape
    n = w_ada.shape[1]
    return pl.pallas_call(
        _ada_kernel,
        grid=(n // tn,),
        in_specs=[
            pl.BlockSpec((bsz, d), lambda j: (0, 0)),
            pl.BlockSpec((d, tn), lambda j: (0, j)),
            pl.BlockSpec((1, tn), lambda j: (0, j)),
        ],
        out_specs=pl.BlockSpec((bsz, 1, tn), lambda j: (0, 0, j)),
        out_shape=jax.ShapeDtypeStruct((bsz, 1, n), F32),
        compiler_params=pltpu.CompilerParams(
            dimension_semantics=("arbitrary",), vmem_limit_bytes=VMEM_LIMIT_BYTES),
        name="adaln_mod",
    )(c, w_ada, b_ada.reshape(1, n))


def _proj_kernel(mod_ref, x_ref, ng_ref, win_ref, qg_ref, kg_ref, wp_ref, bp_ref, ps_ref,
                 q_ref, k_ref, vt_ref, sg_ref, po_ref, wbf, wpbf, qk_gain, ubuf, sbuf_a, sbuf_b,
                 *, ts, d_model, d_attn, d_pool):
    t = pl.program_id(1)
    hist = slice(POOL_PAD - POOL_HIST, POOL_PAD)
    gdim = d_pool // len(POOL_WINDOWS)
    pool_pack = MXU_WIDTH // gdim

    @pl.when(_first_grid_step())
    def _():
        wbf[...] = win_ref[...].astype(BF16)
        n_heads = d_attn // HEAD_DIM
        qk_gain[0:1, :] = jnp.concatenate([qg_ref[...] * (HEAD_DIM ** -0.5 * LOG2E)] * n_heads, axis=1)
        qk_gain[1:2, :] = jnp.concatenate([kg_ref[...]] * n_heads, axis=1)
        wpbf[...] = jnp.zeros_like(wpbf)
        for g in range(len(POOL_WINDOWS)):
            p, i = divmod(g, pool_pack)
            wpbf[p, i * gdim:(i + 1) * gdim, i * gdim:(i + 1) * gdim] = wp_ref[g].astype(BF16)

    @pl.when(t == 0)
    def _():
        ubuf[0:POOL_PAD, :] = jnp.zeros((POOL_PAD, d_pool), F32)

    @pl.when(t > 0)
    def _():
        ubuf[hist, :] = ubuf[ts + POOL_PAD - POOL_HIST:ts + POOL_PAD, :]

    x = x_ref[0]
    ms = jnp.mean(x * x, axis=-1, keepdims=True)
    shift = mod_ref[0, :, 0:d_model]
    scale = mod_ref[0, :, d_model:2 * d_model]
    a = ng_ref[...] * (1.0 + scale)
    hn = ((x * lax.rsqrt(ms + EPS)) * a + shift).astype(BF16)

    def proj(col, width):
        return jnp.dot(hn, wbf[:, col:col + width], preferred_element_type=F32)

    u = proj(4 * d_attn, d_pool)
    g_pool = proj(4 * d_attn + d_pool, d_pool)
    ubuf[POOL_PAD:POOL_PAD + ts, :] = u
    n_lvl = len(POOL_WINDOWS)
    end = POOL_PAD + ts
    src = ubuf
    wsums = []
    for lvl in range(n_lvl):
        shift_rows = 2 ** lvl
        start = SUBLANES * (lvl + 1)
        lanes = slice(lvl * gdim, d_pool)
        summed = src[start:end, lanes] + src[start - shift_rows:end - shift_rows, lanes]
        wsums.append(summed[POOL_PAD - start:, 0:gdim])
        if lvl + 1 < n_lvl:
            dst = sbuf_a if lvl % 2 == 0 else sbuf_b
            dst[start:end, lanes] = summed
            src = dst

    row = lax.broadcasted_iota(jnp.int32, (POOL_HIST, gdim), 0)
    pooled = []
    for g, win in enumerate(POOL_WINDOWS):
        ug = u[:, g * gdim:(g + 1) * gdim]
        cnt = jnp.minimum(t * ts + row + 1, win).astype(F32)
        head = wsums[g][0:POOL_HIST] / cnt - ug[0:POOL_HIST]
        rest = wsums[g][POOL_HIST:] * (1.0 / win) - ug[POOL_HIST:]
        pooled.append(jnp.concatenate([head, rest], axis=0).astype(BF16))
    for p in range(len(POOL_WINDOWS) // pool_pack):
        lo, hi = p * MXU_WIDTH, (p + 1) * MXU_WIDTH
        mixed = jnp.dot(jnp.concatenate(pooled[p * pool_pack:(p + 1) * pool_pack], axis=1), wpbf[p],
                        preferred_element_type=F32) + bp_ref[:, lo:hi]
        po_ref[0, :, lo:hi] = (mixed * ps_ref[:, lo:hi] * _silu(g_pool[:, lo:hi])).astype(BF16)

    r = lax.broadcasted_iota(jnp.int32, (MXU_WIDTH, MXU_WIDTH), 0) // HEAD_DIM
    c = lax.broadcasted_iota(jnp.int32, (MXU_WIDTH, MXU_WIDTH), 1) // HEAD_DIM
    head_avg = jnp.where(r == c, 1.0 / HEAD_DIM, 0.0).astype(BF16)

    def head_norm(v, g):
        sq = (v * v).astype(BF16)
        msq = jnp.concatenate([jnp.dot(sq[:, lo:lo + MXU_WIDTH], head_avg, preferred_element_type=F32)
                               for lo in range(0, d_attn, MXU_WIDTH)], axis=1)
        return (v * lax.rsqrt(msq + EPS)) * g

    q_ref[0] = head_norm(proj(0, d_attn), qk_gain[0:1, :]).astype(BF16)
    k_ref[0] = head_norm(proj(d_attn, d_attn), qk_gain[1:2, :]).astype(BF16)

    vt = proj(2 * d_attn, d_attn).T.astype(BF16)
    for jj in range(ts // ATTN_BLOCK):
        vt_ref[0, jj] = vt[:, jj * ATTN_BLOCK:(jj + 1) * ATTN_BLOCK]

    sg_ref[0] = _silu(proj(3 * d_attn, d_attn)).astype(BF16)


def _proj_call(mod3, x, norm_g, w_in, qg, kg, w_pool, b_pool, pool_scale, *, ts=1024):
    bsz, s, d = x.shape
    d_pool = pool_scale.shape[1]
    d_attn = (w_in.shape[1] - 2 * d_pool) // 4
    assert qg.shape == kg.shape == (1, HEAD_DIM)
    nt = s // ts
    nb = s // ATTN_BLOCK
    kern = functools.partial(_proj_kernel, ts=ts, d_model=d, d_attn=d_attn, d_pool=d_pool)
    const = lambda *shape: pl.BlockSpec(shape, lambda b, t: (0,) * len(shape))
    once = lambda *shape: pl.BlockSpec(shape, lambda b, t: (0,) * len(shape), pipeline_mode=pl.Buffered(1))
    seq_out = lambda width: pl.BlockSpec((1, ts, width), lambda b, t: (b, t, 0))
    pool_rows = POOL_PAD + ts
    return pl.pallas_call(
        kern,
        grid=(bsz, nt),
        in_specs=[
            pl.BlockSpec((1, 1, mod3.shape[2]), lambda b, t: (b, 0, 0)),
            pl.BlockSpec((1, ts, d), lambda b, t: (b, t, 0)),
            const(1, d),
            once(*w_in.shape),
            const(1, HEAD_DIM),
            const(1, HEAD_DIM),
            once(*w_pool.shape),
            const(1, d_pool),
            const(1, d_pool),
        ],
        out_specs=[
            seq_out(d_attn),
            seq_out(d_attn),
            pl.BlockSpec((1, ts // ATTN_BLOCK, d_attn, ATTN_BLOCK), lambda b, t: (b, t, 0, 0)),
            seq_out(d_attn),
            seq_out(d_pool),
        ],
        out_shape=[
            jax.ShapeDtypeStruct((bsz, s, d_attn), BF16),
            jax.ShapeDtypeStruct((bsz, s, d_attn), BF16),
            jax.ShapeDtypeStruct((bsz, nb, d_attn, ATTN_BLOCK), BF16),
            jax.ShapeDtypeStruct((bsz, s, d_attn), BF16),
            jax.ShapeDtypeStruct((bsz, s, d_pool), BF16),
        ],
        scratch_shapes=[pltpu.VMEM(w_in.shape, BF16),
                        pltpu.VMEM((d_pool // MXU_WIDTH, MXU_WIDTH, MXU_WIDTH), BF16),
                        pltpu.VMEM((2, d_attn), F32),
                        pltpu.VMEM((pool_rows, d_pool), F32),
                        pltpu.VMEM((pool_rows, d_pool), F32),
                        pltpu.VMEM((pool_rows, d_pool), F32)],
        compiler_params=pltpu.CompilerParams(
            dimension_semantics=("arbitrary", "arbitrary"), vmem_limit_bytes=VMEM_LIMIT_BYTES),
        name="norm_inproj_pool",
    )(mod3, x, norm_g, w_in, qg, kg, w_pool, b_pool, pool_scale)


def _attn_out_kernel(q_ref, k_ref, vt_ref, sg_ref, p_ref, x_hbm, mod_ref, w_hbm, out_hbm,
                     lr_ref, acc_ref, o_buf, wbf, xbuf, ybuf, x_sem, y_sem, w_sem,
                     *, nblk, g_q, n_hp, out_rows):
    tb = ATTN_BLOCK
    b = pl.program_id(0)
    d_attn = n_hp * LANES
    d_model = xbuf.shape[1]
    n_out = nblk * tb // out_rows

    def x_copy(c):
        rows = pl.ds(c * out_rows, out_rows)
        return pltpu.make_async_copy(x_hbm.at[b, rows, :], xbuf.at[rows, :], x_sem.at[c])

    def out_copy(bb, c):
        rows = pl.ds(c * out_rows, out_rows)
        return pltpu.make_async_copy(ybuf.at[rows, :], out_hbm.at[bb, rows, :], y_sem.at[c])

    def w_copy():
        return pltpu.make_async_copy(w_hbm, ybuf.at[0:wbf.shape[0], :], w_sem.at[0])

    @pl.when(b == 0)
    def _():
        w_copy().start()

    for c in range(n_out):
        x_copy(c).start()

    row = lax.broadcasted_iota(jnp.int32, (tb, tb), 0)
    col = lax.broadcasted_iota(jnp.int32, (tb, tb), 1)
    neg_suffix = jnp.concatenate([jnp.where(col > row, -1.0, 0.0), jnp.where(col == row, -1.0, 0.0)],
                                 axis=1).astype(BF16)
    causal = row < col
    n_grp = 2 * n_hp
    grp_w = g_q * tb

    def mask_diag(v, diag):
        if not diag:
            return v
        nq = v.shape[1] // (n_grp * tb)
        keep = causal[0:v.shape[0]]
        parts = []
        for grp in range(n_grp):
            base = grp * nq * tb
            parts.append(jnp.where(keep, v[:, base:base + tb], 0.0))
            if nq > 1:
                parts.append(v[:, base + tb:base + nq * tb])
        return jnp.concatenate(parts, axis=1)

    def blocks_alive(rem, q_lo, nq):
        n_alive = jnp.int32(0)
        for g in range(nq):
            blk = jnp.concatenate([rem[:, (grp * nq + g) * tb:(grp * nq + g + 1) * tb] for grp in range(n_grp)],
                                  axis=1)
            n_alive = jnp.where(jnp.max(blk) > REM_FLOOR_LOG2, q_lo + g + 1, n_alive)
        return n_alive

    def score(j, qts, q_lo, q_hi, diag):
        nq = q_hi - q_lo
        rows_j = pl.ds(pl.multiple_of(j * tb, tb), tb)
        zs = []
        for hp in range(n_hp):
            qt = qts[hp]
            if nq < g_q:
                qt = jnp.concatenate([qt[:, h * grp_w + q_lo * tb:h * grp_w + q_hi * tb] for h in range(2)],
                                     axis=1)
            zs.append(jnp.dot(k_ref[0, rows_j, hp * LANES:(hp + 1) * LANES], qt, preferred_element_type=F32))
        z = jnp.concatenate(zs, axis=1)
        sp = jnp.maximum(jnp.log(1.0 + jnp.exp2(jnp.minimum(z, SOFTPLUS_CLAMP))) * LOG2E, z)
        neg_log_beta = mask_diag(sp - z, diag)
        sp = mask_diag(sp, diag)
        return sp[0:1, :], jnp.concatenate([sp.astype(BF16), neg_log_beta.astype(BF16)], axis=0)

    def weigh(j, sp_0, terms, q_lo, q_hi, diag):
        nq = q_hi - q_lo
        lanes = [slice(grp * grp_w + q_lo * tb, grp * grp_w + q_hi * tb) for grp in range(n_grp)]
        log_w = jnp.dot(neg_suffix, terms, preferred_element_type=F32)
        log_rem = jnp.concatenate([lr_ref[:, ln] for ln in lanes], axis=1)
        w = mask_diag(jnp.exp2(log_w + log_rem), diag).astype(BF16)
        later_0 = -(log_w[0:1, :] + terms[tb:tb + 1, :].astype(F32))
        new_rem = log_rem - (sp_0 + later_0)
        for grp in range(n_grp):
            hp, h = divmod(grp, 2)
            cols = slice(grp * nq * tb, (grp + 1) * nq * tb)
            lr_ref[:, lanes[grp]] = new_rem[:, cols]
            vt_h = vt_ref[0, j, hp * LANES + h * HEAD_DIM:hp * LANES + (h + 1) * HEAD_DIM, :]
            res = jnp.dot(vt_h, w[:, cols], preferred_element_type=F32)
            acc_ref[h * HEAD_DIM:(h + 1) * HEAD_DIM, hp * grp_w + q_lo * tb:hp * grp_w + q_hi * tb] += res
        return blocks_alive(new_rem, q_lo, nq)

    def step(j, qts, q_lo, q_hi, diag):
        return weigh(j, *score(j, qts, q_lo, q_hi, diag), q_lo, q_hi, diag)

    def emit(sb, g):
        rows = pl.ds(pl.multiple_of((sb * g_q + g) * tb, tb), tb)
        for hp in range(n_hp):
            a = hp * grp_w + g * tb
            gate = sg_ref[0, rows, hp * LANES:(hp + 1) * LANES].astype(F32)
            o_buf[rows, hp * LANES:(hp + 1) * LANES] = (acc_ref[:, a:a + tb].T * gate).astype(BF16)

    def q_super_block(sb, carry):
        qts = []
        head_of_row = lax.broadcasted_iota(jnp.int32, (LANES, grp_w), 0) // HEAD_DIM
        for hp in range(n_hp):
            rows = pl.ds(pl.multiple_of(sb * grp_w, grp_w), grp_w)
            qt = q_ref[0, rows, hp * LANES:(hp + 1) * LANES].astype(F32).T
            qts.append(jnp.concatenate([jnp.where(head_of_row == h, qt, 0.0) for h in range(2)],
                                       axis=1).astype(BF16))
        lr_ref[...] = jnp.zeros_like(lr_ref)
        acc_ref[...] = jnp.zeros_like(acc_ref)
        base = sb * g_q

        def band(offsets):
            spans = [(m, max(m, 0), min(m + BAND, g_q), m >= 0) for m in offsets]
            scored = {}
            for i in range(len(spans) + DIAG_LOOKAHEAD):
                if i < len(spans):
                    m, lo, hi, diag = spans[i]
                    scored[m] = score(base + m, qts, lo, hi, diag)
                if i >= DIAG_LOOKAHEAD:
                    m, lo, hi, diag = spans[i - DIAG_LOOKAHEAD]
                    weigh(base + m, *scored.pop(m), lo, hi, diag)
                    if 0 <= m + BAND - 1 < g_q:
                        emit(sb, m + BAND - 1)

        first_blocks = range(min(BAND - 1, g_q))
        band(range(g_q - 1, -1, -1))
        for g in first_blocks:
            emit(sb, g)

        @pl.when(sb > 0)
        def _():
            band(range(-1, -BAND, -1))
            for g in first_blocks:
                emit(sb, g)

        def more(state):
            j, alive = state
            return (j >= 0) & (alive > 0)

        alive = [blocks_alive(jnp.concatenate([lr_ref[:, grp * grp_w + g * tb:grp * grp_w + (g + 1) * tb]
                                               for grp in range(n_grp)], axis=1), g, 1) for g in range(g_q)]
        for g in range(g_q):
            def one_block_step(state, g=g):
                j, _ = state
                return j - 1, step(j, qts, g, g + 1, False)

            start = (base + g - BAND, alive[g])

            @pl.when(more(start))
            def _(g=g, start=start, one_block_step=one_block_step):
                lax.while_loop(more, one_block_step, start)
                emit(sb, g)
        return carry

    lax.fori_loop(0, nblk // g_q, q_super_block, 0)

    @pl.when(b > 0)
    def _():
        for c in range(n_out):
            out_copy(b - 1, c).wait()

    @pl.when(b == 0)
    def _():
        w_copy().wait()
        wbf[...] = ybuf[0:wbf.shape[0], :].astype(BF16)

    gate = mod_ref[0, :, 2 * d_model:3 * d_model]
    for c in range(n_out):
        rows = slice(c * out_rows, (c + 1) * out_rows)
        x_copy(c).wait()
        y = jnp.dot(o_buf[rows, :], wbf[0:d_attn, :], preferred_element_type=F32)
        y = y + jnp.dot(p_ref[0, rows, :], wbf[d_attn:, :], preferred_element_type=F32)
        ybuf[rows, :] = xbuf[rows, :] + gate * y
        out_copy(b, c).start()

    @pl.when(b == pl.num_programs(0) - 1)
    def _():
        for c in range(n_out):
            out_copy(b, c).wait()


def _attn_out_call(qn, kn, vt, sg, pool_out, x, mod3, w_out, *, g_q=16, out_rows=1024):
    bsz, s, d = x.shape
    d_attn = qn.shape[2]
    nblk = s // ATTN_BLOCK
    assert nblk % g_q == 0 and s % out_rows == 0 and d_attn % LANES == 0
    assert w_out.shape == (d_attn + pool_out.shape[2], d) and w_out.shape[0] <= s
    n_hp = d_attn // LANES
    score_lanes = g_q * n_hp * 2 * ATTN_BLOCK
    seq = pl.BlockSpec((1, s, d_attn), lambda b: (b, 0, 0))
    return pl.pallas_call(
        functools.partial(_attn_out_kernel, nblk=nblk, g_q=g_q, n_hp=n_hp, out_rows=out_rows),
        grid=(bsz,),
        in_specs=[seq, seq, pl.BlockSpec((1, nblk, d_attn, ATTN_BLOCK), lambda b: (b, 0, 0, 0)), seq,
                  pl.BlockSpec((1, s, pool_out.shape[2]), lambda b: (b, 0, 0)),
                  pl.BlockSpec(memory_space=pl.ANY),
                  pl.BlockSpec((1, 1, mod3.shape[2]), lambda b: (b, 0, 0)),
                  pl.BlockSpec(memory_space=pl.ANY)],
        out_specs=pl.BlockSpec(memory_space=pl.ANY),
        out_shape=jax.ShapeDtypeStruct((bsz, s, d), F32),
        scratch_shapes=[pltpu.VMEM((1, score_lanes), F32),
                        pltpu.VMEM((LANES, g_q * n_hp * ATTN_BLOCK), F32),
                        pltpu.VMEM((s, d_attn), BF16),
                        pltpu.VMEM(w_out.shape, BF16),
                        pltpu.VMEM((s, d), F32),
                        pltpu.VMEM((s, d), F32),
                        pltpu.SemaphoreType.DMA((s // out_rows,)),
                        pltpu.SemaphoreType.DMA((s // out_rows,)),
                        pltpu.SemaphoreType.DMA((1,))],
        compiler_params=pltpu.CompilerParams(
            dimension_semantics=("arbitrary",), vmem_limit_bytes=VMEM_LIMIT_BYTES),
        name="stickbreak_attn_outproj",
    )(qn, kn, vt, sg, pool_out, x, mod3, w_out)


def kernel(x, c, w_ada, b_ada, norm_g, w_in, q_norm_g, k_norm_g, w_pool, b_pool, pool_scale, w_out):
    depth = w_ada.shape[0]
    h = x
    for l in range(depth):
        mod3 = _ada_call(c, w_ada[l], b_ada[l])
        qn, kn, vt, sg, pool_out = _proj_call(
            mod3, h, norm_g[l].reshape(1, -1), w_in[l], q_norm_g[l].reshape(1, -1), k_norm_g[l].reshape(1, -1),
            w_pool[l], b_pool[l].reshape(1, -1), pool_scale[l].reshape(1, -1))
        h = _attn_out_call(qn, kn, vt, sg, pool_out, h, mod3, w_out[l])
    return h
```

```python
import functools

import jax
import jax.numpy as jnp
from jax import lax
from jax.experimental import pallas as pl
from jax.experimental.pallas import tpu as pltpu

F32 = jnp.float32
BF16 = jnp.bfloat16

HEAD_DIM = 64
POOL_WINDOWS = (2, 4, 8, 16)
EPS = 1e-6
LOG2E = 1.4426950408889634
SOFTPLUS_CLAMP = 64.0
REM_FLOOR_LOG2 = -136.0
DIAG_LOOKAHEAD = 1
BAND = 3

LANES = 128
SUBLANES = 8
MXU_WIDTH = 256
ATTN_BLOCK = 128
POOL_HIST = max(POOL_WINDOWS)
POOL_PAD = 2 * POOL_HIST
VMEM_LIMIT_BYTES = 58 * 1024 * 1024


def _silu(v):
    h = 0.5 * v
    return h + h * jnp.tanh(h)


def _first_grid_step():
    return (pl.program_id(0) == 0) & (pl.program_id(1) == 0)


def _ada_kernel(c_ref, w_ref, b_ref, o_ref):
    c = c_ref[...]
    ca = _silu(c).astype(BF16)
    mod = jnp.dot(ca, w_ref[...].astype(BF16), preferred_element_type=F32) + b_ref[...]
    for i in range(mod.shape[0]):
        o_ref[i] = mod[i:i + 1, :]


def _ada_call(c, w_ada, b_ada, *, tn=1024):
    bsz, d = c.shape
    n = w_ada.shape[1]
    return pl.pallas_call(
        _ada_kernel,
        grid=(n // tn,),
        in_specs=[
            pl.BlockSpec((bsz, d), lambda j: (0, 0)),
            pl.BlockSpec((d, tn), lambda j: (0, j)),
            pl.BlockSpec((1, tn), lambda j: (0, j)),
        ],
        out_specs=pl.BlockSpec((bsz, 1, tn), lambda j: (0, 0, j)),
        out_shape=jax.ShapeDtypeStruct((bsz, 1, n), F32),
        compiler_params=pltpu.CompilerParams(
            dimension_semantics=("arbitrary",), vmem_limit_bytes=VMEM_LIMIT_BYTES),
        name="adaln_mod",
    )(c, w_ada, b_ada.reshape(1, n))


def _proj_kernel(mod_ref, x_ref, ng_ref, win_ref, qg_ref, kg_ref, wp_ref, bp_ref, ps_ref,
                 q_ref, k_ref, vt_ref, sg_ref, po_ref, wbf, wpbf, qk_gain, ubuf, sbuf_a, sbuf_b,
                 *, ts, d_model, d_attn, d_pool):
    t = pl.program_id(1)
    hist = slice(POOL_PAD - POOL_HIST, POOL_PAD)
    gdim = d_pool // len(POOL_WINDOWS)
    pool_pack = MXU_WIDTH // gdim

    @pl.when(_first_grid_step())
    def _():
        wbf[...] = win_ref[...].astype(BF16)
        n_heads = d_attn // HEAD_DIM
        qk_gain[0:1, :] = jnp.concatenate([qg_ref[...] * (HEAD_DIM ** -0.5 * LOG2E)] * n_heads, axis=1)
        qk_gain[1:2, :] = jnp.concatenate([kg_ref[...]] * n_heads, axis=1)
        wpbf[...] = jnp.zeros_like(wpbf)
        for g in range(len(POOL_WINDOWS)):
            p, i = divmod(g, pool_pack)
            wpbf[p, i * gdim:(i + 1) * gdim, i * gdim:(i + 1) * gdim] = wp_ref[g].astype(BF16)

    @pl.when(t == 0)
    def _():
        ubuf[0:POOL_PAD, :] = jnp.zeros((POOL_PAD, d_pool), F32)

    @pl.when(t > 0)
    def _():
        ubuf[hist, :] = ubuf[ts + POOL_PAD - POOL_HIST:ts + POOL_PAD, :]

    x = x_ref[0]
    ms = jnp.mean(x * x, axis=-1, keepdims=True)
    shift = mod_ref[0, :, 0:d_model]
    scale = mod_ref[0, :, d_model:2 * d_model]
    a = ng_ref[...] * (1.0 + scale)
    hn = ((x * lax.rsqrt(ms + EPS)) * a + shift).astype(BF16)

    def proj(col, width):
        return jnp.dot(hn, wbf[:, col:col + width], preferred_element_type=F32)

    u = proj(4 * d_attn, d_pool)
    g_pool = proj(4 * d_attn + d_pool, d_pool)
    ubuf[POOL_PAD:POOL_PAD + ts, :] = u
    n_lvl = len(POOL_WINDOWS)
    end = POOL_PAD + ts
    src = ubuf
    wsums = []
    for lvl in range(n_lvl):
        shift_rows = 2 ** lvl
        start = SUBLANES * (lvl + 1)
        lanes = slice(lvl * gdim, d_pool)
        summed = src[start:end, lanes] + src[start - shift_rows:end - shift_rows, lanes]
        wsums.append(summed[POOL_PAD - start:, 0:gdim])
        if lvl + 1 < n_lvl:
            dst = sbuf_a if lvl % 2 == 0 else sbuf_b
            dst[start:end, lanes] = summed
            src = dst

    row = lax.broadcasted_iota(jnp.int32, (POOL_HIST, gdim), 0)
    pooled = []
    for g, win in enumerate(POOL_WINDOWS):
        ug = u[:, g * gdim:(g + 1) * gdim]
        cnt = jnp.minimum(t * ts + row + 1, win).astype(F32)
        head = wsums[g][0:POOL_HIST] / cnt - ug[0:POOL_HIST]
        rest = wsums[g][POOL_HIST:] * (1.0 / win) - ug[POOL_HIST:]
        pooled.append(jnp.concatenate([head, rest], axis=0).astype(BF16))
    for p in range(len(POOL_WINDOWS) // pool_pack):
        lo, hi = p * MXU_WIDTH, (p + 1) * MXU_WIDTH
        mixed = jnp.dot(jnp.concatenate(pooled[p * pool_pack:(p + 1) * pool_pack], axis=1), wpbf[p],
                        preferred_element_type=F32) + bp_ref[:, lo:hi]
        po_ref[0, :, lo:hi] = (mixed * ps_ref[:, lo:hi] * _silu(g_pool[:, lo:hi])).astype(BF16)

    r = lax.broadcasted_iota(jnp.int32, (MXU_WIDTH, MXU_WIDTH), 0) // HEAD_DIM
    c = lax.broadcasted_iota(jnp.int32, (MXU_WIDTH, MXU_WIDTH), 1) // HEAD_DIM
    head_avg = jnp.where(r == c, 1.0 / HEAD_DIM, 0.0).astype(BF16)

    def head_norm(v, g):
        sq = (v * v).astype(BF16)
        msq = jnp.concatenate([jnp.dot(sq[:, lo:lo + MXU_WIDTH], head_avg, preferred_element_type=F32)
                               for lo in range(0, d_attn, MXU_WIDTH)], axis=1)
        return (v * lax.rsqrt(msq + EPS)) * g

    q_ref[0] = head_norm(proj(0, d_attn), qk_gain[0:1, :]).astype(BF16)
    k_ref[0] = head_norm(proj(d_attn, d_attn), qk_gain[1:2, :]).astype(BF16)

    vt = proj(2 * d_attn, d_attn).T.astype(BF16)
    for jj in range(ts // ATTN_BLOCK):
        vt_ref[0, jj] = vt[:, jj * ATTN_BLOCK:(jj + 1) * ATTN_BLOCK]

    sg_ref[0] = _silu(proj(3 * d_attn, d_attn)).astype(BF16)


def _proj_call(mod3, x, norm_g, w_in, qg, kg, w_pool, b_pool, pool_scale, *, ts=1024):
    bsz, s, d = x.shape
    d_pool = pool_scale.shape[1]
    d_attn = (w_in.shape[1] - 2 * d_pool) // 4
    assert qg.shape == kg.shape == (1, HEAD_DIM)
    nt = s // ts
    nb = s // ATTN_BLOCK
    kern = functools.partial(_proj_kernel, ts=ts, d_model=d, d_attn=d_attn, d_pool=d_pool)
    const = lambda *shape: pl.BlockSpec(shape, lambda b, t: (0,) * len(shape))
    once = lambda *shape: pl.BlockSpec(shape, lambda b, t: (0,) * len(shape), pipeline_mode=pl.Buffered(1))
    seq_out = lambda width: pl.BlockSpec((1, ts, width), lambda b, t: (b, t, 0))
    pool_rows = POOL_PAD + ts
    return pl.pallas_call(
        kern,
        grid=(bsz, nt),
        in_specs=[
            pl.BlockSpec((1, 1, mod3.shape[2]), lambda b, t: (b, 0, 0)),
            pl.BlockSpec((1, ts, d), lambda b, t: (b, t, 0)),
            const(1, d),
            once(*w_in.shape),
            const(1, HEAD_DIM),
            const(1, HEAD_DIM),
            once(*w_pool.shape),
            const(1, d_pool),
            const(1, d_pool),
        ],
        out_specs=[
            seq_out(d_attn),
            seq_out(d_attn),
            pl.BlockSpec((1, ts // ATTN_BLOCK, d_attn, ATTN_BLOCK), lambda b, t: (b, t, 0, 0)),
            seq_out(d_attn),
            seq_out(d_pool),
        ],
        out_shape=[
            jax.ShapeDtypeStruct((bsz, s, d_attn), BF16),
            jax.ShapeDtypeStruct((bsz, s, d_attn), BF16),
            jax.ShapeDtypeStruct((bsz, nb, d_attn, ATTN_BLOCK), BF16),
            jax.ShapeDtypeStruct((bsz, s, d_attn), BF16),
            jax.ShapeDtypeStruct((bsz, s, d_pool), BF16),
        ],
        scratch_shapes=[pltpu.VMEM(w_in.shape, BF16),
                        pltpu.VMEM((d_pool // MXU_WIDTH, MXU_WIDTH, MXU_WIDTH), BF16),
                        pltpu.VMEM((2, d_attn), F32),
                        pltpu.VMEM((pool_rows, d_pool), F32),
                        pltpu.VMEM((pool_rows, d_pool), F32),
                        pltpu.VMEM((pool_rows, d_pool), F32)],
        compiler_params=pltpu.CompilerParams(
            dimension_semantics=("arbitrary", "arbitrary"), vmem_limit_bytes=VMEM_LIMIT_BYTES),
        name="norm_inproj_pool",
    )(mod3, x, norm_g, w_in, qg, kg, w_pool, b_pool, pool_scale)


def _attn_out_kernel(q_ref, k_ref, vt_ref, sg_ref, p_ref, x_hbm, mod_ref, w_hbm, out_hbm,
                     lr_ref, acc_ref, o_buf, wbf, xbuf, ybuf, x_sem, y_sem, w_sem,
                     *, nblk, g_q, n_hp, out_rows):
    tb = ATTN_BLOCK
    b = pl.program_id(0)
    d_attn = n_hp * LANES
    d_model = xbuf.shape[1]
    n_out = nblk * tb // out_rows

    def x_copy(c):
        rows = pl.ds(c * out_rows, out_rows)
        return pltpu.make_async_copy(x_hbm.at[b, rows, :], xbuf.at[rows, :], x_sem.at[c])

    def out_copy(bb, c):
        rows = pl.ds(c * out_rows, out_rows)
        return pltpu.make_async_copy(ybuf.at[rows, :], out_hbm.at[bb, rows, :], y_sem.at[c])

    def w_copy():
        return pltpu.make_async_copy(w_hbm, ybuf.at[0:wbf.shape[0], :], w_sem.at[0])

    @pl.when(b == 0)
    def _():
        w_copy().start()

    for c in range(n_out):
        x_copy(c).start()

    row = lax.broadcasted_iota(jnp.int32, (tb, tb), 0)
    col = lax.broadcasted_iota(jnp.int32, (tb, tb), 1)
    neg_suffix = jnp.concatenate([jnp.where(col > row, -1.0, 0.0), jnp.where(col == row, -1.0, 0.0)],
                                 axis=1).astype(BF16)
    causal = row < col
    n_grp = 2 * n_hp
    grp_w = g_q * tb

    def mask_diag(v, diag):
        if not diag:
            return v
        nq = v.shape[1] // (n_grp * tb)
        keep = causal[0:v.shape[0]]
        parts = []
        for grp in range(n_grp):
            base = grp * nq * tb
            parts.append(jnp.where(keep, v[:, base:base + tb], 0.0))
            if nq > 1:
                parts.append(v[:, base + tb:base + nq * tb])
        return jnp.concatenate(parts, axis=1)

    def blocks_alive(rem, q_lo, nq):
        n_alive = jnp.int32(0)
        for g in range(nq):
            blk = jnp.concatenate([rem[:, (grp * nq + g) * tb:(grp * nq + g + 1) * tb] for grp in range(n_grp)],
                                  axis=1)
            n_alive = jnp.where(jnp.max(blk) > REM_FLOOR_LOG2, q_lo + g + 1, n_alive)
        return n_alive

    def score(j, qts, q_lo, q_hi, diag):
        nq = q_hi - q_lo
        rows_j = pl.ds(pl.multiple_of(j * tb, tb), tb)
        zs = []
        for hp in range(n_hp):
            qt = qts[hp]
            if nq < g_q:
                qt = jnp.concatenate([qt[:, h * grp_w + q_lo * tb:h * grp_w + q_hi * tb] for h in range(2)],
                                     axis=1)
            zs.append(jnp.dot(k_ref[0, rows_j, hp * LANES:(hp + 1) * LANES], qt, preferred_element_type=F32))
        z = jnp.concatenate(zs, axis=1)
        sp = jnp.maximum(jnp.log(1.0 + jnp.exp2(jnp.minimum(z, SOFTPLUS_CLAMP))) * LOG2E, z)
        neg_log_beta = mask_diag(sp - z, diag)
        sp = mask_diag(sp, diag)
        return sp[0:1, :], jnp.concatenate([sp.astype(BF16), neg_log_beta.astype(BF16)], axis=0)

    def weigh(j, sp_0, terms, q_lo, q_hi, diag):
        nq = q_hi - q_lo
        lanes = [slice(grp * grp_w + q_lo * tb, grp * grp_w + q_hi * tb) for grp in range(n_grp)]
        log_w = jnp.dot(neg_suffix, terms, preferred_element_type=F32)
        log_rem = jnp.concatenate([lr_ref[:, ln] for ln in lanes], axis=1)
        w = mask_diag(jnp.exp2(log_w + log_rem), diag).astype(BF16)
        later_0 = -(log_w[0:1, :] + terms[tb:tb + 1, :].astype(F32))
        new_rem = log_rem - (sp_0 + later_0)
        for grp in range(n_grp):
            hp, h = divmod(grp, 2)
            cols = slice(grp * nq * tb, (grp + 1) * nq * tb)
            lr_ref[:, lanes[grp]] = new_rem[:, cols]
            vt_h = vt_ref[0, j, hp * LANES + h * HEAD_DIM:hp * LANES + (h + 1) * HEAD_DIM, :]
            res = jnp.dot(vt_h, w[:, cols], preferred_element_type=F32)
            acc_ref[h * HEAD_DIM:(h + 1) * HEAD_DIM, hp * grp_w + q_lo * tb:hp * grp_w + q_hi * tb] += res
        return blocks_alive(new_rem, q_lo, nq)

    def step(j, qts, q_lo, q_hi, diag):
        return weigh(j, *score(j, qts, q_lo, q_hi, diag), q_lo, q_hi, diag)

    def emit(sb, g):
        rows = pl.ds(pl.multiple_of((sb * g_q + g) * tb, tb), tb)
        for hp in range(n_hp):
            a = hp * grp_w + g * tb
            gate = sg_ref[0, rows, hp * LANES:(hp + 1) * LANES].astype(F32)
            o_buf[rows, hp * LANES:(hp + 1) * LANES] = (acc_ref[:, a:a + tb].T * gate).astype(BF16)

    def q_super_block(sb, carry):
        qts = []
        head_of_row = lax.broadcasted_iota(jnp.int32, (LANES, grp_w), 0) // HEAD_DIM
        for hp in range(n_hp):
            rows = pl.ds(pl.multiple_of(sb * grp_w, grp_w), grp_w)
            qt = q_ref[0, rows, hp * LANES:(hp + 1) * LANES].astype(F32).T
            qts.append(jnp.concatenate([jnp.where(head_of_row == h, qt, 0.0) for h in range(2)],
                                       axis=1).astype(BF16))
        lr_ref[...] = jnp.zeros_like(lr_ref)
        acc_ref[...] = jnp.zeros_like(acc_ref)
        base = sb * g_q

        def band(offsets):
            spans = [(m, max(m, 0), min(m + BAND, g_q), m >= 0) for m in offsets]
            scored = {}
            for i in range(len(spans) + DIAG_LOOKAHEAD):
                if i < len(spans):
                    m, lo, hi, diag = spans[i]
                    scored[m] = score(base + m, qts, lo, hi, diag)
                if i >= DIAG_LOOKAHEAD:
                    m, lo, hi, diag = spans[i - DIAG_LOOKAHEAD]
                    weigh(base + m, *scored.pop(m), lo, hi, diag)
                    if 0 <= m + BAND - 1 < g_q:
                        emit(sb, m + BAND - 1)

        first_blocks = range(min(BAND - 1, g_q))
        band(range(g_q - 1, -1, -1))
        for g in first_blocks:
            emit(sb, g)

        @pl.when(sb > 0)
        def _():
            band(range(-1, -BAND, -1))
            for g in first_blocks:
                emit(sb, g)

        def more(state):
            j, alive = state
            return (j >= 0) & (alive > 0)

        alive = [blocks_alive(jnp.concatenate([lr_ref[:, grp * grp_w + g * tb:grp * grp_w + (g + 1) * tb]
                                               for grp in range(n_grp)], axis=1), g, 1) for g in range(g_q)]
        for g in range(g_q):
            def one_block_step(state, g=g):
                j, _ = state
                return j - 1, step(j, qts, g, g + 1, False)

            start = (base + g - BAND, alive[g])

            @pl.when(more(start))
            def _(g=g, start=start, one_block_step=one_block_step):
                lax.while_loop(more, one_block_step, start)
                emit(sb, g)
        return carry

    lax.fori_loop(0, nblk // g_q, q_super_block, 0)

    @pl.when(b > 0)
    def _():
        for c in range(n_out):
            out_copy(b - 1, c).wait()

    @pl.when(b == 0)
    def _():
        w_copy().wait()
        wbf[...] = ybuf[0:wbf.shape[0], :].astype(BF16)

    gate = mod_ref[0, :, 2 * d_model:3 * d_model]
    for c in range(n_out):
        rows = slice(c * out_rows, (c + 1) * out_rows)
        x_copy(c).wait()
        y = jnp.dot(o_buf[rows, :], wbf[0:d_attn, :], preferred_element_type=F32)
        y = y + jnp.dot(p_ref[0, rows, :], wbf[d_attn:, :], preferred_element_type=F32)
        ybuf[rows, :] = xbuf[rows, :] + gate * y
        out_copy(b, c).start()

    @pl.when(b == pl.num_programs(0) - 1)
    def _():
        for c in range(n_out):
            out_copy(b, c).wait()


def _attn_out_call(qn, kn, vt, sg, pool_out, x, mod3, w_out, *, g_q=16, out_rows=1024):
    bsz, s, d = x.shape
    d_attn = qn.shape[2]
    nblk = s // ATTN_BLOCK
    assert nblk % g_q == 0 and s % out_rows == 0 and d_attn % LANES == 0
    assert w_out.shape == (d_attn + pool_out.shape[2], d) and w_out.shape[0] <= s
    n_hp = d_attn // LANES
    score_lanes = g_q * n_hp * 2 * ATTN_BLOCK
    seq = pl.BlockSpec((1, s, d_attn), lambda b: (b, 0, 0))
    return pl.pallas_call(
        functools.partial(_attn_out_kernel, nblk=nblk, g_q=g_q, n_hp=n_hp, out_rows=out_rows),
        grid=(bsz,),
        in_specs=[seq, seq, pl.BlockSpec((1, nblk, d_attn, ATTN_BLOCK), lambda b: (b, 0, 0, 0)), seq,
                  pl.BlockSpec((1, s, pool_out.shape[2]), lambda b: (b, 0, 0)),
                  pl.BlockSpec(memory_space=pl.ANY),
                  pl.BlockSpec((1, 1, mod3.shape[2]), lambda b: (b, 0, 0)),
                  pl.BlockSpec(memory_space=pl.ANY)],
        out_specs=pl.BlockSpec(memory_space=pl.ANY),
        out_shape=jax.ShapeDtypeStruct((bsz, s, d), F32),
        scratch_shapes=[pltpu.VMEM((1, score_lanes), F32),
                        pltpu.VMEM((LANES, g_q * n_hp * ATTN_BLOCK), F32),
                        pltpu.VMEM((s, d_attn), BF16),
                        pltpu.VMEM(w_out.shape, BF16),
                        pltpu.VMEM((s, d), F32),
                        pltpu.VMEM((s, d), F32),
                        pltpu.SemaphoreType.DMA((s // out_rows,)),
                        pltpu.SemaphoreType.DMA((s // out_rows,)),
                        pltpu.SemaphoreType.DMA((1,))],
        compiler_params=pltpu.CompilerParams(
            dimension_semantics=("arbitrary",), vmem_limit_bytes=VMEM_LIMIT_BYTES),
        name="stickbreak_attn_outproj",
    )(qn, kn, vt, sg, pool_out, x, mod3, w_out)


def kernel(x, c, w_ada, b_ada, norm_g, w_in, q_norm_g, k_norm_g, w_pool, b_pool, pool_scale, w_out):
    depth = w_ada.shape[0]
    h = x
    for l in range(depth):
        mod3 = _ada_call(c, w_ada[l], b_ada[l])
        qn, kn, vt, sg, pool_out = _proj_call(
            mod3, h, norm_g[l].reshape(1, -1), w_in[l], q_norm_g[l].reshape(1, -1), k_norm_g[l].reshape(1, -1),
            w_pool[l], b_pool[l].reshape(1, -1), pool_scale[l].reshape(1, -1))
        h = _attn_out_call(qn, kn, vt, sg, pool_out, h, mod3, w_out[l])
    return h
```

```python
import functools

import jax
import jax.numpy as jnp
from jax import lax
from jax.experimental import pallas as pl
from jax.experimental.pallas import tpu as pltpu

F32 = jnp.float32
BF16 = jnp.bfloat16

HEAD_DIM = 64
POOL_WINDOWS = (2, 4, 8, 16)
EPS = 1e-6
LOG2E = 1.4426950408889634
SOFTPLUS_CLAMP = 64.0
REM_FLOOR_LOG2 = -136.0
DIAG_LOOKAHEAD = 1
BAND = 3

LANES = 128
SUBLANES = 8
MXU_WIDTH = 256
ATTN_BLOCK = 128
POOL_HIST = max(POOL_WINDOWS)
POOL_PAD = 2 * POOL_HIST
VMEM_LIMIT_BYTES = 58 * 1024 * 1024


def _silu(v):
    h = 0.5 * v
    return h + h * jnp.tanh(h)


def _first_grid_step():
    return (pl.program_id(0) == 0) & (pl.program_id(1) == 0)


def _ada_kernel(c_ref, w_ref, b_ref, o_ref):
    c = c_ref[...]
    ca = _silu(c).astype(BF16)
    mod = jnp.dot(ca, w_ref[...].astype(BF16), preferred_element_type=F32) + b_ref[...]
    for i in range(mod.shape[0]):
        o_ref[i] = mod[i:i + 1, :]


def _ada_call(c, w_ada, b_ada, *, tn=1024):
    bsz, d = c.shape
    n = w_ada.shape[1]
    return pl.pallas_call(
        _ada_kernel,
        grid=(n // tn,),
        in_specs=[
            pl.BlockSpec((bsz, d), lambda j: (0, 0)),
            pl.BlockSpec((d, tn), lambda j: (0, j)),
            pl.BlockSpec((1, tn), lambda j: (0, j)),
        ],
        out_specs=pl.BlockSpec((bsz, 1, tn), lambda j: (0, 0, j)),
        out_shape=jax.ShapeDtypeStruct((bsz, 1, n), F32),
        compiler_params=pltpu.CompilerParams(
            dimension_semantics=("arbitrary",), vmem_limit_bytes=VMEM_LIMIT_BYTES),
        name="adaln_mod",
    )(c, w_ada, b_ada.reshape(1, n))


def _proj_kernel(mod_ref, x_ref, ng_ref, win_ref, qg_ref, kg_ref, wp_ref, bp_ref, ps_ref,
                 q_ref, k_ref, vt_ref, sg_ref, po_ref, wbf, wpbf, qk_gain, ubuf, sbuf_a, sbuf_b,
                 *, ts, d_model, d_attn, d_pool):
    t = pl.program_id(1)
    hist = slice(POOL_PAD - POOL_HIST, POOL_PAD)
    gdim = d_pool // len(POOL_WINDOWS)
    pool_pack = MXU_WIDTH // gdim

    @pl.when(_first_grid_step())
    def _():
        wbf[...] = win_ref[...].astype(BF16)
        n_heads = d_attn // HEAD_DIM
        qk_gain[0:1, :] = jnp.concatenate([qg_ref[...] * (HEAD_DIM ** -0.5 * LOG2E)] * n_heads, axis=1)
        qk_gain[1:2, :] = jnp.concatenate([kg_ref[...]] * n_heads, axis=1)
        wpbf[...] = jnp.zeros_like(wpbf)
        for g in range(len(POOL_WINDOWS)):
            p, i = divmod(g, pool_pack)
            wpbf[p, i * gdim:(i + 1) * gdim, i * gdim:(i + 1) * gdim] = wp_ref[g].astype(BF16)

    @pl.when(t == 0)
    def _():
        ubuf[0:POOL_PAD, :] = jnp.zeros((POOL_PAD, d_pool), F32)

    @pl.when(t > 0)
    def _():
        ubuf[hist, :] = ubuf[ts + POOL_PAD - POOL_HIST:ts + POOL_PAD, :]

    x = x_ref[0]
    ms = jnp.mean(x * x, axis=-1, keepdims=True)
    shift = mod_ref[0, :, 0:d_model]
    scale = mod_ref[0, :, d_model:2 * d_model]
    a = ng_ref[...] * (1.0 + scale)
    hn = ((x * lax.rsqrt(ms + EPS)) * a + shift).astype(BF16)

    def proj(col, width):
        return jnp.dot(hn, wbf[:, col:col + width], preferred_element_type=F32)

    u = proj(4 * d_attn, d_pool)
    g_pool = proj(4 * d_attn + d_pool, d_pool)
    ubuf[POOL_PAD:POOL_PAD + ts, :] = u
    n_lvl = len(POOL_WINDOWS)
    end = POOL_PAD + ts
    src = ubuf
    wsums = []
    for lvl in range(n_lvl):
        shift_rows = 2 ** lvl
        start = SUBLANES * (lvl + 1)
        lanes = slice(lvl * gdim, d_pool)
        summed = src[start:end, lanes] + src[start - shift_rows:end - shift_rows, lanes]
        wsums.append(summed[POOL_PAD - start:, 0:gdim])
        if lvl + 1 < n_lvl:
            dst = sbuf_a if lvl % 2 == 0 else sbuf_b
            dst[start:end, lanes] = summed
            src = dst

    row = lax.broadcasted_iota(jnp.int32, (POOL_HIST, gdim), 0)
    pooled = []
    for g, win in enumerate(POOL_WINDOWS):
        ug = u[:, g * gdim:(g + 1) * gdim]
        cnt = jnp.minimum(t * ts + row + 1, win).astype(F32)
        head = wsums[g][0:POOL_HIST] / cnt - ug[0:POOL_HIST]
        rest = wsums[g][POOL_HIST:] * (1.0 / win) - ug[POOL_HIST:]
        pooled.append(jnp.concatenate([head, rest], axis=0).astype(BF16))
    for p in range(len(POOL_WINDOWS) // pool_pack):
        lo, hi = p * MXU_WIDTH, (p + 1) * MXU_WIDTH
        mixed = jnp.dot(jnp.concatenate(pooled[p * pool_pack:(p + 1) * pool_pack], axis=1), wpbf[p],
                        preferred_element_type=F32) + bp_ref[:, lo:hi]
        po_ref[0, :, lo:hi] = (mixed * ps_ref[:, lo:hi] * _silu(g_pool[:, lo:hi])).astype(BF16)

    r = lax.broadcasted_iota(jnp.int32, (MXU_WIDTH, MXU_WIDTH), 0) // HEAD_DIM
    c = lax.broadcasted_iota(jnp.int32, (MXU_WIDTH, MXU_WIDTH), 1) // HEAD_DIM
    head_avg = jnp.where(r == c, 1.0 / HEAD_DIM, 0.0).astype(BF16)

    def head_norm(v, g):
        sq = (v * v).astype(BF16)
        msq = jnp.concatenate([jnp.dot(sq[:, lo:lo + MXU_WIDTH], head_avg, preferred_element_type=F32)
                               for lo in range(0, d_attn, MXU_WIDTH)], axis=1)
        return (v * lax.rsqrt(msq + EPS)) * g

    q_ref[0] = head_norm(proj(0, d_attn), qk_gain[0:1, :]).astype(BF16)
    k_ref[0] = head_norm(proj(d_attn, d_attn), qk_gain[1:2, :]).astype(BF16)

    vt = proj(2 * d_attn, d_attn).T.astype(BF16)
    for jj in range(ts // ATTN_BLOCK):
        vt_ref[0, jj] = vt[:, jj * ATTN_BLOCK:(jj + 1) * ATTN_BLOCK]

    sg_ref[0] = _silu(proj(3 * d_attn, d_attn)).astype(BF16)


def _proj_call(mod3, x, norm_g, w_in, qg, kg, w_pool, b_pool, pool_scale, *, ts=1024):
    bsz, s, d = x.shape
    d_pool = pool_scale.shape[1]
    d_attn = (w_in.shape[1] - 2 * d_pool) // 4
    assert qg.shape == kg.shape == (1, HEAD_DIM)
    nt = s // ts
    nb = s // ATTN_BLOCK
    kern = functools.partial(_proj_kernel, ts=ts, d_model=d, d_attn=d_attn, d_pool=d_pool)
    const = lambda *shape: pl.BlockSpec(shape, lambda b, t: (0,) * len(shape))
    once = lambda *shape: pl.BlockSpec(shape, lambda b, t: (0,) * len(shape), pipeline_mode=pl.Buffered(1))
    seq_out = lambda width: pl.BlockSpec((1, ts, width), lambda b, t: (b, t, 0))
    pool_rows = POOL_PAD + ts
    return pl.pallas_call(
        kern,
        grid=(bsz, nt),
        in_specs=[
            pl.BlockSpec((1, 1, mod3.shape[2]), lambda b, t: (b, 0, 0)),
            pl.BlockSpec((1, ts, d), lambda b, t: (b, t, 0)),
            const(1, d),
            once(*w_in.shape),
            const(1, HEAD_DIM),
            const(1, HEAD_DIM),
            once(*w_pool.shape),
            const(1, d_pool),
            const(1, d_pool),
        ],
        out_specs=[
            seq_out(d_attn),
            seq_out(d_attn),
            pl.BlockSpec((1, ts // ATTN_BLOCK, d_attn, ATTN_BLOCK), lambda b, t: (b, t, 0, 0)),
            seq_out(d_attn),
            seq_out(d_pool),
        ],
        out_shape=[
            jax.ShapeDtypeStruct((bsz, s, d_attn), BF16),
            jax.ShapeDtypeStruct((bsz, s, d_attn), BF16),
            jax.ShapeDtypeStruct((bsz, nb, d_attn, ATTN_BLOCK), BF16),
            jax.ShapeDtypeStruct((bsz, s, d_attn), BF16),
            jax.ShapeDtypeStruct((bsz, s, d_pool), BF16),
        ],
        scratch_shapes=[pltpu.VMEM(w_in.shape, BF16),
                        pltpu.VMEM((d_pool // MXU_WIDTH, MXU_WIDTH, MXU_WIDTH), BF16),
                        pltpu.VMEM((2, d_attn), F32),
                        pltpu.VMEM((pool_rows, d_pool), F32),
                        pltpu.VMEM((pool_rows, d_pool), F32),
                        pltpu.VMEM((pool_rows, d_pool), F32)],
        compiler_params=pltpu.CompilerParams(
            dimension_semantics=("arbitrary", "arbitrary"), vmem_limit_bytes=VMEM_LIMIT_BYTES),
        name="norm_inproj_pool",
    )(mod3, x, norm_g, w_in, qg, kg, w_pool, b_pool, pool_scale)


def _attn_out_kernel(q_ref, k_ref, vt_ref, sg_ref, p_ref, x_hbm, mod_ref, w_hbm, out_hbm,
                     lr_ref, acc_ref, o_buf, wbf, xbuf, ybuf, x_sem, y_sem, w_sem,
                     *, nblk, g_q, n_hp, out_rows):
    tb = ATTN_BLOCK
    b = pl.program_id(0)
    d_attn = n_hp * LANES
    d_model = xbuf.shape[1]
    n_out = nblk * tb // out_rows

    def x_copy(c):
        rows = pl.ds(c * out_rows, out_rows)
        return pltpu.make_async_copy(x_hbm.at[b, rows, :], xbuf.at[rows, :], x_sem.at[c])

    def out_copy(bb, c):
        rows = pl.ds(c * out_rows, out_rows)
        return pltpu.make_async_copy(ybuf.at[rows, :], out_hbm.at[bb, rows, :], y_sem.at[c])

    def w_copy():
        return pltpu.make_async_copy(w_hbm, ybuf.at[0:wbf.shape[0], :], w_sem.at[0])

    @pl.when(b == 0)
    def _():
        w_copy().start()

    for c in range(n_out):
        x_copy(c).start()

    row = lax.broadcasted_iota(jnp.int32, (tb, tb), 0)
    col = lax.broadcasted_iota(jnp.int32, (tb, tb), 1)
    neg_suffix = jnp.concatenate([jnp.where(col > row, -1.0, 0.0), jnp.where(col == row, -1.0, 0.0)],
                                 axis=1).astype(BF16)
    causal = row < col
    n_grp = 2 * n_hp
    grp_w = g_q * tb

    def mask_diag(v, diag):
        if not diag:
            return v
        nq = v.shape[1] // (n_grp * tb)
        keep = causal[0:v.shape[0]]
        parts = []
        for grp in range(n_grp):
            base = grp * nq * tb
            parts.append(jnp.where(keep, v[:, base:base + tb], 0.0))
            if nq > 1:
                parts.append(v[:, base + tb:base + nq * tb])
        return jnp.concatenate(parts, axis=1)

    def blocks_alive(rem, q_lo, nq):
        n_alive = jnp.int32(0)
        for g in range(nq):
            blk = jnp.concatenate([rem[:, (grp * nq + g) * tb:(grp * nq + g + 1) * tb] for grp in range(n_grp)],
                                  axis=1)
            n_alive = jnp.where(jnp.max(blk) > REM_FLOOR_LOG2, q_lo + g + 1, n_alive)
        return n_alive

    def score(j, qts, q_lo, q_hi, diag):
        nq = q_hi - q_lo
        rows_j = pl.ds(pl.multiple_of(j * tb, tb), tb)
        zs = []
        for hp in range(n_hp):
            qt = qts[hp]
            if nq < g_q:
                qt = jnp.concatenate([qt[:, h * grp_w + q_lo * tb:h * grp_w + q_hi * tb] for h in range(2)],
                                     axis=1)
            zs.append(jnp.dot(k_ref[0, rows_j, hp * LANES:(hp + 1) * LANES], qt, preferred_element_type=F32))
        z = jnp.concatenate(zs, axis=1)
        sp = jnp.maximum(jnp.log(1.0 + jnp.exp2(jnp.minimum(z, SOFTPLUS_CLAMP))) * LOG2E, z)
        neg_log_beta = mask_diag(sp - z, diag)
        sp = mask_diag(sp, diag)
        return sp[0:1, :], jnp.concatenate([sp.astype(BF16), neg_log_beta.astype(BF16)], axis=0)

    def weigh(j, sp_0, terms, q_lo, q_hi, diag):
        nq = q_hi - q_lo
        lanes = [slice(grp * grp_w + q_lo * tb, grp * grp_w + q_hi * tb) for grp in range(n_grp)]
        log_w = jnp.dot(neg_suffix, terms, preferred_element_type=F32)
        log_rem = jnp.concatenate([lr_ref[:, ln] for ln in lanes], axis=1)
        w = mask_diag(jnp.exp2(log_w + log_rem), diag).astype(BF16)
        later_0 = -(log_w[0:1, :] + terms[tb:tb + 1, :].astype(F32))
        new_rem = log_rem - (sp_0 + later_0)
        for grp in range(n_grp):
            hp, h = divmod(grp, 2)
            cols = slice(grp * nq * tb, (grp + 1) * nq * tb)
            lr_ref[:, lanes[grp]] = new_rem[:, cols]
            vt_h = vt_ref[0, j, hp * LANES + h * HEAD_DIM:hp * LANES + (h + 1) * HEAD_DIM, :]
            res = jnp.dot(vt_h, w[:, cols], preferred_element_type=F32)
            acc_ref[h * HEAD_DIM:(h + 1) * HEAD_DIM, hp * grp_w + q_lo * tb:hp * grp_w + q_hi * tb] += res
        return blocks_alive(new_rem, q_lo, nq)

    def step(j, qts, q_lo, q_hi, diag):
        return weigh(j, *score(j, qts, q_lo, q_hi, diag), q_lo, q_hi, diag)

    def emit(sb, g):
        rows = pl.ds(pl.multiple_of((sb * g_q + g) * tb, tb), tb)
        for hp in range(n_hp):
            a = hp * grp_w + g * tb
            gate = sg_ref[0, rows, hp * LANES:(hp + 1) * LANES].astype(F32)
            o_buf[rows, hp * LANES:(hp + 1) * LANES] = (acc_ref[:, a:a + tb].T * gate).astype(BF16)

    def q_super_block(sb, carry):
        qts = []
        head_of_row = lax.broadcasted_iota(jnp.int32, (LANES, grp_w), 0) // HEAD_DIM
        for hp in range(n_hp):
            rows = pl.ds(pl.multiple_of(sb * grp_w, grp_w), grp_w)
            qt = q_ref[0, rows, hp * LANES:(hp + 1) * LANES].astype(F32).T
            qts.append(jnp.concatenate([jnp.where(head_of_row == h, qt, 0.0) for h in range(2)],
                                       axis=1).astype(BF16))
        lr_ref[...] = jnp.zeros_like(lr_ref)
        acc_ref[...] = jnp.zeros_like(acc_ref)
        base = sb * g_q

        def band(offsets):
            spans = [(m, max(m, 0), min(m + BAND, g_q), m >= 0) for m in offsets]
            scored = {}
            for i in range(len(spans) + DIAG_LOOKAHEAD):
                if i < len(spans):
                    m, lo, hi, diag = spans[i]
                    scored[m] = score(base + m, qts, lo, hi, diag)
                if i >= DIAG_LOOKAHEAD:
                    m, lo, hi, diag = spans[i - DIAG_LOOKAHEAD]
                    weigh(base + m, *scored.pop(m), lo, hi, diag)
                    if 0 <= m + BAND - 1 < g_q:
                        emit(sb, m + BAND - 1)

        first_blocks = range(min(BAND - 1, g_q))
        band(range(g_q - 1, -1, -1))
        for g in first_blocks:
            emit(sb, g)

        @pl.when(sb > 0)
        def _():
            band(range(-1, -BAND, -1))
            for g in first_blocks:
                emit(sb, g)

        def more(state):
            j, alive = state
            return (j >= 0) & (alive > 0)

        alive = [blocks_alive(jnp.concatenate([lr_ref[:, grp * grp_w + g * tb:grp * grp_w + (g + 1) * tb]
                                               for grp in range(n_grp)], axis=1), g, 1) for g in range(g_q)]
        def more_pairs(state):
            j, alive = state
            return (j >= 1) & (alive > 0)

        for g in range(g_q):
            def two_block_step(state, g=g):
                j, _ = state
                scored = [score(j - i, qts, g, g + 1, False) for i in range(2)]
                weigh(j, *scored[0], g, g + 1, False)
                return j - 2, weigh(j - 1, *scored[1], g, g + 1, False)

            start = (base + g - BAND, alive[g])

            @pl.when(more(start))
            def _(g=g, start=start, two_block_step=two_block_step):
                rest = lax.while_loop(more_pairs, two_block_step, start)

                @pl.when(more(rest))
                def _():
                    step(rest[0], qts, g, g + 1, False)

                emit(sb, g)
        return carry

    lax.fori_loop(0, nblk // g_q, q_super_block, 0)

    @pl.when(b > 0)
    def _():
        for c in range(n_out):
            out_copy(b - 1, c).wait()

    @pl.when(b == 0)
    def _():
        w_copy().wait()
        wbf[...] = ybuf[0:wbf.shape[0], :].astype(BF16)

    gate = mod_ref[0, :, 2 * d_model:3 * d_model]
    for c in range(n_out):
        rows = slice(c * out_rows, (c + 1) * out_rows)
        x_copy(c).wait()
        y = jnp.dot(o_buf[rows, :], wbf[0:d_attn, :], preferred_element_type=F32)
        y = y + jnp.dot(p_ref[0, rows, :], wbf[d_attn:, :], preferred_element_type=F32)
        ybuf[rows, :] = xbuf[rows, :] + gate * y
        out_copy(b, c).start()

    @pl.when(b == pl.num_programs(0) - 1)
    def _():
        for c in range(n_out):
            out_copy(b, c).wait()


def _attn_out_call(qn, kn, vt, sg, pool_out, x, mod3, w_out, *, g_q=16, out_rows=1024):
    bsz, s, d = x.shape
    d_attn = qn.shape[2]
    nblk = s // ATTN_BLOCK
    assert nblk % g_q == 0 and s % out_rows == 0 and d_attn % LANES == 0
    assert w_out.shape == (d_attn + pool_out.shape[2], d) and w_out.shape[0] <= s
    n_hp = d_attn // LANES
    score_lanes = g_q * n_hp * 2 * ATTN_BLOCK
    seq = pl.BlockSpec((1, s, d_attn), lambda b: (b, 0, 0))
    return pl.pallas_call(
        functools.partial(_attn_out_kernel, nblk=nblk, g_q=g_q, n_hp=n_hp, out_rows=out_rows),
        grid=(bsz,),
        in_specs=[seq, seq, pl.BlockSpec((1, nblk, d_attn, ATTN_BLOCK), lambda b: (b, 0, 0, 0)), seq,
                  pl.BlockSpec((1, s, pool_out.shape[2]), lambda b: (b, 0, 0)),
                  pl.BlockSpec(memory_space=pl.ANY),
                  pl.BlockSpec((1, 1, mod3.shape[2]), lambda b: (b, 0, 0)),
                  pl.BlockSpec(memory_space=pl.ANY)],
        out_specs=pl.BlockSpec(memory_space=pl.ANY),
        out_shape=jax.ShapeDtypeStruct((bsz, s, d), F32),
        scratch_shapes=[pltpu.VMEM((1, score_lanes), F32),
                        pltpu.VMEM((LANES, g_q * n_hp * ATTN_BLOCK), F32),
                        pltpu.VMEM((s, d_attn), BF16),
                        pltpu.VMEM(w_out.shape, BF16),
                        pltpu.VMEM((s, d), F32),
                        pltpu.VMEM((s, d), F32),
                        pltpu.SemaphoreType.DMA((s // out_rows,)),
                        pltpu.SemaphoreType.DMA((s // out_rows,)),
                        pltpu.SemaphoreType.DMA((1,))],
        compiler_params=pltpu.CompilerParams(
            dimension_semantics=("arbitrary",), vmem_limit_bytes=VMEM_LIMIT_BYTES),
        name="stickbreak_attn_outproj",
    )(qn, kn, vt, sg, pool_out, x, mod3, w_out)


def kernel(x, c, w_ada, b_ada, norm_g, w_in, q_norm_g, k_norm_g, w_pool, b_pool, pool_scale, w_out):
    depth = w_ada.shape[0]
    h = x
    for l in range(depth):
        mod3 = _ada_call(c, w_ada[l], b_ada[l])
        qn, kn, vt, sg, pool_out = _proj_call(
            mod3, h, norm_g[l].reshape(1, -1), w_in[l], q_norm_g[l].reshape(1, -1), k_norm_g[l].reshape(1, -1),
            w_pool[l], b_pool[l].reshape(1, -1), pool_scale[l].reshape(1, -1))
        h = _attn_out_call(qn, kn, vt, sg, pool_out, h, mod3, w_out[l])
    return h
```

```python
import functools

import jax
import jax.numpy as jnp
from jax import lax
from jax.experimental import pallas as pl
from jax.experimental.pallas import tpu as pltpu

F32 = jnp.float32
BF16 = jnp.bfloat16

HEAD_DIM = 64
POOL_WINDOWS = (2, 4, 8, 16)
EPS = 1e-6
LOG2E = 1.4426950408889634
SOFTPLUS_CLAMP = 64.0
REM_FLOOR_LOG2 = -136.0
DIAG_LOOKAHEAD = 1
BAND = 3

LANES = 128
SUBLANES = 8
MXU_WIDTH = 256
ATTN_BLOCK = 128
POOL_HIST = max(POOL_WINDOWS)
POOL_PAD = 2 * POOL_HIST
VMEM_LIMIT_BYTES = 58 * 1024 * 1024


def _silu(v):
    h = 0.5 * v
    return h + h * jnp.tanh(h)


def _first_grid_step():
    return (pl.program_id(0) == 0) & (pl.program_id(1) == 0)


def _ada_kernel(c_ref, w_ref, b_ref, o_ref):
    c = c_ref[...]
    ca = _silu(c).astype(BF16)
    mod = jnp.dot(ca, w_ref[...].astype(BF16), preferred_element_type=F32) + b_ref[...]
    for i in range(mod.shape[0]):
        o_ref[i] = mod[i:i + 1, :]


def _ada_call(c, w_ada, b_ada, *, tn=1024):
    bsz, d = c.shape
    n = w_ada.shape[1]
    return pl.pallas_call(
        _ada_kernel,
        grid=(n // tn,),
        in_specs=[
            pl.BlockSpec((bsz, d), lambda j: (0, 0)),
            pl.BlockSpec((d, tn), lambda j: (0, j)),
            pl.BlockSpec((1, tn), lambda j: (0, j)),
        ],
        out_specs=pl.BlockSpec((bsz, 1, tn), lambda j: (0, 0, j)),
        out_shape=jax.ShapeDtypeStruct((bsz, 1, n), F32),
        compiler_params=pltpu.CompilerParams(
            dimension_semantics=("arbitrary",), vmem_limit_bytes=VMEM_LIMIT_BYTES),
        name="adaln_mod",
    )(c, w_ada, b_ada.reshape(1, n))


def _proj_kernel(mod_ref, x_ref, ng_ref, win_ref, qg_ref, kg_ref, wp_ref, bp_ref, ps_ref,
                 q_ref, k_ref, vt_ref, sg_ref, po_ref, wbf, wpbf, qk_gain, ubuf, sbuf_a, sbuf_b,
                 *, ts, d_model, d_attn, d_pool):
    t = pl.program_id(1)
    hist = slice(POOL_PAD - POOL_HIST, POOL_PAD)
    gdim = d_pool // len(POOL_WINDOWS)
    pool_pack = MXU_WIDTH // gdim

    @pl.when(_first_grid_step())
    def _():
        wbf[...] = win_ref[...].astype(BF16)
        n_heads = d_attn // HEAD_DIM
        qk_gain[0:1, :] = jnp.concatenate([qg_ref[...] * (HEAD_DIM ** -0.5 * LOG2E)] * n_heads, axis=1)
        qk_gain[1:2, :] = jnp.concatenate([kg_ref[...]] * n_heads, axis=1)
        wpbf[...] = jnp.zeros_like(wpbf)
        for g in range(len(POOL_WINDOWS)):
            p, i = divmod(g, pool_pack)
            wpbf[p, i * gdim:(i + 1) * gdim, i * gdim:(i + 1) * gdim] = wp_ref[g].astype(BF16)

    @pl.when(t == 0)
    def _():
        ubuf[0:POOL_PAD, :] = jnp.zeros((POOL_PAD, d_pool), F32)

    @pl.when(t > 0)
    def _():
        ubuf[hist, :] = ubuf[ts + POOL_PAD - POOL_HIST:ts + POOL_PAD, :]

    x = x_ref[0]
    ms = jnp.mean(x * x, axis=-1, keepdims=True)
    shift = mod_ref[0, :, 0:d_model]
    scale = mod_ref[0, :, d_model:2 * d_model]
    a = ng_ref[...] * (1.0 + scale)
    hn = ((x * lax.rsqrt(ms + EPS)) * a + shift).astype(BF16)

    def proj(col, width):
        return jnp.dot(hn, wbf[:, col:col + width], preferred_element_type=F32)

    u = proj(4 * d_attn, d_pool)
    g_pool = proj(4 * d_attn + d_pool, d_pool)
    ubuf[POOL_PAD:POOL_PAD + ts, :] = u
    n_lvl = len(POOL_WINDOWS)
    end = POOL_PAD + ts
    src = ubuf
    wsums = []
    for lvl in range(n_lvl):
        shift_rows = 2 ** lvl
        start = SUBLANES * (lvl + 1)
        lanes = slice(lvl * gdim, d_pool)
        summed = src[start:end, lanes] + src[start - shift_rows:end - shift_rows, lanes]
        wsums.append(summed[POOL_PAD - start:, 0:gdim])
        if lvl + 1 < n_lvl:
            dst = sbuf_a if lvl % 2 == 0 else sbuf_b
            dst[start:end, lanes] = summed
            src = dst

    row = lax.broadcasted_iota(jnp.int32, (POOL_HIST, gdim), 0)
    pooled = []
    for g, win in enumerate(POOL_WINDOWS):
        ug = u[:, g * gdim:(g + 1) * gdim]
        cnt = jnp.minimum(t * ts + row + 1, win).astype(F32)
        head = wsums[g][0:POOL_HIST] / cnt - ug[0:POOL_HIST]
        rest = wsums[g][POOL_HIST:] * (1.0 / win) - ug[POOL_HIST:]
        pooled.append(jnp.concatenate([head, rest], axis=0).astype(BF16))
    for p in range(len(POOL_WINDOWS) // pool_pack):
        lo, hi = p * MXU_WIDTH, (p + 1) * MXU_WIDTH
        mixed = jnp.dot(jnp.concatenate(pooled[p * pool_pack:(p + 1) * pool_pack], axis=1), wpbf[p],
                        preferred_element_type=F32) + bp_ref[:, lo:hi]
        po_ref[0, :, lo:hi] = (mixed * ps_ref[:, lo:hi] * _silu(g_pool[:, lo:hi])).astype(BF16)

    r = lax.broadcasted_iota(jnp.int32, (MXU_WIDTH, MXU_WIDTH), 0) // HEAD_DIM
    c = lax.broadcasted_iota(jnp.int32, (MXU_WIDTH, MXU_WIDTH), 1) // HEAD_DIM
    head_avg = jnp.where(r == c, 1.0 / HEAD_DIM, 0.0).astype(BF16)

    def head_norm(v, g):
        sq = (v * v).astype(BF16)
        msq = jnp.concatenate([jnp.dot(sq[:, lo:lo + MXU_WIDTH], head_avg, preferred_element_type=F32)
                               for lo in range(0, d_attn, MXU_WIDTH)], axis=1)
        return (v * lax.rsqrt(msq + EPS)) * g

    q_ref[0] = head_norm(proj(0, d_attn), qk_gain[0:1, :]).astype(BF16)
    k_ref[0] = head_norm(proj(d_attn, d_attn), qk_gain[1:2, :]).astype(BF16)

    vt = proj(2 * d_attn, d_attn).T.astype(BF16)
    for jj in range(ts // ATTN_BLOCK):
        vt_ref[0, jj] = vt[:, jj * ATTN_BLOCK:(jj + 1) * ATTN_BLOCK]

    sg_ref[0] = _silu(proj(3 * d_attn, d_attn)).astype(BF16)


def _proj_call(mod3, x, norm_g, w_in, qg, kg, w_pool, b_pool, pool_scale, *, ts=1024):
    bsz, s, d = x.shape
    d_pool = pool_scale.shape[1]
    d_attn = (w_in.shape[1] - 2 * d_pool) // 4
    assert qg.shape == kg.shape == (1, HEAD_DIM)
    nt = s // ts
    nb = s // ATTN_BLOCK
    kern = functools.partial(_proj_kernel, ts=ts, d_model=d, d_attn=d_attn, d_pool=d_pool)
    const = lambda *shape: pl.BlockSpec(shape, lambda b, t: (0,) * len(shape))
    once = lambda *shape: pl.BlockSpec(shape, lambda b, t: (0,) * len(shape), pipeline_mode=pl.Buffered(1))
    seq_out = lambda width: pl.BlockSpec((1, ts, width), lambda b, t: (b, t, 0))
    pool_rows = POOL_PAD + ts
    return pl.pallas_call(
        kern,
        grid=(bsz, nt),
        in_specs=[
            pl.BlockSpec((1, 1, mod3.shape[2]), lambda b, t: (b, 0, 0)),
            pl.BlockSpec((1, ts, d), lambda b, t: (b, t, 0)),
            const(1, d),
            once(*w_in.shape),
            const(1, HEAD_DIM),
            const(1, HEAD_DIM),
            once(*w_pool.shape),
            const(1, d_pool),
            const(1, d_pool),
        ],
        out_specs=[
            seq_out(d_attn),
            seq_out(d_attn),
            pl.BlockSpec((1, ts // ATTN_BLOCK, d_attn, ATTN_BLOCK), lambda b, t: (b, t, 0, 0)),
            seq_out(d_attn),
            seq_out(d_pool),
        ],
        out_shape=[
            jax.ShapeDtypeStruct((bsz, s, d_attn), BF16),
            jax.ShapeDtypeStruct((bsz, s, d_attn), BF16),
            jax.ShapeDtypeStruct((bsz, nb, d_attn, ATTN_BLOCK), BF16),
            jax.ShapeDtypeStruct((bsz, s, d_attn), BF16),
            jax.ShapeDtypeStruct((bsz, s, d_pool), BF16),
        ],
        scratch_shapes=[pltpu.VMEM(w_in.shape, BF16),
                        pltpu.VMEM((d_pool // MXU_WIDTH, MXU_WIDTH, MXU_WIDTH), BF16),
                        pltpu.VMEM((2, d_attn), F32),
                        pltpu.VMEM((pool_rows, d_pool), F32),
                        pltpu.VMEM((pool_rows, d_pool), F32),
                        pltpu.VMEM((pool_rows, d_pool), F32)],
        compiler_params=pltpu.CompilerParams(
            dimension_semantics=("arbitrary", "arbitrary"), vmem_limit_bytes=VMEM_LIMIT_BYTES),
        name="norm_inproj_pool",
    )(mod3, x, norm_g, w_in, qg, kg, w_pool, b_pool, pool_scale)


def _attn_out_kernel(q_ref, k_ref, vt_ref, sg_ref, p_ref, x_hbm, mod_ref, w_hbm, out_hbm,
                     lr_ref, acc_ref, qt_ref, o_buf, wbf, xbuf, ybuf, x_sem, y_sem, w_sem,
                     *, nblk, g_q, n_hp, out_rows):
    tb = ATTN_BLOCK
    b = pl.program_id(0)
    d_attn = n_hp * LANES
    d_model = xbuf.shape[1]
    n_out = nblk * tb // out_rows

    def x_copy(c):
        rows = pl.ds(c * out_rows, out_rows)
        return pltpu.make_async_copy(x_hbm.at[b, rows, :], xbuf.at[rows, :], x_sem.at[c])

    def out_copy(bb, c):
        rows = pl.ds(c * out_rows, out_rows)
        return pltpu.make_async_copy(ybuf.at[rows, :], out_hbm.at[bb, rows, :], y_sem.at[c])

    def w_copy():
        return pltpu.make_async_copy(w_hbm, ybuf.at[0:wbf.shape[0], :], w_sem.at[0])

    @pl.when(b == 0)
    def _():
        w_copy().start()

    for c in range(n_out):
        x_copy(c).start()

    row = lax.broadcasted_iota(jnp.int32, (tb, tb), 0)
    col = lax.broadcasted_iota(jnp.int32, (tb, tb), 1)
    neg_suffix = jnp.concatenate([jnp.where(col > row, -1.0, 0.0), jnp.where(col == row, -1.0, 0.0)],
                                 axis=1).astype(BF16)
    causal = row < col
    n_grp = 2 * n_hp
    grp_w = g_q * tb

    def span(start, width):
        if isinstance(start, int):
            return slice(start, start + width)
        return pl.ds(pl.multiple_of(start, tb), width)

    def mask_diag(v, diag):
        if not diag:
            return v
        nq = v.shape[1] // (n_grp * tb)
        keep = causal[0:v.shape[0]]
        parts = []
        for grp in range(n_grp):
            base = grp * nq * tb
            parts.append(jnp.where(keep, v[:, base:base + tb], 0.0))
            if nq > 1:
                parts.append(v[:, base + tb:base + nq * tb])
        return jnp.concatenate(parts, axis=1)

    def blocks_alive(rem, q_lo, nq):
        n_alive = jnp.int32(0)
        for g in range(nq):
            blk = jnp.concatenate([rem[:, (grp * nq + g) * tb:(grp * nq + g + 1) * tb] for grp in range(n_grp)],
                                  axis=1)
            n_alive = jnp.where(jnp.max(blk) > REM_FLOOR_LOG2, q_lo + g + 1, n_alive)
        return n_alive

    def score(j, q_lo, nq, diag):
        rows_j = pl.ds(pl.multiple_of(j * tb, tb), tb)
        zs = []
        for hp in range(n_hp):
            if nq == g_q:
                qt = qt_ref[hp]
            else:
                qt = jnp.concatenate([qt_ref[hp, :, span(h * grp_w + q_lo * tb, nq * tb)] for h in range(2)],
                                     axis=1)
            zs.append(jnp.dot(k_ref[0, rows_j, hp * LANES:(hp + 1) * LANES], qt, preferred_element_type=F32))
        z = jnp.concatenate(zs, axis=1)
        sp = jnp.maximum(jnp.log(1.0 + jnp.exp2(jnp.minimum(z, SOFTPLUS_CLAMP))) * LOG2E, z)
        neg_log_beta = mask_diag(sp - z, diag)
        sp = mask_diag(sp, diag)
        return sp[0:1, :], jnp.concatenate([sp.astype(BF16), neg_log_beta.astype(BF16)], axis=0)

    def weigh(j, sp_0, terms, q_lo, nq, diag):
        lanes = [span(grp * grp_w + q_lo * tb, nq * tb) for grp in range(n_grp)]
        log_w = jnp.dot(neg_suffix, terms, preferred_element_type=F32)
        log_rem = jnp.concatenate([lr_ref[:, ln] for ln in lanes], axis=1)
        w = mask_diag(jnp.exp2(log_w + log_rem), diag).astype(BF16)
        later_0 = -(log_w[0:1, :] + terms[tb:tb + 1, :].astype(F32))
        new_rem = log_rem - (sp_0 + later_0)
        for grp in range(n_grp):
            hp, h = divmod(grp, 2)
            cols = slice(grp * nq * tb, (grp + 1) * nq * tb)
            lr_ref[:, lanes[grp]] = new_rem[:, cols]
            vt_h = vt_ref[0, j, hp * LANES + h * HEAD_DIM:hp * LANES + (h + 1) * HEAD_DIM, :]
            res = jnp.dot(vt_h, w[:, cols], preferred_element_type=F32)
            acc_ref[h * HEAD_DIM:(h + 1) * HEAD_DIM, span(hp * grp_w + q_lo * tb, nq * tb)] += res
        return blocks_alive(new_rem, q_lo, nq)

    def step(j, q_lo, nq, diag):
        return weigh(j, *score(j, q_lo, nq, diag), q_lo, nq, diag)

    def emit(sb, g):
        rows = pl.ds(pl.multiple_of((sb * g_q + g) * tb, tb), tb)
        for hp in range(n_hp):
            gate = sg_ref[0, rows, hp * LANES:(hp + 1) * LANES].astype(F32)
            o_buf[rows, hp * LANES:(hp + 1) * LANES] = (acc_ref[:, span(hp * grp_w + g * tb, tb)].T * gate).astype(BF16)

    def q_super_block(sb, carry):
        head_of_row = lax.broadcasted_iota(jnp.int32, (LANES, grp_w), 0) // HEAD_DIM
        for hp in range(n_hp):
            rows = pl.ds(pl.multiple_of(sb * grp_w, grp_w), grp_w)
            qt = q_ref[0, rows, hp * LANES:(hp + 1) * LANES].astype(F32).T
            qt_ref[hp] = jnp.concatenate([jnp.where(head_of_row == h, qt, 0.0) for h in range(2)],
                                         axis=1).astype(BF16)
        lr_ref[...] = jnp.zeros_like(lr_ref)
        acc_ref[...] = jnp.zeros_like(acc_ref)
        base = sb * g_q

        def band(offsets):
            spans = [(m, max(m, 0), min(m + BAND, g_q), m >= 0) for m in offsets]
            scored = {}
            for i in range(len(spans) + DIAG_LOOKAHEAD):
                if i < len(spans):
                    m, lo, hi, diag = spans[i]
                    scored[m] = score(base + m, lo, hi - lo, diag)
                if i >= DIAG_LOOKAHEAD:
                    m, lo, hi, diag = spans[i - DIAG_LOOKAHEAD]
                    weigh(base + m, *scored.pop(m), lo, hi - lo, diag)
                    if 0 <= m + BAND - 1 < g_q:
                        emit(sb, m + BAND - 1)

        first_blocks = range(min(BAND - 1, g_q))
        band(range(g_q - 1, -1, -1))
        for g in first_blocks:
            emit(sb, g)

        @pl.when(sb > 0)
        def _():
            band(range(-1, -BAND, -1))
            for g in first_blocks:
                emit(sb, g)

        def more(state):
            j, alive = state
            return (j >= 0) & (alive > 0)

        live_bits = jnp.int32(0)
        for g in range(g_q):
            rem = jnp.concatenate([lr_ref[:, span(grp * grp_w + g * tb, tb)] for grp in range(n_grp)], axis=1)
            live_bits = live_bits | jnp.where(jnp.max(rem) > REM_FLOOR_LOG2, 1 << g, 0)

        def block_sweep(g, carry):
            start = (base + g - BAND, (live_bits >> g) & 1)

            @pl.when(more(start))
            def _():
                lax.while_loop(more, lambda state: (state[0] - 1, step(state[0], g, 1, False)), start)
                emit(sb, g)
            return carry

        lax.fori_loop(0, g_q, block_sweep, 0)
        return carry

    lax.fori_loop(0, nblk // g_q, q_super_block, 0)

    @pl.when(b > 0)
    def _():
        for c in range(n_out):
            out_copy(b - 1, c).wait()

    @pl.when(b == 0)
    def _():
        w_copy().wait()
        wbf[...] = ybuf[0:wbf.shape[0], :].astype(BF16)

    gate = mod_ref[0, :, 2 * d_model:3 * d_model]
    for c in range(n_out):
        rows = slice(c * out_rows, (c + 1) * out_rows)
        x_copy(c).wait()
        y = jnp.dot(o_buf[rows, :], wbf[0:d_attn, :], preferred_element_type=F32)
        y = y + jnp.dot(p_ref[0, rows, :], wbf[d_attn:, :], preferred_element_type=F32)
        ybuf[rows, :] = xbuf[rows, :] + gate * y
        out_copy(b, c).start()

    @pl.when(b == pl.num_programs(0) - 1)
    def _():
        for c in range(n_out):
            out_copy(b, c).wait()


def _attn_out_call(qn, kn, vt, sg, pool_out, x, mod3, w_out, *, g_q=16, out_rows=1024):
    bsz, s, d = x.shape
    d_attn = qn.shape[2]
    nblk = s // ATTN_BLOCK
    assert nblk % g_q == 0 and s % out_rows == 0 and d_attn % LANES == 0
    assert w_out.shape == (d_attn + pool_out.shape[2], d) and w_out.shape[0] <= s
    n_hp = d_attn // LANES
    score_lanes = g_q * n_hp * 2 * ATTN_BLOCK
    seq = pl.BlockSpec((1, s, d_attn), lambda b: (b, 0, 0))
    return pl.pallas_call(
        functools.partial(_attn_out_kernel, nblk=nblk, g_q=g_q, n_hp=n_hp, out_rows=out_rows),
        grid=(bsz,),
        in_specs=[seq, seq, pl.BlockSpec((1, nblk, d_attn, ATTN_BLOCK), lambda b: (b, 0, 0, 0)), seq,
                  pl.BlockSpec((1, s, pool_out.shape[2]), lambda b: (b, 0, 0)),
                  pl.BlockSpec(memory_space=pl.ANY),
                  pl.BlockSpec((1, 1, mod3.shape[2]), lambda b: (b, 0, 0)),
                  pl.BlockSpec(memory_space=pl.ANY)],
        out_specs=pl.BlockSpec(memory_space=pl.ANY),
        out_shape=jax.ShapeDtypeStruct((bsz, s, d), F32),
        scratch_shapes=[pltpu.VMEM((1, score_lanes), F32),
                        pltpu.VMEM((LANES, g_q * n_hp * ATTN_BLOCK), F32),
                        pltpu.VMEM((n_hp, LANES, 2 * g_q * ATTN_BLOCK), BF16),
                        pltpu.VMEM((s, d_attn), BF16),
                        pltpu.VMEM(w_out.shape, BF16),
                        pltpu.VMEM((s, d), F32),
                        pltpu.VMEM((s, d), F32),
                        pltpu.SemaphoreType.DMA((s // out_rows,)),
                        pltpu.SemaphoreType.DMA((s // out_rows,)),
                        pltpu.SemaphoreType.DMA((1,))],
        compiler_params=pltpu.CompilerParams(
            dimension_semantics=("arbitrary",), vmem_limit_bytes=VMEM_LIMIT_BYTES),
        name="stickbreak_attn_outproj",
    )(qn, kn, vt, sg, pool_out, x, mod3, w_out)


def kernel(x, c, w_ada, b_ada, norm_g, w_in, q_norm_g, k_norm_g, w_pool, b_pool, pool_scale, w_out):
    depth = w_ada.shape[0]
    h = x
    for l in range(depth):
        mod3 = _ada_call(c, w_ada[l], b_ada[l])
        qn, kn, vt, sg, pool_out = _proj_call(
            mod3, h, norm_g[l].reshape(1, -1), w_in[l], q_norm_g[l].reshape(1, -1), k_norm_g[l].reshape(1, -1),
            w_pool[l], b_pool[l].reshape(1, -1), pool_scale[l].reshape(1, -1))
        h = _attn_out_call(qn, kn, vt, sg, pool_out, h, mod3, w_out[l])
    return h
```

```python
import functools

import jax
import jax.numpy as jnp
from jax import lax
from jax.experimental import pallas as pl
from jax.experimental.pallas import tpu as pltpu

F32 = jnp.float32
BF16 = jnp.bfloat16

HEAD_DIM = 64
POOL_WINDOWS = (2, 4, 8, 16)
EPS = 1e-6
LOG2E = 1.4426950408889634
SOFTPLUS_CLAMP = 64.0
REM_FLOOR_LOG2 = -136.0
DIAG_LOOKAHEAD = 1
BAND = 3

LANES = 128
SUBLANES = 8
MXU_WIDTH = 256
ATTN_BLOCK = 128
POOL_HIST = max(POOL_WINDOWS)
POOL_PAD = 2 * POOL_HIST
VMEM_LIMIT_BYTES = 58 * 1024 * 1024


def _silu(v):
    h = 0.5 * v
    return h + h * jnp.tanh(h)


def _first_grid_step():
    return (pl.program_id(0) == 0) & (pl.program_id(1) == 0)


def _ada_kernel(c_ref, w_ref, b_ref, o_ref):
    c = c_ref[...]
    ca = _silu(c).astype(BF16)
    mod = jnp.dot(ca, w_ref[...].astype(BF16), preferred_element_type=F32) + b_ref[...]
    for i in range(mod.shape[0]):
        o_ref[i] = mod[i:i + 1, :]


def _ada_call(c, w_ada, b_ada, *, tn=1024):
    bsz, d = c.shape
    n = w_ada.shape[1]
    return pl.pallas_call(
        _ada_kernel,
        grid=(n // tn,),
        in_specs=[
            pl.BlockSpec((bsz, d), lambda j: (0, 0)),
            pl.BlockSpec((d, tn), lambda j: (0, j)),
            pl.BlockSpec((1, tn), lambda j: (0, j)),
        ],
        out_specs=pl.BlockSpec((bsz, 1, tn), lambda j: (0, 0, j)),
        out_shape=jax.ShapeDtypeStruct((bsz, 1, n), F32),
        compiler_params=pltpu.CompilerParams(
            dimension_semantics=("arbitrary",), vmem_limit_bytes=VMEM_LIMIT_BYTES),
        name="adaln_mod",
    )(c, w_ada, b_ada.reshape(1, n))


def _proj_kernel(mod_ref, x_ref, ng_ref, win_ref, qg_ref, kg_ref, wp_ref, bp_ref, ps_ref,
                 q_ref, k_ref, vt_ref, sg_ref, po_ref, wbf, wpbf, qk_gain, ubuf, sbuf_a, sbuf_b,
                 *, ts, d_model, d_attn, d_pool):
    t = pl.program_id(1)
    hist = slice(POOL_PAD - POOL_HIST, POOL_PAD)
    gdim = d_pool // len(POOL_WINDOWS)
    pool_pack = MXU_WIDTH // gdim

    @pl.when(_first_grid_step())
    def _():
        wbf[...] = win_ref[...].astype(BF16)
        n_heads = d_attn // HEAD_DIM
        qk_gain[0:1, :] = jnp.concatenate([qg_ref[...] * (HEAD_DIM ** -0.5 * LOG2E)] * n_heads, axis=1)
        qk_gain[1:2, :] = jnp.concatenate([kg_ref[...]] * n_heads, axis=1)
        wpbf[...] = jnp.zeros_like(wpbf)
        for g in range(len(POOL_WINDOWS)):
            p, i = divmod(g, pool_pack)
            wpbf[p, i * gdim:(i + 1) * gdim, i * gdim:(i + 1) * gdim] = wp_ref[g].astype(BF16)

    @pl.when(t == 0)
    def _():
        ubuf[0:POOL_PAD, :] = jnp.zeros((POOL_PAD, d_pool), F32)

    @pl.when(t > 0)
    def _():
        ubuf[hist, :] = ubuf[ts + POOL_PAD - POOL_HIST:ts + POOL_PAD, :]

    x = x_ref[0]
    ms = jnp.mean(x * x, axis=-1, keepdims=True)
    shift = mod_ref[0, :, 0:d_model]
    scale = mod_ref[0, :, d_model:2 * d_model]
    a = ng_ref[...] * (1.0 + scale)
    hn = ((x * lax.rsqrt(ms + EPS)) * a + shift).astype(BF16)

    def proj(col, width):
        return jnp.dot(hn, wbf[:, col:col + width], preferred_element_type=F32)

    u = proj(4 * d_attn, d_pool)
    g_pool = proj(4 * d_attn + d_pool, d_pool)
    ubuf[POOL_PAD:POOL_PAD + ts, :] = u
    n_lvl = len(POOL_WINDOWS)
    end = POOL_PAD + ts
    src = ubuf
    wsums = []
    for lvl in range(n_lvl):
        shift_rows = 2 ** lvl
        start = SUBLANES * (lvl + 1)
        lanes = slice(lvl * gdim, d_pool)
        summed = src[start:end, lanes] + src[start - shift_rows:end - shift_rows, lanes]
        wsums.append(summed[POOL_PAD - start:, 0:gdim])
        if lvl + 1 < n_lvl:
            dst = sbuf_a if lvl % 2 == 0 else sbuf_b
            dst[start:end, lanes] = summed
            src = dst

    row = lax.broadcasted_iota(jnp.int32, (POOL_HIST, gdim), 0)
    pooled = []
    for g, win in enumerate(POOL_WINDOWS):
        ug = u[:, g * gdim:(g + 1) * gdim]
        cnt = jnp.minimum(t * ts + row + 1, win).astype(F32)
        head = wsums[g][0:POOL_HIST] / cnt - ug[0:POOL_HIST]
        rest = wsums[g][POOL_HIST:] * (1.0 / win) - ug[POOL_HIST:]
        pooled.append(jnp.concatenate([head, rest], axis=0).astype(BF16))
    for p in range(len(POOL_WINDOWS) // pool_pack):
        lo, hi = p * MXU_WIDTH, (p + 1) * MXU_WIDTH
        mixed = jnp.dot(jnp.concatenate(pooled[p * pool_pack:(p + 1) * pool_pack], axis=1), wpbf[p],
                        preferred_element_type=F32) + bp_ref[:, lo:hi]
        po_ref[0, :, lo:hi] = (mixed * ps_ref[:, lo:hi] * _silu(g_pool[:, lo:hi])).astype(BF16)

    r = lax.broadcasted_iota(jnp.int32, (MXU_WIDTH, MXU_WIDTH), 0) // HEAD_DIM
    c = lax.broadcasted_iota(jnp.int32, (MXU_WIDTH, MXU_WIDTH), 1) // HEAD_DIM
    head_avg = jnp.where(r == c, 1.0 / HEAD_DIM, 0.0).astype(BF16)

    def head_norm(v, g):
        sq = (v * v).astype(BF16)
        msq = jnp.concatenate([jnp.dot(sq[:, lo:lo + MXU_WIDTH], head_avg, preferred_element_type=F32)
                               for lo in range(0, d_attn, MXU_WIDTH)], axis=1)
        return (v * lax.rsqrt(msq + EPS)) * g

    q_ref[0] = head_norm(proj(0, d_attn), qk_gain[0:1, :]).astype(BF16)
    k_ref[0] = head_norm(proj(d_attn, d_attn), qk_gain[1:2, :]).astype(BF16)

    vt = proj(2 * d_attn, d_attn).T.astype(BF16)
    for jj in range(ts // ATTN_BLOCK):
        vt_ref[0, jj] = vt[:, jj * ATTN_BLOCK:(jj + 1) * ATTN_BLOCK]

    sg_ref[0] = _silu(proj(3 * d_attn, d_attn)).astype(BF16)


def _proj_call(mod3, x, norm_g, w_in, qg, kg, w_pool, b_pool, pool_scale, *, ts=1024):
    bsz, s, d = x.shape
    d_pool = pool_scale.shape[1]
    d_attn = (w_in.shape[1] - 2 * d_pool) // 4
    assert qg.shape == kg.shape == (1, HEAD_DIM)
    nt = s // ts
    nb = s // ATTN_BLOCK
    kern = functools.partial(_proj_kernel, ts=ts, d_model=d, d_attn=d_attn, d_pool=d_pool)
    const = lambda *shape: pl.BlockSpec(shape, lambda b, t: (0,) * len(shape))
    once = lambda *shape: pl.BlockSpec(shape, lambda b, t: (0,) * len(shape), pipeline_mode=pl.Buffered(1))
    seq_out = lambda width: pl.BlockSpec((1, ts, width), lambda b, t: (b, t, 0))
    pool_rows = POOL_PAD + ts
    return pl.pallas_call(
        kern,
        grid=(bsz, nt),
        in_specs=[
            pl.BlockSpec((1, 1, mod3.shape[2]), lambda b, t: (b, 0, 0)),
            pl.BlockSpec((1, ts, d), lambda b, t: (b, t, 0)),
            const(1, d),
            once(*w_in.shape),
            const(1, HEAD_DIM),
            const(1, HEAD_DIM),
            once(*w_pool.shape),
            const(1, d_pool),
            const(1, d_pool),
        ],
        out_specs=[
            seq_out(d_attn),
            seq_out(d_attn),
            pl.BlockSpec((1, ts // ATTN_BLOCK, d_attn, ATTN_BLOCK), lambda b, t: (b, t, 0, 0)),
            seq_out(d_attn),
            seq_out(d_pool),
        ],
        out_shape=[
            jax.ShapeDtypeStruct((bsz, s, d_attn), BF16),
            jax.ShapeDtypeStruct((bsz, s, d_attn), BF16),
            jax.ShapeDtypeStruct((bsz, nb, d_attn, ATTN_BLOCK), BF16),
            jax.ShapeDtypeStruct((bsz, s, d_attn), BF16),
            jax.ShapeDtypeStruct((bsz, s, d_pool), BF16),
        ],
        scratch_shapes=[pltpu.VMEM(w_in.shape, BF16),
                        pltpu.VMEM((d_pool // MXU_WIDTH, MXU_WIDTH, MXU_WIDTH), BF16),
                        pltpu.VMEM((2, d_attn), F32),
                        pltpu.VMEM((pool_rows, d_pool), F32),
                        pltpu.VMEM((pool_rows, d_pool), F32),
                        pltpu.VMEM((pool_rows, d_pool), F32)],
        compiler_params=pltpu.CompilerParams(
            dimension_semantics=("arbitrary", "arbitrary"), vmem_limit_bytes=VMEM_LIMIT_BYTES),
        name="norm_inproj_pool",
    )(mod3, x, norm_g, w_in, qg, kg, w_pool, b_pool, pool_scale)


def _attn_out_kernel(q_ref, k_ref, vt_ref, sg_ref, p_ref, x_hbm, mod_ref, w_hbm, out_hbm,
                     lr_ref, acc_ref, o_buf, wbf, xbuf, ybuf, x_sem, y_sem, w_sem,
                     *, nblk, g_q, n_hp, out_rows):
    tb = ATTN_BLOCK
    b = pl.program_id(0)
    d_attn = n_hp * LANES
    d_model = xbuf.shape[1]
    n_out = nblk * tb // out_rows

    def x_copy(c):
        rows = pl.ds(c * out_rows, out_rows)
        return pltpu.make_async_copy(x_hbm.at[b, rows, :], xbuf.at[rows, :], x_sem.at[c])

    def out_copy(bb, c):
        rows = pl.ds(c * out_rows, out_rows)
        return pltpu.make_async_copy(ybuf.at[rows, :], out_hbm.at[bb, rows, :], y_sem.at[c])

    def w_copy():
        return pltpu.make_async_copy(w_hbm, ybuf.at[0:wbf.shape[0], :], w_sem.at[0])

    @pl.when(b == 0)
    def _():
        w_copy().start()

    for c in range(n_out):
        x_copy(c).start()

    row = lax.broadcasted_iota(jnp.int32, (tb, tb), 0)
    col = lax.broadcasted_iota(jnp.int32, (tb, tb), 1)
    neg_suffix = jnp.concatenate([jnp.where(col > row, -1.0, 0.0), jnp.where(col == row, -1.0, 0.0)],
                                 axis=1).astype(BF16)
    causal = row < col
    n_grp = 2 * n_hp
    grp_w = g_q * tb

    def mask_diag(v, diag):
        if not diag:
            return v
        nq = v.shape[1] // (n_grp * tb)
        keep = causal[0:v.shape[0]]
        parts = []
        for grp in range(n_grp):
            base = grp * nq * tb
            parts.append(jnp.where(keep, v[:, base:base + tb], 0.0))
            if nq > 1:
                parts.append(v[:, base + tb:base + nq * tb])
        return jnp.concatenate(parts, axis=1)

    def blocks_alive(rem, q_lo, nq):
        n_alive = jnp.int32(0)
        for g in range(nq):
            blk = jnp.concatenate([rem[:, (grp * nq + g) * tb:(grp * nq + g + 1) * tb] for grp in range(n_grp)],
                                  axis=1)
            n_alive = jnp.where(jnp.max(blk) > REM_FLOOR_LOG2, q_lo + g + 1, n_alive)
        return n_alive

    def score(j, qts, q_lo, q_hi, diag):
        nq = q_hi - q_lo
        rows_j = pl.ds(pl.multiple_of(j * tb, tb), tb)
        zs = []
        for hp in range(n_hp):
            qt = qts[hp]
            if nq < g_q:
                qt = jnp.concatenate([qt[:, h * grp_w + q_lo * tb:h * grp_w + q_hi * tb] for h in range(2)],
                                     axis=1)
            zs.append(jnp.dot(k_ref[0, rows_j, hp * LANES:(hp + 1) * LANES], qt, preferred_element_type=F32))
        z = jnp.concatenate(zs, axis=1)
        sp = jnp.maximum(jnp.log(1.0 + jnp.exp2(jnp.minimum(z, SOFTPLUS_CLAMP))) * LOG2E, z)
        neg_log_beta = mask_diag(sp - z, diag)
        sp = mask_diag(sp, diag)
        return sp[0:1, :], jnp.concatenate([sp.astype(BF16), neg_log_beta.astype(BF16)], axis=0)

    def weigh(j, sp_0, terms, q_lo, q_hi, diag):
        nq = q_hi - q_lo
        lanes = [slice(grp * grp_w + q_lo * tb, grp * grp_w + q_hi * tb) for grp in range(n_grp)]
        log_w = jnp.dot(neg_suffix, terms, preferred_element_type=F32)
        log_rem = jnp.concatenate([lr_ref[:, ln] for ln in lanes], axis=1)
        w = mask_diag(jnp.exp2(log_w + log_rem), diag).astype(BF16)
        later_0 = -(log_w[0:1, :] + terms[tb:tb + 1, :].astype(F32))
        new_rem = log_rem - (sp_0 + later_0)
        for grp in range(n_grp):
            hp, h = divmod(grp, 2)
            cols = slice(grp * nq * tb, (grp + 1) * nq * tb)
            lr_ref[:, lanes[grp]] = new_rem[:, cols]
            vt_h = vt_ref[0, j, hp * LANES + h * HEAD_DIM:hp * LANES + (h + 1) * HEAD_DIM, :]
            res = jnp.dot(vt_h, w[:, cols], preferred_element_type=F32)
            acc_ref[h * HEAD_DIM:(h + 1) * HEAD_DIM, hp * grp_w + q_lo * tb:hp * grp_w + q_hi * tb] += res
        return blocks_alive(new_rem, q_lo, nq)

    def step(j, qts, q_lo, q_hi, diag):
        return weigh(j, *score(j, qts, q_lo, q_hi, diag), q_lo, q_hi, diag)

    def emit(sb, g):
        rows = pl.ds(pl.multiple_of((sb * g_q + g) * tb, tb), tb)
        for hp in range(n_hp):
            a = hp * grp_w + g * tb
            gate = sg_ref[0, rows, hp * LANES:(hp + 1) * LANES].astype(F32)
            o_buf[rows, hp * LANES:(hp + 1) * LANES] = (acc_ref[:, a:a + tb].T * gate).astype(BF16)

    def q_super_block(sb, carry):
        qts = []
        head_of_row = lax.broadcasted_iota(jnp.int32, (LANES, grp_w), 0) // HEAD_DIM
        for hp in range(n_hp):
            rows = pl.ds(pl.multiple_of(sb * grp_w, grp_w), grp_w)
            qt = q_ref[0, rows, hp * LANES:(hp + 1) * LANES].astype(F32).T
            qts.append(jnp.concatenate([jnp.where(head_of_row == h, qt, 0.0) for h in range(2)],
                                       axis=1).astype(BF16))
        lr_ref[...] = jnp.zeros_like(lr_ref)
        acc_ref[...] = jnp.zeros_like(acc_ref)
        base = sb * g_q

        def band(offsets):
            spans = [(m, max(m, 0), min(m + BAND, g_q), m >= 0) for m in offsets]
            scored = {}
            for i in range(len(spans) + DIAG_LOOKAHEAD):
                if i < len(spans):
                    m, lo, hi, diag = spans[i]
                    scored[m] = score(base + m, qts, lo, hi, diag)
                if i >= DIAG_LOOKAHEAD:
                    m, lo, hi, diag = spans[i - DIAG_LOOKAHEAD]
                    weigh(base + m, *scored.pop(m), lo, hi, diag)
                    if 0 <= m + BAND - 1 < g_q:
                        emit(sb, m + BAND - 1)

        first_blocks = range(min(BAND - 1, g_q))
        band(range(g_q - 1, -1, -1))
        for g in first_blocks:
            emit(sb, g)

        @pl.when(sb > 0)
        def _():
            band(range(-1, -BAND, -1))
            for g in first_blocks:
                emit(sb, g)

        def more(state):
            j, alive = state
            return (j >= 0) & (alive > 0)

        alive = [blocks_alive(jnp.concatenate([lr_ref[:, grp * grp_w + g * tb:grp * grp_w + (g + 1) * tb]
                                               for grp in range(n_grp)], axis=1), g, 1) for g in range(g_q)]
        starts = [(base + g - BAND, alive[g]) for g in range(g_q)]

        @pl.when(functools.reduce(jnp.logical_or, [more(start) for start in starts]))
        def _():
            for g, start in enumerate(starts):
                def one_block_step(state, g=g):
                    j, _ = state
                    return j - 1, step(j, qts, g, g + 1, False)

                @pl.when(more(start))
                def _(g=g, start=start, one_block_step=one_block_step):
                    lax.while_loop(more, one_block_step, start)
                    emit(sb, g)
        return carry

    lax.fori_loop(0, nblk // g_q, q_super_block, 0)

    @pl.when(b > 0)
    def _():
        for c in range(n_out):
            out_copy(b - 1, c).wait()

    @pl.when(b == 0)
    def _():
        w_copy().wait()
        wbf[...] = ybuf[0:wbf.shape[0], :].astype(BF16)

    gate = mod_ref[0, :, 2 * d_model:3 * d_model]
    for c in range(n_out):
        rows = slice(c * out_rows, (c + 1) * out_rows)
        x_copy(c).wait()
        y = jnp.dot(o_buf[rows, :], wbf[0:d_attn, :], preferred_element_type=F32)
        y = y + jnp.dot(p_ref[0, rows, :], wbf[d_attn:, :], preferred_element_type=F32)
        ybuf[rows, :] = xbuf[rows, :] + gate * y
        out_copy(b, c).start()

    @pl.when(b == pl.num_programs(0) - 1)
    def _():
        for c in range(n_out):
            out_copy(b, c).wait()


def _attn_out_call(qn, kn, vt, sg, pool_out, x, mod3, w_out, *, g_q=16, out_rows=1024):
    bsz, s, d = x.shape
    d_attn = qn.shape[2]
    nblk = s // ATTN_BLOCK
    assert nblk % g_q == 0 and s % out_rows == 0 and d_attn % LANES == 0
    assert w_out.shape == (d_attn + pool_out.shape[2], d) and w_out.shape[0] <= s
    n_hp = d_attn // LANES
    score_lanes = g_q * n_hp * 2 * ATTN_BLOCK
    seq = pl.BlockSpec((1, s, d_attn), lambda b: (b, 0, 0))
    return pl.pallas_call(
        functools.partial(_attn_out_kernel, nblk=nblk, g_q=g_q, n_hp=n_hp, out_rows=out_rows),
        grid=(bsz,),
        in_specs=[seq, seq, pl.BlockSpec((1, nblk, d_attn, ATTN_BLOCK), lambda b: (b, 0, 0, 0)), seq,
                  pl.BlockSpec((1, s, pool_out.shape[2]), lambda b: (b, 0, 0)),
                  pl.BlockSpec(memory_space=pl.ANY),
                  pl.BlockSpec((1, 1, mod3.shape[2]), lambda b: (b, 0, 0)),
                  pl.BlockSpec(memory_space=pl.ANY)],
        out_specs=pl.BlockSpec(memory_space=pl.ANY),
        out_shape=jax.ShapeDtypeStruct((bsz, s, d), F32),
        scratch_shapes=[pltpu.VMEM((1, score_lanes), F32),
                        pltpu.VMEM((LANES, g_q * n_hp * ATTN_BLOCK), F32),
                        pltpu.VMEM((s, d_attn), BF16),
                        pltpu.VMEM(w_out.shape, BF16),
                        pltpu.VMEM((s, d), F32),
                        pltpu.VMEM((s, d), F32),
                        pltpu.SemaphoreType.DMA((s // out_rows,)),
                        pltpu.SemaphoreType.DMA((s // out_rows,)),
                        pltpu.SemaphoreType.DMA((1,))],
        compiler_params=pltpu.CompilerParams(
            dimension_semantics=("arbitrary",), vmem_limit_bytes=VMEM_LIMIT_BYTES),
        name="stickbreak_attn_outproj",
    )(qn, kn, vt, sg, pool_out, x, mod3, w_out)


def kernel(x, c, w_ada, b_ada, norm_g, w_in, q_norm_g, k_norm_g, w_pool, b_pool, pool_scale, w_out):
    depth = w_ada.shape[0]
    h = x
    for l in range(depth):
        mod3 = _ada_call(c, w_ada[l], b_ada[l])
        qn, kn, vt, sg, pool_out = _proj_call(
            mod3, h, norm_g[l].reshape(1, -1), w_in[l], q_norm_g[l].reshape(1, -1), k_norm_g[l].reshape(1, -1),
            w_pool[l], b_pool[l].reshape(1, -1), pool_scale[l].reshape(1, -1))
        h = _attn_out_call(qn, kn, vt, sg, pool_out, h, mod3, w_out[l])
    return h
```

```python
import functools

import jax
import jax.numpy as jnp
from jax import lax
from jax.experimental import pallas as pl
from jax.experimental.pallas import tpu as pltpu

F32 = jnp.float32
BF16 = jnp.bfloat16

HEAD_DIM = 64
POOL_WINDOWS = (2, 4, 8, 16)
EPS = 1e-6
LOG2E = 1.4426950408889634
SOFTPLUS_CLAMP = 64.0
REM_FLOOR_LOG2 = -136.0
DIAG_LOOKAHEAD = 1
BAND = 3
BAND_TAIL_KEYS = 64

LANES = 128
SUBLANES = 8
MXU_WIDTH = 256
ATTN_BLOCK = 128
POOL_HIST = max(POOL_WINDOWS)
POOL_PAD = 2 * POOL_HIST
VMEM_LIMIT_BYTES = 58 * 1024 * 1024


def _silu(v):
    h = 0.5 * v
    return h + h * jnp.tanh(h)


def _first_grid_step():
    return (pl.program_id(0) == 0) & (pl.program_id(1) == 0)


def _ada_kernel(c_ref, w_ref, b_ref, o_ref):
    c = c_ref[...]
    ca = _silu(c).astype(BF16)
    mod = jnp.dot(ca, w_ref[...].astype(BF16), preferred_element_type=F32) + b_ref[...]
    for i in range(mod.shape[0]):
        o_ref[i] = mod[i:i + 1, :]


def _ada_call(c, w_ada, b_ada, *, tn=1024):
    bsz, d = c.shape
    n = w_ada.shape[1]
    return pl.pallas_call(
        _ada_kernel,
        grid=(n // tn,),
        in_specs=[
            pl.BlockSpec((bsz, d), lambda j: (0, 0)),
            pl.BlockSpec((d, tn), lambda j: (0, j)),
            pl.BlockSpec((1, tn), lambda j: (0, j)),
        ],
        out_specs=pl.BlockSpec((bsz, 1, tn), lambda j: (0, 0, j)),
        out_shape=jax.ShapeDtypeStruct((bsz, 1, n), F32),
        compiler_params=pltpu.CompilerParams(
            dimension_semantics=("arbitrary",), vmem_limit_bytes=VMEM_LIMIT_BYTES),
        name="adaln_mod",
    )(c, w_ada, b_ada.reshape(1, n))


def _proj_kernel(mod_ref, x_ref, ng_ref, win_ref, qg_ref, kg_ref, wp_ref, bp_ref, ps_ref,
                 q_ref, k_ref, vt_ref, sg_ref, po_ref, wbf, wpbf, qk_gain, ubuf, sbuf_a, sbuf_b,
                 *, ts, d_model, d_attn, d_pool):
    t = pl.program_id(1)
    hist = slice(POOL_PAD - POOL_HIST, POOL_PAD)
    gdim = d_pool // len(POOL_WINDOWS)
    pool_pack = MXU_WIDTH // gdim

    @pl.when(_first_grid_step())
    def _():
        wbf[...] = win_ref[...].astype(BF16)
        n_heads = d_attn // HEAD_DIM
        qk_gain[0:1, :] = jnp.concatenate([qg_ref[...] * (HEAD_DIM ** -0.5 * LOG2E)] * n_heads, axis=1)
        qk_gain[1:2, :] = jnp.concatenate([kg_ref[...]] * n_heads, axis=1)
        wpbf[...] = jnp.zeros_like(wpbf)
        for g in range(len(POOL_WINDOWS)):
            p, i = divmod(g, pool_pack)
            wpbf[p, i * gdim:(i + 1) * gdim, i * gdim:(i + 1) * gdim] = wp_ref[g].astype(BF16)

    @pl.when(t == 0)
    def _():
        ubuf[0:POOL_PAD, :] = jnp.zeros((POOL_PAD, d_pool), F32)

    @pl.when(t > 0)
    def _():
        ubuf[hist, :] = ubuf[ts + POOL_PAD - POOL_HIST:ts + POOL_PAD, :]

    x = x_ref[0]
    ms = jnp.mean(x * x, axis=-1, keepdims=True)
    shift = mod_ref[0, :, 0:d_model]
    scale = mod_ref[0, :, d_model:2 * d_model]
    a = ng_ref[...] * (1.0 + scale)
    hn = ((x * lax.rsqrt(ms + EPS)) * a + shift).astype(BF16)

    def proj(col, width):
        return jnp.dot(hn, wbf[:, col:col + width], preferred_element_type=F32)

    u = proj(4 * d_attn, d_pool)
    g_pool = proj(4 * d_attn + d_pool, d_pool)
    ubuf[POOL_PAD:POOL_PAD + ts, :] = u
    n_lvl = len(POOL_WINDOWS)
    end = POOL_PAD + ts
    src = ubuf
    wsums = []
    for lvl in range(n_lvl):
        shift_rows = 2 ** lvl
        start = SUBLANES * (lvl + 1)
        lanes = slice(lvl * gdim, d_pool)
        summed = src[start:end, lanes] + src[start - shift_rows:end - shift_rows, lanes]
        wsums.append(summed[POOL_PAD - start:, 0:gdim])
        if lvl + 1 < n_lvl:
            dst = sbuf_a if lvl % 2 == 0 else sbuf_b
            dst[start:end, lanes] = summed
            src = dst

    row = lax.broadcasted_iota(jnp.int32, (POOL_HIST, gdim), 0)
    pooled = []
    for g, win in enumerate(POOL_WINDOWS):
        ug = u[:, g * gdim:(g + 1) * gdim]
        cnt = jnp.minimum(t * ts + row + 1, win).astype(F32)
        head = wsums[g][0:POOL_HIST] / cnt - ug[0:POOL_HIST]
        rest = wsums[g][POOL_HIST:] * (1.0 / win) - ug[POOL_HIST:]
        pooled.append(jnp.concatenate([head, rest], axis=0).astype(BF16))
    for p in range(len(POOL_WINDOWS) // pool_pack):
        lo, hi = p * MXU_WIDTH, (p + 1) * MXU_WIDTH
        mixed = jnp.dot(jnp.concatenate(pooled[p * pool_pack:(p + 1) * pool_pack], axis=1), wpbf[p],
                        preferred_element_type=F32) + bp_ref[:, lo:hi]
        po_ref[0, :, lo:hi] = (mixed * ps_ref[:, lo:hi] * _silu(g_pool[:, lo:hi])).astype(BF16)

    r = lax.broadcasted_iota(jnp.int32, (MXU_WIDTH, MXU_WIDTH), 0) // HEAD_DIM
    c = lax.broadcasted_iota(jnp.int32, (MXU_WIDTH, MXU_WIDTH), 1) // HEAD_DIM
    head_avg = jnp.where(r == c, 1.0 / HEAD_DIM, 0.0).astype(BF16)

    def head_norm(v, g):
        sq = (v * v).astype(BF16)
        msq = jnp.concatenate([jnp.dot(sq[:, lo:lo + MXU_WIDTH], head_avg, preferred_element_type=F32)
                               for lo in range(0, d_attn, MXU_WIDTH)], axis=1)
        return (v * lax.rsqrt(msq + EPS)) * g

    q_ref[0] = head_norm(proj(0, d_attn), qk_gain[0:1, :]).astype(BF16)
    k_ref[0] = head_norm(proj(d_attn, d_attn), qk_gain[1:2, :]).astype(BF16)

    vt = proj(2 * d_attn, d_attn).T.astype(BF16)
    for jj in range(ts // ATTN_BLOCK):
        vt_ref[0, jj] = vt[:, jj * ATTN_BLOCK:(jj + 1) * ATTN_BLOCK]

    sg_ref[0] = _silu(proj(3 * d_attn, d_attn)).astype(BF16)


def _proj_call(mod3, x, norm_g, w_in, qg, kg, w_pool, b_pool, pool_scale, *, ts=1024):
    bsz, s, d = x.shape
    d_pool = pool_scale.shape[1]
    d_attn = (w_in.shape[1] - 2 * d_pool) // 4
    assert qg.shape == kg.shape == (1, HEAD_DIM)
    nt = s // ts
    nb = s // ATTN_BLOCK
    kern = functools.partial(_proj_kernel, ts=ts, d_model=d, d_attn=d_attn, d_pool=d_pool)
    const = lambda *shape: pl.BlockSpec(shape, lambda b, t: (0,) * len(shape))
    once = lambda *shape: pl.BlockSpec(shape, lambda b, t: (0,) * len(shape), pipeline_mode=pl.Buffered(1))
    seq_out = lambda width: pl.BlockSpec((1, ts, width), lambda b, t: (b, t, 0))
    pool_rows = POOL_PAD + ts
    return pl.pallas_call(
        kern,
        grid=(bsz, nt),
        in_specs=[
            pl.BlockSpec((1, 1, mod3.shape[2]), lambda b, t: (b, 0, 0)),
            pl.BlockSpec((1, ts, d), lambda b, t: (b, t, 0)),
            const(1, d),
            once(*w_in.shape),
            const(1, HEAD_DIM),
            const(1, HEAD_DIM),
            once(*w_pool.shape),
            const(1, d_pool),
            const(1, d_pool),
        ],
        out_specs=[
            seq_out(d_attn),
            seq_out(d_attn),
            pl.BlockSpec((1, ts // ATTN_BLOCK, d_attn, ATTN_BLOCK), lambda b, t: (b, t, 0, 0)),
            seq_out(d_attn),
            seq_out(d_pool),
        ],
        out_shape=[
            jax.ShapeDtypeStruct((bsz, s, d_attn), BF16),
            jax.ShapeDtypeStruct((bsz, s, d_attn), BF16),
            jax.ShapeDtypeStruct((bsz, nb, d_attn, ATTN_BLOCK), BF16),
            jax.ShapeDtypeStruct((bsz, s, d_attn), BF16),
            jax.ShapeDtypeStruct((bsz, s, d_pool), BF16),
        ],
        scratch_shapes=[pltpu.VMEM(w_in.shape, BF16),
                        pltpu.VMEM((d_pool // MXU_WIDTH, MXU_WIDTH, MXU_WIDTH), BF16),
                        pltpu.VMEM((2, d_attn), F32),
                        pltpu.VMEM((pool_rows, d_pool), F32),
                        pltpu.VMEM((pool_rows, d_pool), F32),
                        pltpu.VMEM((pool_rows, d_pool), F32)],
        compiler_params=pltpu.CompilerParams(
            dimension_semantics=("arbitrary", "arbitrary"), vmem_limit_bytes=VMEM_LIMIT_BYTES),
        name="norm_inproj_pool",
    )(mod3, x, norm_g, w_in, qg, kg, w_pool, b_pool, pool_scale)


def _attn_out_kernel(q_ref, k_ref, vt_ref, sg_ref, p_ref, x_hbm, mod_ref, w_hbm, out_hbm,
                     lr_ref, acc_ref, o_buf, wbf, xbuf, ybuf, x_sem, y_sem, w_sem,
                     *, nblk, g_q, n_hp, out_rows):
    tb = ATTN_BLOCK
    b = pl.program_id(0)
    d_attn = n_hp * LANES
    d_model = xbuf.shape[1]
    n_out = nblk * tb // out_rows

    def x_copy(c):
        rows = pl.ds(c * out_rows, out_rows)
        return pltpu.make_async_copy(x_hbm.at[b, rows, :], xbuf.at[rows, :], x_sem.at[c])

    def out_copy(bb, c):
        rows = pl.ds(c * out_rows, out_rows)
        return pltpu.make_async_copy(ybuf.at[rows, :], out_hbm.at[bb, rows, :], y_sem.at[c])

    def w_copy():
        return pltpu.make_async_copy(w_hbm, ybuf.at[0:wbf.shape[0], :], w_sem.at[0])

    @pl.when(b == 0)
    def _():
        w_copy().start()

    for c in range(n_out):
        x_copy(c).start()

    row = lax.broadcasted_iota(jnp.int32, (tb, tb), 0)
    col = lax.broadcasted_iota(jnp.int32, (tb, tb), 1)
    def neg_suffix(nk):
        return jnp.concatenate([jnp.where(col > row, -1.0, 0.0)[0:nk, 0:nk],
                                jnp.where(col == row, -1.0, 0.0)[0:nk, 0:nk]], axis=1).astype(BF16)

    def drop_newest(v, skip_newest):
        if skip_newest is None:
            return v
        return jnp.where(skip_newest & (row[:, 0:1] >= tb - BAND_TAIL_KEYS), 0.0, v)
    causal = row < col
    n_grp = 2 * n_hp
    grp_w = g_q * tb

    def mask_diag(v, diag):
        if not diag:
            return v
        nq = v.shape[1] // (n_grp * tb)
        keep = causal[0:v.shape[0]]
        parts = []
        for grp in range(n_grp):
            base = grp * nq * tb
            parts.append(jnp.where(keep, v[:, base:base + tb], 0.0))
            if nq > 1:
                parts.append(v[:, base + tb:base + nq * tb])
        return jnp.concatenate(parts, axis=1)

    def blocks_alive(rem, q_lo, nq):
        n_alive = jnp.int32(0)
        for g in range(nq):
            blk = jnp.concatenate([rem[:, (grp * nq + g) * tb:(grp * nq + g + 1) * tb] for grp in range(n_grp)],
                                  axis=1)
            n_alive = jnp.where(jnp.max(blk) > REM_FLOOR_LOG2, q_lo + g + 1, n_alive)
        return n_alive

    def score(j, keys, qts, q_lo, q_hi, diag, skip_newest=None):
        nq = q_hi - q_lo
        rows_j = pl.ds(pl.multiple_of(j * tb + keys[0], SUBLANES), keys[1] - keys[0])
        zs = []
        for hp in range(n_hp):
            qt = qts[hp]
            if nq < g_q:
                qt = jnp.concatenate([qt[:, h * grp_w + q_lo * tb:h * grp_w + q_hi * tb] for h in range(2)],
                                     axis=1)
            zs.append(jnp.dot(k_ref[0, rows_j, hp * LANES:(hp + 1) * LANES], qt, preferred_element_type=F32))
        z = jnp.concatenate(zs, axis=1)
        sp = jnp.maximum(jnp.log(1.0 + jnp.exp2(jnp.minimum(z, SOFTPLUS_CLAMP))) * LOG2E, z)
        neg_log_beta = drop_newest(mask_diag(sp - z, diag), skip_newest)
        sp = drop_newest(mask_diag(sp, diag), skip_newest)
        return sp[0:1, :], jnp.concatenate([sp.astype(BF16), neg_log_beta.astype(BF16)], axis=0)

    def weigh(j, keys, sp_0, terms, q_lo, q_hi, diag, skip_newest=None):
        nq = q_hi - q_lo
        nk = keys[1] - keys[0]
        lanes = [slice(grp * grp_w + q_lo * tb, grp * grp_w + q_hi * tb) for grp in range(n_grp)]
        log_w = jnp.dot(neg_suffix(nk), terms, preferred_element_type=F32)
        log_rem = jnp.concatenate([lr_ref[:, ln] for ln in lanes], axis=1)
        w = drop_newest(mask_diag(jnp.exp2(log_w + log_rem), diag), skip_newest).astype(BF16)
        later_0 = -(log_w[0:1, :] + terms[nk:nk + 1, :].astype(F32))
        new_rem = log_rem - (sp_0 + later_0)
        for grp in range(n_grp):
            hp, h = divmod(grp, 2)
            cols = slice(grp * nq * tb, (grp + 1) * nq * tb)
            lr_ref[:, lanes[grp]] = new_rem[:, cols]
            vt_h = vt_ref[0, j, hp * LANES + h * HEAD_DIM:hp * LANES + (h + 1) * HEAD_DIM, keys[0]:keys[1]]
            res = jnp.dot(vt_h, w[:, cols], preferred_element_type=F32)
            acc_ref[h * HEAD_DIM:(h + 1) * HEAD_DIM, hp * grp_w + q_lo * tb:hp * grp_w + q_hi * tb] += res
        return blocks_alive(new_rem, q_lo, nq)

    def step(j, keys, qts, q_lo, q_hi, diag, skip_newest=None):
        return weigh(j, keys, *score(j, keys, qts, q_lo, q_hi, diag, skip_newest), q_lo, q_hi, diag, skip_newest)

    def emit(sb, g):
        rows = pl.ds(pl.multiple_of((sb * g_q + g) * tb, tb), tb)
        for hp in range(n_hp):
            a = hp * grp_w + g * tb
            gate = sg_ref[0, rows, hp * LANES:(hp + 1) * LANES].astype(F32)
            o_buf[rows, hp * LANES:(hp + 1) * LANES] = (acc_ref[:, a:a + tb].T * gate).astype(BF16)

    def q_super_block(sb, carry):
        qts = []
        head_of_row = lax.broadcasted_iota(jnp.int32, (LANES, grp_w), 0) // HEAD_DIM
        for hp in range(n_hp):
            rows = pl.ds(pl.multiple_of(sb * grp_w, grp_w), grp_w)
            qt = q_ref[0, rows, hp * LANES:(hp + 1) * LANES].astype(F32).T
            qts.append(jnp.concatenate([jnp.where(head_of_row == h, qt, 0.0) for h in range(2)],
                                       axis=1).astype(BF16))
        lr_ref[...] = jnp.zeros_like(lr_ref)
        acc_ref[...] = jnp.zeros_like(acc_ref)
        base = sb * g_q

        all_keys = (0, tb)
        tail_keys = (tb - BAND_TAIL_KEYS, tb)

        def band(offsets):
            spans = []
            for m in offsets:
                lo, hi = max(m, 0), min(m + BAND - 1, g_q)
                if lo < hi:
                    spans.append((m, all_keys, lo, hi, m >= 0, None))
                if 0 <= m + BAND - 1 < g_q:
                    spans.append((m, tail_keys, m + BAND - 1, m + BAND, False, m + BAND - 1))
            scored = {}
            for i in range(len(spans) + DIAG_LOOKAHEAD):
                if i < len(spans):
                    m, keys, lo, hi, diag, _ = spans[i]
                    scored[i] = score(base + m, keys, qts, lo, hi, diag)
                if i >= DIAG_LOOKAHEAD:
                    m, keys, lo, hi, diag, done = spans[i - DIAG_LOOKAHEAD]
                    weigh(base + m, keys, *scored.pop(i - DIAG_LOOKAHEAD), lo, hi, diag)
                    if done is not None:
                        emit(sb, done)

        first_blocks = range(min(BAND - 1, g_q))
        band(range(g_q - 1, -1, -1))
        for g in first_blocks:
            emit(sb, g)

        @pl.when(sb > 0)
        def _():
            band(range(-1, -BAND, -1))
            for g in first_blocks:
                emit(sb, g)

        def more(state):
            j, alive = state
            return (j >= 0) & (alive > 0)

        alive = [blocks_alive(jnp.concatenate([lr_ref[:, grp * grp_w + g * tb:grp * grp_w + (g + 1) * tb]
                                               for grp in range(n_grp)], axis=1), g, 1) for g in range(g_q)]
        for g in range(g_q):
            oldest = base + g - (BAND - 1)

            def one_block_step(state, g=g, oldest=oldest):
                j, _ = state
                return j - 1, step(j, all_keys, qts, g, g + 1, False, skip_newest=(j == oldest))

            start = (oldest, alive[g])

            @pl.when(more(start))
            def _(g=g, start=start, one_block_step=one_block_step):
                lax.while_loop(more, one_block_step, start)
                emit(sb, g)
        return carry

    lax.fori_loop(0, nblk // g_q, q_super_block, 0)

    @pl.when(b > 0)
    def _():
        for c in range(n_out):
            out_copy(b - 1, c).wait()

    @pl.when(b == 0)
    def _():
        w_copy().wait()
        wbf[...] = ybuf[0:wbf.shape[0], :].astype(BF16)

    gate = mod_ref[0, :, 2 * d_model:3 * d_model]
    for c in range(n_out):
        rows = slice(c * out_rows, (c + 1) * out_rows)
        x_copy(c).wait()
        y = jnp.dot(o_buf[rows, :], wbf[0:d_attn, :], preferred_element_type=F32)
        y = y + jnp.dot(p_ref[0, rows, :], wbf[d_attn:, :], preferred_element_type=F32)
        ybuf[rows, :] = xbuf[rows, :] + gate * y
        out_copy(b, c).start()

    @pl.when(b == pl.num_programs(0) - 1)
    def _():
        for c in range(n_out):
            out_copy(b, c).wait()


def _attn_out_call(qn, kn, vt, sg, pool_out, x, mod3, w_out, *, g_q=16, out_rows=1024):
    bsz, s, d = x.shape
    d_attn = qn.shape[2]
    nblk = s // ATTN_BLOCK
    assert nblk % g_q == 0 and s % out_rows == 0 and d_attn % LANES == 0
    assert w_out.shape == (d_attn + pool_out.shape[2], d) and w_out.shape[0] <= s
    n_hp = d_attn // LANES
    score_lanes = g_q * n_hp * 2 * ATTN_BLOCK
    seq = pl.BlockSpec((1, s, d_attn), lambda b: (b, 0, 0))
    return pl.pallas_call(
        functools.partial(_attn_out_kernel, nblk=nblk, g_q=g_q, n_hp=n_hp, out_rows=out_rows),
        grid=(bsz,),
        in_specs=[seq, seq, pl.BlockSpec((1, nblk, d_attn, ATTN_BLOCK), lambda b: (b, 0, 0, 0)), seq,
                  pl.BlockSpec((1, s, pool_out.shape[2]), lambda b: (b, 0, 0)),
                  pl.BlockSpec(memory_space=pl.ANY),
                  pl.BlockSpec((1, 1, mod3.shape[2]), lambda b: (b, 0, 0)),
                  pl.BlockSpec(memory_space=pl.ANY)],
        out_specs=pl.BlockSpec(memory_space=pl.ANY),
        out_shape=jax.ShapeDtypeStruct((bsz, s, d), F32),
        scratch_shapes=[pltpu.VMEM((1, score_lanes), F32),
                        pltpu.VMEM((LANES, g_q * n_hp * ATTN_BLOCK), F32),
                        pltpu.VMEM((s, d_attn), BF16),
                        pltpu.VMEM(w_out.shape, BF16),
                        pltpu.VMEM((s, d), F32),
                        pltpu.VMEM((s, d), F32),
                        pltpu.SemaphoreType.DMA((s // out_rows,)),
                        pltpu.SemaphoreType.DMA((s // out_rows,)),
                        pltpu.SemaphoreType.DMA((1,))],
        compiler_params=pltpu.CompilerParams(
            dimension_semantics=("arbitrary",), vmem_limit_bytes=VMEM_LIMIT_BYTES),
        name="stickbreak_attn_outproj",
    )(qn, kn, vt, sg, pool_out, x, mod3, w_out)


def kernel(x, c, w_ada, b_ada, norm_g, w_in, q_norm_g, k_norm_g, w_pool, b_pool, pool_scale, w_out):
    depth = w_ada.shape[0]
    h = x
    for l in range(depth):
        mod3 = _ada_call(c, w_ada[l], b_ada[l])
        qn, kn, vt, sg, pool_out = _proj_call(
            mod3, h, norm_g[l].reshape(1, -1), w_in[l], q_norm_g[l].reshape(1, -1), k_norm_g[l].reshape(1, -1),
            w_pool[l], b_pool[l].reshape(1, -1), pool_scale[l].reshape(1, -1))
        h = _attn_out_call(qn, kn, vt, sg, pool_out, h, mod3, w_out[l])
    return h
```

```python
import functools

import jax
import jax.numpy as jnp
from jax import lax
from jax.experimental import pallas as pl
from jax.experimental.pallas import tpu as pltpu

F32 = jnp.float32
BF16 = jnp.bfloat16

HEAD_DIM = 64
POOL_WINDOWS = (2, 4, 8, 16)
EPS = 1e-6
LOG2E = 1.4426950408889634
SOFTPLUS_CLAMP = 64.0
REM_FLOOR_LOG2 = -136.0
DIAG_LOOKAHEAD = 1
BAND = 3

LANES = 128
SUBLANES = 8
MXU_WIDTH = 256
ATTN_BLOCK = 128
POOL_HIST = max(POOL_WINDOWS)
POOL_PAD = 2 * POOL_HIST
VMEM_LIMIT_BYTES = 58 * 1024 * 1024


def _silu(v):
    h = 0.5 * v
    return h + h * jnp.tanh(h)


def _first_grid_step():
    return (pl.program_id(0) == 0) & (pl.program_id(1) == 0)


def _ada_kernel(c_ref, w_ref, b_ref, o_ref):
    c = c_ref[...]
    ca = _silu(c).astype(BF16)
    mod = jnp.dot(ca, w_ref[...].astype(BF16), preferred_element_type=F32) + b_ref[...]
    for i in range(mod.shape[0]):
        o_ref[i] = mod[i:i + 1, :]


def _ada_call(c, w_ada, b_ada, *, tn=1024):
    bsz, d = c.shape
    n = w_ada.shape[1]
    return pl.pallas_call(
        _ada_kernel,
        grid=(n // tn,),
        in_specs=[
            pl.BlockSpec((bsz, d), lambda j: (0, 0)),
            pl.BlockSpec((d, tn), lambda j: (0, j)),
            pl.BlockSpec((1, tn), lambda j: (0, j)),
        ],
        out_specs=pl.BlockSpec((bsz, 1, tn), lambda j: (0, 0, j)),
        out_shape=jax.ShapeDtypeStruct((bsz, 1, n), F32),
        compiler_params=pltpu.CompilerParams(
            dimension_semantics=("arbitrary",), vmem_limit_bytes=VMEM_LIMIT_BYTES),
        name="adaln_mod",
    )(c, w_ada, b_ada.reshape(1, n))


def _proj_kernel(mod_ref, x_ref, ng_ref, win_ref, qg_ref, kg_ref, wp_ref, bp_ref, ps_ref,
                 q_ref, k_ref, vt_ref, sg_ref, po_ref, wbf, wpbf, qk_gain, ubuf, sbuf_a, sbuf_b,
                 *, ts, d_model, d_attn, d_pool):
    t = pl.program_id(1)
    hist = slice(POOL_PAD - POOL_HIST, POOL_PAD)
    gdim = d_pool // len(POOL_WINDOWS)
    pool_pack = MXU_WIDTH // gdim

    @pl.when(_first_grid_step())
    def _():
        wbf[...] = win_ref[...].astype(BF16)
        n_heads = d_attn // HEAD_DIM
        qk_gain[0:1, :] = jnp.concatenate([qg_ref[...] * (HEAD_DIM ** -0.5 * LOG2E)] * n_heads, axis=1)
        qk_gain[1:2, :] = jnp.concatenate([kg_ref[...]] * n_heads, axis=1)
        wpbf[...] = jnp.zeros_like(wpbf)
        for g in range(len(POOL_WINDOWS)):
            p, i = divmod(g, pool_pack)
            wpbf[p, i * gdim:(i + 1) * gdim, i * gdim:(i + 1) * gdim] = wp_ref[g].astype(BF16)

    @pl.when(t == 0)
    def _():
        ubuf[0:POOL_PAD, :] = jnp.zeros((POOL_PAD, d_pool), F32)

    @pl.when(t > 0)
    def _():
        ubuf[hist, :] = ubuf[ts + POOL_PAD - POOL_HIST:ts + POOL_PAD, :]

    x = x_ref[0]
    ms = jnp.mean(x * x, axis=-1, keepdims=True)
    shift = mod_ref[0, :, 0:d_model]
    scale = mod_ref[0, :, d_model:2 * d_model]
    a = ng_ref[...] * (1.0 + scale)
    hn = ((x * lax.rsqrt(ms + EPS)) * a + shift).astype(BF16)

    def proj(col, width):
        return jnp.dot(hn, wbf[:, col:col + width], preferred_element_type=F32)

    u = proj(4 * d_attn, d_pool)
    g_pool = proj(4 * d_attn + d_pool, d_pool)
    ubuf[POOL_PAD:POOL_PAD + ts, :] = u
    n_lvl = len(POOL_WINDOWS)
    end = POOL_PAD + ts
    src = ubuf
    wsums = []
    for lvl in range(n_lvl):
        shift_rows = 2 ** lvl
        start = SUBLANES * (lvl + 1)
        lanes = slice(lvl * gdim, d_pool)
        summed = src[start:end, lanes] + src[start - shift_rows:end - shift_rows, lanes]
        wsums.append(summed[POOL_PAD - start:, 0:gdim])
        if lvl + 1 < n_lvl:
            dst = sbuf_a if lvl % 2 == 0 else sbuf_b
            dst[start:end, lanes] = summed
            src = dst

    row = lax.broadcasted_iota(jnp.int32, (POOL_HIST, gdim), 0)
    pooled = []
    for g, win in enumerate(POOL_WINDOWS):
        ug = u[:, g * gdim:(g + 1) * gdim]
        cnt = jnp.minimum(t * ts + row + 1, win).astype(F32)
        head = wsums[g][0:POOL_HIST] / cnt - ug[0:POOL_HIST]
        rest = wsums[g][POOL_HIST:] * (1.0 / win) - ug[POOL_HIST:]
        pooled.append(jnp.concatenate([head, rest], axis=0).astype(BF16))
    for p in range(len(POOL_WINDOWS) // pool_pack):
        lo, hi = p * MXU_WIDTH, (p + 1) * MXU_WIDTH
        mixed = jnp.dot(jnp.concatenate(pooled[p * pool_pack:(p + 1) * pool_pack], axis=1), wpbf[p],
                        preferred_element_type=F32) + bp_ref[:, lo:hi]
        po_ref[0, :, lo:hi] = (mixed * ps_ref[:, lo:hi] * _silu(g_pool[:, lo:hi])).astype(BF16)

    r = lax.broadcasted_iota(jnp.int32, (MXU_WIDTH, MXU_WIDTH), 0) // HEAD_DIM
    c = lax.broadcasted_iota(jnp.int32, (MXU_WIDTH, MXU_WIDTH), 1) // HEAD_DIM
    head_avg = jnp.where(r == c, 1.0 / HEAD_DIM, 0.0).astype(BF16)

    def head_norm(v, g):
        sq = (v * v).astype(BF16)
        msq = jnp.concatenate([jnp.dot(sq[:, lo:lo + MXU_WIDTH], head_avg, preferred_element_type=F32)
                               for lo in range(0, d_attn, MXU_WIDTH)], axis=1)
        return (v * lax.rsqrt(msq + EPS)) * g

    q_ref[0] = head_norm(proj(0, d_attn), qk_gain[0:1, :]).astype(BF16)
    k_ref[0] = head_norm(proj(d_attn, d_attn), qk_gain[1:2, :]).astype(BF16)

    vt = proj(2 * d_attn, d_attn).T.astype(BF16)
    for jj in range(ts // ATTN_BLOCK):
        vt_ref[0, jj] = vt[:, jj * ATTN_BLOCK:(jj + 1) * ATTN_BLOCK]

    sg_ref[0] = _silu(proj(3 * d_attn, d_attn)).astype(BF16)


def _proj_call(mod3, x, norm_g, w_in, qg, kg, w_pool, b_pool, pool_scale, *, ts=1024):
    bsz, s, d = x.shape
    d_pool = pool_scale.shape[1]
    d_attn = (w_in.shape[1] - 2 * d_pool) // 4
    assert qg.shape == kg.shape == (1, HEAD_DIM)
    nt = s // ts
    nb = s // ATTN_BLOCK
    kern = functools.partial(_proj_kernel, ts=ts, d_model=d, d_attn=d_attn, d_pool=d_pool)
    const = lambda *shape: pl.BlockSpec(shape, lambda b, t: (0,) * len(shape))
    once = lambda *shape: pl.BlockSpec(shape, lambda b, t: (0,) * len(shape), pipeline_mode=pl.Buffered(1))
    seq_out = lambda width: pl.BlockSpec((1, ts, width), lambda b, t: (b, t, 0))
    pool_rows = POOL_PAD + ts
    return pl.pallas_call(
        kern,
        grid=(bsz, nt),
        in_specs=[
            pl.BlockSpec((1, 1, mod3.shape[2]), lambda b, t: (b, 0, 0)),
            pl.BlockSpec((1, ts, d), lambda b, t: (b, t, 0)),
            const(1, d),
            once(*w_in.shape),
            const(1, HEAD_DIM),
            const(1, HEAD_DIM),
            once(*w_pool.shape),
            const(1, d_pool),
            const(1, d_pool),
        ],
        out_specs=[
            seq_out(d_attn),
            seq_out(d_attn),
            pl.BlockSpec((1, ts // ATTN_BLOCK, d_attn, ATTN_BLOCK), lambda b, t: (b, t, 0, 0)),
            seq_out(d_attn),
            seq_out(d_pool),
        ],
        out_shape=[
            jax.ShapeDtypeStruct((bsz, s, d_attn), BF16),
            jax.ShapeDtypeStruct((bsz, s, d_attn), BF16),
            jax.ShapeDtypeStruct((bsz, nb, d_attn, ATTN_BLOCK), BF16),
            jax.ShapeDtypeStruct((bsz, s, d_attn), BF16),
            jax.ShapeDtypeStruct((bsz, s, d_pool), BF16),
        ],
        scratch_shapes=[pltpu.VMEM(w_in.shape, BF16),
                        pltpu.VMEM((d_pool // MXU_WIDTH, MXU_WIDTH, MXU_WIDTH), BF16),
                        pltpu.VMEM((2, d_attn), F32),
                        pltpu.VMEM((pool_rows, d_pool), F32),
                        pltpu.VMEM((pool_rows, d_pool), F32),
                        pltpu.VMEM((pool_rows, d_pool), F32)],
        compiler_params=pltpu.CompilerParams(
            dimension_semantics=("arbitrary", "arbitrary"), vmem_limit_bytes=VMEM_LIMIT_BYTES),
        name="norm_inproj_pool",
    )(mod3, x, norm_g, w_in, qg, kg, w_pool, b_pool, pool_scale)


def _attn_out_kernel(q_ref, k_ref, vt_ref, sg_ref, p_ref, x_hbm, mod_ref, w_hbm, out_hbm,
                     lr_ref, acc_ref, o_buf, wbf, xbuf, ybuf, x_sem, y_sem, w_sem,
                     *, nblk, g_q, n_hp, out_rows):
    tb = ATTN_BLOCK
    b = pl.program_id(0)
    d_attn = n_hp * LANES
    d_model = xbuf.shape[1]
    n_out = nblk * tb // out_rows

    def x_copy(c):
        rows = pl.ds(c * out_rows, out_rows)
        return pltpu.make_async_copy(x_hbm.at[b, rows, :], xbuf.at[rows, :], x_sem.at[c])

    def out_copy(bb, c):
        rows = pl.ds(c * out_rows, out_rows)
        return pltpu.make_async_copy(ybuf.at[rows, :], out_hbm.at[bb, rows, :], y_sem.at[c])

    def w_copy():
        return pltpu.make_async_copy(w_hbm, ybuf.at[0:wbf.shape[0], :], w_sem.at[0])

    @pl.when(b == 0)
    def _():
        w_copy().start()

    for c in range(n_out):
        x_copy(c).start()

    row = lax.broadcasted_iota(jnp.int32, (tb, tb), 0)
    col = lax.broadcasted_iota(jnp.int32, (tb, tb), 1)
    neg_suffix = jnp.concatenate([jnp.where(col > row, -1.0, 0.0), jnp.where(col == row, -1.0, 0.0)],
                                 axis=1).astype(BF16)
    causal = row < col
    n_grp = 2 * n_hp
    grp_w = g_q * tb

    def mask_diag(v, diag):
        if not diag:
            return v
        nq = v.shape[1] // (n_grp * tb)
        keep = causal[0:v.shape[0]]
        parts = []
        for grp in range(n_grp):
            base = grp * nq * tb
            parts.append(jnp.where(keep, v[:, base:base + tb], 0.0))
            if nq > 1:
                parts.append(v[:, base + tb:base + nq * tb])
        return jnp.concatenate(parts, axis=1)

    def blocks_alive(rem, q_lo, nq):
        n_alive = jnp.int32(0)
        for g in range(nq):
            blk = jnp.concatenate([rem[:, (grp * nq + g) * tb:(grp * nq + g + 1) * tb] for grp in range(n_grp)],
                                  axis=1)
            n_alive = jnp.where(jnp.max(blk) > REM_FLOOR_LOG2, q_lo + g + 1, n_alive)
        return n_alive

    def score(j, qts, q_lo, q_hi, diag):
        nq = q_hi - q_lo
        rows_j = pl.ds(pl.multiple_of(j * tb, tb), tb)
        zs = []
        for hp in range(n_hp):
            qt = qts[hp]
            if nq < g_q:
                qt = jnp.concatenate([qt[:, h * grp_w + q_lo * tb:h * grp_w + q_hi * tb] for h in range(2)],
                                     axis=1)
            zs.append(jnp.dot(k_ref[0, rows_j, hp * LANES:(hp + 1) * LANES], qt, preferred_element_type=F32))
        z = jnp.concatenate(zs, axis=1)
        sp = jnp.maximum(jnp.log(1.0 + jnp.exp2(jnp.minimum(z, SOFTPLUS_CLAMP))) * LOG2E, z)
        neg_log_beta = mask_diag(sp - z, diag)
        sp = mask_diag(sp, diag)
        return sp[0:1, :], jnp.concatenate([sp.astype(BF16), neg_log_beta.astype(BF16)], axis=0)

    def weigh(j, sp_0, terms, q_lo, q_hi, diag):
        nq = q_hi - q_lo
        lanes = [slice(grp * grp_w + q_lo * tb, grp * grp_w + q_hi * tb) for grp in range(n_grp)]
        log_w = jnp.dot(neg_suffix, terms, preferred_element_type=F32)
        log_rem = jnp.concatenate([lr_ref[:, ln] for ln in lanes], axis=1)
        w = mask_diag(jnp.exp2(log_w + log_rem), diag).astype(BF16)
        later_0 = -(log_w[0:1, :] + terms[tb:tb + 1, :].astype(F32))
        new_rem = log_rem - (sp_0 + later_0)
        for grp in range(n_grp):
            hp, h = divmod(grp, 2)
            cols = slice(grp * nq * tb, (grp + 1) * nq * tb)
            lr_ref[:, lanes[grp]] = new_rem[:, cols]
            vt_h = vt_ref[0, j, hp * LANES + h * HEAD_DIM:hp * LANES + (h + 1) * HEAD_DIM, :]
            res = jnp.dot(vt_h, w[:, cols], preferred_element_type=F32)
            acc_ref[h * HEAD_DIM:(h + 1) * HEAD_DIM, hp * grp_w + q_lo * tb:hp * grp_w + q_hi * tb] += res
        return blocks_alive(new_rem, q_lo, nq)

    def step(j, qts, q_lo, q_hi, diag):
        return weigh(j, *score(j, qts, q_lo, q_hi, diag), q_lo, q_hi, diag)

    def emit(sb, g):
        rows = pl.ds(pl.multiple_of((sb * g_q + g) * tb, tb), tb)
        for hp in range(n_hp):
            a = hp * grp_w + g * tb
            gate = sg_ref[0, rows, hp * LANES:(hp + 1) * LANES].astype(F32)
            o_buf[rows, hp * LANES:(hp + 1) * LANES] = (acc_ref[:, a:a + tb].T * gate).astype(BF16)

    def q_super_block(sb, carry):
        qts = []
        head_of_row = lax.broadcasted_iota(jnp.int32, (LANES, grp_w), 0) // HEAD_DIM
        for hp in range(n_hp):
            rows = pl.ds(pl.multiple_of(sb * grp_w, grp_w), grp_w)
            qt = q_ref[0, rows, hp * LANES:(hp + 1) * LANES].astype(F32).T
            qts.append(jnp.concatenate([jnp.where(head_of_row == h, qt, 0.0) for h in range(2)],
                                       axis=1).astype(BF16))
        lr_ref[...] = jnp.zeros_like(lr_ref)
        acc_ref[...] = jnp.zeros_like(acc_ref)
        base = sb * g_q

        def band(offsets):
            spans = [(m, max(m, 0), min(m + BAND, g_q), m >= 0) for m in offsets]
            scored = {}
            for i in range(len(spans) + DIAG_LOOKAHEAD):
                if i < len(spans):
                    m, lo, hi, diag = spans[i]
                    scored[m] = score(base + m, qts, lo, hi, diag)
                if i >= DIAG_LOOKAHEAD:
                    m, lo, hi, diag = spans[i - DIAG_LOOKAHEAD]
                    weigh(base + m, *scored.pop(m), lo, hi, diag)
                    if 0 <= m + BAND - 1 < g_q:
                        emit(sb, m + BAND - 1)

        first_blocks = range(min(BAND - 1, g_q))
        band(range(g_q - 1, -1, -1))
        for g in first_blocks:
            emit(sb, g)

        @pl.when(sb > 0)
        def _():
            band(range(-1, -BAND, -1))
            for g in first_blocks:
                emit(sb, g)

        def more(state):
            j, alive = state
            return (j >= 0) & (alive > 0)

        alive = [blocks_alive(jnp.concatenate([lr_ref[:, grp * grp_w + g * tb:grp * grp_w + (g + 1) * tb]
                                               for grp in range(n_grp)], axis=1), g, 1) for g in range(g_q)]
        for g in range(g_q):
            def one_block_step(state, g=g):
                j, _ = state
                return j - 1, step(j, qts, g, g + 1, False)

            start = (base + g - BAND, alive[g])

            @pl.when(more(start))
            def _(g=g, start=start, one_block_step=one_block_step):
                lax.while_loop(more, one_block_step, start)
                emit(sb, g)
        return carry

    lax.fori_loop(0, nblk // g_q, q_super_block, 0)

    @pl.when(b > 0)
    def _():
        for c in range(n_out):
            out_copy(b - 1, c).wait()

    @pl.when(b == 0)
    def _():
        w_copy().wait()
        wbf[...] = ybuf[0:wbf.shape[0], :].astype(BF16)

    gate = mod_ref[0, :, 2 * d_model:3 * d_model]
    for c in range(n_out):
        x_copy(c).wait()
    for c in range(n_out):
        rows = slice(c * out_rows, (c + 1) * out_rows)
        y = jnp.dot(o_buf[rows, :], wbf[0:d_attn, :], preferred_element_type=F32)
        y = y + jnp.dot(p_ref[0, rows, :], wbf[d_attn:, :], preferred_element_type=F32)
        res = xbuf[rows, :] + gate * y
        if c > 0:
            out_copy(b, c - 1).start()
        ybuf[rows, :] = res
    out_copy(b, n_out - 1).start()

    @pl.when(b == pl.num_programs(0) - 1)
    def _():
        for c in range(n_out):
            out_copy(b, c).wait()


def _attn_out_call(qn, kn, vt, sg, pool_out, x, mod3, w_out, *, g_q=16, out_rows=1024):
    bsz, s, d = x.shape
    d_attn = qn.shape[2]
    nblk = s // ATTN_BLOCK
    assert nblk % g_q == 0 and s % out_rows == 0 and d_attn % LANES == 0
    assert w_out.shape == (d_attn + pool_out.shape[2], d) and w_out.shape[0] <= s
    n_hp = d_attn // LANES
    score_lanes = g_q * n_hp * 2 * ATTN_BLOCK
    seq = pl.BlockSpec((1, s, d_attn), lambda b: (b, 0, 0))
    return pl.pallas_call(
        functools.partial(_attn_out_kernel, nblk=nblk, g_q=g_q, n_hp=n_hp, out_rows=out_rows),
        grid=(bsz,),
        in_specs=[seq, seq, pl.BlockSpec((1, nblk, d_attn, ATTN_BLOCK), lambda b: (b, 0, 0, 0)), seq,
                  pl.BlockSpec((1, s, pool_out.shape[2]), lambda b: (b, 0, 0)),
                  pl.BlockSpec(memory_space=pl.ANY),
                  pl.BlockSpec((1, 1, mod3.shape[2]), lambda b: (b, 0, 0)),
                  pl.BlockSpec(memory_space=pl.ANY)],
        out_specs=pl.BlockSpec(memory_space=pl.ANY),
        out_shape=jax.ShapeDtypeStruct((bsz, s, d), F32),
        scratch_shapes=[pltpu.VMEM((1, score_lanes), F32),
                        pltpu.VMEM((LANES, g_q * n_hp * ATTN_BLOCK), F32),
                        pltpu.VMEM((s, d_attn), BF16),
                        pltpu.VMEM(w_out.shape, BF16),
                        pltpu.VMEM((s, d), F32),
                        pltpu.VMEM((s, d), F32),
                        pltpu.SemaphoreType.DMA((s // out_rows,)),
                        pltpu.SemaphoreType.DMA((s // out_rows,)),
                        pltpu.SemaphoreType.DMA((1,))],
        compiler_params=pltpu.CompilerParams(
            dimension_semantics=("arbitrary",), vmem_limit_bytes=VMEM_LIMIT_BYTES),
        name="stickbreak_attn_outproj",
    )(qn, kn, vt, sg, pool_out, x, mod3, w_out)


def kernel(x, c, w_ada, b_ada, norm_g, w_in, q_norm_g, k_norm_g, w_pool, b_pool, pool_scale, w_out):
    depth = w_ada.shape[0]
    h = x
    for l in range(depth):
        mod3 = _ada_call(c, w_ada[l], b_ada[l])
        qn, kn, vt, sg, pool_out = _proj_call(
            mod3, h, norm_g[l].reshape(1, -1), w_in[l], q_norm_g[l].reshape(1, -1), k_norm_g[l].reshape(1, -1),
            w_pool[l], b_pool[l].reshape(1, -1), pool_scale[l].reshape(1, -1))
        h = _attn_out_call(qn, kn, vt, sg, pool_out, h, mod3, w_out[l])
    return h
```

```python
import functools

import jax
import jax.numpy as jnp
from jax import lax
from jax.experimental import pallas as pl
from jax.experimental.pallas import tpu as pltpu

F32 = jnp.float32
BF16 = jnp.bfloat16

HEAD_DIM = 64
POOL_WINDOWS = (2, 4, 8, 16)
EPS = 1e-6
LOG2E = 1.4426950408889634
SOFTPLUS_CLAMP = 64.0
REM_FLOOR_LOG2 = -136.0
DIAG_LOOKAHEAD = 1
BAND = 3

LANES = 128
SUBLANES = 8
MXU_WIDTH = 256
ATTN_BLOCK = 128
POOL_HIST = max(POOL_WINDOWS)
POOL_PAD = 2 * POOL_HIST
VMEM_LIMIT_BYTES = 58 * 1024 * 1024


def _silu(v):
    h = 0.5 * v
    return h + h * jnp.tanh(h)


def _first_grid_step():
    return (pl.program_id(0) == 0) & (pl.program_id(1) == 0)


def _ada_kernel(c_ref, w_ref, b_ref, o_ref):
    c = c_ref[...]
    ca = _silu(c).astype(BF16)
    mod = jnp.dot(ca, w_ref[...].astype(BF16), preferred_element_type=F32) + b_ref[...]
    for i in range(mod.shape[0]):
        o_ref[i] = mod[i:i + 1, :]


def _ada_call(c, w_ada, b_ada, *, tn=1024):
    bsz, d = c.shape
    n = w_ada.shape[1]
    return pl.pallas_call(
        _ada_kernel,
        grid=(n // tn,),
        in_specs=[
            pl.BlockSpec((bsz, d), lambda j: (0, 0)),
            pl.BlockSpec((d, tn), lambda j: (0, j)),
            pl.BlockSpec((1, tn), lambda j: (0, j)),
        ],
        out_specs=pl.BlockSpec((bsz, 1, tn), lambda j: (0, 0, j)),
        out_shape=jax.ShapeDtypeStruct((bsz, 1, n), F32),
        compiler_params=pltpu.CompilerParams(
            dimension_semantics=("arbitrary",), vmem_limit_bytes=VMEM_LIMIT_BYTES),
        name="adaln_mod",
    )(c, w_ada, b_ada.reshape(1, n))


def _proj_kernel(mod_ref, x_ref, ng_ref, win_ref, qg_ref, kg_ref, wp_ref, bp_ref, ps_ref,
                 q_ref, k_ref, vt_ref, sg_ref, po_ref, wbf, wpbf, qk_gain, ubuf, sbuf_a, sbuf_b,
                 *, ts, d_model, d_attn, d_pool):
    t = pl.program_id(1)
    hist = slice(POOL_PAD - POOL_HIST, POOL_PAD)
    gdim = d_pool // len(POOL_WINDOWS)
    pool_pack = MXU_WIDTH // gdim

    @pl.when(_first_grid_step())
    def _():
        wbf[...] = win_ref[...].astype(BF16)
        n_heads = d_attn // HEAD_DIM
        qk_gain[0:1, :] = jnp.concatenate([qg_ref[...] * (HEAD_DIM ** -0.5 * LOG2E)] * n_heads, axis=1)
        qk_gain[1:2, :] = jnp.concatenate([kg_ref[...]] * n_heads, axis=1)
        wpbf[...] = jnp.zeros_like(wpbf)
        for g in range(len(POOL_WINDOWS)):
            p, i = divmod(g, pool_pack)
            wpbf[p, i * gdim:(i + 1) * gdim, i * gdim:(i + 1) * gdim] = wp_ref[g].astype(BF16)

    @pl.when(t == 0)
    def _():
        ubuf[0:POOL_PAD, :] = jnp.zeros((POOL_PAD, d_pool), F32)

    @pl.when(t > 0)
    def _():
        ubuf[hist, :] = ubuf[ts + POOL_PAD - POOL_HIST:ts + POOL_PAD, :]

    x = x_ref[0]
    ms = jnp.mean(x * x, axis=-1, keepdims=True)
    shift = mod_ref[0, :, 0:d_model]
    scale = mod_ref[0, :, d_model:2 * d_model]
    a = ng_ref[...] * (1.0 + scale)
    hn = ((x * lax.rsqrt(ms + EPS)) * a + shift).astype(BF16)

    def proj(col, width):
        return jnp.dot(hn, wbf[:, col:col + width], preferred_element_type=F32)

    u = proj(4 * d_attn, d_pool)
    g_pool = proj(4 * d_attn + d_pool, d_pool)
    ubuf[POOL_PAD:POOL_PAD + ts, :] = u
    n_lvl = len(POOL_WINDOWS)
    end = POOL_PAD + ts
    src = ubuf
    wsums = []
    for lvl in range(n_lvl):
        shift_rows = 2 ** lvl
        start = SUBLANES * (lvl + 1)
        lanes = slice(lvl * gdim, d_pool)
        summed = src[start:end, lanes] + src[start - shift_rows:end - shift_rows, lanes]
        wsums.append(summed[POOL_PAD - start:, 0:gdim])
        if lvl + 1 < n_lvl:
            dst = sbuf_a if lvl % 2 == 0 else sbuf_b
            dst[start:end, lanes] = summed
            src = dst

    row = lax.broadcasted_iota(jnp.int32, (POOL_HIST, gdim), 0)
    pooled = []
    for g, win in enumerate(POOL_WINDOWS):
        ug = u[:, g * gdim:(g + 1) * gdim]
        cnt = jnp.minimum(t * ts + row + 1, win).astype(F32)
        head = wsums[g][0:POOL_HIST] / cnt - ug[0:POOL_HIST]
        rest = wsums[g][POOL_HIST:] * (1.0 / win) - ug[POOL_HIST:]
        pooled.append(jnp.concatenate([head, rest], axis=0).astype(BF16))
    for p in range(len(POOL_WINDOWS) // pool_pack):
        lo, hi = p * MXU_WIDTH, (p + 1) * MXU_WIDTH
        mixed = jnp.dot(jnp.concatenate(pooled[p * pool_pack:(p + 1) * pool_pack], axis=1), wpbf[p],
                        preferred_element_type=F32) + bp_ref[:, lo:hi]
        po_ref[0, :, lo:hi] = (mixed * ps_ref[:, lo:hi] * _silu(g_pool[:, lo:hi])).astype(BF16)

    r = lax.broadcasted_iota(jnp.int32, (MXU_WIDTH, MXU_WIDTH), 0) // HEAD_DIM
    c = lax.broadcasted_iota(jnp.int32, (MXU_WIDTH, MXU_WIDTH), 1) // HEAD_DIM
    head_avg = jnp.where(r == c, 1.0 / HEAD_DIM, 0.0).astype(BF16)

    def head_norm(v, g):
        sq = (v * v).astype(BF16)
        msq = jnp.concatenate([jnp.dot(sq[:, lo:lo + MXU_WIDTH], head_avg, preferred_element_type=F32)
                               for lo in range(0, d_attn, MXU_WIDTH)], axis=1)
        return (v * lax.rsqrt(msq + EPS)) * g

    q_ref[0] = head_norm(proj(0, d_attn), qk_gain[0:1, :]).astype(BF16)
    k_ref[0] = head_norm(proj(d_attn, d_attn), qk_gain[1:2, :]).astype(BF16)

    vt = proj(2 * d_attn, d_attn).T.astype(BF16)
    for jj in range(ts // ATTN_BLOCK):
        vt_ref[0, jj] = vt[:, jj * ATTN_BLOCK:(jj + 1) * ATTN_BLOCK]

    sg_ref[0] = _silu(proj(3 * d_attn, d_attn)).astype(BF16)


def _proj_call(mod3, x, norm_g, w_in, qg, kg, w_pool, b_pool, pool_scale, *, ts=1024):
    bsz, s, d = x.shape
    d_pool = pool_scale.shape[1]
    d_attn = (w_in.shape[1] - 2 * d_pool) // 4
    assert qg.shape == kg.shape == (1, HEAD_DIM)
    nt = s // ts
    nb = s // ATTN_BLOCK
    kern = functools.partial(_proj_kernel, ts=ts, d_model=d, d_attn=d_attn, d_pool=d_pool)
    const = lambda *shape: pl.BlockSpec(shape, lambda b, t: (0,) * len(shape))
    once = lambda *shape: pl.BlockSpec(shape, lambda b, t: (0,) * len(shape), pipeline_mode=pl.Buffered(1))
    seq_out = lambda width: pl.BlockSpec((1, ts, width), lambda b, t: (b, t, 0))
    pool_rows = POOL_PAD + ts
    return pl.pallas_call(
        kern,
        grid=(bsz, nt),
        in_specs=[
            pl.BlockSpec((1, 1, mod3.shape[2]), lambda b, t: (b, 0, 0)),
            pl.BlockSpec((1, ts, d), lambda b, t: (b, t, 0)),
            const(1, d),
            once(*w_in.shape),
            const(1, HEAD_DIM),
            const(1, HEAD_DIM),
            once(*w_pool.shape),
            const(1, d_pool),
            const(1, d_pool),
        ],
        out_specs=[
            seq_out(d_attn),
            seq_out(d_attn),
            pl.BlockSpec((1, ts // ATTN_BLOCK, d_attn, ATTN_BLOCK), lambda b, t: (b, t, 0, 0)),
            seq_out(d_attn),
            seq_out(d_pool),
        ],
        out_shape=[
            jax.ShapeDtypeStruct((bsz, s, d_attn), BF16),
            jax.ShapeDtypeStruct((bsz, s, d_attn), BF16),
            jax.ShapeDtypeStruct((bsz, nb, d_attn, ATTN_BLOCK), BF16),
            jax.ShapeDtypeStruct((bsz, s, d_attn), BF16),
            jax.ShapeDtypeStruct((bsz, s, d_pool), BF16),
        ],
        scratch_shapes=[pltpu.VMEM(w_in.shape, BF16),
                        pltpu.VMEM((d_pool // MXU_WIDTH, MXU_WIDTH, MXU_WIDTH), BF16),
                        pltpu.VMEM((2, d_attn), F32),
                        pltpu.VMEM((pool_rows, d_pool), F32),
                        pltpu.VMEM((pool_rows, d_pool), F32),
                        pltpu.VMEM((pool_rows, d_pool), F32)],
        compiler_params=pltpu.CompilerParams(
            dimension_semantics=("arbitrary", "arbitrary"), vmem_limit_bytes=VMEM_LIMIT_BYTES),
        name="norm_inproj_pool",
    )(mod3, x, norm_g, w_in, qg, kg, w_pool, b_pool, pool_scale)


def _attn_out_kernel(q_ref, k_ref, vt_ref, sg_ref, p_ref, x_hbm, mod_ref, w_hbm, out_hbm,
                     lr_ref, acc_ref, o_buf, wbf, xbuf, ybuf, x_sem, y_sem, w_sem,
                     *, nblk, g_q, n_hp, out_rows):
    tb = ATTN_BLOCK
    b = pl.program_id(0)
    d_attn = n_hp * LANES
    d_model = xbuf.shape[1]
    n_out = nblk * tb // out_rows

    def x_copy(c):
        rows = pl.ds(c * out_rows, out_rows)
        return pltpu.make_async_copy(x_hbm.at[b, rows, :], xbuf.at[rows, :], x_sem.at[c])

    def out_copy(bb, c):
        rows = pl.ds(c * out_rows, out_rows)
        return pltpu.make_async_copy(ybuf.at[rows, :], out_hbm.at[bb, rows, :], y_sem.at[c])

    def w_copy():
        return pltpu.make_async_copy(w_hbm, ybuf.at[0:wbf.shape[0], :], w_sem.at[0])

    @pl.when(b == 0)
    def _():
        w_copy().start()

    for c in range(n_out):
        x_copy(c).start()

    row = lax.broadcasted_iota(jnp.int32, (tb, tb), 0)
    col = lax.broadcasted_iota(jnp.int32, (tb, tb), 1)
    neg_suffix = jnp.concatenate([jnp.where(col > row, -1.0, 0.0), jnp.where(col == row, -1.0, 0.0)],
                                 axis=1).astype(BF16)
    causal = row < col
    n_grp = 2 * n_hp
    grp_w = g_q * tb

    def mask_diag(v, diag):
        if not diag:
            return v
        nq = v.shape[1] // (n_grp * tb)
        keep = causal[0:v.shape[0]]
        parts = []
        for grp in range(n_grp):
            base = grp * nq * tb
            parts.append(jnp.where(keep, v[:, base:base + tb], 0.0))
            if nq > 1:
                parts.append(v[:, base + tb:base + nq * tb])
        return jnp.concatenate(parts, axis=1)

    def blocks_alive(rem, q_lo, nq):
        n_alive = jnp.int32(0)
        for g in range(nq):
            blk = jnp.concatenate([rem[:, (grp * nq + g) * tb:(grp * nq + g + 1) * tb] for grp in range(n_grp)],
                                  axis=1)
            n_alive = jnp.where(jnp.max(blk) > REM_FLOOR_LOG2, q_lo + g + 1, n_alive)
        return n_alive

    def score(j, qts, q_lo, q_hi, diag):
        nq = q_hi - q_lo
        rows_j = pl.ds(pl.multiple_of(j * tb, tb), tb)
        zs = []
        for hp in range(n_hp):
            qt = qts[hp]
            if nq < g_q:
                qt = jnp.concatenate([qt[:, h * grp_w + q_lo * tb:h * grp_w + q_hi * tb] for h in range(2)],
                                     axis=1)
            zs.append(jnp.dot(k_ref[0, rows_j, hp * LANES:(hp + 1) * LANES], qt, preferred_element_type=F32))
        z = jnp.concatenate(zs, axis=1)
        sp = jnp.maximum(jnp.log(1.0 + jnp.exp2(jnp.minimum(z, SOFTPLUS_CLAMP))) * LOG2E, z)
        neg_log_beta = mask_diag(sp - z, diag)
        sp = mask_diag(sp, diag)
        return sp[0:1, :], jnp.concatenate([sp.astype(BF16), neg_log_beta.astype(BF16)], axis=0)

    def weigh(j, sp_0, terms, q_lo, q_hi, diag):
        nq = q_hi - q_lo
        lanes = [slice(grp * grp_w + q_lo * tb, grp * grp_w + q_hi * tb) for grp in range(n_grp)]
        log_w = jnp.dot(neg_suffix, terms, preferred_element_type=F32)
        log_rem = jnp.concatenate([lr_ref[:, ln] for ln in lanes], axis=1)
        w = mask_diag(jnp.exp2(log_w + log_rem), diag).astype(BF16)
        later_0 = -(log_w[0:1, :] + terms[tb:tb + 1, :].astype(F32))
        new_rem = log_rem - (sp_0 + later_0)
        for grp in range(n_grp):
            hp, h = divmod(grp, 2)
            cols = slice(grp * nq * tb, (grp + 1) * nq * tb)
            lr_ref[:, lanes[grp]] = new_rem[:, cols]
            vt_h = vt_ref[0, j, hp * LANES + h * HEAD_DIM:hp * LANES + (h + 1) * HEAD_DIM, :]
            res = jnp.dot(vt_h, w[:, cols], preferred_element_type=F32)
            acc_ref[h * HEAD_DIM:(h + 1) * HEAD_DIM, hp * grp_w + q_lo * tb:hp * grp_w + q_hi * tb] += res
        return blocks_alive(new_rem, q_lo, nq)

    def step(j, qts, q_lo, q_hi, diag):
        return weigh(j, *score(j, qts, q_lo, q_hi, diag), q_lo, q_hi, diag)

    def emit(sb, g):
        rows = pl.ds(pl.multiple_of((sb * g_q + g) * tb, tb), tb)
        for hp in range(n_hp):
            a = hp * grp_w + g * tb
            gate = sg_ref[0, rows, hp * LANES:(hp + 1) * LANES].astype(F32)
            o_buf[rows, hp * LANES:(hp + 1) * LANES] = (acc_ref[:, a:a + tb].T * gate).astype(BF16)

    def q_super_block(sb, carry):
        qts = []
        head_of_row = lax.broadcasted_iota(jnp.int32, (LANES, grp_w), 0) // HEAD_DIM
        for hp in range(n_hp):
            rows = pl.ds(pl.multiple_of(sb * grp_w, grp_w), grp_w)
            qt = q_ref[0, rows, hp * LANES:(hp + 1) * LANES].astype(F32).T
            qts.append(jnp.concatenate([jnp.where(head_of_row == h, qt, 0.0) for h in range(2)],
                                       axis=1).astype(BF16))
        lr_ref[...] = jnp.zeros_like(lr_ref)
        acc_ref[...] = jnp.zeros_like(acc_ref)
        base = sb * g_q

        def band(offsets):
            spans = [(m, max(m, 0), min(m + BAND, g_q), m >= 0) for m in offsets]
            scored = {}
            for i in range(len(spans) + DIAG_LOOKAHEAD):
                if i < len(spans):
                    m, lo, hi, diag = spans[i]
                    scored[m] = score(base + m, qts, lo, hi, diag)
                if i >= DIAG_LOOKAHEAD:
                    m, lo, hi, diag = spans[i - DIAG_LOOKAHEAD]
                    weigh(base + m, *scored.pop(m), lo, hi, diag)
                    if 0 <= m + BAND - 1 < g_q:
                        emit(sb, m + BAND - 1)

        first_blocks = range(min(BAND - 1, g_q))
        band(range(g_q - 1, -1, -1))
        for g in first_blocks:
            emit(sb, g)

        @pl.when(sb > 0)
        def _():
            band(range(-1, -BAND, -1))
            for g in first_blocks:
                emit(sb, g)

        def more(state):
            j, alive = state
            return (j >= 0) & (alive > 0)

        alive = [blocks_alive(jnp.concatenate([lr_ref[:, grp * grp_w + g * tb:grp * grp_w + (g + 1) * tb]
                                               for grp in range(n_grp)], axis=1), g, 1) for g in range(g_q)]
        for g in range(g_q):
            def one_block_step(state, g=g):
                j, _ = state
                return j - 1, step(j, qts, g, g + 1, False)

            start = (base + g - BAND, alive[g])

            @pl.when(more(start))
            def _(g=g, start=start, one_block_step=one_block_step):
                lax.while_loop(more, one_block_step, start)
                emit(sb, g)
        return carry

    lax.fori_loop(0, nblk // g_q, q_super_block, 0)

    @pl.when(b > 0)
    def _():
        for c in range(n_out):
            out_copy(b - 1, c).wait()

    @pl.when(b == 0)
    def _():
        w_copy().wait()
        wbf[...] = ybuf[0:wbf.shape[0], :].astype(BF16)

    gate = mod_ref[0, :, 2 * d_model:3 * d_model]
    for c in range(n_out):
        x_copy(c).wait()
    for c in range(n_out):
        rows = slice(c * out_rows, (c + 1) * out_rows)
        y = jnp.dot(o_buf[rows, :], wbf[0:d_attn, :], preferred_element_type=F32)
        y = y + jnp.dot(p_ref[0, rows, :], wbf[d_attn:, :], preferred_element_type=F32)
        res = xbuf[rows, :] + gate * y
        if c > 0:
            out_copy(b, c - 1).start()
        ybuf[rows, :] = res
    out_copy(b, n_out - 1).start()

    @pl.when(b == pl.num_programs(0) - 1)
    def _():
        for c in range(n_out):
            out_copy(b, c).wait()


def _attn_out_call(qn, kn, vt, sg, pool_out, x, mod3, w_out, *, g_q=16, out_rows=512):
    bsz, s, d = x.shape
    d_attn = qn.shape[2]
    nblk = s // ATTN_BLOCK
    assert nblk % g_q == 0 and s % out_rows == 0 and d_attn % LANES == 0
    assert w_out.shape == (d_attn + pool_out.shape[2], d) and w_out.shape[0] <= s
    n_hp = d_attn // LANES
    score_lanes = g_q * n_hp * 2 * ATTN_BLOCK
    seq = pl.BlockSpec((1, s, d_attn), lambda b: (b, 0, 0))
    return pl.pallas_call(
        functools.partial(_attn_out_kernel, nblk=nblk, g_q=g_q, n_hp=n_hp, out_rows=out_rows),
        grid=(bsz,),
        in_specs=[seq, seq, pl.BlockSpec((1, nblk, d_attn, ATTN_BLOCK), lambda b: (b, 0, 0, 0)), seq,
                  pl.BlockSpec((1, s, pool_out.shape[2]), lambda b: (b, 0, 0)),
                  pl.BlockSpec(memory_space=pl.ANY),
                  pl.BlockSpec((1, 1, mod3.shape[2]), lambda b: (b, 0, 0)),
                  pl.BlockSpec(memory_space=pl.ANY)],
        out_specs=pl.BlockSpec(memory_space=pl.ANY),
        out_shape=jax.ShapeDtypeStruct((bsz, s, d), F32),
        scratch_shapes=[pltpu.VMEM((1, score_lanes), F32),
                        pltpu.VMEM((LANES, g_q * n_hp * ATTN_BLOCK), F32),
                        pltpu.VMEM((s, d_attn), BF16),
                        pltpu.VMEM(w_out.shape, BF16),
                        pltpu.VMEM((s, d), F32),
                        pltpu.VMEM((s, d), F32),
                        pltpu.SemaphoreType.DMA((s // out_rows,)),
                        pltpu.SemaphoreType.DMA((s // out_rows,)),
                        pltpu.SemaphoreType.DMA((1,))],
        compiler_params=pltpu.CompilerParams(
            dimension_semantics=("arbitrary",), vmem_limit_bytes=VMEM_LIMIT_BYTES),
        name="stickbreak_attn_outproj",
    )(qn, kn, vt, sg, pool_out, x, mod3, w_out)


def kernel(x, c, w_ada, b_ada, norm_g, w_in, q_norm_g, k_norm_g, w_pool, b_pool, pool_scale, w_out):
    depth = w_ada.shape[0]
    h = x
    for l in range(depth):
        mod3 = _ada_call(c, w_ada[l], b_ada[l])
        qn, kn, vt, sg, pool_out = _proj_call(
            mod3, h, norm_g[l].reshape(1, -1), w_in[l], q_norm_g[l].reshape(1, -1), k_norm_g[l].reshape(1, -1),
            w_pool[l], b_pool[l].reshape(1, -1), pool_scale[l].reshape(1, -1))
        h = _attn_out_call(qn, kn, vt, sg, pool_out, h, mod3, w_out[l])
    return h
```

```python
import functools

import jax
import jax.numpy as jnp
from jax import lax
from jax.experimental import pallas as pl
from jax.experimental.pallas import tpu as pltpu

F32 = jnp.float32
BF16 = jnp.bfloat16

HEAD_DIM = 64
POOL_WINDOWS = (2, 4, 8, 16)
EPS = 1e-6
LOG2E = 1.4426950408889634
SOFTPLUS_CLAMP = 64.0
REM_FLOOR_LOG2 = -136.0
DIAG_LOOKAHEAD = 1
BAND = 3

LANES = 128
SUBLANES = 8
MXU_WIDTH = 256
ATTN_BLOCK = 128
POOL_HIST = max(POOL_WINDOWS)
POOL_PAD = 2 * POOL_HIST
VMEM_LIMIT_BYTES = 58 * 1024 * 1024


def _silu(v):
    h = 0.5 * v
    return h + h * jnp.tanh(h)


def _first_grid_step():
    return (pl.program_id(0) == 0) & (pl.program_id(1) == 0)


def _ada_kernel(c_ref, w_ref, b_ref, o_ref):
    c = c_ref[...]
    ca = _silu(c).astype(BF16)
    mod = jnp.dot(ca, w_ref[...].astype(BF16), preferred_element_type=F32) + b_ref[...]
    for i in range(mod.shape[0]):
        o_ref[i] = mod[i:i + 1, :]


def _ada_call(c, w_ada, b_ada, *, tn=1024):
    bsz, d = c.shape
    n = w_ada.shape[1]
    return pl.pallas_call(
        _ada_kernel,
        grid=(n // tn,),
        in_specs=[
            pl.BlockSpec((bsz, d), lambda j: (0, 0)),
            pl.BlockSpec((d, tn), lambda j: (0, j)),
            pl.BlockSpec((1, tn), lambda j: (0, j)),
        ],
        out_specs=pl.BlockSpec((bsz, 1, tn), lambda j: (0, 0, j)),
        out_shape=jax.ShapeDtypeStruct((bsz, 1, n), F32),
        compiler_params=pltpu.CompilerParams(
            dimension_semantics=("arbitrary",), vmem_limit_bytes=VMEM_LIMIT_BYTES),
        name="adaln_mod",
    )(c, w_ada, b_ada.reshape(1, n))


def _proj_kernel(mod_ref, x_ref, ng_ref, win_ref, qg_ref, kg_ref, wp_ref, bp_ref, ps_ref,
                 q_ref, k_ref, vt_ref, sg_ref, po_ref, wbf, wpbf, qk_gain, ubuf, sbuf_a, sbuf_b,
                 *, ts, d_model, d_attn, d_pool):
    t = pl.program_id(1)
    hist = slice(POOL_PAD - POOL_HIST, POOL_PAD)
    gdim = d_pool // len(POOL_WINDOWS)
    pool_pack = MXU_WIDTH // gdim

    @pl.when(_first_grid_step())
    def _():
        wbf[...] = win_ref[...].astype(BF16)
        n_heads = d_attn // HEAD_DIM
        qk_gain[0:1, :] = jnp.concatenate([qg_ref[...] * (HEAD_DIM ** -0.5 * LOG2E)] * n_heads, axis=1)
        qk_gain[1:2, :] = jnp.concatenate([kg_ref[...]] * n_heads, axis=1)
        wpbf[...] = jnp.zeros_like(wpbf)
        for g in range(len(POOL_WINDOWS)):
            p, i = divmod(g, pool_pack)
            wpbf[p, i * gdim:(i + 1) * gdim, i * gdim:(i + 1) * gdim] = wp_ref[g].astype(BF16)

    @pl.when(t == 0)
    def _():
        ubuf[0:POOL_PAD, :] = jnp.zeros((POOL_PAD, d_pool), F32)

    @pl.when(t > 0)
    def _():
        ubuf[hist, :] = ubuf[ts + POOL_PAD - POOL_HIST:ts + POOL_PAD, :]

    x = x_ref[0]
    ms = jnp.mean(x * x, axis=-1, keepdims=True)
    shift = mod_ref[0, :, 0:d_model]
    scale = mod_ref[0, :, d_model:2 * d_model]
    a = ng_ref[...] * (1.0 + scale)
    hn = ((x * lax.rsqrt(ms + EPS)) * a + shift).astype(BF16)

    def proj(col, width):
        return jnp.dot(hn, wbf[:, col:col + width], preferred_element_type=F32)

    u = proj(4 * d_attn, d_pool)
    g_pool = proj(4 * d_attn + d_pool, d_pool)
    ubuf[POOL_PAD:POOL_PAD + ts, :] = u
    n_lvl = len(POOL_WINDOWS)
    end = POOL_PAD + ts
    src = ubuf
    wsums = []
    for lvl in range(n_lvl):
        shift_rows = 2 ** lvl
        start = SUBLANES * (lvl + 1)
        lanes = slice(lvl * gdim, d_pool)
        summed = src[start:end, lanes] + src[start - shift_rows:end - shift_rows, lanes]
        wsums.append(summed[POOL_PAD - start:, 0:gdim])
        if lvl + 1 < n_lvl:
            dst = sbuf_a if lvl % 2 == 0 else sbuf_b
            dst[start:end, lanes] = summed
            src = dst

    row = lax.broadcasted_iota(jnp.int32, (POOL_HIST, gdim), 0)
    pooled = []
    for g, win in enumerate(POOL_WINDOWS):
        ug = u[:, g * gdim:(g + 1) * gdim]
        cnt = jnp.minimum(t * ts + row + 1, win).astype(F32)
        head = wsums[g][0:POOL_HIST] / cnt - ug[0:POOL_HIST]
        rest = wsums[g][POOL_HIST:] * (1.0 / win) - ug[POOL_HIST:]
        pooled.append(jnp.concatenate([head, rest], axis=0).astype(BF16))
    for p in range(len(POOL_WINDOWS) // pool_pack):
        lo, hi = p * MXU_WIDTH, (p + 1) * MXU_WIDTH
        mixed = jnp.dot(jnp.concatenate(pooled[p * pool_pack:(p + 1) * pool_pack], axis=1), wpbf[p],
                        preferred_element_type=F32) + bp_ref[:, lo:hi]
        po_ref[0, :, lo:hi] = (mixed * ps_ref[:, lo:hi] * _silu(g_pool[:, lo:hi])).astype(BF16)

    r = lax.broadcasted_iota(jnp.int32, (MXU_WIDTH, MXU_WIDTH), 0) // HEAD_DIM
    c = lax.broadcasted_iota(jnp.int32, (MXU_WIDTH, MXU_WIDTH), 1) // HEAD_DIM
    head_avg = jnp.where(r == c, 1.0 / HEAD_DIM, 0.0).astype(BF16)

    def head_norm(v, g):
        sq = (v * v).astype(BF16)
        msq = jnp.concatenate([jnp.dot(sq[:, lo:lo + MXU_WIDTH], head_avg, preferred_element_type=F32)
                               for lo in range(0, d_attn, MXU_WIDTH)], axis=1)
        return (v * lax.rsqrt(msq + EPS)) * g

    q_ref[0] = head_norm(proj(0, d_attn), qk_gain[0:1, :]).astype(BF16)
    k_ref[0] = head_norm(proj(d_attn, d_attn), qk_gain[1:2, :]).astype(BF16)

    vt = proj(2 * d_attn, d_attn).T.astype(BF16)
    for jj in range(ts // ATTN_BLOCK):
        vt_ref[0, jj] = vt[:, jj * ATTN_BLOCK:(jj + 1) * ATTN_BLOCK]

    sg_ref[0] = _silu(proj(3 * d_attn, d_attn)).astype(BF16)


def _proj_call(mod3, x, norm_g, w_in, qg, kg, w_pool, b_pool, pool_scale, *, ts=1024):
    bsz, s, d = x.shape
    d_pool = pool_scale.shape[1]
    d_attn = (w_in.shape[1] - 2 * d_pool) // 4
    assert qg.shape == kg.shape == (1, HEAD_DIM)
    nt = s // ts
    nb = s // ATTN_BLOCK
    kern = functools.partial(_proj_kernel, ts=ts, d_model=d, d_attn=d_attn, d_pool=d_pool)
    const = lambda *shape: pl.BlockSpec(shape, lambda b, t: (0,) * len(shape))
    once = lambda *shape: pl.BlockSpec(shape, lambda b, t: (0,) * len(shape), pipeline_mode=pl.Buffered(1))
    seq_out = lambda width: pl.BlockSpec((1, ts, width), lambda b, t: (b, t, 0))
    pool_rows = POOL_PAD + ts
    return pl.pallas_call(
        kern,
        grid=(bsz, nt),
        in_specs=[
            pl.BlockSpec((1, 1, mod3.shape[2]), lambda b, t: (b, 0, 0)),
            pl.BlockSpec((1, ts, d), lambda b, t: (b, t, 0)),
            const(1, d),
            once(*w_in.shape),
            const(1, HEAD_DIM),
            const(1, HEAD_DIM),
            once(*w_pool.shape),
            const(1, d_pool),
            const(1, d_pool),
        ],
        out_specs=[
            seq_out(d_attn),
            seq_out(d_attn),
            pl.BlockSpec((1, ts // ATTN_BLOCK, d_attn, ATTN_BLOCK), lambda b, t: (b, t, 0, 0)),
            seq_out(d_attn),
            seq_out(d_pool),
        ],
        out_shape=[
            jax.ShapeDtypeStruct((bsz, s, d_attn), BF16),
            jax.ShapeDtypeStruct((bsz, s, d_attn), BF16),
            jax.ShapeDtypeStruct((bsz, nb, d_attn, ATTN_BLOCK), BF16),
            jax.ShapeDtypeStruct((bsz, s, d_attn), BF16),
            jax.ShapeDtypeStruct((bsz, s, d_pool), BF16),
        ],
        scratch_shapes=[pltpu.VMEM(w_in.shape, BF16),
                        pltpu.VMEM((d_pool // MXU_WIDTH, MXU_WIDTH, MXU_WIDTH), BF16),
                        pltpu.VMEM((2, d_attn), F32),
                        pltpu.VMEM((pool_rows, d_pool), F32),
                        pltpu.VMEM((pool_rows, d_pool), F32),
                        pltpu.VMEM((pool_rows, d_pool), F32)],
        compiler_params=pltpu.CompilerParams(
            dimension_semantics=("arbitrary", "arbitrary"), vmem_limit_bytes=VMEM_LIMIT_BYTES),
        name="norm_inproj_pool",
    )(mod3, x, norm_g, w_in, qg, kg, w_pool, b_pool, pool_scale)


def _attn_out_kernel(q_ref, k_ref, vt_ref, sg_ref, p_ref, x_hbm, mod_ref, w_hbm, out_hbm,
                     lr_ref, acc_ref, o_buf, wbf, xbuf, ybuf, x_sem, y_sem, w_sem,
                     *, nblk, g_q, n_hp, out_rows):
    tb = ATTN_BLOCK
    b = pl.program_id(0)
    d_attn = n_hp * LANES
    d_model = xbuf.shape[1]
    n_out = nblk * tb // out_rows

    def x_copy(c):
        rows = pl.ds(c * out_rows, out_rows)
        return pltpu.make_async_copy(x_hbm.at[b, rows, :], xbuf.at[rows, :], x_sem.at[c])

    def out_copy(bb, c):
        rows = pl.ds(c * out_rows, out_rows)
        return pltpu.make_async_copy(ybuf.at[rows, :], out_hbm.at[bb, rows, :], y_sem.at[c])

    def w_copy():
        return pltpu.make_async_copy(w_hbm, ybuf.at[0:wbf.shape[0], :], w_sem.at[0])

    @pl.when(b == 0)
    def _():
        w_copy().start()

    for c in range(n_out):
        x_copy(c).start()

    row = lax.broadcasted_iota(jnp.int32, (tb, tb), 0)
    col = lax.broadcasted_iota(jnp.int32, (tb, tb), 1)
    neg_suffix = jnp.concatenate([jnp.where(col > row, -1.0, 0.0), jnp.where(col == row, -1.0, 0.0)],
                                 axis=1).astype(BF16)
    causal = row < col
    n_grp = 2 * n_hp
    grp_w = g_q * tb

    def mask_diag(v, diag):
        if not diag:
            return v
        nq = v.shape[1] // (n_grp * tb)
        keep = causal[0:v.shape[0]]
        parts = []
        for grp in range(n_grp):
            base = grp * nq * tb
            parts.append(jnp.where(keep, v[:, base:base + tb], 0.0))
            if nq > 1:
                parts.append(v[:, base + tb:base + nq * tb])
        return jnp.concatenate(parts, axis=1)

    def blocks_alive(rem, q_lo, nq):
        n_alive = jnp.int32(0)
        for g in range(nq):
            blk = jnp.concatenate([rem[:, (grp * nq + g) * tb:(grp * nq + g + 1) * tb] for grp in range(n_grp)],
                                  axis=1)
            n_alive = jnp.where(jnp.max(blk) > REM_FLOOR_LOG2, q_lo + g + 1, n_alive)
        return n_alive

    def score(j, qts, q_lo, q_hi, diag):
        nq = q_hi - q_lo
        rows_j = pl.ds(pl.multiple_of(j * tb, tb), tb)
        zs = []
        for hp in range(n_hp):
            qt = qts[hp]
            if nq < g_q:
                qt = jnp.concatenate([qt[:, h * grp_w + q_lo * tb:h * grp_w + q_hi * tb] for h in range(2)],
                                     axis=1)
            zs.append(jnp.dot(k_ref[0, rows_j, hp * LANES:(hp + 1) * LANES], qt, preferred_element_type=F32))
        z = jnp.concatenate(zs, axis=1)
        sp = jnp.maximum(jnp.log(1.0 + jnp.exp2(jnp.minimum(z, SOFTPLUS_CLAMP))) * LOG2E, z)
        neg_log_beta = mask_diag(sp - z, diag)
        sp = mask_diag(sp, diag)
        return sp[0:1, :], jnp.concatenate([sp.astype(BF16), neg_log_beta.astype(BF16)], axis=0)

    def weigh(j, sp_0, terms, q_lo, q_hi, diag):
        nq = q_hi - q_lo
        lanes = [slice(grp * grp_w + q_lo * tb, grp * grp_w + q_hi * tb) for grp in range(n_grp)]
        log_w = jnp.dot(neg_suffix, terms, preferred_element_type=F32)
        log_rem = jnp.concatenate([lr_ref[:, ln] for ln in lanes], axis=1)
        w = mask_diag(jnp.exp2(log_w + log_rem), diag).astype(BF16)
        later_0 = -(log_w[0:1, :] + terms[tb:tb + 1, :].astype(F32))
        new_rem = log_rem - (sp_0 + later_0)
        for grp in range(n_grp):
            hp, h = divmod(grp, 2)
            cols = slice(grp * nq * tb, (grp + 1) * nq * tb)
            lr_ref[:, lanes[grp]] = new_rem[:, cols]
            vt_h = vt_ref[0, j, hp * LANES + h * HEAD_DIM:hp * LANES + (h + 1) * HEAD_DIM, :]
            res = jnp.dot(vt_h, w[:, cols], preferred_element_type=F32)
            acc_ref[h * HEAD_DIM:(h + 1) * HEAD_DIM, hp * grp_w + q_lo * tb:hp * grp_w + q_hi * tb] += res
        return blocks_alive(new_rem, q_lo, nq)

    def step(j, qts, q_lo, q_hi, diag):
        return weigh(j, *score(j, qts, q_lo, q_hi, diag), q_lo, q_hi, diag)

    def emit(sb, g):
        rows = pl.ds(pl.multiple_of((sb * g_q + g) * tb, tb), tb)
        for hp in range(n_hp):
            a = hp * grp_w + g * tb
            gate = sg_ref[0, rows, hp * LANES:(hp + 1) * LANES].astype(F32)
            o_buf[rows, hp * LANES:(hp + 1) * LANES] = (acc_ref[:, a:a + tb].T * gate).astype(BF16)

    def q_super_block(sb, carry):
        qts = []
        head_of_row = lax.broadcasted_iota(jnp.int32, (LANES, grp_w), 0) // HEAD_DIM
        for hp in range(n_hp):
            rows = pl.ds(pl.multiple_of(sb * grp_w, grp_w), grp_w)
            qt = q_ref[0, rows, hp * LANES:(hp + 1) * LANES].astype(F32).T
            qts.append(jnp.concatenate([jnp.where(head_of_row == h, qt, 0.0) for h in range(2)],
                                       axis=1).astype(BF16))
        lr_ref[...] = jnp.zeros_like(lr_ref)
        acc_ref[...] = jnp.zeros_like(acc_ref)
        base = sb * g_q

        def band(offsets):
            spans = [(m, max(m, 0), min(m + BAND, g_q), m >= 0) for m in offsets]
            scored = {}
            for i in range(len(spans) + DIAG_LOOKAHEAD):
                if i < len(spans):
                    m, lo, hi, diag = spans[i]
                    scored[m] = score(base + m, qts, lo, hi, diag)
                if i >= DIAG_LOOKAHEAD:
                    m, lo, hi, diag = spans[i - DIAG_LOOKAHEAD]
                    weigh(base + m, *scored.pop(m), lo, hi, diag)
                    if 0 <= m + BAND - 1 < g_q:
                        emit(sb, m + BAND - 1)

        first_blocks = range(min(BAND - 1, g_q))
        band(range(g_q - 1, -1, -1))
        for g in first_blocks:
            emit(sb, g)

        @pl.when(sb > 0)
        def _():
            band(range(-1, -BAND, -1))
            for g in first_blocks:
                emit(sb, g)

        def more(state):
            j, alive = state
            return (j >= 0) & (alive > 0)

        alive = [blocks_alive(jnp.concatenate([lr_ref[:, grp * grp_w + g * tb:grp * grp_w + (g + 1) * tb]
                                               for grp in range(n_grp)], axis=1), g, 1) for g in range(g_q)]
        for g in range(g_q):
            def one_block_step(state, g=g):
                j, _ = state
                return j - 1, step(j, qts, g, g + 1, False)

            start = (base + g - BAND, alive[g])

            @pl.when(more(start))
            def _(g=g, start=start, one_block_step=one_block_step):
                lax.while_loop(more, one_block_step, start)
                emit(sb, g)
        return carry

    lax.fori_loop(0, nblk // g_q, q_super_block, 0)

    @pl.when(b > 0)
    def _():
        for c in range(n_out):
            out_copy(b - 1, c).wait()

    @pl.when(b == 0)
    def _():
        w_copy().wait()
        wbf[...] = ybuf[0:wbf.shape[0], :].astype(BF16)

    gate = mod_ref[0, :, 2 * d_model:3 * d_model]
    for c in range(n_out):
        x_copy(c).wait()
    for c in range(n_out):
        rows = slice(c * out_rows, (c + 1) * out_rows)
        y = jnp.dot(o_buf[rows, :], wbf[0:d_attn, :], preferred_element_type=F32)
        y = y + jnp.dot(p_ref[0, rows, :], wbf[d_attn:, :], preferred_element_type=F32)
        res = xbuf[rows, :] + gate * y
        if c > 0:
            out_copy(b, c - 1).start()
        ybuf[rows, :] = res
    out_copy(b, n_out - 1).start()

    @pl.when(b == pl.num_programs(0) - 1)
    def _():
        for c in range(n_out):
            out_copy(b, c).wait()


def _attn_out_call(qn, kn, vt, sg, pool_out, x, mod3, w_out, *, g_q=16, out_rows=2048):
    bsz, s, d = x.shape
    d_attn = qn.shape[2]
    nblk = s // ATTN_BLOCK
    assert nblk % g_q == 0 and s % out_rows == 0 and d_attn % LANES == 0
    assert w_out.shape == (d_attn + pool_out.shape[2], d) and w_out.shape[0] <= s
    n_hp = d_attn // LANES
    score_lanes = g_q * n_hp * 2 * ATTN_BLOCK
    seq = pl.BlockSpec((1, s, d_attn), lambda b: (b, 0, 0))
    return pl.pallas_call(
        functools.partial(_attn_out_kernel, nblk=nblk, g_q=g_q, n_hp=n_hp, out_rows=out_rows),
        grid=(bsz,),
        in_specs=[seq, seq, pl.BlockSpec((1, nblk, d_attn, ATTN_BLOCK), lambda b: (b, 0, 0, 0)), seq,
                  pl.BlockSpec((1, s, pool_out.shape[2]), lambda b: (b, 0, 0)),
                  pl.BlockSpec(memory_space=pl.ANY),
                  pl.BlockSpec((1, 1, mod3.shape[2]), lambda b: (b, 0, 0)),
                  pl.BlockSpec(memory_space=pl.ANY)],
        out_specs=pl.BlockSpec(memory_space=pl.ANY),
        out_shape=jax.ShapeDtypeStruct((bsz, s, d), F32),
        scratch_shapes=[pltpu.VMEM((1, score_lanes), F32),
                        pltpu.VMEM((LANES, g_q * n_hp * ATTN_BLOCK), F32),
                        pltpu.VMEM((s, d_attn), BF16),
                        pltpu.VMEM(w_out.shape, BF16),
                        pltpu.VMEM((s, d), F32),
                        pltpu.VMEM((s, d), F32),
                        pltpu.SemaphoreType.DMA((s // out_rows,)),
                        pltpu.SemaphoreType.DMA((s // out_rows,)),
                        pltpu.SemaphoreType.DMA((1,))],
        compiler_params=pltpu.CompilerParams(
            dimension_semantics=("arbitrary",), vmem_limit_bytes=VMEM_LIMIT_BYTES),
        name="stickbreak_attn_outproj",
    )(qn, kn, vt, sg, pool_out, x, mod3, w_out)


def kernel(x, c, w_ada, b_ada, norm_g, w_in, q_norm_g, k_norm_g, w_pool, b_pool, pool_scale, w_out):
    depth = w_ada.shape[0]
    h = x
    for l in range(depth):
        mod3 = _ada_call(c, w_ada[l], b_ada[l])
        qn, kn, vt, sg, pool_out = _proj_call(
            mod3, h, norm_g[l].reshape(1, -1), w_in[l], q_norm_g[l].reshape(1, -1), k_norm_g[l].reshape(1, -1),
            w_pool[l], b_pool[l].reshape(1, -1), pool_scale[l].reshape(1, -1))
        h = _attn_out_call(qn, kn, vt, sg, pool_out, h, mod3, w_out[l])
    return h
```
